```python
import math
import jax, jax.numpy as jnp
from jax import lax
import numpy as np

D_MODEL = 2048
BATCH = 4
SEQ = 2048
DEPTH = 1

GRID_W = 64
ROPE_THETA = 10000.0
Q_BLOCK = 128
EPS = 1e-6

HEAD_DIM = 128
HA = 8
HKV = 2
G_A = HA // HKV

HB = 8
QK_NOPE = 128
QK_ROPE = 64
V_DIM = 128
KV_RANK = 512
QK_B = QK_NOPE + QK_ROPE

W_QA = HA * HEAD_DIM
W_KA = HKV * HEAD_DIM
W_VA = HKV * HEAD_DIM
W_QB = HB * QK_B
W_CKV = KV_RANK
W_KPE = QK_ROPE
W_IN = W_QA + W_KA + W_VA + W_QB + W_CKV + W_KPE

MIX_A = HA * HEAD_DIM
MIX_B = HB * V_DIM
MIX_W = MIX_A + MIX_B

D_FF = 4 * D_MODEL
N_MOD = 6

kernel_name = "hybrid_gqa_mla_axialrope_block"


def rms_norm(x, g):
    xf = x.astype(jnp.float32)
    y = xf * lax.rsqrt(jnp.mean(xf * xf, axis=-1, keepdims=True) + EPS)
    return (y * g.astype(jnp.float32)).astype(x.dtype)


def rope_tables_2d(seq_len, dim):
    rows = seq_len // GRID_W
    r, cl = jnp.meshgrid(jnp.arange(rows), jnp.arange(GRID_W), indexing="ij")
    row = r.reshape(-1).astype(jnp.float32)
    col = cl.reshape(-1).astype(jnp.float32)
    half = dim // 2
    inv = ROPE_THETA ** (-jnp.arange(0, half, 2, dtype=jnp.float32) / half)
    ang_r = row[:, None] * inv[None, :]
    ang_c = col[:, None] * inv[None, :]
    ang = jnp.concatenate([ang_r, ang_r, ang_c, ang_c], axis=-1)
    return jnp.cos(ang), jnp.sin(ang)


def _rotate_half(v):
    h = v.shape[-1] // 2
    return jnp.concatenate([-v[..., h:], v[..., :h]], axis=-1)


def apply_rope_2d(x, cos, sin):
    half = x.shape[-1] // 2
    rot = jnp.concatenate([_rotate_half(x[..., :half]), _rotate_half(x[..., half:])], axis=-1)
    return (x.astype(jnp.float32) * cos + rot.astype(jnp.float32) * sin).astype(x.dtype)


def gqa_blocked(q, k, v):
    B, _, _, S, D = q.shape
    scale = 1.0 / math.sqrt(D)

    def block(i):
        qb = lax.dynamic_slice_in_dim(q, i * Q_BLOCK, Q_BLOCK, axis=3)
        s = jnp.einsum("bkgqd,bksd->bkgqs", qb, k).astype(jnp.float32) * scale
        p = jax.nn.softmax(s, axis=-1).astype(v.dtype)
        return jnp.einsum("bkgqs,bksd->bkgqd", p, v)

    o = lax.map(block, jnp.arange(S // Q_BLOCK))
    o = jnp.transpose(o, (1, 0, 4, 2, 3, 5))
    return o.reshape(B, S, HA * D)


def mla_blocked(q_nope, q_pe, k_nope, k_pe, v):
    B, H, S, _ = q_nope.shape
    scale = 1.0 / math.sqrt(QK_B)

    def block(i):
        qn = lax.dynamic_slice_in_dim(q_nope, i * Q_BLOCK, Q_BLOCK, axis=2)
        qp = lax.dynamic_slice_in_dim(q_pe, i * Q_BLOCK, Q_BLOCK, axis=2)
        s = (jnp.einsum("bhqd,bhsd->bhqs", qn, k_nope)
             + jnp.einsum("bhqr,bsr->bhqs", qp, k_pe)).astype(jnp.float32) * scale
        p = jax.nn.softmax(s, axis=-1).astype(v.dtype)
        return jnp.einsum("bhqs,bhsd->bhqd", p, v)

    o = lax.map(block, jnp.arange(S // Q_BLOCK))
    o = jnp.transpose(o, (1, 0, 3, 2, 4))
    return o.reshape(B, S, H * V_DIM)


def setup_inputs(seed: int = 0) -> dict:
    key = jax.random.key(seed)
    ks = jax.random.split(key, 20)
    f = jnp.float32

    def w(k, shape, fan_in):
        return jax.random.normal(k, shape, f) * (fan_in ** -0.5)

    def gain(k, shape):
        return 1.0 + 0.05 * jax.random.normal(k, shape, f)

    L = DEPTH
    return {
        "x": jax.random.normal(ks[0], (BATCH, SEQ, D_MODEL), f),
        "c": jax.random.normal(ks[1], (BATCH, D_MODEL), f),
        "w_ada": w(ks[2], (L, D_MODEL, N_MOD * D_MODEL), D_MODEL),
        "b_ada": 0.02 * jax.random.normal(ks[3], (L, N_MOD * D_MODEL), f),
        "g_pre_attn": gain(ks[4], (L, D_MODEL)),
        "w_in": w(ks[5], (L, D_MODEL, W_IN), D_MODEL),
        "g_q_a": gain(ks[6], (L, HEAD_DIM)),
        "g_k_a": gain(ks[7], (L, HEAD_DIM)),
        "g_ckv": gain(ks[8], (L, KV_RANK)),
        "w_kv_b": w(ks[9], (L, KV_RANK, HB * (QK_NOPE + V_DIM)), KV_RANK),
        "g_out_a": gain(ks[10], (L, MIX_A)),
        "g_out_b": gain(ks[11], (L, MIX_B)),
        "w_out": w(ks[12], (L, MIX_W, D_MODEL), MIX_W),
        "g_post_attn": gain(ks[13], (L, D_MODEL)),
        "g_pre_mlp": gain(ks[14], (L, D_MODEL)),
        "w_mlp_in": w(ks[15], (L, D_MODEL, D_FF), D_MODEL),
        "w_mlp_out": w(ks[16], (L, D_FF, D_MODEL), D_FF),
        "g_post_mlp": gain(ks[17], (L, D_MODEL)),
    }


def reference(x, c, w_ada, b_ada, g_pre_attn, w_in, g_q_a, g_k_a, g_ckv, w_kv_b,
              g_out_a, g_out_b, w_out, g_post_attn, g_pre_mlp, w_mlp_in, w_mlp_out,
              g_post_mlp):
    B, S, _ = x.shape
    cos_a, sin_a = rope_tables_2d(S, HEAD_DIM)
    cos_b, sin_b = rope_tables_2d(S, QK_ROPE)
    c_act = jax.nn.silu(c)

    offs = np.cumsum([0, W_QA, W_KA, W_VA, W_QB, W_CKV, W_KPE])

    for l in range(DEPTH):
        mod = c_act @ w_ada[l] + b_ada[l]
        sh_a, sc_a, gt_a, sh_m, sc_m, gt_m = [m[:, None, :] for m in jnp.split(mod, N_MOD, axis=-1)]

        h = rms_norm(x, g_pre_attn[l]) * (1.0 + sc_a) + sh_a
        proj = h @ w_in[l]
        q_a, k_a, v_a, q_b, ckv, k_pe = [proj[..., offs[i]:offs[i + 1]] for i in range(6)]

        q_a = rms_norm(q_a.reshape(B, S, HA, HEAD_DIM), g_q_a[l]).transpose(0, 2, 1, 3)
        k_a = rms_norm(k_a.reshape(B, S, HKV, HEAD_DIM), g_k_a[l]).transpose(0, 2, 1, 3)
        v_a = v_a.reshape(B, S, HKV, HEAD_DIM).transpose(0, 2, 1, 3)
        q_a = apply_rope_2d(q_a, cos_a, sin_a).reshape(B, HKV, G_A, S, HEAD_DIM)
        k_a = apply_rope_2d(k_a, cos_a, sin_a)
        o_a = gqa_blocked(q_a, k_a, v_a)

        q_b = q_b.reshape(B, S, HB, QK_B).transpose(0, 2, 1, 3)
        q_nope = q_b[..., :QK_NOPE]
        q_pe = apply_rope_2d(q_b[..., QK_NOPE:], cos_b, sin_b)
        kv = rms_norm(ckv, g_ckv[l]) @ w_kv_b[l]
        kv = kv.reshape(B, S, HB, QK_NOPE + V_DIM).transpose(0, 2, 1, 3)
        k_nope, v_b = kv[..., :QK_NOPE], kv[..., QK_NOPE:]
        k_pe = apply_rope_2d(k_pe, cos_b, sin_b)
        o_b = mla_blocked(q_nope, q_pe, k_nope, k_pe, v_b)

        o = jnp.concatenate([rms_norm(o_a, g_out_a[l]), rms_norm(o_b, g_out_b[l])], axis=-1)
        o = o @ w_out[l]
        x = x + gt_a * rms_norm(o, g_post_attn[l])

        h = rms_norm(x, g_pre_mlp[l]) * (1.0 + sc_m) + sh_m
        y = jnp.square(jax.nn.relu(h @ w_mlp_in[l])) @ w_mlp_out[l]
        x = x + gt_m * rms_norm(y, g_post_mlp[l])

    return x
```

```python
import functools
import math

import numpy as np
import jax
import jax.numpy as jnp
from jax import lax
from jax.experimental import pallas as pl
from jax.experimental.pallas import tpu as pltpu

D_MODEL = 2048
GRID_W = 64
ROPE_THETA = 10000.0
EPS = 1e-6

HEAD_DIM = 128
HA = 8
HKV = 2
G_A = HA // HKV

HB = 8
QK_NOPE = 128
QK_ROPE = 64
V_DIM = 128
KV_RANK = 512
QK_B = QK_NOPE + QK_ROPE
QK_B_PAD = 256

W_QA = HA * HEAD_DIM
W_KA = HKV * HEAD_DIM
W_VA = HKV * HEAD_DIM
W_QB = HB * QK_B
MIX_A = HA * HEAD_DIM
MIX_B = HB * V_DIM
D_FF = 4 * D_MODEL
N_MOD = 6

LANES = 128
VMEM_LIMIT = 56 * 1024 * 1024

F32 = jnp.float32
BF16 = jnp.bfloat16


def _rope_tables(seq_len, dim):
    rows = seq_len // GRID_W
    pos = np.arange(seq_len)
    row = (pos // GRID_W).astype(np.float64)
    col = (pos % GRID_W).astype(np.float64)
    half = dim // 2
    inv = ROPE_THETA ** (-np.arange(0, half, 2, dtype=np.float64) / half)
    ang_r = row[:, None] * inv[None, :]
    ang_c = col[:, None] * inv[None, :]
    ang = np.concatenate([ang_r, ang_r, ang_c, ang_c], axis=-1)
    cos, sin = np.cos(ang), np.sin(ang)
    quarter = dim // 4
    sign = np.where((np.arange(dim) % half) < quarter, -1.0, 1.0)
    sin = sin * sign[None, :]
    reps = LANES // dim
    cos = np.tile(cos, (1, reps))
    sin = np.tile(sin, (1, reps))
    del rows
    return jnp.asarray(cos, F32), jnp.asarray(sin, F32)


def _rms(x, g):
    ms = jnp.mean(x * x, axis=-1, keepdims=True)
    return x * lax.rsqrt(ms + EPS) * g


def _rope(x, cos, sin_signed, quarter):
    lane = lax.broadcasted_iota(jnp.int32, x.shape, 1)
    take_up = (lane % (2 * quarter)) < quarter
    up = pltpu.roll(x, LANES - quarter, axis=1)
    down = pltpu.roll(x, quarter, axis=1)
    rot = jnp.where(take_up, up, down)
    return x * cos + rot * sin_signed


def _ada_kernel(c_ref, w_ref, b_ref, o_ref):
    c = c_ref[...]
    c_act = (c * jax.nn.sigmoid(c)).astype(BF16)
    w = w_ref[...].astype(BF16)
    o_ref[...] = jnp.dot(c_act, w, preferred_element_type=F32) + b_ref[...]


def _ada(c_pad, w_ada, b_ada):
    m, d = c_pad.shape
    n = w_ada.shape[1]
    tn = 1024
    return pl.pallas_call(
        _ada_kernel,
        grid=(n // tn,),
        in_specs=[
            pl.BlockSpec((m, d), lambda j: (0, 0)),
            pl.BlockSpec((d, tn), lambda j: (0, j)),
            pl.BlockSpec((1, tn), lambda j: (0, j)),
        ],
        out_specs=pl.BlockSpec((m, tn), lambda j: (0, j)),
        out_shape=jax.ShapeDtypeStruct((m, n), F32),
        compiler_params=pltpu.CompilerParams(
            dimension_semantics=("arbitrary",), vmem_limit_bytes=VMEM_LIMIT),
        name="ada_mod",
    )(c_pad, w_ada, b_ada)


def _inproj_kernel(x_ref, sc_ref, sh_ref, gpre_ref, w_ref, wkv_ref, gq_ref, gk_ref,
                   gckv_ref, cosa_ref, sina_ref, cosb_ref, sinb_ref,
                   qa_ref, ka_ref, va_ref, qb_ref, kb_ref, vb_ref):
    x = x_ref[0]
    h = _rms(x, gpre_ref[...]) * (1.0 + sc_ref[0]) + sh_ref[0]
    hb = h.astype(BF16)
    cosa, sina = cosa_ref[...], sina_ref[...]
    cosb, sinb = cosb_ref[...], sinb_ref[...]
    scale_a = 1.0 / math.sqrt(HEAD_DIM)
    scale_b = 1.0 / math.sqrt(QK_B)

    qa = jnp.dot(hb, w_ref[:, 0:W_QA], preferred_element_type=F32)
    gq = gq_ref[...] * scale_a
    for hd in range(HA):
        sl = slice(hd * HEAD_DIM, (hd + 1) * HEAD_DIM)
        q = _rope(_rms(qa[:, sl], gq), cosa, sina, HEAD_DIM // 4)
        qa_ref[0, :, sl] = q.astype(BF16)

    kva = jnp.dot(hb, w_ref[:, W_QA:W_QA + W_KA + W_VA], preferred_element_type=F32)
    for hd in range(HKV):
        sl = slice(hd * HEAD_DIM, (hd + 1) * HEAD_DIM)
        k = _rope(_rms(kva[:, sl], gk_ref[...]), cosa, sina, HEAD_DIM // 4)
        ka_ref[0, :, sl] = k.astype(BF16)
    va_ref[0] = kva[:, W_KA:].astype(BF16)

    o_qn = W_QA + W_KA + W_VA
    qn = jnp.dot(hb, w_ref[:, o_qn:o_qn + HB * QK_NOPE], preferred_element_type=F32)
    o_pe = o_qn + HB * QK_NOPE
    n_rest = HB * QK_ROPE + KV_RANK + LANES
    rest = jnp.dot(hb, w_ref[:, o_pe:o_pe + n_rest], preferred_element_type=F32)
    lane = lax.broadcasted_iota(jnp.int32, (x.shape[0], LANES), 1)
    low = lane < QK_ROPE
    for c in range(HB * QK_ROPE // LANES):
        pe = _rope(rest[:, c * LANES:(c + 1) * LANES], cosb, sinb, QK_ROPE // 4) * scale_b
        even = jnp.where(low, pe, 0.0)
        odd = jnp.where(low, pltpu.roll(pe, QK_ROPE, axis=1), 0.0)
        for hd, val in ((2 * c, even), (2 * c + 1, odd)):
            base = hd * QK_B_PAD
            qb_ref[0, :, base:base + QK_NOPE] = (
                qn[:, hd * QK_NOPE:(hd + 1) * QK_NOPE] * scale_b).astype(BF16)
            qb_ref[0, :, base + QK_NOPE:base + QK_B_PAD] = val.astype(BF16)

    o_ckv = HB * QK_ROPE
    ckv = _rms(rest[:, o_ckv:o_ckv + KV_RANK], gckv_ref[...]).astype(BF16)
    kv = jnp.dot(ckv, wkv_ref[...], preferred_element_type=F32)
    kpe = _rope(rest[:, o_ckv + KV_RANK:], cosb, sinb, QK_ROPE // 4).astype(BF16)
    for hd in range(HB):
        base = hd * (QK_NOPE + V_DIM)
        kb_ref[0, :, hd * QK_B_PAD:hd * QK_B_PAD + QK_NOPE] = (
            kv[:, base:base + QK_NOPE].astype(BF16))
        kb_ref[0, :, hd * QK_B_PAD + QK_NOPE:(hd + 1) * QK_B_PAD] = kpe
        vb_ref[0, :, hd * V_DIM:(hd + 1) * V_DIM] = (
            kv[:, base + QK_NOPE:base + QK_NOPE + V_DIM].astype(BF16))


def _inproj(x, sc, sh, g_pre, w_in_p, w_kv, g_q, g_k, g_ckv, tabs, tm):
    b, s, d = x.shape
    n_in = w_in_p.shape[1]
    cosa, sina, cosb, sinb = tabs
    row = lambda bi, i: (bi, i, 0)
    per_b = lambda bi, i: (bi, 0, 0)
    const = lambda bi, i: (0, 0)
    tab = lambda bi, i: (i, 0)
    vec = lambda n: pl.BlockSpec((1, n), const)
    out_widths = (MIX_A, W_KA, W_VA, HB * QK_B_PAD, HB * QK_B_PAD, MIX_B)
    return pl.pallas_call(
        _inproj_kernel,
        grid=(b, s // tm),
        in_specs=[
            pl.BlockSpec((1, tm, d), row),
            pl.BlockSpec((1, 1, d), per_b),
            pl.BlockSpec((1, 1, d), per_b),
            vec(d),
            pl.BlockSpec((d, n_in), const, pipeline_mode=pl.Buffered(1)),
            pl.BlockSpec((KV_RANK, HB * (QK_NOPE + V_DIM)), const,
                         pipeline_mode=pl.Buffered(1)),
            vec(HEAD_DIM), vec(HEAD_DIM), vec(KV_RANK),
            pl.BlockSpec((tm, LANES), tab), pl.BlockSpec((tm, LANES), tab),
            pl.BlockSpec((tm, LANES), tab), pl.BlockSpec((tm, LANES), tab),
        ],
        out_specs=[pl.BlockSpec((1, tm, w), row) for w in out_widths],
        out_shape=[jax.ShapeDtypeStruct((b, s, w), BF16) for w in out_widths],
        compiler_params=pltpu.CompilerParams(
            dimension_semantics=("arbitrary", "arbitrary"),
            vmem_limit_bytes=VMEM_LIMIT),
        name="in_proj",
    )(x, sc, sh, g_pre, w_in_p, w_kv, g_q, g_k, g_ckv, cosa, sina, cosb, sinb)


def _attn_kernel(q_ref, k_ref, v_ref, o_ref, *, n_heads, dk):
    k = k_ref[0]
    v = v_ref[0]
    for hd in range(n_heads):
        q = q_ref[0, :, hd * dk:(hd + 1) * dk]
        s = lax.dot_general(q, k, (((1,), (1,)), ((), ())), preferred_element_type=F32)
        m = jnp.max(s, axis=-1, keepdims=True)
        p = jnp.exp(s - m)
        l = jnp.sum(p, axis=-1, keepdims=True)
        o = jnp.dot(p.astype(BF16), v, preferred_element_type=F32)
        o_ref[0, :, hd * V_DIM:(hd + 1) * V_DIM] = (o / l).astype(BF16)


def _attention(q, k, v, *, n_groups, n_heads, dk, tq, name):
    b, s, _ = q.shape
    return pl.pallas_call(
        functools.partial(_attn_kernel, n_heads=n_heads, dk=dk),
        grid=(b, n_groups, s // tq),
        in_specs=[
            pl.BlockSpec((1, tq, n_heads * dk), lambda bi, g, i: (bi, i, g)),
            pl.BlockSpec((1, s, dk), lambda bi, g, i: (bi, 0, g)),
            pl.BlockSpec((1, s, V_DIM), lambda bi, g, i: (bi, 0, g)),
        ],
        out_specs=pl.BlockSpec((1, tq, n_heads * V_DIM), lambda bi, g, i: (bi, i, g)),
        out_shape=jax.ShapeDtypeStruct((b, s, n_groups * n_heads * V_DIM), BF16),
        compiler_params=pltpu.CompilerParams(
            dimension_semantics=("arbitrary", "arbitrary", "arbitrary"),
            vmem_limit_bytes=VMEM_LIMIT),
        name=name,
    )(q, k, v)


def _outproj_kernel(oa_ref, ob_ref, x_ref, gt_ref, ga_ref, gb_ref, w_ref, gpost_ref, o_ref):
    na = _rms(oa_ref[0].astype(F32), ga_ref[...]).astype(BF16)
    nb = _rms(ob_ref[0].astype(F32), gb_ref[...]).astype(BF16)
    o = jnp.dot(na, w_ref[0:MIX_A, :], preferred_element_type=F32)
    o = o + jnp.dot(nb, w_ref[MIX_A:, :], preferred_element_type=F32)
    o_ref[0] = x_ref[0] + gt_ref[0] * _rms(o, gpost_ref[...])


def _outproj(o_a, o_b, x, gt, g_a, g_b, w_out, g_post, tm):
    b, s, d = x.shape
    row = lambda bi, i: (bi, i, 0)
    per_b = lambda bi, i: (bi, 0, 0)
    const = lambda bi, i: (0, 0)
    return pl.pallas_call(
        _outproj_kernel,
        grid=(b, s // tm),
        in_specs=[
            pl.BlockSpec((1, tm, MIX_A), row),
            pl.BlockSpec((1, tm, MIX_B), row),
            pl.BlockSpec((1, tm, d), row),
            pl.BlockSpec((1, 1, d), per_b),
            pl.BlockSpec((1, MIX_A), const),
            pl.BlockSpec((1, MIX_B), const),
            pl.BlockSpec((MIX_A + MIX_B, d), const, pipeline_mode=pl.Buffered(1)),
            pl.BlockSpec((1, d), const),
        ],
        out_specs=pl.BlockSpec((1, tm, d), row),
        out_shape=jax.ShapeDtypeStruct((b, s, d), F32),
        compiler_params=pltpu.CompilerParams(
            dimension_semantics=("arbitrary", "arbitrary"),
            vmem_limit_bytes=VMEM_LIMIT),
        name="out_proj",
    )(o_a, o_b, x, gt, g_a, g_b, w_out, g_post)


def _mlp_kernel(x_ref, sc_ref, sh_ref, gt_ref, gpre_ref, w1_ref, w2_ref, gpost_ref,
                o_ref, h_scr, acc_scr):
    f = pl.program_id(2)

    @pl.when(f == 0)
    def _():
        h = _rms(x_ref[0], gpre_ref[...]) * (1.0 + sc_ref[0]) + sh_ref[0]
        h_scr[...] = h.astype(BF16)
        acc_scr[...] = jnp.zeros_like(acc_scr)

    u = jnp.dot(h_scr[...], w1_ref[...], preferred_element_type=F32)
    u = jnp.maximum(u, 0.0)
    acc_scr[...] += jnp.dot((u * u).astype(BF16), w2_ref[...], preferred_element_type=F32)

    @pl.when(f == pl.num_programs(2) - 1)
    def _():
        o_ref[0] = x_ref[0] + gt_ref[0] * _rms(acc_scr[...], gpost_ref[...])


def _mlp(x, sc, sh, gt, g_pre, w1, w2, g_post, tm, tf):
    b, s, d = x.shape
    dff = w1.shape[1]
    row = lambda bi, i, f: (bi, i, 0)
    per_b = lambda bi, i, f: (bi, 0, 0)
    const = lambda bi, i, f: (0, 0)
    return pl.pallas_call(
        _mlp_kernel,
        grid=(b, s // tm, dff // tf),
        in_specs=[
            pl.BlockSpec((1, tm, d), row),
            pl.BlockSpec((1, 1, d), per_b),
            pl.BlockSpec((1, 1, d), per_b),
            pl.BlockSpec((1, 1, d), per_b),
            pl.BlockSpec((1, d), const),
            pl.BlockSpec((d, tf), lambda bi, i, f: (0, f)),
            pl.BlockSpec((tf, d), lambda bi, i, f: (f, 0)),
            pl.BlockSpec((1, d), const),
        ],
        out_specs=pl.BlockSpec((1, tm, d), row),
        out_shape=jax.ShapeDtypeStruct((b, s, d), F32),
        scratch_shapes=[pltpu.VMEM((tm, d), BF16), pltpu.VMEM((tm, d), F32)],
        compiler_params=pltpu.CompilerParams(
            dimension_semantics=("arbitrary", "arbitrary", "arbitrary"),
            vmem_limit_bytes=VMEM_LIMIT),
        name="mlp",
    )(x, sc, sh, gt, g_pre, w1, w2, g_post)


def _permute_w_in(w):
    d = w.shape[0]
    o_qb = W_QA + W_KA + W_VA
    qb = w[:, o_qb:o_qb + W_QB].reshape(d, HB, QK_B)
    qn = qb[:, :, :QK_NOPE].reshape(d, HB * QK_NOPE)
    qpe = qb[:, :, QK_NOPE:].reshape(d, HB * QK_ROPE)
    tail = w[:, o_qb + W_QB:]
    pad = jnp.zeros((d, LANES - QK_ROPE), w.dtype)
    return jnp.concatenate([w[:, :o_qb], qn, qpe, tail, pad], axis=1).astype(BF16)


def kernel(x, c, w_ada, b_ada, g_pre_attn, w_in, g_q_a, g_k_a, g_ckv, w_kv_b, g_out_a,
           g_out_b, w_out, g_post_attn, g_pre_mlp, w_mlp_in, w_mlp_out, g_post_mlp):
    b, s, d = x.shape
    depth = w_ada.shape[0]
    tabs = _rope_tables(s, HEAD_DIM) + _rope_tables(s, QK_ROPE)
    c_pad = jnp.pad(c, ((0, 8 - b), (0, 0)))

    for l in range(depth):
        mod = _ada(c_pad, w_ada[l], b_ada[l][None, :])[:b]
        sh_a, sc_a, gt_a, sh_m, sc_m, gt_m = [
            m.reshape(b, 1, d) for m in jnp.split(mod, N_MOD, axis=-1)]

        q_a, k_a, v_a, q_b, k_b, v_b = _inproj(
            x, sc_a, sh_a, g_pre_attn[l][None, :], _permute_w_in(w_in[l]),
            w_kv_b[l].astype(BF16), g_q_a[l][None, :], g_k_a[l][None, :],
            g_ckv[l][None, :], tabs, tm=256)

        o_a = _attention(q_a, k_a, v_a, n_groups=HKV, n_heads=G_A, dk=HEAD_DIM,
                         tq=256, name="attn_gqa")
        o_b = _attention(q_b, k_b, v_b, n_groups=HB, n_heads=1, dk=QK_B_PAD,
                         tq=256, name="attn_mla")

        x = _outproj(o_a, o_b, x, gt_a, g_out_a[l][None, :], g_out_b[l][None, :],
                     w_out[l].astype(BF16), g_post_attn[l][None, :], tm=256)

        x = _mlp(x, sc_m, sh_m, gt_m, g_pre_mlp[l][None, :], w_mlp_in[l].astype(BF16),
                 w_mlp_out[l].astype(BF16), g_post_mlp[l][None, :], tm=512, tf=1024)
    return x
```

```python
import functools
import math

import numpy as np
import jax
import jax.numpy as jnp
from jax import lax
from jax.experimental import pallas as pl
from jax.experimental.pallas import tpu as pltpu

D_MODEL = 2048
GRID_W = 64
ROPE_THETA = 10000.0
EPS = 1e-6

HEAD_DIM = 128
HA = 8
HKV = 2
G_A = HA // HKV

HB = 8
QK_NOPE = 128
QK_ROPE = 64
V_DIM = 128
KV_RANK = 512
QK_B = QK_NOPE + QK_ROPE
QK_B_PAD = 256

W_QA = HA * HEAD_DIM
W_KA = HKV * HEAD_DIM
W_VA = HKV * HEAD_DIM
W_QB = HB * QK_B
MIX_A = HA * HEAD_DIM
MIX_B = HB * V_DIM
D_FF = 4 * D_MODEL
N_MOD = 6

LOG2_E = math.log2(math.e)
LANES = 128
VMEM_LIMIT = 56 * 1024 * 1024

F32 = jnp.float32
BF16 = jnp.bfloat16


def _rope_tables(seq_len, dim):
    rows = seq_len // GRID_W
    pos = np.arange(seq_len)
    row = (pos // GRID_W).astype(np.float64)
    col = (pos % GRID_W).astype(np.float64)
    half = dim // 2
    inv = ROPE_THETA ** (-np.arange(0, half, 2, dtype=np.float64) / half)
    ang_r = row[:, None] * inv[None, :]
    ang_c = col[:, None] * inv[None, :]
    ang = np.concatenate([ang_r, ang_r, ang_c, ang_c], axis=-1)
    cos, sin = np.cos(ang), np.sin(ang)
    quarter = dim // 4
    sign = np.where((np.arange(dim) % half) < quarter, -1.0, 1.0)
    sin = sin * sign[None, :]
    reps = LANES // dim
    cos = np.tile(cos, (1, reps))
    sin = np.tile(sin, (1, reps))
    del rows
    return jnp.asarray(cos, F32), jnp.asarray(sin, F32)


def _rms(x, g):
    ms = jnp.mean(x * x, axis=-1, keepdims=True)
    return x * lax.rsqrt(ms + EPS) * g


def _rope(x, cos, sin_signed, quarter):
    lane = lax.broadcasted_iota(jnp.int32, x.shape, 1)
    take_up = (lane % (2 * quarter)) < quarter
    up = pltpu.roll(x, LANES - quarter, axis=1)
    down = pltpu.roll(x, quarter, axis=1)
    rot = jnp.where(take_up, up, down)
    return x * cos + rot * sin_signed


def _ada_kernel(c_ref, w_ref, b_ref, o_ref):
    c = c_ref[...]
    c_act = (c * jax.nn.sigmoid(c)).astype(BF16)
    w = w_ref[...].astype(BF16)
    o_ref[...] = jnp.dot(c_act, w, preferred_element_type=F32) + b_ref[...]


def _ada(c_pad, w_ada, b_ada):
    m, d = c_pad.shape
    n = w_ada.shape[1]
    tn = 1024
    return pl.pallas_call(
        _ada_kernel,
        grid=(n // tn,),
        in_specs=[
            pl.BlockSpec((m, d), lambda j: (0, 0)),
            pl.BlockSpec((d, tn), lambda j: (0, j)),
            pl.BlockSpec((1, tn), lambda j: (0, j)),
        ],
        out_specs=pl.BlockSpec((m, tn), lambda j: (0, j)),
        out_shape=jax.ShapeDtypeStruct((m, n), F32),
        compiler_params=pltpu.CompilerParams(
            dimension_semantics=("arbitrary",), vmem_limit_bytes=VMEM_LIMIT),
        name="ada_mod",
    )(c_pad, w_ada, b_ada)


def _inproj_kernel(x_ref, sc_ref, sh_ref, gpre_ref, w_ref, wkv_ref, gq_ref, gk_ref,
                   gckv_ref, cosa_ref, sina_ref, cosb_ref, sinb_ref,
                   qa_ref, ka_ref, va_ref, qb_ref, kb_ref, vb_ref):
    x = x_ref[0]
    h = _rms(x, gpre_ref[...]) * (1.0 + sc_ref[0]) + sh_ref[0]
    hb = h.astype(BF16)
    cosa, sina = cosa_ref[...], sina_ref[...]
    cosb, sinb = cosb_ref[...], sinb_ref[...]
    scale_a = LOG2_E / math.sqrt(HEAD_DIM)
    scale_b = LOG2_E / math.sqrt(QK_B)

    qa = jnp.dot(hb, w_ref[:, 0:W_QA], preferred_element_type=F32)
    gq = gq_ref[...] * scale_a
    for hd in range(HA):
        sl = slice(hd * HEAD_DIM, (hd + 1) * HEAD_DIM)
        q = _rope(_rms(qa[:, sl], gq), cosa, sina, HEAD_DIM // 4)
        qa_ref[0, :, sl] = q.astype(BF16)

    kva = jnp.dot(hb, w_ref[:, W_QA:W_QA + W_KA + W_VA], preferred_element_type=F32)
    for hd in range(HKV):
        sl = slice(hd * HEAD_DIM, (hd + 1) * HEAD_DIM)
        k = _rope(_rms(kva[:, sl], gk_ref[...]), cosa, sina, HEAD_DIM // 4)
        ka_ref[0, :, sl] = k.astype(BF16)
    va_ref[0] = kva[:, W_KA:].astype(BF16)

    o_qn = W_QA + W_KA + W_VA
    qn = jnp.dot(hb, w_ref[:, o_qn:o_qn + HB * QK_NOPE], preferred_element_type=F32)
    o_pe = o_qn + HB * QK_NOPE
    n_rest = HB * QK_ROPE + KV_RANK + LANES
    rest = jnp.dot(hb, w_ref[:, o_pe:o_pe + n_rest], preferred_element_type=F32)
    lane = lax.broadcasted_iota(jnp.int32, (x.shape[0], LANES), 1)
    low = lane < QK_ROPE
    for c in range(HB * QK_ROPE // LANES):
        pe = _rope(rest[:, c * LANES:(c + 1) * LANES], cosb, sinb, QK_ROPE // 4) * scale_b
        even = jnp.where(low, pe, 0.0)
        odd = jnp.where(low, pltpu.roll(pe, QK_ROPE, axis=1), 0.0)
        for hd, val in ((2 * c, even), (2 * c + 1, odd)):
            base = hd * QK_B_PAD
            qb_ref[0, :, base:base + QK_NOPE] = (
                qn[:, hd * QK_NOPE:(hd + 1) * QK_NOPE] * scale_b).astype(BF16)
            qb_ref[0, :, base + QK_NOPE:base + QK_B_PAD] = val.astype(BF16)

    o_ckv = HB * QK_ROPE
    ckv = _rms(rest[:, o_ckv:o_ckv + KV_RANK], gckv_ref[...]).astype(BF16)
    kv = jnp.dot(ckv, wkv_ref[...], preferred_element_type=F32)
    kpe = _rope(rest[:, o_ckv + KV_RANK:], cosb, sinb, QK_ROPE // 4).astype(BF16)
    for hd in range(HB):
        base = hd * (QK_NOPE + V_DIM)
        kb_ref[0, :, hd * QK_B_PAD:hd * QK_B_PAD + QK_NOPE] = (
            kv[:, base:base + QK_NOPE].astype(BF16))
        kb_ref[0, :, hd * QK_B_PAD + QK_NOPE:(hd + 1) * QK_B_PAD] = kpe
        vb_ref[0, :, hd * V_DIM:(hd + 1) * V_DIM] = (
            kv[:, base + QK_NOPE:base + QK_NOPE + V_DIM].astype(BF16))


def _inproj(x, sc, sh, g_pre, w_in_p, w_kv, g_q, g_k, g_ckv, tabs, tm):
    b, s, d = x.shape
    n_in = w_in_p.shape[1]
    cosa, sina, cosb, sinb = tabs
    row = lambda bi, i: (bi, i, 0)
    per_b = lambda bi, i: (bi, 0, 0)
    const = lambda bi, i: (0, 0)
    tab = lambda bi, i: (i, 0)
    vec = lambda n: pl.BlockSpec((1, n), const)
    out_widths = (MIX_A, W_KA, W_VA, HB * QK_B_PAD, HB * QK_B_PAD, MIX_B)
    return pl.pallas_call(
        _inproj_kernel,
        grid=(b, s // tm),
        in_specs=[
            pl.BlockSpec((1, tm, d), row),
            pl.BlockSpec((1, 1, d), per_b),
            pl.BlockSpec((1, 1, d), per_b),
            vec(d),
            pl.BlockSpec((d, n_in), const, pipeline_mode=pl.Buffered(1)),
            pl.BlockSpec((KV_RANK, HB * (QK_NOPE + V_DIM)), const,
                         pipeline_mode=pl.Buffered(1)),
            vec(HEAD_DIM), vec(HEAD_DIM), vec(KV_RANK),
            pl.BlockSpec((tm, LANES), tab), pl.BlockSpec((tm, LANES), tab),
            pl.BlockSpec((tm, LANES), tab), pl.BlockSpec((tm, LANES), tab),
        ],
        out_specs=[pl.BlockSpec((1, tm, w), row) for w in out_widths],
        out_shape=[jax.ShapeDtypeStruct((b, s, w), BF16) for w in out_widths],
        compiler_params=pltpu.CompilerParams(
            dimension_semantics=("arbitrary", "arbitrary"),
            vmem_limit_bytes=VMEM_LIMIT),
        name="in_proj",
    )(x, sc, sh, g_pre, w_in_p, w_kv, g_q, g_k, g_ckv, cosa, sina, cosb, sinb)


def _attn_kernel(q_ref, k_ref, v_ref, o_ref, *, n_heads, dk, shared_kv):
    for hd in range(n_heads):
        kv_hd = 0 if shared_kv else hd
        k = k_ref[0, :, kv_hd * dk:(kv_hd + 1) * dk]
        v = v_ref[0, :, kv_hd * V_DIM:(kv_hd + 1) * V_DIM]
        q = q_ref[0, :, hd * dk:(hd + 1) * dk]
        s = lax.dot_general(q, k, (((1,), (1,)), ((), ())), preferred_element_type=F32)
        m = jnp.max(s, axis=-1, keepdims=True)
        p = jnp.exp2(s - m)
        l = jnp.sum(p, axis=-1, keepdims=True)
        o = jnp.dot(p.astype(BF16), v, preferred_element_type=F32)
        o_ref[0, :, hd * V_DIM:(hd + 1) * V_DIM] = (o / l).astype(BF16)


def _attention(q, k, v, *, n_groups, n_heads, dk, shared_kv, tq, name):
    b, s, _ = q.shape
    n_kv = 1 if shared_kv else n_heads
    return pl.pallas_call(
        functools.partial(_attn_kernel, n_heads=n_heads, dk=dk, shared_kv=shared_kv),
        grid=(b, n_groups, s // tq),
        in_specs=[
            pl.BlockSpec((1, tq, n_heads * dk), lambda bi, g, i: (bi, i, g)),
            pl.BlockSpec((1, s, n_kv * dk), lambda bi, g, i: (bi, 0, g)),
            pl.BlockSpec((1, s, n_kv * V_DIM), lambda bi, g, i: (bi, 0, g)),
        ],
        out_specs=pl.BlockSpec((1, tq, n_heads * V_DIM), lambda bi, g, i: (bi, i, g)),
        out_shape=jax.ShapeDtypeStruct((b, s, n_groups * n_heads * V_DIM), BF16),
        compiler_params=pltpu.CompilerParams(
            dimension_semantics=("arbitrary", "arbitrary", "arbitrary"),
            vmem_limit_bytes=VMEM_LIMIT),
        name=name,
    )(q, k, v)


def _outproj_kernel(oa_ref, ob_ref, x_ref, gt_ref, ga_ref, gb_ref, w_ref, gpost_ref, o_ref):
    na = _rms(oa_ref[0].astype(F32), ga_ref[...]).astype(BF16)
    nb = _rms(ob_ref[0].astype(F32), gb_ref[...]).astype(BF16)
    o = jnp.dot(na, w_ref[0:MIX_A, :], preferred_element_type=F32)
    o = o + jnp.dot(nb, w_ref[MIX_A:, :], preferred_element_type=F32)
    o_ref[0] = x_ref[0] + gt_ref[0] * _rms(o, gpost_ref[...])


def _outproj(o_a, o_b, x, gt, g_a, g_b, w_out, g_post, tm):
    b, s, d = x.shape
    row = lambda bi, i: (bi, i, 0)
    per_b = lambda bi, i: (bi, 0, 0)
    const = lambda bi, i: (0, 0)
    return pl.pallas_call(
        _outproj_kernel,
        grid=(b, s // tm),
        in_specs=[
            pl.BlockSpec((1, tm, MIX_A), row),
            pl.BlockSpec((1, tm, MIX_B), row),
            pl.BlockSpec((1, tm, d), row),
            pl.BlockSpec((1, 1, d), per_b),
            pl.BlockSpec((1, MIX_A), const),
            pl.BlockSpec((1, MIX_B), const),
            pl.BlockSpec((MIX_A + MIX_B, d), const, pipeline_mode=pl.Buffered(1)),
            pl.BlockSpec((1, d), const),
        ],
        out_specs=pl.BlockSpec((1, tm, d), row),
        out_shape=jax.ShapeDtypeStruct((b, s, d), F32),
        compiler_params=pltpu.CompilerParams(
            dimension_semantics=("arbitrary", "arbitrary"),
            vmem_limit_bytes=VMEM_LIMIT),
        name="out_proj",
    )(o_a, o_b, x, gt, g_a, g_b, w_out, g_post)


def _mlp_kernel(x_ref, sc_ref, sh_ref, gt_ref, gpre_ref, w1_ref, w2_ref, gpost_ref,
                o_ref, h_scr, acc_scr):
    f = pl.program_id(2)

    @pl.when(f == 0)
    def _():
        h = _rms(x_ref[0], gpre_ref[...]) * (1.0 + sc_ref[0]) + sh_ref[0]
        h_scr[...] = h.astype(BF16)
        acc_scr[...] = jnp.zeros_like(acc_scr)

    u = jnp.dot(h_scr[...], w1_ref[...], preferred_element_type=F32)
    u = jnp.maximum(u, 0.0)
    acc_scr[...] += jnp.dot((u * u).astype(BF16), w2_ref[...], preferred_element_type=F32)

    @pl.when(f == pl.num_programs(2) - 1)
    def _():
        o_ref[0] = x_ref[0] + gt_ref[0] * _rms(acc_scr[...], gpost_ref[...])


def _mlp(x, sc, sh, gt, g_pre, w1, w2, g_post, tm, tf):
    b, s, d = x.shape
    dff = w1.shape[1]
    row = lambda bi, i, f: (bi, i, 0)
    per_b = lambda bi, i, f: (bi, 0, 0)
    const = lambda bi, i, f: (0, 0)
    return pl.pallas_call(
        _mlp_kernel,
        grid=(b, s // tm, dff // tf),
        in_specs=[
            pl.BlockSpec((1, tm, d), row),
            pl.BlockSpec((1, 1, d), per_b),
            pl.BlockSpec((1, 1, d), per_b),
            pl.BlockSpec((1, 1, d), per_b),
            pl.BlockSpec((1, d), const),
            pl.BlockSpec((d, tf), lambda bi, i, f: (0, f)),
            pl.BlockSpec((tf, d), lambda bi, i, f: (f, 0)),
            pl.BlockSpec((1, d), const),
        ],
        out_specs=pl.BlockSpec((1, tm, d), row),
        out_shape=jax.ShapeDtypeStruct((b, s, d), F32),
        scratch_shapes=[pltpu.VMEM((tm, d), BF16), pltpu.VMEM((tm, d), F32)],
        compiler_params=pltpu.CompilerParams(
            dimension_semantics=("arbitrary", "arbitrary", "arbitrary"),
            vmem_limit_bytes=VMEM_LIMIT),
        name="mlp",
    )(x, sc, sh, gt, g_pre, w1, w2, g_post)


def _permute_w_in(w):
    d = w.shape[0]
    o_qb = W_QA + W_KA + W_VA
    qb = w[:, o_qb:o_qb + W_QB].reshape(d, HB, QK_B)
    qn = qb[:, :, :QK_NOPE].reshape(d, HB * QK_NOPE)
    qpe = qb[:, :, QK_NOPE:].reshape(d, HB * QK_ROPE)
    tail = w[:, o_qb + W_QB:]
    pad = jnp.zeros((d, LANES - QK_ROPE), w.dtype)
    return jnp.concatenate([w[:, :o_qb], qn, qpe, tail, pad], axis=1).astype(BF16)


def kernel(x, c, w_ada, b_ada, g_pre_attn, w_in, g_q_a, g_k_a, g_ckv, w_kv_b, g_out_a,
           g_out_b, w_out, g_post_attn, g_pre_mlp, w_mlp_in, w_mlp_out, g_post_mlp):
    b, s, d = x.shape
    depth = w_ada.shape[0]
    tabs = _rope_tables(s, HEAD_DIM) + _rope_tables(s, QK_ROPE)
    c_pad = jnp.pad(c, ((0, 8 - b), (0, 0)))

    for l in range(depth):
        mod = _ada(c_pad, w_ada[l], b_ada[l][None, :])[:b]
        sh_a, sc_a, gt_a, sh_m, sc_m, gt_m = [
            m.reshape(b, 1, d) for m in jnp.split(mod, N_MOD, axis=-1)]

        q_a, k_a, v_a, q_b, k_b, v_b = _inproj(
            x, sc_a, sh_a, g_pre_attn[l][None, :], _permute_w_in(w_in[l]),
            w_kv_b[l].astype(BF16), g_q_a[l][None, :], g_k_a[l][None, :],
            g_ckv[l][None, :], tabs, tm=256)

        o_a = _attention(q_a, k_a, v_a, n_groups=HKV, n_heads=G_A, dk=HEAD_DIM,
                         shared_kv=True, tq=512, name="attn_gqa")
        o_b = _attention(q_b, k_b, v_b, n_groups=HB // 4, n_heads=4, dk=QK_B_PAD,
                         shared_kv=False, tq=256, name="attn_mla")

        x = _outproj(o_a, o_b, x, gt_a, g_out_a[l][None, :], g_out_b[l][None, :],
                     w_out[l].astype(BF16), g_post_attn[l][None, :], tm=256)

        x = _mlp(x, sc_m, sh_m, gt_m, g_pre_mlp[l][None, :], w_mlp_in[l].astype(BF16),
                 w_mlp_out[l].astype(BF16), g_post_mlp[l][None, :], tm=512, tf=1024)
    return x
```

```python
import functools
import math

import numpy as np
import jax
import jax.numpy as jnp
from jax import lax
from jax.experimental import pallas as pl
from jax.experimental.pallas import tpu as pltpu

D_MODEL = 2048
GRID_W = 64
ROPE_THETA = 10000.0
EPS = 1e-6

HEAD_DIM = 128
HA = 8
HKV = 2
G_A = HA // HKV

HB = 8
QK_NOPE = 128
QK_ROPE = 64
V_DIM = 128
KV_RANK = 512
QK_B = QK_NOPE + QK_ROPE
QK_B_PAD = 256

W_QA = HA * HEAD_DIM
W_KA = HKV * HEAD_DIM
W_VA = HKV * HEAD_DIM
W_QB = HB * QK_B
MIX_A = HA * HEAD_DIM
MIX_B = HB * V_DIM
D_FF = 4 * D_MODEL
N_MOD = 6

LOG2_E = math.log2(math.e)
LANES = 128
VMEM_LIMIT = 56 * 1024 * 1024

F32 = jnp.float32
BF16 = jnp.bfloat16


def _rope_tables(seq_len, dim):
    rows = seq_len // GRID_W
    pos = np.arange(seq_len)
    row = (pos // GRID_W).astype(np.float64)
    col = (pos % GRID_W).astype(np.float64)
    half = dim // 2
    inv = ROPE_THETA ** (-np.arange(0, half, 2, dtype=np.float64) / half)
    ang_r = row[:, None] * inv[None, :]
    ang_c = col[:, None] * inv[None, :]
    ang = np.concatenate([ang_r, ang_r, ang_c, ang_c], axis=-1)
    cos, sin = np.cos(ang), np.sin(ang)
    quarter = dim // 4
    sign = np.where((np.arange(dim) % half) < quarter, -1.0, 1.0)
    sin = sin * sign[None, :]
    reps = LANES // dim
    cos = np.tile(cos, (1, reps))
    sin = np.tile(sin, (1, reps))
    del rows
    return jnp.asarray(cos, F32), jnp.asarray(sin, F32)


def _rms(x, g):
    ms = jnp.mean(x * x, axis=-1, keepdims=True)
    return x * lax.rsqrt(ms + EPS) * g


def _rope(x, cos, sin_signed, quarter):
    lane = lax.broadcasted_iota(jnp.int32, x.shape, 1)
    take_up = (lane % (2 * quarter)) < quarter
    up = pltpu.roll(x, LANES - quarter, axis=1)
    down = pltpu.roll(x, quarter, axis=1)
    rot = jnp.where(take_up, up, down)
    return x * cos + rot * sin_signed


def _ada_kernel(c_ref, w_ref, b_ref, o_ref):
    c = c_ref[...]
    c_act = (c * jax.nn.sigmoid(c)).astype(BF16)
    w = w_ref[...].astype(BF16)
    o_ref[...] = jnp.dot(c_act, w, preferred_element_type=F32) + b_ref[...]


def _ada(c_pad, w_ada, b_ada):
    m, d = c_pad.shape
    n = w_ada.shape[1]
    tn = 1024
    return pl.pallas_call(
        _ada_kernel,
        grid=(n // tn,),
        in_specs=[
            pl.BlockSpec((m, d), lambda j: (0, 0)),
            pl.BlockSpec((d, tn), lambda j: (0, j)),
            pl.BlockSpec((1, tn), lambda j: (0, j)),
        ],
        out_specs=pl.BlockSpec((m, tn), lambda j: (0, j)),
        out_shape=jax.ShapeDtypeStruct((m, n), F32),
        compiler_params=pltpu.CompilerParams(
            dimension_semantics=("arbitrary",), vmem_limit_bytes=VMEM_LIMIT),
        name="ada_mod",
    )(c_pad, w_ada, b_ada)


def _inproj_kernel(x_ref, sc_ref, sh_ref, gpre_ref, w_ref, wkv_ref, gq_ref, gk_ref,
                   gckv_ref, cosa_ref, sina_ref, cosb_ref, sinb_ref,
                   qa_ref, ka_ref, va_ref, qb_ref, kb_ref, vb_ref):
    x = x_ref[0]
    h = _rms(x, gpre_ref[...]) * (1.0 + sc_ref[0]) + sh_ref[0]
    hb = h.astype(BF16)
    cosa, sina = cosa_ref[...], sina_ref[...]
    cosb, sinb = cosb_ref[...], sinb_ref[...]
    scale_a = LOG2_E / math.sqrt(HEAD_DIM)
    scale_b = LOG2_E / math.sqrt(QK_B)

    qa = jnp.dot(hb, w_ref[:, 0:W_QA], preferred_element_type=F32)
    gq = gq_ref[...] * scale_a
    for hd in range(HA):
        sl = slice(hd * HEAD_DIM, (hd + 1) * HEAD_DIM)
        q = _rope(_rms(qa[:, sl], gq), cosa, sina, HEAD_DIM // 4)
        qa_ref[0, :, sl] = q.astype(BF16)

    kva = jnp.dot(hb, w_ref[:, W_QA:W_QA + W_KA + W_VA], preferred_element_type=F32)
    for hd in range(HKV):
        sl = slice(hd * HEAD_DIM, (hd + 1) * HEAD_DIM)
        k = _rope(_rms(kva[:, sl], gk_ref[...]), cosa, sina, HEAD_DIM // 4)
        ka_ref[0, :, sl] = k.astype(BF16)
    va_ref[0] = kva[:, W_KA:].astype(BF16)

    o_qb = W_QA + W_KA + W_VA
    qb = jnp.dot(hb, w_ref[:, o_qb:o_qb + W_QB], preferred_element_type=F32)
    lane = lax.broadcasted_iota(jnp.int32, (x.shape[0], LANES), 1)
    low = lane < QK_ROPE
    swap = lambda t: pltpu.roll(t, QK_ROPE, axis=1)
    for pair in range(HB // 2):
        t0, t1, t2 = (qb[:, (3 * pair + i) * LANES:(3 * pair + i + 1) * LANES] for i in range(3))
        r1 = _rope(t1, cosb, sinb, QK_ROPE // 4)
        r2 = _rope(t2, cosb, sinb, QK_ROPE // 4)
        heads = (
            (2 * pair, t0, jnp.where(low, r1, 0.0)),
            (2 * pair + 1, jnp.where(low, swap(t1), swap(t2)), jnp.where(low, swap(r2), 0.0)),
        )
        for hd, nope, pe in heads:
            base = hd * QK_B_PAD
            qb_ref[0, :, base:base + QK_NOPE] = (nope * scale_b).astype(BF16)
            qb_ref[0, :, base + QK_NOPE:base + QK_B_PAD] = (pe * scale_b).astype(BF16)

    o_ckv = o_qb + W_QB
    rest = jnp.dot(hb, w_ref[:, o_ckv:o_ckv + KV_RANK + LANES], preferred_element_type=F32)
    ckv = _rms(rest[:, :KV_RANK], gckv_ref[...]).astype(BF16)
    kv = jnp.dot(ckv, wkv_ref[...], preferred_element_type=F32)
    kpe = _rope(rest[:, KV_RANK:], cosb, sinb, QK_ROPE // 4).astype(BF16)
    for hd in range(HB):
        base = hd * (QK_NOPE + V_DIM)
        kb_ref[0, :, hd * QK_B_PAD:hd * QK_B_PAD + QK_NOPE] = (
            kv[:, base:base + QK_NOPE].astype(BF16))
        kb_ref[0, :, hd * QK_B_PAD + QK_NOPE:(hd + 1) * QK_B_PAD] = kpe
        vb_ref[0, :, hd * V_DIM:(hd + 1) * V_DIM] = (
            kv[:, base + QK_NOPE:base + QK_NOPE + V_DIM].astype(BF16))


def _inproj(x, sc, sh, g_pre, w_in_p, w_kv, g_q, g_k, g_ckv, tabs, tm):
    b, s, d = x.shape
    n_in = w_in_p.shape[1]
    cosa, sina, cosb, sinb = tabs
    row = lambda bi, i: (bi, i, 0)
    per_b = lambda bi, i: (bi, 0, 0)
    const = lambda bi, i: (0, 0)
    tab = lambda bi, i: (i, 0)
    vec = lambda n: pl.BlockSpec((1, n), const)
    out_widths = (MIX_A, W_KA, W_VA, HB * QK_B_PAD, HB * QK_B_PAD, MIX_B)
    return pl.pallas_call(
        _inproj_kernel,
        grid=(b, s // tm),
        in_specs=[
            pl.BlockSpec((1, tm, d), row),
            pl.BlockSpec((1, 1, d), per_b),
            pl.BlockSpec((1, 1, d), per_b),
            vec(d),
            pl.BlockSpec((d, n_in), const, pipeline_mode=pl.Buffered(1)),
            pl.BlockSpec((KV_RANK, HB * (QK_NOPE + V_DIM)), const,
                         pipeline_mode=pl.Buffered(1)),
            vec(HEAD_DIM), vec(HEAD_DIM), vec(KV_RANK),
            pl.BlockSpec((tm, LANES), tab), pl.BlockSpec((tm, LANES), tab),
            pl.BlockSpec((tm, LANES), tab), pl.BlockSpec((tm, LANES), tab),
        ],
        out_specs=[pl.BlockSpec((1, tm, w), row) for w in out_widths],
        out_shape=[jax.ShapeDtypeStruct((b, s, w), BF16) for w in out_widths],
        compiler_params=pltpu.CompilerParams(
            dimension_semantics=("arbitrary", "arbitrary"),
            vmem_limit_bytes=VMEM_LIMIT),
        name="in_proj",
    )(x, sc, sh, g_pre, w_in_p, w_kv, g_q, g_k, g_ckv, cosa, sina, cosb, sinb)


def _attn_kernel(q0_ref, q1_ref, q2_ref, k_ref, kn_ref, v_ref, o_ref, s_scr, *,
                 n_heads, dk, shared_kv, tq):
    def scores(q_ref, keys_ref, slot):
        for hd in range(n_heads):
            kv_hd = 0 if shared_kv else hd
            k = keys_ref[0, :, kv_hd * dk:(kv_hd + 1) * dk]
            q = q_ref[0, :, hd * dk:(hd + 1) * dk]
            s_scr[slot, hd] = lax.dot_general(
                q, k, (((1,), (1,)), ((), ())), preferred_element_type=F32)

    def softmax_pv(slot, row0):
        for hd in range(n_heads):
            kv_hd = 0 if shared_kv else hd
            v = v_ref[0, :, kv_hd * V_DIM:(kv_hd + 1) * V_DIM]
            s = s_scr[slot, hd]
            m = jnp.max(s, axis=-1, keepdims=True)
            p = jnp.exp2(s - m).astype(BF16)
            v_ext = jnp.concatenate([v, jnp.ones_like(v)], axis=1)
            o_ext = jnp.dot(p, v_ext, preferred_element_type=F32)
            o = o_ext[:, :V_DIM] / o_ext[:, V_DIM:]
            o_ref[0, row0:row0 + tq, hd * V_DIM:(hd + 1) * V_DIM] = o.astype(BF16)

    @pl.when(pl.program_id(0) == 0)
    def _():
        scores(q0_ref, k_ref, 0)

    scores(q1_ref, k_ref, 1)
    softmax_pv(0, 0)
    scores(q2_ref, kn_ref, 0)
    softmax_pv(1, tq)


def _attention(q, k, v, *, n_groups, n_heads, dk, shared_kv, tq, name):
    b, s, _ = q.shape
    n_kv = 1 if shared_kv else n_heads
    n_blk = s // tq
    assert n_blk % 2 == 0
    n_total = b * n_groups * n_blk

    def q_map(t):
        return (t // (n_groups * n_blk), t % n_blk, (t // n_blk) % n_groups)

    def kv_map(t):
        return (t // (n_groups * n_blk), 0, (t // n_blk) % n_groups)

    def out_map(j):
        bi, i, g = q_map(2 * j)
        return (bi, i // 2, g)

    nxt = lambda j: jnp.minimum(2 * j + 2, n_total - 1)
    q_spec = lambda f: pl.BlockSpec((1, tq, n_heads * dk), f)
    return pl.pallas_call(
        functools.partial(_attn_kernel, n_heads=n_heads, dk=dk, shared_kv=shared_kv, tq=tq),
        grid=(n_total // 2,),
        in_specs=[
            q_spec(lambda j: (0, 0, 0)),
            q_spec(lambda j: q_map(2 * j + 1)),
            q_spec(lambda j: q_map(nxt(j))),
            pl.BlockSpec((1, s, n_kv * dk), lambda j: kv_map(2 * j)),
            pl.BlockSpec((1, s, n_kv * dk), lambda j: kv_map(nxt(j))),
            pl.BlockSpec((1, s, n_kv * V_DIM), lambda j: kv_map(2 * j)),
        ],
        out_specs=pl.BlockSpec((1, 2 * tq, n_heads * V_DIM), out_map),
        out_shape=jax.ShapeDtypeStruct((b, s, n_groups * n_heads * V_DIM), BF16),
        scratch_shapes=[pltpu.VMEM((2, n_heads, tq, s), F32)],
        compiler_params=pltpu.CompilerParams(
            dimension_semantics=("arbitrary",), vmem_limit_bytes=VMEM_LIMIT),
        name=name,
    )(q, q, q, k, k, v)


def _outproj_kernel(oa_ref, ob_ref, x_ref, gt_ref, ga_ref, gb_ref, w_ref, gpost_ref, o_ref):
    na = _rms(oa_ref[0].astype(F32), ga_ref[...]).astype(BF16)
    nb = _rms(ob_ref[0].astype(F32), gb_ref[...]).astype(BF16)
    o = jnp.dot(na, w_ref[0:MIX_A, :], preferred_element_type=F32)
    o = o + jnp.dot(nb, w_ref[MIX_A:, :], preferred_element_type=F32)
    o_ref[0] = x_ref[0] + gt_ref[0] * _rms(o, gpost_ref[...])


def _outproj(o_a, o_b, x, gt, g_a, g_b, w_out, g_post, tm):
    b, s, d = x.shape
    row = lambda bi, i: (bi, i, 0)
    per_b = lambda bi, i: (bi, 0, 0)
    const = lambda bi, i: (0, 0)
    return pl.pallas_call(
        _outproj_kernel,
        grid=(b, s // tm),
        in_specs=[
            pl.BlockSpec((1, tm, MIX_A), row),
            pl.BlockSpec((1, tm, MIX_B), row),
            pl.BlockSpec((1, tm, d), row),
            pl.BlockSpec((1, 1, d), per_b),
            pl.BlockSpec((1, MIX_A), const),
            pl.BlockSpec((1, MIX_B), const),
            pl.BlockSpec((MIX_A + MIX_B, d), const, pipeline_mode=pl.Buffered(1)),
            pl.BlockSpec((1, d), const),
        ],
        out_specs=pl.BlockSpec((1, tm, d), row),
        out_shape=jax.ShapeDtypeStruct((b, s, d), F32),
        compiler_params=pltpu.CompilerParams(
            dimension_semantics=("arbitrary", "arbitrary"),
            vmem_limit_bytes=VMEM_LIMIT),
        name="out_proj",
    )(o_a, o_b, x, gt, g_a, g_b, w_out, g_post)


def _mlp_kernel(x_ref, sc_ref, sh_ref, gt_ref, gpre_ref, w1_ref, w2_ref, gpost_ref,
                o_ref, h_scr, acc_scr):
    f = pl.program_id(2)

    @pl.when(f == 0)
    def _():
        h = _rms(x_ref[0], gpre_ref[...]) * (1.0 + sc_ref[0]) + sh_ref[0]
        h_scr[...] = h.astype(BF16)
        acc_scr[...] = jnp.zeros_like(acc_scr)

    u = jnp.dot(h_scr[...], w1_ref[...], preferred_element_type=F32)
    u = jnp.maximum(u, 0.0)
    acc_scr[...] += jnp.dot((u * u).astype(BF16), w2_ref[...], preferred_element_type=F32)

    @pl.when(f == pl.num_programs(2) - 1)
    def _():
        o_ref[0] = x_ref[0] + gt_ref[0] * _rms(acc_scr[...], gpost_ref[...])


def _mlp(x, sc, sh, gt, g_pre, w1, w2, g_post, tm, tf):
    b, s, d = x.shape
    dff = w1.shape[1]
    row = lambda bi, i, f: (bi, i, 0)
    per_b = lambda bi, i, f: (bi, 0, 0)
    const = lambda bi, i, f: (0, 0)
    return pl.pallas_call(
        _mlp_kernel,
        grid=(b, s // tm, dff // tf),
        in_specs=[
            pl.BlockSpec((1, tm, d), row),
            pl.BlockSpec((1, 1, d), per_b),
            pl.BlockSpec((1, 1, d), per_b),
            pl.BlockSpec((1, 1, d), per_b),
            pl.BlockSpec((1, d), const),
            pl.BlockSpec((d, tf), lambda bi, i, f: (0, f)),
            pl.BlockSpec((tf, d), lambda bi, i, f: (f, 0)),
            pl.BlockSpec((1, d), const),
        ],
        out_specs=pl.BlockSpec((1, tm, d), row),
        out_shape=jax.ShapeDtypeStruct((b, s, d), F32),
        scratch_shapes=[pltpu.VMEM((tm, d), BF16), pltpu.VMEM((tm, d), F32)],
        compiler_params=pltpu.CompilerParams(
            dimension_semantics=("arbitrary", "arbitrary", "arbitrary"),
            vmem_limit_bytes=VMEM_LIMIT),
        name="mlp",
    )(x, sc, sh, gt, g_pre, w1, w2, g_post)


def _prep_w_in(w):
    return jnp.pad(w.astype(BF16), ((0, 0), (0, LANES - QK_ROPE)))


def kernel(x, c, w_ada, b_ada, g_pre_attn, w_in, g_q_a, g_k_a, g_ckv, w_kv_b, g_out_a,
           g_out_b, w_out, g_post_attn, g_pre_mlp, w_mlp_in, w_mlp_out, g_post_mlp):
    b, s, d = x.shape
    depth = w_ada.shape[0]
    tabs = _rope_tables(s, HEAD_DIM) + _rope_tables(s, QK_ROPE)
    c_pad = jnp.pad(c, ((0, 8 - b), (0, 0)))

    for l in range(depth):
        mod = _ada(c_pad, w_ada[l], b_ada[l][None, :])[:b]
        sh_a, sc_a, gt_a, sh_m, sc_m, gt_m = [
            m.reshape(b, 1, d) for m in jnp.split(mod, N_MOD, axis=-1)]

        q_a, k_a, v_a, q_b, k_b, v_b = _inproj(
            x, sc_a, sh_a, g_pre_attn[l][None, :], _prep_w_in(w_in[l]),
            w_kv_b[l].astype(BF16), g_q_a[l][None, :], g_k_a[l][None, :],
            g_ckv[l][None, :], tabs, tm=256)

        o_a = _attention(q_a, k_a, v_a, n_groups=HKV, n_heads=G_A, dk=HEAD_DIM,
                         shared_kv=True, tq=256, name="attn_gqa")
        o_b = _attention(q_b, k_b, v_b, n_groups=HB // 4, n_heads=4, dk=QK_B_PAD,
                         shared_kv=False, tq=256, name="attn_mla")

        x = _outproj(o_a, o_b, x, gt_a, g_out_a[l][None, :], g_out_b[l][None, :],
                     w_out[l].astype(BF16), g_post_attn[l][None, :], tm=256)

        x = _mlp(x, sc_m, sh_m, gt_m, g_pre_mlp[l][None, :], w_mlp_in[l].astype(BF16),
                 w_mlp_out[l].astype(BF16), g_post_mlp[l][None, :], tm=512, tf=1024)
    return x
```

```python
import functools
import math

import numpy as np
import jax
import jax.numpy as jnp
from jax import lax
from jax.experimental import pallas as pl
from jax.experimental.pallas import tpu as pltpu

D_MODEL = 2048
GRID_W = 64
ROPE_THETA = 10000.0
EPS = 1e-6

HEAD_DIM = 128
HA = 8
HKV = 2
G_A = HA // HKV

HB = 8
QK_NOPE = 128
QK_ROPE = 64
V_DIM = 128
KV_RANK = 512
QK_B = QK_NOPE + QK_ROPE
QK_B_PAD = 256

W_QA = HA * HEAD_DIM
W_KA = HKV * HEAD_DIM
W_VA = HKV * HEAD_DIM
W_QB = HB * QK_B
MIX_A = HA * HEAD_DIM
MIX_B = HB * V_DIM
D_FF = 4 * D_MODEL
N_MOD = 6

LOG2_E = math.log2(math.e)
LANES = 128
VMEM_LIMIT = 56 * 1024 * 1024

F32 = jnp.float32
BF16 = jnp.bfloat16


def _rope_tables(seq_len, dim):
    rows = seq_len // GRID_W
    pos = np.arange(seq_len)
    row = (pos // GRID_W).astype(np.float64)
    col = (pos % GRID_W).astype(np.float64)
    half = dim // 2
    inv = ROPE_THETA ** (-np.arange(0, half, 2, dtype=np.float64) / half)
    ang_r = row[:, None] * inv[None, :]
    ang_c = col[:, None] * inv[None, :]
    ang = np.concatenate([ang_r, ang_r, ang_c, ang_c], axis=-1)
    cos, sin = np.cos(ang), np.sin(ang)
    quarter = dim // 4
    sign = np.where((np.arange(dim) % half) < quarter, -1.0, 1.0)
    sin = sin * sign[None, :]
    reps = LANES // dim
    cos = np.tile(cos, (1, reps))
    sin = np.tile(sin, (1, reps))
    del rows
    return jnp.asarray(cos, F32), jnp.asarray(sin, F32)


def _rms(x, g):
    ms = jnp.mean(x * x, axis=-1, keepdims=True)
    return x * lax.rsqrt(ms + EPS) * g


def _rope(x, cos, sin_signed, quarter):
    lane = lax.broadcasted_iota(jnp.int32, x.shape, 1)
    take_up = (lane % (2 * quarter)) < quarter
    up = pltpu.roll(x, LANES - quarter, axis=1)
    down = pltpu.roll(x, quarter, axis=1)
    rot = jnp.where(take_up, up, down)
    return x * cos + rot * sin_signed


def _ada_kernel(c_ref, w_ref, b_ref, o_ref):
    c = c_ref[...]
    c_act = (c * jax.nn.sigmoid(c)).astype(BF16)
    w = w_ref[...].astype(BF16)
    o_ref[...] = jnp.dot(c_act, w, preferred_element_type=F32) + b_ref[...]


def _ada(c_pad, w_ada, b_ada):
    m, d = c_pad.shape
    n = w_ada.shape[1]
    tn = 1024
    return pl.pallas_call(
        _ada_kernel,
        grid=(n // tn,),
        in_specs=[
            pl.BlockSpec((m, d), lambda j: (0, 0)),
            pl.BlockSpec((d, tn), lambda j: (0, j)),
            pl.BlockSpec((1, tn), lambda j: (0, j)),
        ],
        out_specs=pl.BlockSpec((m, tn), lambda j: (0, j)),
        out_shape=jax.ShapeDtypeStruct((m, n), F32),
        compiler_params=pltpu.CompilerParams(
            dimension_semantics=("arbitrary",), vmem_limit_bytes=VMEM_LIMIT),
        name="ada_mod",
    )(c_pad, w_ada, b_ada)


def _inproj_kernel(x_ref, sc_ref, sh_ref, gpre_ref, w_ref, wkv_ref, gq_ref, gk_ref,
                   gckv_ref, cosa_ref, sina_ref, cosb_ref, sinb_ref,
                   qa_ref, ka_ref, va_ref, qb_ref, kb_ref, vb_ref):
    x = x_ref[0]
    h = _rms(x, gpre_ref[...]) * (1.0 + sc_ref[0]) + sh_ref[0]
    hb = h.astype(BF16)
    cosa, sina = cosa_ref[...], sina_ref[...]
    cosb, sinb = cosb_ref[...], sinb_ref[...]
    scale_a = LOG2_E / math.sqrt(HEAD_DIM)
    scale_b = LOG2_E / math.sqrt(QK_B)

    qa = jnp.dot(hb, w_ref[:, 0:W_QA], preferred_element_type=F32)
    gq = gq_ref[...] * scale_a
    for hd in range(HA):
        sl = slice(hd * HEAD_DIM, (hd + 1) * HEAD_DIM)
        q = _rope(_rms(qa[:, sl], gq), cosa, sina, HEAD_DIM // 4)
        qa_ref[0, :, sl] = q.astype(BF16)

    kva = jnp.dot(hb, w_ref[:, W_QA:W_QA + W_KA + W_VA], preferred_element_type=F32)
    for hd in range(HKV):
        sl = slice(hd * HEAD_DIM, (hd + 1) * HEAD_DIM)
        k = _rope(_rms(kva[:, sl], gk_ref[...]), cosa, sina, HEAD_DIM // 4)
        ka_ref[0, :, sl] = k.astype(BF16)
    va_ref[0] = kva[:, W_KA:].astype(BF16)

    o_qb = W_QA + W_KA + W_VA
    qb = jnp.dot(hb, w_ref[:, o_qb:o_qb + W_QB], preferred_element_type=F32)
    lane = lax.broadcasted_iota(jnp.int32, (x.shape[0], LANES), 1)
    low = lane < QK_ROPE
    swap = lambda t: pltpu.roll(t, QK_ROPE, axis=1)
    for pair in range(HB // 2):
        t0, t1, t2 = (qb[:, (3 * pair + i) * LANES:(3 * pair + i + 1) * LANES] for i in range(3))
        r1 = _rope(t1, cosb, sinb, QK_ROPE // 4)
        r2 = _rope(t2, cosb, sinb, QK_ROPE // 4)
        heads = (
            (2 * pair, t0, jnp.where(low, r1, 0.0)),
            (2 * pair + 1, jnp.where(low, swap(t1), swap(t2)), jnp.where(low, swap(r2), 0.0)),
        )
        for hd, nope, pe in heads:
            base = hd * QK_B_PAD
            qb_ref[0, :, base:base + QK_NOPE] = (nope * scale_b).astype(BF16)
            qb_ref[0, :, base + QK_NOPE:base + QK_B_PAD] = (pe * scale_b).astype(BF16)

    o_ckv = o_qb + W_QB
    rest = jnp.dot(hb, w_ref[:, o_ckv:o_ckv + KV_RANK + LANES], preferred_element_type=F32)
    ckv = _rms(rest[:, :KV_RANK], gckv_ref[...]).astype(BF16)
    kv = jnp.dot(ckv, wkv_ref[...], preferred_element_type=F32)
    kpe = _rope(rest[:, KV_RANK:], cosb, sinb, QK_ROPE // 4).astype(BF16)
    for hd in range(HB):
        base = hd * (QK_NOPE + V_DIM)
        kb_ref[0, :, hd * QK_B_PAD:hd * QK_B_PAD + QK_NOPE] = (
            kv[:, base:base + QK_NOPE].astype(BF16))
        kb_ref[0, :, hd * QK_B_PAD + QK_NOPE:(hd + 1) * QK_B_PAD] = kpe
        vb_ref[0, :, hd * V_DIM:(hd + 1) * V_DIM] = (
            kv[:, base + QK_NOPE:base + QK_NOPE + V_DIM].astype(BF16))


def _inproj(x, sc, sh, g_pre, w_in_p, w_kv, g_q, g_k, g_ckv, tabs, tm):
    b, s, d = x.shape
    n_in = w_in_p.shape[1]
    cosa, sina, cosb, sinb = tabs
    row = lambda bi, i: (bi, i, 0)
    per_b = lambda bi, i: (bi, 0, 0)
    const = lambda bi, i: (0, 0)
    tab = lambda bi, i: (i, 0)
    vec = lambda n: pl.BlockSpec((1, n), const)
    out_widths = (MIX_A, W_KA, W_VA, HB * QK_B_PAD, HB * QK_B_PAD, MIX_B)
    return pl.pallas_call(
        _inproj_kernel,
        grid=(b, s // tm),
        in_specs=[
            pl.BlockSpec((1, tm, d), row),
            pl.BlockSpec((1, 1, d), per_b),
            pl.BlockSpec((1, 1, d), per_b),
            vec(d),
            pl.BlockSpec((d, n_in), const, pipeline_mode=pl.Buffered(1)),
            pl.BlockSpec((KV_RANK, HB * (QK_NOPE + V_DIM)), const,
                         pipeline_mode=pl.Buffered(1)),
            vec(HEAD_DIM), vec(HEAD_DIM), vec(KV_RANK),
            pl.BlockSpec((tm, LANES), tab), pl.BlockSpec((tm, LANES), tab),
            pl.BlockSpec((tm, LANES), tab), pl.BlockSpec((tm, LANES), tab),
        ],
        out_specs=[pl.BlockSpec((1, tm, w), row) for w in out_widths],
        out_shape=[jax.ShapeDtypeStruct((b, s, w), BF16) for w in out_widths],
        compiler_params=pltpu.CompilerParams(
            dimension_semantics=("arbitrary", "arbitrary"),
            vmem_limit_bytes=VMEM_LIMIT),
        name="in_proj",
    )(x, sc, sh, g_pre, w_in_p, w_kv, g_q, g_k, g_ckv, cosa, sina, cosb, sinb)


def _attn_kernel(q0_ref, q1_ref, q2_ref, k_ref, kn_ref, v_ref, *rest,
                 n_heads, dk, shared_kv, tq, n_side):
    side_in = rest[:n_side]
    o_ref = rest[n_side]
    side_out = rest[n_side + 1:2 * n_side + 1]
    s_scr = rest[2 * n_side + 1]
    for w_ref, wb_ref in zip(side_in, side_out):
        wb_ref[...] = w_ref[...].astype(BF16)

    def scores(q_ref, keys_ref, slot):
        for hd in range(n_heads):
            kv_hd = 0 if shared_kv else hd
            k = keys_ref[0, :, kv_hd * dk:(kv_hd + 1) * dk]
            q = q_ref[0, :, hd * dk:(hd + 1) * dk]
            s_scr[slot, hd] = lax.dot_general(
                q, k, (((1,), (1,)), ((), ())), preferred_element_type=F32)

    def softmax_pv(slot, row0):
        for hd in range(n_heads):
            kv_hd = 0 if shared_kv else hd
            v = v_ref[0, :, kv_hd * V_DIM:(kv_hd + 1) * V_DIM]
            s = s_scr[slot, hd]
            m = jnp.max(s, axis=-1, keepdims=True)
            p = jnp.exp2(s - m).astype(BF16)
            v_ext = jnp.concatenate([v, jnp.ones_like(v)], axis=1)
            o_ext = jnp.dot(p, v_ext, preferred_element_type=F32)
            o = o_ext[:, :V_DIM] / o_ext[:, V_DIM:]
            o_ref[0, row0:row0 + tq, hd * V_DIM:(hd + 1) * V_DIM] = o.astype(BF16)

    @pl.when(pl.program_id(0) == 0)
    def _():
        scores(q0_ref, k_ref, 0)

    scores(q1_ref, k_ref, 1)
    softmax_pv(0, 0)
    scores(q2_ref, kn_ref, 0)
    softmax_pv(1, tq)


def _attention(q, k, v, side, *, n_groups, n_heads, dk, shared_kv, tq, name):
    b, s, _ = q.shape
    n_kv = 1 if shared_kv else n_heads
    n_blk = s // tq
    assert n_blk % 2 == 0
    n_total = b * n_groups * n_blk

    def q_map(t):
        return (t // (n_groups * n_blk), t % n_blk, (t // n_blk) % n_groups)

    def kv_map(t):
        return (t // (n_groups * n_blk), 0, (t // n_blk) % n_groups)

    def out_map(j):
        bi, i, g = q_map(2 * j)
        return (bi, i // 2, g)

    nxt = lambda j: jnp.minimum(2 * j + 2, n_total - 1)
    q_spec = lambda f: pl.BlockSpec((1, tq, n_heads * dk), f)
    n_steps = n_total // 2
    side_specs = [pl.BlockSpec((w.shape[0] // n_steps, w.shape[1]), lambda j: (j, 0))
                  for w in side]
    return pl.pallas_call(
        functools.partial(_attn_kernel, n_heads=n_heads, dk=dk, shared_kv=shared_kv,
                          tq=tq, n_side=len(side)),
        grid=(n_steps,),
        in_specs=[
            q_spec(lambda j: (0, 0, 0)),
            q_spec(lambda j: q_map(2 * j + 1)),
            q_spec(lambda j: q_map(nxt(j))),
            pl.BlockSpec((1, s, n_kv * dk), lambda j: kv_map(2 * j)),
            pl.BlockSpec((1, s, n_kv * dk), lambda j: kv_map(nxt(j))),
            pl.BlockSpec((1, s, n_kv * V_DIM), lambda j: kv_map(2 * j)),
        ] + side_specs,
        out_specs=[pl.BlockSpec((1, 2 * tq, n_heads * V_DIM), out_map)] + side_specs,
        out_shape=[jax.ShapeDtypeStruct((b, s, n_groups * n_heads * V_DIM), BF16)]
        + [jax.ShapeDtypeStruct(w.shape, BF16) for w in side],
        scratch_shapes=[pltpu.VMEM((2, n_heads, tq, s), F32)],
        compiler_params=pltpu.CompilerParams(
            dimension_semantics=("arbitrary",), vmem_limit_bytes=VMEM_LIMIT),
        name=name,
    )(q, q, q, k, k, v, *side)


def _outproj_kernel(oa_ref, ob_ref, x_ref, gt_ref, ga_ref, gb_ref, w_ref, gpost_ref, o_ref):
    na = _rms(oa_ref[0].astype(F32), ga_ref[...]).astype(BF16)
    nb = _rms(ob_ref[0].astype(F32), gb_ref[...]).astype(BF16)
    o = jnp.dot(na, w_ref[0:MIX_A, :], preferred_element_type=F32)
    o = o + jnp.dot(nb, w_ref[MIX_A:, :], preferred_element_type=F32)
    o_ref[0] = x_ref[0] + gt_ref[0] * _rms(o, gpost_ref[...])


def _outproj(o_a, o_b, x, gt, g_a, g_b, w_out, g_post, tm):
    b, s, d = x.shape
    row = lambda bi, i: (bi, i, 0)
    per_b = lambda bi, i: (bi, 0, 0)
    const = lambda bi, i: (0, 0)
    return pl.pallas_call(
        _outproj_kernel,
        grid=(b, s // tm),
        in_specs=[
            pl.BlockSpec((1, tm, MIX_A), row),
            pl.BlockSpec((1, tm, MIX_B), row),
            pl.BlockSpec((1, tm, d), row),
            pl.BlockSpec((1, 1, d), per_b),
            pl.BlockSpec((1, MIX_A), const),
            pl.BlockSpec((1, MIX_B), const),
            pl.BlockSpec((MIX_A + MIX_B, d), const, pipeline_mode=pl.Buffered(1)),
            pl.BlockSpec((1, d), const),
        ],
        out_specs=pl.BlockSpec((1, tm, d), row),
        out_shape=jax.ShapeDtypeStruct((b, s, d), F32),
        compiler_params=pltpu.CompilerParams(
            dimension_semantics=("arbitrary", "arbitrary"),
            vmem_limit_bytes=VMEM_LIMIT),
        name="out_proj",
    )(o_a, o_b, x, gt, g_a, g_b, w_out, g_post)


def _mlp_kernel(x_ref, sc_ref, sh_ref, gt_ref, gpre_ref, w1_ref, w2_ref, gpost_ref,
                o_ref, h_scr, acc_scr):
    f = pl.program_id(2)

    @pl.when(f == 0)
    def _():
        h = _rms(x_ref[0], gpre_ref[...]) * (1.0 + sc_ref[0]) + sh_ref[0]
        h_scr[...] = h.astype(BF16)
        acc_scr[...] = jnp.zeros_like(acc_scr)

    u = jnp.dot(h_scr[...], w1_ref[...], preferred_element_type=F32)
    u = jnp.maximum(u, 0.0)
    acc_scr[...] += jnp.dot((u * u).astype(BF16), w2_ref[...], preferred_element_type=F32)

    @pl.when(f == pl.num_programs(2) - 1)
    def _():
        o_ref[0] = x_ref[0] + gt_ref[0] * _rms(acc_scr[...], gpost_ref[...])


def _mlp(x, sc, sh, gt, g_pre, w1, w2, g_post, tm, tf):
    b, s, d = x.shape
    dff = w1.shape[1]
    row = lambda bi, i, f: (bi, i, 0)
    per_b = lambda bi, i, f: (bi, 0, 0)
    const = lambda bi, i, f: (0, 0)
    return pl.pallas_call(
        _mlp_kernel,
        grid=(b, s // tm, dff // tf),
        in_specs=[
            pl.BlockSpec((1, tm, d), row),
            pl.BlockSpec((1, 1, d), per_b),
            pl.BlockSpec((1, 1, d), per_b),
            pl.BlockSpec((1, 1, d), per_b),
            pl.BlockSpec((1, d), const),
            pl.BlockSpec((d, tf), lambda bi, i, f: (0, f)),
            pl.BlockSpec((tf, d), lambda bi, i, f: (f, 0)),
            pl.BlockSpec((1, d), const),
        ],
        out_specs=pl.BlockSpec((1, tm, d), row),
        out_shape=jax.ShapeDtypeStruct((b, s, d), F32),
        scratch_shapes=[pltpu.VMEM((tm, d), BF16), pltpu.VMEM((tm, d), F32)],
        compiler_params=pltpu.CompilerParams(
            dimension_semantics=("arbitrary", "arbitrary", "arbitrary"),
            vmem_limit_bytes=VMEM_LIMIT),
        name="mlp",
    )(x, sc, sh, gt, g_pre, w1, w2, g_post)


def _cast_pad_kernel(w_ref, o_ref):
    n = w_ref.shape[1]
    o_ref[:, :n] = w_ref[...].astype(BF16)
    if o_ref.shape[1] > n:
        o_ref[:, n:] = jnp.zeros((o_ref.shape[0], o_ref.shape[1] - n), BF16)


def _cast_pad(w, tr):
    r, n = w.shape
    n_pad = -(-n // LANES) * LANES
    return pl.pallas_call(
        _cast_pad_kernel,
        grid=(r // tr,),
        in_specs=[pl.BlockSpec((tr, n), lambda i: (i, 0))],
        out_specs=pl.BlockSpec((tr, n_pad), lambda i: (i, 0)),
        out_shape=jax.ShapeDtypeStruct((r, n_pad), BF16),
        compiler_params=pltpu.CompilerParams(
            dimension_semantics=("arbitrary",), vmem_limit_bytes=VMEM_LIMIT),
        name="cast_pad",
    )(w)


def kernel(x, c, w_ada, b_ada, g_pre_attn, w_in, g_q_a, g_k_a, g_ckv, w_kv_b, g_out_a,
           g_out_b, w_out, g_post_attn, g_pre_mlp, w_mlp_in, w_mlp_out, g_post_mlp):
    b, s, d = x.shape
    depth = w_ada.shape[0]
    tabs = _rope_tables(s, HEAD_DIM) + _rope_tables(s, QK_ROPE)
    c_pad = jnp.pad(c, ((0, 8 - b), (0, 0)))

    for l in range(depth):
        mod = _ada(c_pad, w_ada[l], b_ada[l][None, :])[:b]
        sh_a, sc_a, gt_a, sh_m, sc_m, gt_m = [
            m.reshape(b, 1, d) for m in jnp.split(mod, N_MOD, axis=-1)]

        q_a, k_a, v_a, q_b, k_b, v_b = _inproj(
            x, sc_a, sh_a, g_pre_attn[l][None, :], _cast_pad(w_in[l], tr=256),
            _cast_pad(w_kv_b[l], tr=256), g_q_a[l][None, :], g_k_a[l][None, :],
            g_ckv[l][None, :], tabs, tm=256)

        o_a, w1_bf, wo_bf = _attention(
            q_a, k_a, v_a, [w_mlp_in[l], w_out[l]], n_groups=HKV, n_heads=G_A,
            dk=HEAD_DIM, shared_kv=True, tq=256, name="attn_gqa")
        o_b, w2_bf = _attention(
            q_b, k_b, v_b, [w_mlp_out[l]], n_groups=HB // 4, n_heads=4,
            dk=QK_B_PAD, shared_kv=False, tq=256, name="attn_mla")

        x = _outproj(o_a, o_b, x, gt_a, g_out_a[l][None, :], g_out_b[l][None, :],
                     wo_bf, g_post_attn[l][None, :], tm=256)

        x = _mlp(x, sc_m, sh_m, gt_m, g_pre_mlp[l][None, :], w1_bf, w2_bf,
                 g_post_mlp[l][None, :], tm=512, tf=1024)
    return x
```

```python
import functools
import math

import numpy as np
import jax
import jax.numpy as jnp
from jax import lax
from jax.experimental import pallas as pl
from jax.experimental.pallas import tpu as pltpu

D_MODEL = 2048
GRID_W = 64
ROPE_THETA = 10000.0
EPS = 1e-6

HEAD_DIM = 128
HA = 8
HKV = 2
G_A = HA // HKV

HB = 8
QK_NOPE = 128
QK_ROPE = 64
V_DIM = 128
KV_RANK = 512
QK_B = QK_NOPE + QK_ROPE
QK_B_PAD = 256

W_QA = HA * HEAD_DIM
W_KA = HKV * HEAD_DIM
W_VA = HKV * HEAD_DIM
W_QB = HB * QK_B
MIX_A = HA * HEAD_DIM
MIX_B = HB * V_DIM
D_FF = 4 * D_MODEL
N_MOD = 6

LOG2_E = math.log2(math.e)
LANES = 128
ROW_CHUNK = 16
VMEM_LIMIT = 56 * 1024 * 1024

F32 = jnp.float32
BF16 = jnp.bfloat16


def _rope_tables(seq_len, dim):
    rows = seq_len // GRID_W
    pos = np.arange(seq_len)
    row = (pos // GRID_W).astype(np.float64)
    col = (pos % GRID_W).astype(np.float64)
    half = dim // 2
    inv = ROPE_THETA ** (-np.arange(0, half, 2, dtype=np.float64) / half)
    ang_r = row[:, None] * inv[None, :]
    ang_c = col[:, None] * inv[None, :]
    ang = np.concatenate([ang_r, ang_r, ang_c, ang_c], axis=-1)
    cos, sin = np.cos(ang), np.sin(ang)
    quarter = dim // 4
    sign = np.where((np.arange(dim) % half) < quarter, -1.0, 1.0)
    sin = sin * sign[None, :]
    reps = LANES // dim
    cos = np.tile(cos, (1, reps))
    sin = np.tile(sin, (1, reps))
    del rows
    return jnp.asarray(cos, F32), jnp.asarray(sin, F32)


def _unit_rms(x):
    ms = jnp.mean(x * x, axis=-1, keepdims=True)
    return x * lax.rsqrt(ms + EPS)


def _rms(x, g):
    return _unit_rms(x) * g


def _row_chunks(n_rows):
    return [slice(r, r + ROW_CHUNK) for r in range(0, n_rows, ROW_CHUNK)]


def _norm_mod_rows(load, store, n_rows, g, sc, sh):
    gain = g * (1.0 + sc)
    for rows in _row_chunks(n_rows):
        store(rows, (_unit_rms(load(rows)) * gain + sh).astype(BF16))


def _gated_residual_rows(load_x, load_y, store, n_rows, g, gt):
    gain = gt * g
    for rows in _row_chunks(n_rows):
        store(rows, load_x(rows) + _unit_rms(load_y(rows)) * gain)


def _dot_nt(a, w_rows):
    return lax.dot_general(a, w_rows, (((1,), (1,)), ((), ())), preferred_element_type=F32)


def _rope(x, cos, sin_signed, quarter):
    lane = lax.broadcasted_iota(jnp.int32, x.shape, 1)
    take_up = (lane % (2 * quarter)) < quarter
    up = pltpu.roll(x, LANES - quarter, axis=1)
    down = pltpu.roll(x, quarter, axis=1)
    rot = jnp.where(take_up, up, down)
    return x * cos + rot * sin_signed


def _ada_kernel(c_ref, w_ref, b_ref, o_ref):
    c = c_ref[...]
    c_act = (c * jax.nn.sigmoid(c)).astype(BF16)
    w = w_ref[...].astype(BF16)
    o_ref[...] = jnp.dot(c_act, w, preferred_element_type=F32) + b_ref[...]


def _ada(c_pad, w_ada, b_ada):
    m, d = c_pad.shape
    n = w_ada.shape[1]
    tn = 1024
    return pl.pallas_call(
        _ada_kernel,
        grid=(n // tn,),
        in_specs=[
            pl.BlockSpec((m, d), lambda j: (0, 0)),
            pl.BlockSpec((d, tn), lambda j: (0, j)),
            pl.BlockSpec((1, tn), lambda j: (0, j)),
        ],
        out_specs=pl.BlockSpec((m, tn), lambda j: (0, j)),
        out_shape=jax.ShapeDtypeStruct((m, n), F32),
        compiler_params=pltpu.CompilerParams(
            dimension_semantics=("arbitrary",), vmem_limit_bytes=VMEM_LIMIT),
        name="ada_mod",
    )(c_pad, w_ada, b_ada)


def _inproj_kernel(x_ref, sc_ref, sh_ref, gpre_ref, w_ref, wkv_ref, gq_ref, gk_ref,
                   gckv_ref, cosa_ref, sina_ref, cosb_ref, sinb_ref,
                   qa_ref, ka_ref, va_ref, qb_ref, kb_ref, vb_ref, h_scr):
    tm = h_scr.shape[0]

    def store_h(rows, val):
        h_scr[rows, :] = val

    _norm_mod_rows(lambda rows: x_ref[0, rows, :], store_h, tm,
                   gpre_ref[...], sc_ref[0], sh_ref[0])
    hb = h_scr[...]
    cosa, sina = cosa_ref[...], sina_ref[...]
    cosb, sinb = cosb_ref[...], sinb_ref[...]
    scale_a = LOG2_E / math.sqrt(HEAD_DIM)
    scale_b = LOG2_E / math.sqrt(QK_B)

    qa = _dot_nt(hb, w_ref[0:W_QA, :])
    gq = gq_ref[...] * scale_a
    for hd in range(HA):
        sl = slice(hd * HEAD_DIM, (hd + 1) * HEAD_DIM)
        q = _rope(_rms(qa[:, sl], gq), cosa, sina, HEAD_DIM // 4)
        qa_ref[0, :, sl] = q.astype(BF16)

    kva = _dot_nt(hb, w_ref[W_QA:W_QA + W_KA + W_VA, :])
    for hd in range(HKV):
        sl = slice(hd * HEAD_DIM, (hd + 1) * HEAD_DIM)
        k = _rope(_rms(kva[:, sl], gk_ref[...]), cosa, sina, HEAD_DIM // 4)
        ka_ref[0, :, sl] = k.astype(BF16)
    va_ref[0] = kva[:, W_KA:].astype(BF16)

    o_qb = W_QA + W_KA + W_VA
    qb = _dot_nt(hb, w_ref[o_qb:o_qb + W_QB, :])
    lane = lax.broadcasted_iota(jnp.int32, (tm, LANES), 1)
    low = lane < QK_ROPE
    swap = lambda t: pltpu.roll(t, QK_ROPE, axis=1)
    for pair in range(HB // 2):
        t0, t1, t2 = (qb[:, (3 * pair + i) * LANES:(3 * pair + i + 1) * LANES] for i in range(3))
        r1 = _rope(t1, cosb, sinb, QK_ROPE // 4)
        r2 = _rope(t2, cosb, sinb, QK_ROPE // 4)
        heads = (
            (2 * pair, t0, jnp.where(low, r1, 0.0)),
            (2 * pair + 1, jnp.where(low, swap(t1), swap(t2)), jnp.where(low, swap(r2), 0.0)),
        )
        for hd, nope, pe in heads:
            base = hd * QK_B_PAD
            qb_ref[0, :, base:base + QK_NOPE] = (nope * scale_b).astype(BF16)
            qb_ref[0, :, base + QK_NOPE:base + QK_B_PAD] = (pe * scale_b).astype(BF16)

    o_ckv = o_qb + W_QB
    w_rest = jnp.concatenate(
        [w_ref[o_ckv:, :], jnp.zeros((LANES - QK_ROPE, w_ref.shape[1]), BF16)], axis=0)
    rest = _dot_nt(hb, w_rest)
    ckv = _rms(rest[:, :KV_RANK], gckv_ref[...]).astype(BF16)
    kv = jnp.dot(ckv, wkv_ref[...], preferred_element_type=F32)
    kpe = _rope(rest[:, KV_RANK:], cosb, sinb, QK_ROPE // 4).astype(BF16)
    for hd in range(HB):
        base = hd * (QK_NOPE + V_DIM)
        kb_ref[0, :, hd * QK_B_PAD:hd * QK_B_PAD + QK_NOPE] = (
            kv[:, base:base + QK_NOPE].astype(BF16))
        kb_ref[0, :, hd * QK_B_PAD + QK_NOPE:(hd + 1) * QK_B_PAD] = kpe
        vb_ref[0, :, hd * V_DIM:(hd + 1) * V_DIM] = (
            kv[:, base + QK_NOPE:base + QK_NOPE + V_DIM].astype(BF16))


def _inproj(x, sc, sh, g_pre, w_in_p, w_kv, g_q, g_k, g_ckv, tabs, tm):
    b, s, d = x.shape
    n_in = w_in_p.shape[0]
    cosa, sina, cosb, sinb = tabs
    row = lambda bi, i: (bi, i, 0)
    per_b = lambda bi, i: (bi, 0, 0)
    const = lambda bi, i: (0, 0)
    tab = lambda bi, i: (i, 0)
    vec = lambda n: pl.BlockSpec((1, n), const)
    out_widths = (MIX_A, W_KA, W_VA, HB * QK_B_PAD, HB * QK_B_PAD, MIX_B)
    return pl.pallas_call(
        _inproj_kernel,
        grid=(b, s // tm),
        in_specs=[
            pl.BlockSpec((1, tm, d), row),
            pl.BlockSpec((1, 1, d), per_b),
            pl.BlockSpec((1, 1, d), per_b),
            vec(d),
            pl.BlockSpec((n_in, d), const, pipeline_mode=pl.Buffered(1)),
            pl.BlockSpec((KV_RANK, HB * (QK_NOPE + V_DIM)), const,
                         pipeline_mode=pl.Buffered(1)),
            vec(HEAD_DIM), vec(HEAD_DIM), vec(KV_RANK),
            pl.BlockSpec((tm, LANES), tab), pl.BlockSpec((tm, LANES), tab),
            pl.BlockSpec((tm, LANES), tab), pl.BlockSpec((tm, LANES), tab),
        ],
        out_specs=[pl.BlockSpec((1, tm, w), row) for w in out_widths],
        out_shape=[jax.ShapeDtypeStruct((b, s, w), BF16) for w in out_widths],
        scratch_shapes=[pltpu.VMEM((tm, d), BF16)],
        compiler_params=pltpu.CompilerParams(
            dimension_semantics=("arbitrary", "arbitrary"),
            vmem_limit_bytes=VMEM_LIMIT),
        name="in_proj",
    )(x, sc, sh, g_pre, w_in_p, w_kv, g_q, g_k, g_ckv, cosa, sina, cosb, sinb)


def _attn_kernel(q0_ref, q1_ref, q2_ref, k_ref, kn_ref, v_ref, *rest,
                 n_heads, dk, shared_kv, tq, n_side):
    side_in = rest[:n_side]
    o_ref = rest[n_side]
    side_out = rest[n_side + 1:2 * n_side + 1]
    s_scr = rest[2 * n_side + 1]
    for w_ref, wb_ref in zip(side_in, side_out):
        wb_ref[...] = w_ref[...].astype(BF16)

    def scores(q_ref, keys_ref, slot):
        for hd in range(n_heads):
            kv_hd = 0 if shared_kv else hd
            k = keys_ref[0, :, kv_hd * dk:(kv_hd + 1) * dk]
            q = q_ref[0, :, hd * dk:(hd + 1) * dk]
            s_scr[slot, hd] = lax.dot_general(
                q, k, (((1,), (1,)), ((), ())), preferred_element_type=F32)

    def softmax_pv(slot, row0):
        for hd in range(n_heads):
            kv_hd = 0 if shared_kv else hd
            v = v_ref[0, :, kv_hd * V_DIM:(kv_hd + 1) * V_DIM]
            s = s_scr[slot, hd]
            m = jnp.max(s, axis=-1, keepdims=True)
            p = jnp.exp2(s - m).astype(BF16)
            v_ext = jnp.concatenate([v, jnp.ones_like(v)], axis=1)
            o_ext = jnp.dot(p, v_ext, preferred_element_type=F32)
            o = o_ext[:, :V_DIM] / o_ext[:, V_DIM:]
            o_ref[0, row0:row0 + tq, hd * V_DIM:(hd + 1) * V_DIM] = o.astype(BF16)

    @pl.when(pl.program_id(0) == 0)
    def _():
        scores(q0_ref, k_ref, 0)

    scores(q1_ref, k_ref, 1)
    softmax_pv(0, 0)
    scores(q2_ref, kn_ref, 0)
    softmax_pv(1, tq)


def _attention(q, k, v, side, *, n_groups, n_heads, dk, shared_kv, tq, name):
    b, s, _ = q.shape
    n_kv = 1 if shared_kv else n_heads
    n_blk = s // tq
    assert n_blk % 2 == 0
    n_total = b * n_groups * n_blk

    def q_map(t):
        return (t // (n_groups * n_blk), t % n_blk, (t // n_blk) % n_groups)

    def kv_map(t):
        return (t // (n_groups * n_blk), 0, (t // n_blk) % n_groups)

    def out_map(j):
        bi, i, g = q_map(2 * j)
        return (bi, i // 2, g)

    nxt = lambda j: jnp.minimum(2 * j + 2, n_total - 1)
    q_spec = lambda f: pl.BlockSpec((1, tq, n_heads * dk), f)
    n_steps = n_total // 2
    side_specs = [pl.BlockSpec((w.shape[0] // n_steps, w.shape[1]), lambda j: (j, 0))
                  for w in side]
    return pl.pallas_call(
        functools.partial(_attn_kernel, n_heads=n_heads, dk=dk, shared_kv=shared_kv,
                          tq=tq, n_side=len(side)),
        grid=(n_steps,),
        in_specs=[
            q_spec(lambda j: (0, 0, 0)),
            q_spec(lambda j: q_map(2 * j + 1)),
            q_spec(lambda j: q_map(nxt(j))),
            pl.BlockSpec((1, s, n_kv * dk), lambda j: kv_map(2 * j)),
            pl.BlockSpec((1, s, n_kv * dk), lambda j: kv_map(nxt(j))),
            pl.BlockSpec((1, s, n_kv * V_DIM), lambda j: kv_map(2 * j)),
        ] + side_specs,
        out_specs=[pl.BlockSpec((1, 2 * tq, n_heads * V_DIM), out_map)] + side_specs,
        out_shape=[jax.ShapeDtypeStruct((b, s, n_groups * n_heads * V_DIM), BF16)]
        + [jax.ShapeDtypeStruct(w.shape, BF16) for w in side],
        scratch_shapes=[pltpu.VMEM((2, n_heads, tq, s), F32)],
        compiler_params=pltpu.CompilerParams(
            dimension_semantics=("arbitrary",), vmem_limit_bytes=VMEM_LIMIT),
        name=name,
    )(q, q, q, k, k, v, *side)


def _outproj_kernel(oa_ref, ob_ref, x_ref, gt_ref, ga_ref, gb_ref, w_ref, gpost_ref, o_ref,
                    n_scr, y_scr):
    n_rows = n_scr.shape[0]
    for rows in _row_chunks(n_rows):
        n_scr[rows, 0:MIX_A] = _rms(oa_ref[0, rows, :].astype(F32), ga_ref[...]).astype(BF16)
        n_scr[rows, MIX_A:] = _rms(ob_ref[0, rows, :].astype(F32), gb_ref[...]).astype(BF16)
    y_scr[...] = jnp.dot(n_scr[...], w_ref[...], preferred_element_type=F32)

    def store_o(rows, val):
        o_ref[0, rows, :] = val

    _gated_residual_rows(lambda rows: x_ref[0, rows, :], lambda rows: y_scr[rows, :],
                         store_o, n_rows, gpost_ref[...], gt_ref[0])


def _outproj(o_a, o_b, x, gt, g_a, g_b, w_out, g_post, tm):
    b, s, d = x.shape
    row = lambda bi, i: (bi, i, 0)
    per_b = lambda bi, i: (bi, 0, 0)
    const = lambda bi, i: (0, 0)
    return pl.pallas_call(
        _outproj_kernel,
        grid=(b, s // tm),
        in_specs=[
            pl.BlockSpec((1, tm, MIX_A), row),
            pl.BlockSpec((1, tm, MIX_B), row),
            pl.BlockSpec((1, tm, d), row),
            pl.BlockSpec((1, 1, d), per_b),
            pl.BlockSpec((1, MIX_A), const),
            pl.BlockSpec((1, MIX_B), const),
            pl.BlockSpec((MIX_A + MIX_B, d), const, pipeline_mode=pl.Buffered(1)),
            pl.BlockSpec((1, d), const),
        ],
        out_specs=pl.BlockSpec((1, tm, d), row),
        out_shape=jax.ShapeDtypeStruct((b, s, d), F32),
        scratch_shapes=[pltpu.VMEM((tm, MIX_A + MIX_B), BF16), pltpu.VMEM((tm, d), F32)],
        compiler_params=pltpu.CompilerParams(
            dimension_semantics=("arbitrary", "arbitrary"),
            vmem_limit_bytes=VMEM_LIMIT),
        name="out_proj",
    )(o_a, o_b, x, gt, g_a, g_b, w_out, g_post)


def _mlp_kernel(x_ref, sc_ref, sh_ref, gt_ref, gpre_ref, w1_ref, w2_ref, gpost_ref,
                o_ref, h_scr, acc_scr):
    f = pl.program_id(2)

    n_rows = h_scr.shape[0]

    def store_h(rows, val):
        h_scr[rows, :] = val

    def store_o(rows, val):
        o_ref[0, rows, :] = val

    @pl.when(f == 0)
    def _():
        _norm_mod_rows(lambda rows: x_ref[0, rows, :], store_h, n_rows,
                       gpre_ref[...], sc_ref[0], sh_ref[0])
        acc_scr[...] = jnp.zeros_like(acc_scr)

    u = jnp.dot(h_scr[...], w1_ref[...], preferred_element_type=F32)
    u = jnp.maximum(u, 0.0)
    acc_scr[...] += jnp.dot((u * u).astype(BF16), w2_ref[...], preferred_element_type=F32)

    @pl.when(f == pl.num_programs(2) - 1)
    def _():
        _gated_residual_rows(lambda rows: x_ref[0, rows, :], lambda rows: acc_scr[rows, :],
                             store_o, n_rows, gpost_ref[...], gt_ref[0])


def _mlp(x, sc, sh, gt, g_pre, w1, w2, g_post, tm, tf):
    b, s, d = x.shape
    dff = w1.shape[1]
    row = lambda bi, i, f: (bi, i, 0)
    per_b = lambda bi, i, f: (bi, 0, 0)
    const = lambda bi, i, f: (0, 0)
    return pl.pallas_call(
        _mlp_kernel,
        grid=(b, s // tm, dff // tf),
        in_specs=[
            pl.BlockSpec((1, tm, d), row),
            pl.BlockSpec((1, 1, d), per_b),
            pl.BlockSpec((1, 1, d), per_b),
            pl.BlockSpec((1, 1, d), per_b),
            pl.BlockSpec((1, d), const),
            pl.BlockSpec((d, tf), lambda bi, i, f: (0, f)),
            pl.BlockSpec((tf, d), lambda bi, i, f: (f, 0)),
            pl.BlockSpec((1, d), const),
        ],
        out_specs=pl.BlockSpec((1, tm, d), row),
        out_shape=jax.ShapeDtypeStruct((b, s, d), F32),
        scratch_shapes=[pltpu.VMEM((tm, d), BF16), pltpu.VMEM((tm, d), F32)],
        compiler_params=pltpu.CompilerParams(
            dimension_semantics=("arbitrary", "arbitrary", "arbitrary"),
            vmem_limit_bytes=VMEM_LIMIT),
        name="mlp",
    )(x, sc, sh, gt, g_pre, w1, w2, g_post)


def _cast_pad_kernel(w_ref, o_ref):
    n = w_ref.shape[1]
    o_ref[:, :n] = w_ref[...].astype(BF16)
    if o_ref.shape[1] > n:
        o_ref[:, n:] = jnp.zeros((o_ref.shape[0], o_ref.shape[1] - n), BF16)


def _cast_pad(w, tr):
    r, n = w.shape
    n_pad = -(-n // LANES) * LANES
    return pl.pallas_call(
        _cast_pad_kernel,
        grid=(r // tr,),
        in_specs=[pl.BlockSpec((tr, n), lambda i: (i, 0))],
        out_specs=pl.BlockSpec((tr, n_pad), lambda i: (i, 0)),
        out_shape=jax.ShapeDtypeStruct((r, n_pad), BF16),
        compiler_params=pltpu.CompilerParams(
            dimension_semantics=("arbitrary",), vmem_limit_bytes=VMEM_LIMIT),
        name="cast_pad",
    )(w)


def kernel(x, c, w_ada, b_ada, g_pre_attn, w_in, g_q_a, g_k_a, g_ckv, w_kv_b, g_out_a,
           g_out_b, w_out, g_post_attn, g_pre_mlp, w_mlp_in, w_mlp_out, g_post_mlp):
    b, s, d = x.shape
    depth = w_ada.shape[0]
    tabs = _rope_tables(s, HEAD_DIM) + _rope_tables(s, QK_ROPE)
    c_pad = jnp.pad(c, ((0, 8 - b), (0, 0)))

    for l in range(depth):
        mod = _ada(c_pad, w_ada[l], b_ada[l][None, :])[:b]
        sh_a, sc_a, gt_a, sh_m, sc_m, gt_m = [
            m.reshape(b, 1, d) for m in jnp.split(mod, N_MOD, axis=-1)]

        q_a, k_a, v_a, q_b, k_b, v_b = _inproj(
            x, sc_a, sh_a, g_pre_attn[l][None, :], _cast_pad(w_in[l].T, tr=192),
            _cast_pad(w_kv_b[l], tr=256), g_q_a[l][None, :], g_k_a[l][None, :],
            g_ckv[l][None, :], tabs, tm=256)

        o_a, w1_bf, wo_bf = _attention(
            q_a, k_a, v_a, [w_mlp_in[l], w_out[l]], n_groups=HKV, n_heads=G_A,
            dk=HEAD_DIM, shared_kv=True, tq=256, name="attn_gqa")
        o_b, w2_bf = _attention(
            q_b, k_b, v_b, [w_mlp_out[l]], n_groups=HB // 4, n_heads=4,
            dk=QK_B_PAD, shared_kv=False, tq=256, name="attn_mla")

        x = _outproj(o_a, o_b, x, gt_a, g_out_a[l][None, :], g_out_b[l][None, :],
                     wo_bf, g_post_attn[l][None, :], tm=256)

        x = _mlp(x, sc_m, sh_m, gt_m, g_pre_mlp[l][None, :], w1_bf, w2_bf,
                 g_post_mlp[l][None, :], tm=512, tf=1024)
    return x
```

```python
import functools
import math

import numpy as np
import jax
import jax.numpy as jnp
from jax import lax
from jax.experimental import pallas as pl
from jax.experimental.pallas import tpu as pltpu

D_MODEL = 2048
GRID_W = 64
ROPE_THETA = 10000.0
EPS = 1e-6

HEAD_DIM = 128
HA = 8
HKV = 2
G_A = HA // HKV

HB = 8
QK_NOPE = 128
QK_ROPE = 64
V_DIM = 128
KV_RANK = 512
QK_B = QK_NOPE + QK_ROPE
QK_B_PAD = 256

W_QA = HA * HEAD_DIM
W_KA = HKV * HEAD_DIM
W_VA = HKV * HEAD_DIM
W_QB = HB * QK_B
MIX_A = HA * HEAD_DIM
MIX_B = HB * V_DIM
D_FF = 4 * D_MODEL
N_MOD = 6

LOG2_E = math.log2(math.e)
LANES = 128
ROW_CHUNK = 16
VMEM_LIMIT = 56 * 1024 * 1024

F32 = jnp.float32
BF16 = jnp.bfloat16


def _rope_tables(seq_len, dim):
    rows = seq_len // GRID_W
    pos = np.arange(seq_len)
    row = (pos // GRID_W).astype(np.float64)
    col = (pos % GRID_W).astype(np.float64)
    half = dim // 2
    inv = ROPE_THETA ** (-np.arange(0, half, 2, dtype=np.float64) / half)
    ang_r = row[:, None] * inv[None, :]
    ang_c = col[:, None] * inv[None, :]
    ang = np.concatenate([ang_r, ang_r, ang_c, ang_c], axis=-1)
    cos, sin = np.cos(ang), np.sin(ang)
    quarter = dim // 4
    sign = np.where((np.arange(dim) % half) < quarter, -1.0, 1.0)
    sin = sin * sign[None, :]
    reps = LANES // dim
    cos = np.tile(cos, (1, reps))
    sin = np.tile(sin, (1, reps))
    del rows
    return jnp.asarray(cos, F32), jnp.asarray(sin, F32)


def _unit_rms(x):
    ms = jnp.mean(x * x, axis=-1, keepdims=True)
    return x * lax.rsqrt(ms + EPS)


def _rms(x, g):
    return _unit_rms(x) * g


def _row_chunks(span):
    span = span if isinstance(span, slice) else slice(0, span)
    return [slice(r, r + ROW_CHUNK) for r in range(span.start, span.stop, ROW_CHUNK)]


def _norm_mod_rows(load, store, span, g, sc, sh):
    gain = g * (1.0 + sc)
    for rows in _row_chunks(span):
        store(rows, (_unit_rms(load(rows)) * gain + sh).astype(BF16))


def _gated_residual_rows(load_x, load_y, store, span, g, gt):
    gain = gt * g
    for rows in _row_chunks(span):
        store(rows, load_x(rows) + _unit_rms(load_y(rows)) * gain)


def _dot_nt(a, w_rows):
    return lax.dot_general(a, w_rows, (((1,), (1,)), ((), ())), preferred_element_type=F32)


def _rope(x, cos, sin_signed, quarter):
    lane = lax.broadcasted_iota(jnp.int32, x.shape, 1)
    take_up = (lane % (2 * quarter)) < quarter
    up = pltpu.roll(x, LANES - quarter, axis=1)
    down = pltpu.roll(x, quarter, axis=1)
    rot = jnp.where(take_up, up, down)
    return x * cos + rot * sin_signed


def _ada_kernel(c_ref, w_ref, b_ref, o_ref):
    c = c_ref[...]
    c_act = (c * jax.nn.sigmoid(c)).astype(BF16)
    w = w_ref[...].astype(BF16)
    o_ref[...] = jnp.dot(c_act, w, preferred_element_type=F32) + b_ref[...]


def _ada(c_pad, w_ada, b_ada):
    m, d = c_pad.shape
    n = w_ada.shape[1]
    tn = 1024
    return pl.pallas_call(
        _ada_kernel,
        grid=(n // tn,),
        in_specs=[
            pl.BlockSpec((m, d), lambda j: (0, 0)),
            pl.BlockSpec((d, tn), lambda j: (0, j)),
            pl.BlockSpec((1, tn), lambda j: (0, j)),
        ],
        out_specs=pl.BlockSpec((m, tn), lambda j: (0, j)),
        out_shape=jax.ShapeDtypeStruct((m, n), F32),
        compiler_params=pltpu.CompilerParams(
            dimension_semantics=("arbitrary",), vmem_limit_bytes=VMEM_LIMIT),
        name="ada_mod",
    )(c_pad, w_ada, b_ada)


def _inproj_kernel(x_ref, sc_ref, sh_ref, gpre_ref, w_ref, wkv_ref, gq_ref, gk_ref,
                   gckv_ref, cosa_ref, sina_ref, cosb_ref, sinb_ref,
                   qa_ref, ka_ref, va_ref, qb_ref, kb_ref, vb_ref, h_scr):
    tm = h_scr.shape[0]

    def store_h(rows, val):
        h_scr[rows, :] = val

    _norm_mod_rows(lambda rows: x_ref[0, rows, :], store_h, tm,
                   gpre_ref[...], sc_ref[0], sh_ref[0])
    hb = h_scr[...]
    cosa, sina = cosa_ref[...], sina_ref[...]
    cosb, sinb = cosb_ref[...], sinb_ref[...]
    scale_a = LOG2_E / math.sqrt(HEAD_DIM)
    scale_b = LOG2_E / math.sqrt(QK_B)

    qa = _dot_nt(hb, w_ref[0:W_QA, :])
    gq = gq_ref[...] * scale_a
    for hd in range(HA):
        sl = slice(hd * HEAD_DIM, (hd + 1) * HEAD_DIM)
        q = _rope(_rms(qa[:, sl], gq), cosa, sina, HEAD_DIM // 4)
        qa_ref[0, :, sl] = q.astype(BF16)

    kva = _dot_nt(hb, w_ref[W_QA:W_QA + W_KA + W_VA, :])
    for hd in range(HKV):
        sl = slice(hd * HEAD_DIM, (hd + 1) * HEAD_DIM)
        k = _rope(_rms(kva[:, sl], gk_ref[...]), cosa, sina, HEAD_DIM // 4)
        ka_ref[0, :, sl] = k.astype(BF16)
    va_ref[0] = kva[:, W_KA:].astype(BF16)

    o_qb = W_QA + W_KA + W_VA
    qb = _dot_nt(hb, w_ref[o_qb:o_qb + W_QB, :])
    lane = lax.broadcasted_iota(jnp.int32, (tm, LANES), 1)
    low = lane < QK_ROPE
    swap = lambda t: pltpu.roll(t, QK_ROPE, axis=1)
    for pair in range(HB // 2):
        t0, t1, t2 = (qb[:, (3 * pair + i) * LANES:(3 * pair + i + 1) * LANES] for i in range(3))
        r1 = _rope(t1, cosb, sinb, QK_ROPE // 4)
        r2 = _rope(t2, cosb, sinb, QK_ROPE // 4)
        heads = (
            (2 * pair, t0, jnp.where(low, r1, 0.0)),
            (2 * pair + 1, jnp.where(low, swap(t1), swap(t2)), jnp.where(low, swap(r2), 0.0)),
        )
        for hd, nope, pe in heads:
            base = hd * QK_B_PAD
            qb_ref[0, :, base:base + QK_NOPE] = (nope * scale_b).astype(BF16)
            qb_ref[0, :, base + QK_NOPE:base + QK_B_PAD] = (pe * scale_b).astype(BF16)

    o_ckv = o_qb + W_QB
    w_rest = jnp.concatenate(
        [w_ref[o_ckv:, :], jnp.zeros((LANES - QK_ROPE, w_ref.shape[1]), BF16)], axis=0)
    rest = _dot_nt(hb, w_rest)
    ckv = _rms(rest[:, :KV_RANK], gckv_ref[...]).astype(BF16)
    kv = jnp.dot(ckv, wkv_ref[...], preferred_element_type=F32)
    kpe = _rope(rest[:, KV_RANK:], cosb, sinb, QK_ROPE // 4).astype(BF16)
    for hd in range(HB):
        base = hd * (QK_NOPE + V_DIM)
        kb_ref[0, :, hd * QK_B_PAD:hd * QK_B_PAD + QK_NOPE] = (
            kv[:, base:base + QK_NOPE].astype(BF16))
        kb_ref[0, :, hd * QK_B_PAD + QK_NOPE:(hd + 1) * QK_B_PAD] = kpe
        vb_ref[0, :, hd * V_DIM:(hd + 1) * V_DIM] = (
            kv[:, base + QK_NOPE:base + QK_NOPE + V_DIM].astype(BF16))


def _inproj(x, sc, sh, g_pre, w_in_p, w_kv, g_q, g_k, g_ckv, tabs, tm):
    b, s, d = x.shape
    n_in = w_in_p.shape[0]
    cosa, sina, cosb, sinb = tabs
    row = lambda bi, i: (bi, i, 0)
    per_b = lambda bi, i: (bi, 0, 0)
    const = lambda bi, i: (0, 0)
    tab = lambda bi, i: (i, 0)
    vec = lambda n: pl.BlockSpec((1, n), const)
    out_widths = (MIX_A, W_KA, W_VA, HB * QK_B_PAD, HB * QK_B_PAD, MIX_B)
    return pl.pallas_call(
        _inproj_kernel,
        grid=(b, s // tm),
        in_specs=[
            pl.BlockSpec((1, tm, d), row),
            pl.BlockSpec((1, 1, d), per_b),
            pl.BlockSpec((1, 1, d), per_b),
            vec(d),
            pl.BlockSpec((n_in, d), const, pipeline_mode=pl.Buffered(1)),
            pl.BlockSpec((KV_RANK, HB * (QK_NOPE + V_DIM)), const,
                         pipeline_mode=pl.Buffered(1)),
            vec(HEAD_DIM), vec(HEAD_DIM), vec(KV_RANK),
            pl.BlockSpec((tm, LANES), tab), pl.BlockSpec((tm, LANES), tab),
            pl.BlockSpec((tm, LANES), tab), pl.BlockSpec((tm, LANES), tab),
        ],
        out_specs=[pl.BlockSpec((1, tm, w), row) for w in out_widths],
        out_shape=[jax.ShapeDtypeStruct((b, s, w), BF16) for w in out_widths],
        scratch_shapes=[pltpu.VMEM((tm, d), BF16)],
        compiler_params=pltpu.CompilerParams(
            dimension_semantics=("arbitrary", "arbitrary"),
            vmem_limit_bytes=VMEM_LIMIT),
        name="in_proj",
    )(x, sc, sh, g_pre, w_in_p, w_kv, g_q, g_k, g_ckv, cosa, sina, cosb, sinb)


def _attn_kernel(q0_ref, q1_ref, q2_ref, k_ref, kn_ref, v_ref, *rest,
                 n_heads, dk, shared_kv, tq, n_side):
    side_in = rest[:n_side]
    o_ref = rest[n_side]
    side_out = rest[n_side + 1:2 * n_side + 1]
    s_scr = rest[2 * n_side + 1]
    for w_ref, wb_ref in zip(side_in, side_out):
        wb_ref[...] = w_ref[...].astype(BF16)

    def scores(q_ref, keys_ref, slot):
        for hd in range(n_heads):
            kv_hd = 0 if shared_kv else hd
            k = keys_ref[0, :, kv_hd * dk:(kv_hd + 1) * dk]
            q = q_ref[0, :, hd * dk:(hd + 1) * dk]
            s_scr[slot, hd] = lax.dot_general(
                q, k, (((1,), (1,)), ((), ())), preferred_element_type=F32)

    def softmax_pv(slot, row0):
        for hd in range(n_heads):
            kv_hd = 0 if shared_kv else hd
            v = v_ref[0, :, kv_hd * V_DIM:(kv_hd + 1) * V_DIM]
            s = s_scr[slot, hd]
            m = jnp.max(s, axis=-1, keepdims=True)
            p = jnp.exp2(s - m).astype(BF16)
            v_ext = jnp.concatenate([v, jnp.ones_like(v)], axis=1)
            o_ext = jnp.dot(p, v_ext, preferred_element_type=F32)
            o = o_ext[:, :V_DIM] / o_ext[:, V_DIM:]
            o_ref[0, row0:row0 + tq, hd * V_DIM:(hd + 1) * V_DIM] = o.astype(BF16)

    @pl.when(pl.program_id(0) == 0)
    def _():
        scores(q0_ref, k_ref, 0)

    scores(q1_ref, k_ref, 1)
    softmax_pv(0, 0)
    scores(q2_ref, kn_ref, 0)
    softmax_pv(1, tq)


def _attention(q, k, v, side, *, n_groups, n_heads, dk, shared_kv, tq, name):
    b, s, _ = q.shape
    n_kv = 1 if shared_kv else n_heads
    n_blk = s // tq
    assert n_blk % 2 == 0
    n_total = b * n_groups * n_blk

    def q_map(t):
        return (t // (n_groups * n_blk), t % n_blk, (t // n_blk) % n_groups)

    def kv_map(t):
        return (t // (n_groups * n_blk), 0, (t // n_blk) % n_groups)

    def out_map(j):
        bi, i, g = q_map(2 * j)
        return (bi, i // 2, g)

    nxt = lambda j: jnp.minimum(2 * j + 2, n_total - 1)
    q_spec = lambda f: pl.BlockSpec((1, tq, n_heads * dk), f)
    n_steps = n_total // 2
    side_specs = [pl.BlockSpec((w.shape[0] // n_steps, w.shape[1]), lambda j: (j, 0))
                  for w in side]
    return pl.pallas_call(
        functools.partial(_attn_kernel, n_heads=n_heads, dk=dk, shared_kv=shared_kv,
                          tq=tq, n_side=len(side)),
        grid=(n_steps,),
        in_specs=[
            q_spec(lambda j: (0, 0, 0)),
            q_spec(lambda j: q_map(2 * j + 1)),
            q_spec(lambda j: q_map(nxt(j))),
            pl.BlockSpec((1, s, n_kv * dk), lambda j: kv_map(2 * j)),
            pl.BlockSpec((1, s, n_kv * dk), lambda j: kv_map(nxt(j))),
            pl.BlockSpec((1, s, n_kv * V_DIM), lambda j: kv_map(2 * j)),
        ] + side_specs,
        out_specs=[pl.BlockSpec((1, 2 * tq, n_heads * V_DIM), out_map)] + side_specs,
        out_shape=[jax.ShapeDtypeStruct((b, s, n_groups * n_heads * V_DIM), BF16)]
        + [jax.ShapeDtypeStruct(w.shape, BF16) for w in side],
        scratch_shapes=[pltpu.VMEM((2, n_heads, tq, s), F32)],
        compiler_params=pltpu.CompilerParams(
            dimension_semantics=("arbitrary",), vmem_limit_bytes=VMEM_LIMIT),
        name=name,
    )(q, q, q, k, k, v, *side)


def _outproj_kernel(oa_ref, ob_ref, x_ref, gt_ref, ga_ref, gb_ref, w_ref, gpost_ref, o_ref,
                    n_scr, y_scr):
    n_rows = n_scr.shape[0]

    def store_o(rows, val):
        o_ref[0, rows, :] = val

    for rows in _row_chunks(n_rows):
        n_scr[rows, 0:MIX_A] = _rms(oa_ref[0, rows, :].astype(F32), ga_ref[...]).astype(BF16)
        n_scr[rows, MIX_A:] = _rms(ob_ref[0, rows, :].astype(F32), gb_ref[...]).astype(BF16)
    y_scr[...] = jnp.dot(n_scr[...], w_ref[...], preferred_element_type=F32)
    _gated_residual_rows(lambda rows: x_ref[0, rows, :], lambda rows: y_scr[rows, :],
                         store_o, n_rows, gpost_ref[...], gt_ref[0])


def _outproj(o_a, o_b, x, gt, g_a, g_b, w_out, g_post, tm):
    b, s, d = x.shape
    row = lambda bi, i: (bi, i, 0)
    per_b = lambda bi, i: (bi, 0, 0)
    const = lambda bi, i: (0, 0)
    return pl.pallas_call(
        _outproj_kernel,
        grid=(b, s // tm),
        in_specs=[
            pl.BlockSpec((1, tm, MIX_A), row),
            pl.BlockSpec((1, tm, MIX_B), row),
            pl.BlockSpec((1, tm, d), row),
            pl.BlockSpec((1, 1, d), per_b),
            pl.BlockSpec((1, MIX_A), const),
            pl.BlockSpec((1, MIX_B), const),
            pl.BlockSpec((MIX_A + MIX_B, d), const, pipeline_mode=pl.Buffered(1)),
            pl.BlockSpec((1, d), const),
        ],
        out_specs=pl.BlockSpec((1, tm, d), row),
        out_shape=jax.ShapeDtypeStruct((b, s, d), F32),
        scratch_shapes=[pltpu.VMEM((tm, MIX_A + MIX_B), BF16), pltpu.VMEM((tm, d), F32)],
        compiler_params=pltpu.CompilerParams(
            dimension_semantics=("arbitrary", "arbitrary"),
            vmem_limit_bytes=VMEM_LIMIT),
        name="out_proj",
    )(o_a, o_b, x, gt, g_a, g_b, w_out, g_post)


def _mlp_kernel(x_ref, sc_ref, sh_ref, gt_ref, gpre_ref, w1_ref, w2_ref, gpost_ref,
                o_ref, h_scr, acc_scr, *, n_split):
    f = pl.program_id(2)
    last = pl.num_programs(2) - 1
    n_rows = h_scr.shape[0]
    parts = [slice(r, r + n_rows // n_split) for r in range(0, n_rows, n_rows // n_split)]

    def store_h(rows, val):
        h_scr[rows, :] = val

    def store_o(rows, val):
        o_ref[0, rows, :] = val

    def ffn(rows):
        u = jnp.dot(h_scr[rows, :], w1_ref[...], preferred_element_type=F32)
        u = jnp.maximum(u, 0.0)
        return jnp.dot((u * u).astype(BF16), w2_ref[...], preferred_element_type=F32)

    @pl.when(f == 0)
    def _():
        for part in parts:
            _norm_mod_rows(lambda rows: x_ref[0, rows, :], store_h, part,
                           gpre_ref[...], sc_ref[0], sh_ref[0])
            acc_scr[part, :] = ffn(part)

    @pl.when((f > 0) & (f < last))
    def _():
        acc_scr[...] += ffn(slice(0, n_rows))

    @pl.when(f == last)
    def _():
        for part in parts:
            acc_scr[part, :] += ffn(part)
            _gated_residual_rows(lambda rows: x_ref[0, rows, :], lambda rows: acc_scr[rows, :],
                                 store_o, part, gpost_ref[...], gt_ref[0])


def _mlp(x, sc, sh, gt, g_pre, w1, w2, g_post, tm, tf):
    b, s, d = x.shape
    dff = w1.shape[1]
    row = lambda bi, i, f: (bi, i, 0)
    per_b = lambda bi, i, f: (bi, 0, 0)
    const = lambda bi, i, f: (0, 0)
    return pl.pallas_call(
        functools.partial(_mlp_kernel, n_split=2),
        grid=(b, s // tm, dff // tf),
        in_specs=[
            pl.BlockSpec((1, tm, d), row),
            pl.BlockSpec((1, 1, d), per_b),
            pl.BlockSpec((1, 1, d), per_b),
            pl.BlockSpec((1, 1, d), per_b),
            pl.BlockSpec((1, d), const),
            pl.BlockSpec((d, tf), lambda bi, i, f: (0, f)),
            pl.BlockSpec((tf, d), lambda bi, i, f: (f, 0)),
            pl.BlockSpec((1, d), const),
        ],
        out_specs=pl.BlockSpec((1, tm, d), row),
        out_shape=jax.ShapeDtypeStruct((b, s, d), F32),
        scratch_shapes=[pltpu.VMEM((tm, d), BF16), pltpu.VMEM((tm, d), F32)],
        compiler_params=pltpu.CompilerParams(
            dimension_semantics=("arbitrary", "arbitrary", "arbitrary"),
            vmem_limit_bytes=VMEM_LIMIT),
        name="mlp",
    )(x, sc, sh, gt, g_pre, w1, w2, g_post)


def _cast_pad_kernel(w_ref, o_ref):
    n = w_ref.shape[1]
    o_ref[:, :n] = w_ref[...].astype(BF16)
    if o_ref.shape[1] > n:
        o_ref[:, n:] = jnp.zeros((o_ref.shape[0], o_ref.shape[1] - n), BF16)


def _cast_pad(w, tr):
    r, n = w.shape
    n_pad = -(-n // LANES) * LANES
    return pl.pallas_call(
        _cast_pad_kernel,
        grid=(r // tr,),
        in_specs=[pl.BlockSpec((tr, n), lambda i: (i, 0))],
        out_specs=pl.BlockSpec((tr, n_pad), lambda i: (i, 0)),
        out_shape=jax.ShapeDtypeStruct((r, n_pad), BF16),
        compiler_params=pltpu.CompilerParams(
            dimension_semantics=("arbitrary",), vmem_limit_bytes=VMEM_LIMIT),
        name="cast_pad",
    )(w)


def kernel(x, c, w_ada, b_ada, g_pre_attn, w_in, g_q_a, g_k_a, g_ckv, w_kv_b, g_out_a,
           g_out_b, w_out, g_post_attn, g_pre_mlp, w_mlp_in, w_mlp_out, g_post_mlp):
    b, s, d = x.shape
    depth = w_ada.shape[0]
    tabs = _rope_tables(s, HEAD_DIM) + _rope_tables(s, QK_ROPE)
    c_pad = jnp.pad(c, ((0, 8 - b), (0, 0)))

    for l in range(depth):
        mod = _ada(c_pad, w_ada[l], b_ada[l][None, :])[:b]
        sh_a, sc_a, gt_a, sh_m, sc_m, gt_m = [
            m.reshape(b, 1, d) for m in jnp.split(mod, N_MOD, axis=-1)]

        q_a, k_a, v_a, q_b, k_b, v_b = _inproj(
            x, sc_a, sh_a, g_pre_attn[l][None, :], _cast_pad(w_in[l].T, tr=192),
            _cast_pad(w_kv_b[l], tr=256), g_q_a[l][None, :], g_k_a[l][None, :],
            g_ckv[l][None, :], tabs, tm=512)

        o_a, w1_bf, wo_bf = _attention(
            q_a, k_a, v_a, [w_mlp_in[l], w_out[l]], n_groups=HKV, n_heads=G_A,
            dk=HEAD_DIM, shared_kv=True, tq=256, name="attn_gqa")
        o_b, w2_bf = _attention(
            q_b, k_b, v_b, [w_mlp_out[l]], n_groups=HB // 4, n_heads=4,
            dk=QK_B_PAD, shared_kv=False, tq=256, name="attn_mla")

        x = _outproj(o_a, o_b, x, gt_a, g_out_a[l][None, :], g_out_b[l][None, :],
                     wo_bf, g_post_attn[l][None, :], tm=512)

        x = _mlp(x, sc_m, sh_m, gt_m, g_pre_mlp[l][None, :], w1_bf, w2_bf,
                 g_post_mlp[l][None, :], tm=512, tf=1024)
    return x
```

```python
import functools
import math

import numpy as np
import jax
import jax.numpy as jnp
from jax import lax
from jax.experimental import pallas as pl
from jax.experimental.pallas import tpu as pltpu

D_MODEL = 2048
GRID_W = 64
ROPE_THETA = 10000.0
EPS = 1e-6

HEAD_DIM = 128
HA = 8
HKV = 2
G_A = HA // HKV

HB = 8
QK_NOPE = 128
QK_ROPE = 64
V_DIM = 128
KV_RANK = 512
QK_B = QK_NOPE + QK_ROPE
QK_B_PAD = 256

W_QA = HA * HEAD_DIM
W_KA = HKV * HEAD_DIM
W_VA = HKV * HEAD_DIM
W_QB = HB * QK_B
MIX_A = HA * HEAD_DIM
MIX_B = HB * V_DIM
D_FF = 4 * D_MODEL
N_MOD = 6

LOG2_E = math.log2(math.e)
LANES = 128
ROW_CHUNK = 16
VMEM_LIMIT = 56 * 1024 * 1024

F32 = jnp.float32
BF16 = jnp.bfloat16


def _rope_tables(seq_len, dim):
    rows = seq_len // GRID_W
    pos = np.arange(seq_len)
    row = (pos // GRID_W).astype(np.float64)
    col = (pos % GRID_W).astype(np.float64)
    half = dim // 2
    inv = ROPE_THETA ** (-np.arange(0, half, 2, dtype=np.float64) / half)
    ang_r = row[:, None] * inv[None, :]
    ang_c = col[:, None] * inv[None, :]
    ang = np.concatenate([ang_r, ang_r, ang_c, ang_c], axis=-1)
    cos, sin = np.cos(ang), np.sin(ang)
    quarter = dim // 4
    sign = np.where((np.arange(dim) % half) < quarter, -1.0, 1.0)
    sin = sin * sign[None, :]
    reps = LANES // dim
    cos = np.tile(cos, (1, reps))
    sin = np.tile(sin, (1, reps))
    del rows
    return jnp.asarray(cos, F32), jnp.asarray(sin, F32)


def _unit_rms(x):
    ms = jnp.mean(x * x, axis=-1, keepdims=True)
    return x * lax.rsqrt(ms + EPS)


def _rms(x, g):
    return _unit_rms(x) * g


def _row_chunks(span):
    span = span if isinstance(span, slice) else slice(0, span)
    return [slice(r, r + ROW_CHUNK) for r in range(span.start, span.stop, ROW_CHUNK)]


def _norm_mod_rows(load, store, span, g, sc, sh):
    gain = g * (1.0 + sc)
    for rows in _row_chunks(span):
        store(rows, (_unit_rms(load(rows)) * gain + sh).astype(BF16))


def _gated_residual_rows(load_x, load_y, store, span, g, gt):
    gain = gt * g
    for rows in _row_chunks(span):
        store(rows, load_x(rows) + _unit_rms(load_y(rows)) * gain)


def _dot_nt(a, w_rows):
    return lax.dot_general(a, w_rows, (((1,), (1,)), ((), ())), preferred_element_type=F32)


def _rope(x, cos, sin_signed, quarter):
    lane = lax.broadcasted_iota(jnp.int32, x.shape, 1)
    take_up = (lane % (2 * quarter)) < quarter
    up = pltpu.roll(x, LANES - quarter, axis=1)
    down = pltpu.roll(x, quarter, axis=1)
    rot = jnp.where(take_up, up, down)
    return x * cos + rot * sin_signed


def _ada_kernel(c_ref, w_ref, b_ref, o_ref):
    c = c_ref[...]
    c_act = (c * jax.nn.sigmoid(c)).astype(BF16)
    w = w_ref[...].astype(BF16)
    o_ref[...] = jnp.dot(c_act, w, preferred_element_type=F32) + b_ref[...]


def _ada(c_pad, w_ada, b_ada):
    m, d = c_pad.shape
    n = w_ada.shape[1]
    tn = 1024
    return pl.pallas_call(
        _ada_kernel,
        grid=(n // tn,),
        in_specs=[
            pl.BlockSpec((m, d), lambda j: (0, 0)),
            pl.BlockSpec((d, tn), lambda j: (0, j)),
            pl.BlockSpec((1, tn), lambda j: (0, j)),
        ],
        out_specs=pl.BlockSpec((m, tn), lambda j: (0, j)),
        out_shape=jax.ShapeDtypeStruct((m, n), F32),
        compiler_params=pltpu.CompilerParams(
            dimension_semantics=("arbitrary",), vmem_limit_bytes=VMEM_LIMIT),
        name="ada_mod",
    )(c_pad, w_ada, b_ada)


def _inproj_kernel(x_ref, sc_ref, sh_ref, gpre_ref, w_ref, wkv_ref, gq_ref, gk_ref,
                   gckv_ref, cosa_ref, sina_ref, cosb_ref, sinb_ref,
                   qa_ref, ka_ref, va_ref, qb_ref, kb_ref, vb_ref, h_scr):
    tm = h_scr.shape[0]

    def store_h(rows, val):
        h_scr[rows, :] = val

    _norm_mod_rows(lambda rows: x_ref[0, rows, :], store_h, tm,
                   gpre_ref[...], sc_ref[0], sh_ref[0])
    hb = h_scr[...]
    cosa, sina = cosa_ref[...], sina_ref[...]
    cosb, sinb = cosb_ref[...], sinb_ref[...]
    scale_a = LOG2_E / math.sqrt(HEAD_DIM)
    scale_b = LOG2_E / math.sqrt(QK_B)

    qa = _dot_nt(hb, w_ref[0:W_QA, :])
    gq = gq_ref[...] * scale_a
    for hd in range(HA):
        sl = slice(hd * HEAD_DIM, (hd + 1) * HEAD_DIM)
        q = _rope(_rms(qa[:, sl], gq), cosa, sina, HEAD_DIM // 4)
        qa_ref[0, :, sl] = q.astype(BF16)

    kva = _dot_nt(hb, w_ref[W_QA:W_QA + W_KA + W_VA, :])
    for hd in range(HKV):
        sl = slice(hd * HEAD_DIM, (hd + 1) * HEAD_DIM)
        k = _rope(_rms(kva[:, sl], gk_ref[...]), cosa, sina, HEAD_DIM // 4)
        ka_ref[0, :, sl] = k.astype(BF16)
    va_ref[0] = kva[:, W_KA:].astype(BF16)

    o_qb = W_QA + W_KA + W_VA
    qb = _dot_nt(hb, w_ref[o_qb:o_qb + W_QB, :])
    lane = lax.broadcasted_iota(jnp.int32, (tm, LANES), 1)
    low = lane < QK_ROPE
    swap = lambda t: pltpu.roll(t, QK_ROPE, axis=1)
    for pair in range(HB // 2):
        t0, t1, t2 = (qb[:, (3 * pair + i) * LANES:(3 * pair + i + 1) * LANES] for i in range(3))
        r1 = _rope(t1, cosb, sinb, QK_ROPE // 4)
        r2 = _rope(t2, cosb, sinb, QK_ROPE // 4)
        heads = (
            (2 * pair, t0, jnp.where(low, r1, 0.0)),
            (2 * pair + 1, jnp.where(low, swap(t1), swap(t2)), jnp.where(low, swap(r2), 0.0)),
        )
        for hd, nope, pe in heads:
            base = hd * QK_B_PAD
            qb_ref[0, :, base:base + QK_NOPE] = (nope * scale_b).astype(BF16)
            qb_ref[0, :, base + QK_NOPE:base + QK_B_PAD] = (pe * scale_b).astype(BF16)

    o_ckv = o_qb + W_QB
    w_rest = jnp.concatenate(
        [w_ref[o_ckv:, :], jnp.zeros((LANES - QK_ROPE, w_ref.shape[1]), BF16)], axis=0)
    rest = _dot_nt(hb, w_rest)
    ckv = _rms(rest[:, :KV_RANK], gckv_ref[...]).astype(BF16)
    kv = jnp.dot(ckv, wkv_ref[...], preferred_element_type=F32)
    kpe = _rope(rest[:, KV_RANK:], cosb, sinb, QK_ROPE // 4).astype(BF16)
    for hd in range(HB):
        base = hd * (QK_NOPE + V_DIM)
        kb_ref[0, :, hd * QK_B_PAD:hd * QK_B_PAD + QK_NOPE] = (
            kv[:, base:base + QK_NOPE].astype(BF16))
        kb_ref[0, :, hd * QK_B_PAD + QK_NOPE:(hd + 1) * QK_B_PAD] = kpe
        vb_ref[0, :, hd * V_DIM:(hd + 1) * V_DIM] = (
            kv[:, base + QK_NOPE:base + QK_NOPE + V_DIM].astype(BF16))


def _inproj(x, sc, sh, g_pre, w_in_p, w_kv, g_q, g_k, g_ckv, tabs, tm):
    b, s, d = x.shape
    n_in = w_in_p.shape[0]
    cosa, sina, cosb, sinb = tabs
    row = lambda bi, i: (bi, i, 0)
    per_b = lambda bi, i: (bi, 0, 0)
    const = lambda bi, i: (0, 0)
    tab = lambda bi, i: (i, 0)
    vec = lambda n: pl.BlockSpec((1, n), const)
    out_widths = (MIX_A, W_KA, W_VA, HB * QK_B_PAD, HB * QK_B_PAD, MIX_B)
    return pl.pallas_call(
        _inproj_kernel,
        grid=(b, s // tm),
        in_specs=[
            pl.BlockSpec((1, tm, d), row),
            pl.BlockSpec((1, 1, d), per_b),
            pl.BlockSpec((1, 1, d), per_b),
            vec(d),
            pl.BlockSpec((n_in, d), const, pipeline_mode=pl.Buffered(1)),
            pl.BlockSpec((KV_RANK, HB * (QK_NOPE + V_DIM)), const,
                         pipeline_mode=pl.Buffered(1)),
            vec(HEAD_DIM), vec(HEAD_DIM), vec(KV_RANK),
            pl.BlockSpec((tm, LANES), tab), pl.BlockSpec((tm, LANES), tab),
            pl.BlockSpec((tm, LANES), tab), pl.BlockSpec((tm, LANES), tab),
        ],
        out_specs=[pl.BlockSpec((1, tm, w), row) for w in out_widths],
        out_shape=[jax.ShapeDtypeStruct((b, s, w), BF16) for w in out_widths],
        scratch_shapes=[pltpu.VMEM((tm, d), BF16)],
        compiler_params=pltpu.CompilerParams(
            dimension_semantics=("arbitrary", "arbitrary"),
            vmem_limit_bytes=VMEM_LIMIT),
        name="in_proj",
    )(x, sc, sh, g_pre, w_in_p, w_kv, g_q, g_k, g_ckv, cosa, sina, cosb, sinb)


def _attn_kernel(*refs, n_heads, dk, shared_kv, tq, n_blocks, n_side):
    q0_ref, q_refs = refs[0], refs[1:1 + n_blocks]
    k_ref, kn_ref, v_ref = refs[1 + n_blocks:4 + n_blocks]
    rest = refs[4 + n_blocks:]
    side_in = rest[:n_side]
    o_ref = rest[n_side]
    side_out = rest[n_side + 1:2 * n_side + 1]
    s_scr = rest[2 * n_side + 1]

    def scores(q_ref, keys_ref, slot):
        for hd in range(n_heads):
            kv_hd = 0 if shared_kv else hd
            k = keys_ref[0, :, kv_hd * dk:(kv_hd + 1) * dk]
            q = q_ref[0, :, hd * dk:(hd + 1) * dk]
            s_scr[slot, hd] = lax.dot_general(
                q, k, (((1,), (1,)), ((), ())), preferred_element_type=F32)

    def softmax_pv(slot, row0):
        for hd in range(n_heads):
            kv_hd = 0 if shared_kv else hd
            v = v_ref[0, :, kv_hd * V_DIM:(kv_hd + 1) * V_DIM]
            s = s_scr[slot, hd]
            m = jnp.max(s, axis=-1, keepdims=True)
            p = jnp.exp2(s - m).astype(BF16)
            v_ext = jnp.concatenate([v, jnp.ones_like(v)], axis=1)
            o_ext = jnp.dot(p, v_ext, preferred_element_type=F32)
            o = o_ext[:, :V_DIM] / o_ext[:, V_DIM:]
            o_ref[0, row0:row0 + tq, hd * V_DIM:(hd + 1) * V_DIM] = o.astype(BF16)

    @pl.when(pl.program_id(0) == 0)
    def _():
        scores(q0_ref, k_ref, 0)

    for i in range(n_blocks):
        keys_ref = kn_ref if i == n_blocks - 1 else k_ref
        scores(q_refs[i], keys_ref, (i + 1) % 2)
        softmax_pv(i % 2, i * tq)

    for w_ref, wb_ref in zip(side_in, side_out):
        wb_ref[...] = w_ref[...].astype(BF16)


def _attention(q, k, v, side, *, n_groups, n_heads, dk, shared_kv, tq, n_blocks, name):
    b, s, _ = q.shape
    n_kv = 1 if shared_kv else n_heads
    n_blk = s // tq
    assert n_blocks % 2 == 0 and n_blk % n_blocks == 0
    n_total = b * n_groups * n_blk
    n_steps = n_total // n_blocks

    def q_map(t):
        return (t // (n_groups * n_blk), t % n_blk, (t // n_blk) % n_groups)

    def kv_map(t):
        return (t // (n_groups * n_blk), 0, (t // n_blk) % n_groups)

    def out_map(j):
        bi, i, g = q_map(n_blocks * j)
        return (bi, i // n_blocks, g)

    def ahead(i):
        return lambda j: q_map(jnp.minimum(n_blocks * j + i + 1, n_total - 1))

    q_spec = lambda f: pl.BlockSpec((1, tq, n_heads * dk), f)
    side_specs = [pl.BlockSpec((w.shape[0] // n_steps, w.shape[1]), lambda j: (j, 0))
                  for w in side]
    return pl.pallas_call(
        functools.partial(_attn_kernel, n_heads=n_heads, dk=dk, shared_kv=shared_kv,
                          tq=tq, n_blocks=n_blocks, n_side=len(side)),
        grid=(n_steps,),
        in_specs=[q_spec(lambda j: (0, 0, 0))]
        + [q_spec(ahead(i)) for i in range(n_blocks)]
        + [
            pl.BlockSpec((1, s, n_kv * dk), lambda j: kv_map(n_blocks * j)),
            pl.BlockSpec((1, s, n_kv * dk),
                         lambda j: kv_map(jnp.minimum(n_blocks * (j + 1), n_total - 1))),
            pl.BlockSpec((1, s, n_kv * V_DIM), lambda j: kv_map(n_blocks * j)),
        ] + side_specs,
        out_specs=[pl.BlockSpec((1, n_blocks * tq, n_heads * V_DIM), out_map)] + side_specs,
        out_shape=[jax.ShapeDtypeStruct((b, s, n_groups * n_heads * V_DIM), BF16)]
        + [jax.ShapeDtypeStruct(w.shape, BF16) for w in side],
        scratch_shapes=[pltpu.VMEM((2, n_heads, tq, s), F32)],
        compiler_params=pltpu.CompilerParams(
            dimension_semantics=("arbitrary",), vmem_limit_bytes=VMEM_LIMIT),
        name=name,
    )(*([q] * (n_blocks + 1)), k, k, v, *side)


def _outproj_kernel(oa_ref, ob_ref, x_ref, gt_ref, ga_ref, gb_ref, w_ref, gpost_ref, o_ref,
                    n_scr, y_scr):
    n_rows = n_scr.shape[0]

    def store_o(rows, val):
        o_ref[0, rows, :] = val

    for rows in _row_chunks(n_rows):
        n_scr[rows, 0:MIX_A] = _rms(oa_ref[0, rows, :].astype(F32), ga_ref[...]).astype(BF16)
        n_scr[rows, MIX_A:] = _rms(ob_ref[0, rows, :].astype(F32), gb_ref[...]).astype(BF16)
    y_scr[...] = jnp.dot(n_scr[...], w_ref[...], preferred_element_type=F32)
    _gated_residual_rows(lambda rows: x_ref[0, rows, :], lambda rows: y_scr[rows, :],
                         store_o, n_rows, gpost_ref[...], gt_ref[0])


def _outproj(o_a, o_b, x, gt, g_a, g_b, w_out, g_post, tm):
    b, s, d = x.shape
    row = lambda bi, i: (bi, i, 0)
    per_b = lambda bi, i: (bi, 0, 0)
    const = lambda bi, i: (0, 0)
    return pl.pallas_call(
        _outproj_kernel,
        grid=(b, s // tm),
        in_specs=[
            pl.BlockSpec((1, tm, MIX_A), row),
            pl.BlockSpec((1, tm, MIX_B), row),
            pl.BlockSpec((1, tm, d), row),
            pl.BlockSpec((1, 1, d), per_b),
            pl.BlockSpec((1, MIX_A), const),
            pl.BlockSpec((1, MIX_B), const),
            pl.BlockSpec((MIX_A + MIX_B, d), const, pipeline_mode=pl.Buffered(1)),
            pl.BlockSpec((1, d), const),
        ],
        out_specs=pl.BlockSpec((1, tm, d), row),
        out_shape=jax.ShapeDtypeStruct((b, s, d), F32),
        scratch_shapes=[pltpu.VMEM((tm, MIX_A + MIX_B), BF16), pltpu.VMEM((tm, d), F32)],
        compiler_params=pltpu.CompilerParams(
            dimension_semantics=("arbitrary", "arbitrary"),
            vmem_limit_bytes=VMEM_LIMIT),
        name="out_proj",
    )(o_a, o_b, x, gt, g_a, g_b, w_out, g_post)


def _mlp_kernel(x_ref, sc_ref, sh_ref, gt_ref, gpre_ref, w1_ref, w2_ref, gpost_ref,
                o_ref, h_scr, acc_scr, *, n_split):
    f = pl.program_id(2)
    last = pl.num_programs(2) - 1
    n_rows = h_scr.shape[0]
    parts = [slice(r, r + n_rows // n_split) for r in range(0, n_rows, n_rows // n_split)]

    def store_h(rows, val):
        h_scr[rows, :] = val

    def store_o(rows, val):
        o_ref[0, rows, :] = val

    def ffn(rows):
        u = jnp.dot(h_scr[rows, :], w1_ref[...], preferred_element_type=F32)
        u = jnp.maximum(u, 0.0)
        return jnp.dot((u * u).astype(BF16), w2_ref[...], preferred_element_type=F32)

    @pl.when(f == 0)
    def _():
        for part in parts:
            _norm_mod_rows(lambda rows: x_ref[0, rows, :], store_h, part,
                           gpre_ref[...], sc_ref[0], sh_ref[0])
            acc_scr[part, :] = ffn(part)

    @pl.when((f > 0) & (f < last))
    def _():
        acc_scr[...] += ffn(slice(0, n_rows))

    @pl.when(f == last)
    def _():
        for part in parts:
            acc_scr[part, :] += ffn(part)
            _gated_residual_rows(lambda rows: x_ref[0, rows, :], lambda rows: acc_scr[rows, :],
                                 store_o, part, gpost_ref[...], gt_ref[0])


def _mlp(x, sc, sh, gt, g_pre, w1, w2, g_post, tm, tf):
    b, s, d = x.shape
    dff = w1.shape[1]
    row = lambda bi, i, f: (bi, i, 0)
    per_b = lambda bi, i, f: (bi, 0, 0)
    const = lambda bi, i, f: (0, 0)
    return pl.pallas_call(
        functools.partial(_mlp_kernel, n_split=2),
        grid=(b, s // tm, dff // tf),
        in_specs=[
            pl.BlockSpec((1, tm, d), row),
            pl.BlockSpec((1, 1, d), per_b),
            pl.BlockSpec((1, 1, d), per_b),
            pl.BlockSpec((1, 1, d), per_b),
            pl.BlockSpec((1, d), const),
            pl.BlockSpec((d, tf), lambda bi, i, f: (0, f)),
            pl.BlockSpec((tf, d), lambda bi, i, f: (f, 0)),
            pl.BlockSpec((1, d), const),
        ],
        out_specs=pl.BlockSpec((1, tm, d), row),
        out_shape=jax.ShapeDtypeStruct((b, s, d), F32),
        scratch_shapes=[pltpu.VMEM((tm, d), BF16), pltpu.VMEM((tm, d), F32)],
        compiler_params=pltpu.CompilerParams(
            dimension_semantics=("arbitrary", "arbitrary", "arbitrary"),
            vmem_limit_bytes=VMEM_LIMIT),
        name="mlp",
    )(x, sc, sh, gt, g_pre, w1, w2, g_post)


def _cast_pad_kernel(w_ref, o_ref):
    n = w_ref.shape[1]
    o_ref[:, :n] = w_ref[...].astype(BF16)
    if o_ref.shape[1] > n:
        o_ref[:, n:] = jnp.zeros((o_ref.shape[0], o_ref.shape[1] - n), BF16)


def _cast_pad(w, tr):
    r, n = w.shape
    n_pad = -(-n // LANES) * LANES
    return pl.pallas_call(
        _cast_pad_kernel,
        grid=(r // tr,),
        in_specs=[pl.BlockSpec((tr, n), lambda i: (i, 0))],
        out_specs=pl.BlockSpec((tr, n_pad), lambda i: (i, 0)),
        out_shape=jax.ShapeDtypeStruct((r, n_pad), BF16),
        compiler_params=pltpu.CompilerParams(
            dimension_semantics=("arbitrary",), vmem_limit_bytes=VMEM_LIMIT),
        name="cast_pad",
    )(w)


def kernel(x, c, w_ada, b_ada, g_pre_attn, w_in, g_q_a, g_k_a, g_ckv, w_kv_b, g_out_a,
           g_out_b, w_out, g_post_attn, g_pre_mlp, w_mlp_in, w_mlp_out, g_post_mlp):
    b, s, d = x.shape
    depth = w_ada.shape[0]
    tabs = _rope_tables(s, HEAD_DIM) + _rope_tables(s, QK_ROPE)
    c_pad = jnp.pad(c, ((0, 8 - b), (0, 0)))

    for l in range(depth):
        mod = _ada(c_pad, w_ada[l], b_ada[l][None, :])[:b]
        sh_a, sc_a, gt_a, sh_m, sc_m, gt_m = [
            m.reshape(b, 1, d) for m in jnp.split(mod, N_MOD, axis=-1)]

        q_a, k_a, v_a, q_b, k_b, v_b = _inproj(
            x, sc_a, sh_a, g_pre_attn[l][None, :], _cast_pad(w_in[l].T, tr=192),
            _cast_pad(w_kv_b[l], tr=256), g_q_a[l][None, :], g_k_a[l][None, :],
            g_ckv[l][None, :], tabs, tm=512)

        o_a, w1_bf, w2_bf = _attention(
            q_a, k_a, v_a, [w_mlp_in[l], w_mlp_out[l]], n_groups=HKV, n_heads=G_A,
            dk=HEAD_DIM, shared_kv=True, tq=256, n_blocks=4, name="attn_gqa")
        o_b, wo_bf = _attention(
            q_b, k_b, v_b, [w_out[l]], n_groups=HB // 4, n_heads=4,
            dk=QK_B_PAD, shared_kv=False, tq=256, n_blocks=4, name="attn_mla")

        x = _outproj(o_a, o_b, x, gt_a, g_out_a[l][None, :], g_out_b[l][None, :],
                     wo_bf, g_post_attn[l][None, :], tm=512)

        x = _mlp(x, sc_m, sh_m, gt_m, g_pre_mlp[l][None, :], w1_bf, w2_bf,
                 g_post_mlp[l][None, :], tm=512, tf=1024)
    return x
```

```python
import functools
import math

import numpy as np
import jax
import jax.numpy as jnp
from jax import lax
from jax.experimental import pallas as pl
from jax.experimental.pallas import tpu as pltpu

D_MODEL = 2048
GRID_W = 64
ROPE_THETA = 10000.0
EPS = 1e-6

HEAD_DIM = 128
HA = 8
HKV = 2
G_A = HA // HKV

HB = 8
QK_NOPE = 128
QK_ROPE = 64
V_DIM = 128
KV_RANK = 512
QK_B = QK_NOPE + QK_ROPE
QK_B_PAD = 256

W_QA = HA * HEAD_DIM
W_KA = HKV * HEAD_DIM
W_VA = HKV * HEAD_DIM
W_QB = HB * QK_B
MIX_A = HA * HEAD_DIM
MIX_B = HB * V_DIM
D_FF = 4 * D_MODEL
N_MOD = 6

LOG2_E = math.log2(math.e)
LANES = 128
ROW_CHUNK = 16
VMEM_LIMIT = 56 * 1024 * 1024

F32 = jnp.float32
BF16 = jnp.bfloat16


def _rope_tables(seq_len, dim):
    rows = seq_len // GRID_W
    pos = np.arange(seq_len)
    row = (pos // GRID_W).astype(np.float64)
    col = (pos % GRID_W).astype(np.float64)
    half = dim // 2
    inv = ROPE_THETA ** (-np.arange(0, half, 2, dtype=np.float64) / half)
    ang_r = row[:, None] * inv[None, :]
    ang_c = col[:, None] * inv[None, :]
    ang = np.concatenate([ang_r, ang_r, ang_c, ang_c], axis=-1)
    cos, sin = np.cos(ang), np.sin(ang)
    quarter = dim // 4
    sign = np.where((np.arange(dim) % half) < quarter, -1.0, 1.0)
    sin = sin * sign[None, :]
    reps = LANES // dim
    cos = np.tile(cos, (1, reps))
    sin = np.tile(sin, (1, reps))
    del rows
    return jnp.asarray(cos, F32), jnp.asarray(sin, F32)


def _unit_rms(x):
    ms = jnp.mean(x * x, axis=-1, keepdims=True)
    return x * lax.rsqrt(ms + EPS)


def _rms(x, g):
    return _unit_rms(x) * g


def _row_chunks(span):
    span = span if isinstance(span, slice) else slice(0, span)
    return [slice(r, r + ROW_CHUNK) for r in range(span.start, span.stop, ROW_CHUNK)]


def _norm_mod_rows(load, store, span, g, sc, sh):
    gain = g * (1.0 + sc)
    for rows in _row_chunks(span):
        store(rows, (_unit_rms(load(rows)) * gain + sh).astype(BF16))


def _gated_residual_rows(load_x, load_y, store, span, g, gt):
    gain = gt * g
    for rows in _row_chunks(span):
        store(rows, load_x(rows) + _unit_rms(load_y(rows)) * gain)


def _dot_nt(a, w_rows):
    return lax.dot_general(a, w_rows, (((1,), (1,)), ((), ())), preferred_element_type=F32)


def _rope(x, cos, sin_signed, quarter):
    lane = lax.broadcasted_iota(jnp.int32, x.shape, 1)
    take_up = (lane % (2 * quarter)) < quarter
    up = pltpu.roll(x, LANES - quarter, axis=1)
    down = pltpu.roll(x, quarter, axis=1)
    rot = jnp.where(take_up, up, down)
    return x * cos + rot * sin_signed


def _ada_kernel(c_ref, w_ref, b_ref, o_ref):
    c = c_ref[...]
    c_act = (c * jax.nn.sigmoid(c)).astype(BF16)
    w = w_ref[...].astype(BF16)
    o_ref[...] = jnp.dot(c_act, w, preferred_element_type=F32) + b_ref[...]


def _ada(c_pad, w_ada, b_ada):
    m, d = c_pad.shape
    n = w_ada.shape[1]
    tn = 1024
    return pl.pallas_call(
        _ada_kernel,
        grid=(n // tn,),
        in_specs=[
            pl.BlockSpec((m, d), lambda j: (0, 0)),
            pl.BlockSpec((d, tn), lambda j: (0, j)),
            pl.BlockSpec((1, tn), lambda j: (0, j)),
        ],
        out_specs=pl.BlockSpec((m, tn), lambda j: (0, j)),
        out_shape=jax.ShapeDtypeStruct((m, n), F32),
        compiler_params=pltpu.CompilerParams(
            dimension_semantics=("arbitrary",), vmem_limit_bytes=VMEM_LIMIT),
        name="ada_mod",
    )(c_pad, w_ada, b_ada)


def _inproj_kernel(x_ref, sc_ref, sh_ref, gpre_ref, w_ref, wkv_ref, gq_ref, gk_ref,
                   gckv_ref, cosa_ref, sina_ref, cosb_ref, sinb_ref,
                   qa_ref, ka_ref, va_ref, qb_ref, kb_ref, vb_ref, h_scr):
    tm = h_scr.shape[0]

    def store_h(rows, val):
        h_scr[rows, :] = val

    _norm_mod_rows(lambda rows: x_ref[0, rows, :], store_h, tm,
                   gpre_ref[...], sc_ref[0], sh_ref[0])
    hb = h_scr[...]
    cosa, sina = cosa_ref[...], sina_ref[...]
    cosb, sinb = cosb_ref[...], sinb_ref[...]
    scale_a = LOG2_E / math.sqrt(HEAD_DIM)
    scale_b = LOG2_E / math.sqrt(QK_B)

    qa = _dot_nt(hb, w_ref[0:W_QA, :])
    gq = gq_ref[...] * scale_a
    for hd in range(HA):
        sl = slice(hd * HEAD_DIM, (hd + 1) * HEAD_DIM)
        q = _rope(_rms(qa[:, sl], gq), cosa, sina, HEAD_DIM // 4)
        qa_ref[0, :, sl] = q.astype(BF16)

    kva = _dot_nt(hb, w_ref[W_QA:W_QA + W_KA + W_VA, :])
    for hd in range(HKV):
        sl = slice(hd * HEAD_DIM, (hd + 1) * HEAD_DIM)
        k = _rope(_rms(kva[:, sl], gk_ref[...]), cosa, sina, HEAD_DIM // 4)
        ka_ref[0, :, sl] = k.astype(BF16)
    va_ref[0] = kva[:, W_KA:].astype(BF16)

    o_qb = W_QA + W_KA + W_VA
    qb = _dot_nt(hb, w_ref[o_qb:o_qb + W_QB, :])
    lane = lax.broadcasted_iota(jnp.int32, (tm, LANES), 1)
    low = lane < QK_ROPE
    swap = lambda t: pltpu.roll(t, QK_ROPE, axis=1)
    for pair in range(HB // 2):
        t0, t1, t2 = (qb[:, (3 * pair + i) * LANES:(3 * pair + i + 1) * LANES] for i in range(3))
        r1 = _rope(t1, cosb, sinb, QK_ROPE // 4)
        r2 = _rope(t2, cosb, sinb, QK_ROPE // 4)
        heads = (
            (2 * pair, t0, jnp.where(low, r1, 0.0)),
            (2 * pair + 1, jnp.where(low, swap(t1), swap(t2)), jnp.where(low, swap(r2), 0.0)),
        )
        for hd, nope, pe in heads:
            base = hd * QK_B_PAD
            qb_ref[0, :, base:base + QK_NOPE] = (nope * scale_b).astype(BF16)
            qb_ref[0, :, base + QK_NOPE:base + QK_B_PAD] = (pe * scale_b).astype(BF16)

    o_ckv = o_qb + W_QB
    w_rest = jnp.concatenate(
        [w_ref[o_ckv:, :], jnp.zeros((LANES - QK_ROPE, w_ref.shape[1]), BF16)], axis=0)
    rest = _dot_nt(hb, w_rest)
    ckv = _rms(rest[:, :KV_RANK], gckv_ref[...]).astype(BF16)
    kv = jnp.dot(ckv, wkv_ref[...], preferred_element_type=F32)
    kpe = _rope(rest[:, KV_RANK:], cosb, sinb, QK_ROPE // 4).astype(BF16)
    for hd in range(HB):
        base = hd * (QK_NOPE + V_DIM)
        kb_ref[0, :, hd * QK_B_PAD:hd * QK_B_PAD + QK_NOPE] = (
            kv[:, base:base + QK_NOPE].astype(BF16))
        kb_ref[0, :, hd * QK_B_PAD + QK_NOPE:(hd + 1) * QK_B_PAD] = kpe
        vb_ref[0, :, hd * V_DIM:(hd + 1) * V_DIM] = (
            kv[:, base + QK_NOPE:base + QK_NOPE + V_DIM].astype(BF16))


def _inproj(x, sc, sh, g_pre, w_in_p, w_kv, g_q, g_k, g_ckv, tabs, tm):
    b, s, d = x.shape
    n_in = w_in_p.shape[0]
    cosa, sina, cosb, sinb = tabs
    row = lambda bi, i: (bi, i, 0)
    per_b = lambda bi, i: (bi, 0, 0)
    const = lambda bi, i: (0, 0)
    tab = lambda bi, i: (i, 0)
    vec = lambda n: pl.BlockSpec((1, n), const)
    out_widths = (MIX_A, W_KA, W_VA, HB * QK_B_PAD, HB * QK_B_PAD, MIX_B)
    return pl.pallas_call(
        _inproj_kernel,
        grid=(b, s // tm),
        in_specs=[
            pl.BlockSpec((1, tm, d), row),
            pl.BlockSpec((1, 1, d), per_b),
            pl.BlockSpec((1, 1, d), per_b),
            vec(d),
            pl.BlockSpec((n_in, d), const, pipeline_mode=pl.Buffered(1)),
            pl.BlockSpec((KV_RANK, HB * (QK_NOPE + V_DIM)), const,
                         pipeline_mode=pl.Buffered(1)),
            vec(HEAD_DIM), vec(HEAD_DIM), vec(KV_RANK),
            pl.BlockSpec((tm, LANES), tab), pl.BlockSpec((tm, LANES), tab),
            pl.BlockSpec((tm, LANES), tab), pl.BlockSpec((tm, LANES), tab),
        ],
        out_specs=[pl.BlockSpec((1, tm, w), row) for w in out_widths],
        out_shape=[jax.ShapeDtypeStruct((b, s, w), BF16) for w in out_widths],
        scratch_shapes=[pltpu.VMEM((tm, d), BF16)],
        compiler_params=pltpu.CompilerParams(
            dimension_semantics=("arbitrary", "arbitrary"),
            vmem_limit_bytes=VMEM_LIMIT),
        name="in_proj",
    )(x, sc, sh, g_pre, w_in_p, w_kv, g_q, g_k, g_ckv, cosa, sina, cosb, sinb)


def _attn_kernel(*refs, n_heads, dk, shared_kv, tq, n_blocks, n_side):
    q0_ref, q_refs = refs[0], refs[1:1 + n_blocks]
    k_ref, kn_ref, v_ref = refs[1 + n_blocks:4 + n_blocks]
    rest = refs[4 + n_blocks:]
    side_in = rest[:n_side]
    o_ref = rest[n_side]
    side_out = rest[n_side + 1:2 * n_side + 1]
    s_scr, m_scr = rest[2 * n_side + 1:]

    def scores(q_ref, keys_ref, slot):
        for hd in range(n_heads):
            kv_hd = 0 if shared_kv else hd
            k = keys_ref[0, :, kv_hd * dk:(kv_hd + 1) * dk]
            q = q_ref[0, :, hd * dk:(hd + 1) * dk]
            s = lax.dot_general(q, k, (((1,), (1,)), ((), ())), preferred_element_type=F32)
            s_scr[slot, hd] = s
            m_scr[slot, hd] = jnp.max(s, axis=-1, keepdims=True)

    def softmax_pv(slot, row0):
        for hd in range(n_heads):
            kv_hd = 0 if shared_kv else hd
            v = v_ref[0, :, kv_hd * V_DIM:(kv_hd + 1) * V_DIM]
            p = jnp.exp2(s_scr[slot, hd] - m_scr[slot, hd]).astype(BF16)
            v_ext = jnp.concatenate([v, jnp.ones_like(v)], axis=1)
            o_ext = jnp.dot(p, v_ext, preferred_element_type=F32)
            o = o_ext[:, :V_DIM] / o_ext[:, V_DIM:]
            o_ref[0, row0:row0 + tq, hd * V_DIM:(hd + 1) * V_DIM] = o.astype(BF16)

    @pl.when(pl.program_id(0) == 0)
    def _():
        scores(q0_ref, k_ref, 0)

    for i in range(n_blocks):
        keys_ref = kn_ref if i == n_blocks - 1 else k_ref
        scores(q_refs[i], keys_ref, (i + 1) % 2)
        for w_ref, wb_ref in zip(side_in, side_out):
            n = w_ref.shape[0] // n_blocks
            wb_ref[i * n:(i + 1) * n, :] = w_ref[i * n:(i + 1) * n, :].astype(BF16)
        softmax_pv(i % 2, i * tq)


def _attention(q, k, v, side, *, n_groups, n_heads, dk, shared_kv, tq, n_blocks, name):
    b, s, _ = q.shape
    n_kv = 1 if shared_kv else n_heads
    n_blk = s // tq
    assert n_blocks % 2 == 0 and n_blk % n_blocks == 0
    n_total = b * n_groups * n_blk
    n_steps = n_total // n_blocks

    def q_map(t):
        return (t // (n_groups * n_blk), t % n_blk, (t // n_blk) % n_groups)

    def kv_map(t):
        return (t // (n_groups * n_blk), 0, (t // n_blk) % n_groups)

    def out_map(j):
        bi, i, g = q_map(n_blocks * j)
        return (bi, i // n_blocks, g)

    def ahead(i):
        return lambda j: q_map(jnp.minimum(n_blocks * j + i + 1, n_total - 1))

    q_spec = lambda f: pl.BlockSpec((1, tq, n_heads * dk), f)
    side_specs = [pl.BlockSpec((w.shape[0] // n_steps, w.shape[1]), lambda j: (j, 0))
                  for w in side]
    return pl.pallas_call(
        functools.partial(_attn_kernel, n_heads=n_heads, dk=dk, shared_kv=shared_kv,
                          tq=tq, n_blocks=n_blocks, n_side=len(side)),
        grid=(n_steps,),
        in_specs=[q_spec(lambda j: (0, 0, 0))]
        + [q_spec(ahead(i)) for i in range(n_blocks)]
        + [
            pl.BlockSpec((1, s, n_kv * dk), lambda j: kv_map(n_blocks * j)),
            pl.BlockSpec((1, s, n_kv * dk),
                         lambda j: kv_map(jnp.minimum(n_blocks * (j + 1), n_total - 1))),
            pl.BlockSpec((1, s, n_kv * V_DIM), lambda j: kv_map(n_blocks * j)),
        ] + side_specs,
        out_specs=[pl.BlockSpec((1, n_blocks * tq, n_heads * V_DIM), out_map)] + side_specs,
        out_shape=[jax.ShapeDtypeStruct((b, s, n_groups * n_heads * V_DIM), BF16)]
        + [jax.ShapeDtypeStruct(w.shape, BF16) for w in side],
        scratch_shapes=[pltpu.VMEM((2, n_heads, tq, s), F32),
                        pltpu.VMEM((2, n_heads, tq, 1), F32)],
        compiler_params=pltpu.CompilerParams(
            dimension_semantics=("arbitrary",), vmem_limit_bytes=VMEM_LIMIT),
        name=name,
    )(*([q] * (n_blocks + 1)), k, k, v, *side)


def _outproj_kernel(oa_ref, ob_ref, x_ref, gt_ref, ga_ref, gb_ref, w_ref, gpost_ref, o_ref,
                    n_scr, y_scr):
    n_rows = n_scr.shape[0]

    def store_o(rows, val):
        o_ref[0, rows, :] = val

    for rows in _row_chunks(n_rows):
        n_scr[rows, 0:MIX_A] = _rms(oa_ref[0, rows, :].astype(F32), ga_ref[...]).astype(BF16)
        n_scr[rows, MIX_A:] = _rms(ob_ref[0, rows, :].astype(F32), gb_ref[...]).astype(BF16)
    y_scr[...] = jnp.dot(n_scr[...], w_ref[...], preferred_element_type=F32)
    _gated_residual_rows(lambda rows: x_ref[0, rows, :], lambda rows: y_scr[rows, :],
                         store_o, n_rows, gpost_ref[...], gt_ref[0])


def _outproj(o_a, o_b, x, gt, g_a, g_b, w_out, g_post, tm):
    b, s, d = x.shape
    row = lambda bi, i: (bi, i, 0)
    per_b = lambda bi, i: (bi, 0, 0)
    const = lambda bi, i: (0, 0)
    return pl.pallas_call(
        _outproj_kernel,
        grid=(b, s // tm),
        in_specs=[
            pl.BlockSpec((1, tm, MIX_A), row),
            pl.BlockSpec((1, tm, MIX_B), row),
            pl.BlockSpec((1, tm, d), row),
            pl.BlockSpec((1, 1, d), per_b),
            pl.BlockSpec((1, MIX_A), const),
            pl.BlockSpec((1, MIX_B), const),
            pl.BlockSpec((MIX_A + MIX_B, d), const, pipeline_mode=pl.Buffered(1)),
            pl.BlockSpec((1, d), const),
        ],
        out_specs=pl.BlockSpec((1, tm, d), row),
        out_shape=jax.ShapeDtypeStruct((b, s, d), F32),
        scratch_shapes=[pltpu.VMEM((tm, MIX_A + MIX_B), BF16), pltpu.VMEM((tm, d), F32)],
        compiler_params=pltpu.CompilerParams(
            dimension_semantics=("arbitrary", "arbitrary"),
            vmem_limit_bytes=VMEM_LIMIT),
        name="out_proj",
    )(o_a, o_b, x, gt, g_a, g_b, w_out, g_post)


def _mlp_kernel(x_ref, sc_ref, sh_ref, gt_ref, gpre_ref, w1_ref, w2_ref, gpost_ref,
                o_ref, h_scr, acc_scr, *, n_split):
    f = pl.program_id(2)
    last = pl.num_programs(2) - 1
    n_rows = h_scr.shape[0]
    parts = [slice(r, r + n_rows // n_split) for r in range(0, n_rows, n_rows // n_split)]

    def store_h(rows, val):
        h_scr[rows, :] = val

    def store_o(rows, val):
        o_ref[0, rows, :] = val

    def ffn(rows):
        u = jnp.dot(h_scr[rows, :], w1_ref[...], preferred_element_type=F32)
        u = jnp.maximum(u, 0.0)
        return jnp.dot((u * u).astype(BF16), w2_ref[...], preferred_element_type=F32)

    @pl.when(f == 0)
    def _():
        for part in parts:
            _norm_mod_rows(lambda rows: x_ref[0, rows, :], store_h, part,
                           gpre_ref[...], sc_ref[0], sh_ref[0])
            acc_scr[part, :] = ffn(part)

    @pl.when((f > 0) & (f < last))
    def _():
        acc_scr[...] += ffn(slice(0, n_rows))

    @pl.when(f == last)
    def _():
        for part in parts:
            acc_scr[part, :] += ffn(part)
            _gated_residual_rows(lambda rows: x_ref[0, rows, :], lambda rows: acc_scr[rows, :],
                                 store_o, part, gpost_ref[...], gt_ref[0])


def _mlp(x, sc, sh, gt, g_pre, w1, w2, g_post, tm, tf):
    b, s, d = x.shape
    dff = w1.shape[1]
    row = lambda bi, i, f: (bi, i, 0)
    per_b = lambda bi, i, f: (bi, 0, 0)
    const = lambda bi, i, f: (0, 0)
    return pl.pallas_call(
        functools.partial(_mlp_kernel, n_split=2),
        grid=(b, s // tm, dff // tf),
        in_specs=[
            pl.BlockSpec((1, tm, d), row),
            pl.BlockSpec((1, 1, d), per_b),
            pl.BlockSpec((1, 1, d), per_b),
            pl.BlockSpec((1, 1, d), per_b),
            pl.BlockSpec((1, d), const),
            pl.BlockSpec((d, tf), lambda bi, i, f: (0, f)),
            pl.BlockSpec((tf, d), lambda bi, i, f: (f, 0)),
            pl.BlockSpec((1, d), const),
        ],
        out_specs=pl.BlockSpec((1, tm, d), row),
        out_shape=jax.ShapeDtypeStruct((b, s, d), F32),
        scratch_shapes=[pltpu.VMEM((tm, d), BF16), pltpu.VMEM((tm, d), F32)],
        compiler_params=pltpu.CompilerParams(
            dimension_semantics=("arbitrary", "arbitrary", "arbitrary"),
            vmem_limit_bytes=VMEM_LIMIT),
        name="mlp",
    )(x, sc, sh, gt, g_pre, w1, w2, g_post)


def _cast_pad_kernel(w_ref, o_ref):
    n = w_ref.shape[1]
    o_ref[:, :n] = w_ref[...].astype(BF16)
    if o_ref.shape[1] > n:
        o_ref[:, n:] = jnp.zeros((o_ref.shape[0], o_ref.shape[1] - n), BF16)


def _cast_pad(w, tr):
    r, n = w.shape
    n_pad = -(-n // LANES) * LANES
    return pl.pallas_call(
        _cast_pad_kernel,
        grid=(r // tr,),
        in_specs=[pl.BlockSpec((tr, n), lambda i: (i, 0))],
        out_specs=pl.BlockSpec((tr, n_pad), lambda i: (i, 0)),
        out_shape=jax.ShapeDtypeStruct((r, n_pad), BF16),
        compiler_params=pltpu.CompilerParams(
            dimension_semantics=("arbitrary",), vmem_limit_bytes=VMEM_LIMIT),
        name="cast_pad",
    )(w)


def kernel(x, c, w_ada, b_ada, g_pre_attn, w_in, g_q_a, g_k_a, g_ckv, w_kv_b, g_out_a,
           g_out_b, w_out, g_post_attn, g_pre_mlp, w_mlp_in, w_mlp_out, g_post_mlp):
    b, s, d = x.shape
    depth = w_ada.shape[0]
    tabs = _rope_tables(s, HEAD_DIM) + _rope_tables(s, QK_ROPE)
    c_pad = jnp.pad(c, ((0, 8 - b), (0, 0)))

    for l in range(depth):
        mod = _ada(c_pad, w_ada[l], b_ada[l][None, :])[:b]
        sh_a, sc_a, gt_a, sh_m, sc_m, gt_m = [
            m.reshape(b, 1, d) for m in jnp.split(mod, N_MOD, axis=-1)]

        q_a, k_a, v_a, q_b, k_b, v_b = _inproj(
            x, sc_a, sh_a, g_pre_attn[l][None, :], _cast_pad(w_in[l].T, tr=608),
            _cast_pad(w_kv_b[l], tr=256), g_q_a[l][None, :], g_k_a[l][None, :],
            g_ckv[l][None, :], tabs, tm=512)

        o_a, w1_bf, w2_bf = _attention(
            q_a, k_a, v_a, [w_mlp_in[l], w_mlp_out[l]], n_groups=HKV, n_heads=G_A,
            dk=HEAD_DIM, shared_kv=True, tq=256, n_blocks=4, name="attn_gqa")
        o_b, wo_bf = _attention(
            q_b, k_b, v_b, [w_out[l]], n_groups=HB // 4, n_heads=4,
            dk=QK_B_PAD, shared_kv=False, tq=256, n_blocks=4, name="attn_mla")

        x = _outproj(o_a, o_b, x, gt_a, g_out_a[l][None, :], g_out_b[l][None, :],
                     wo_bf, g_post_attn[l][None, :], tm=512)

        x = _mlp(x, sc_m, sh_m, gt_m, g_pre_mlp[l][None, :], w1_bf, w2_bf,
                 g_post_mlp[l][None, :], tm=512, tf=1024)
    return x
```

```python
import functools
import math

import numpy as np
import jax
import jax.numpy as jnp
from jax import lax
from jax.experimental import pallas as pl
from jax.experimental.pallas import tpu as pltpu

D_MODEL = 2048
GRID_W = 64
ROPE_THETA = 10000.0
EPS = 1e-6

HEAD_DIM = 128
HA = 8
HKV = 2
G_A = HA // HKV

HB = 8
QK_NOPE = 128
QK_ROPE = 64
V_DIM = 128
KV_RANK = 512
QK_B = QK_NOPE + QK_ROPE
QK_B_PAD = 256

W_QA = HA * HEAD_DIM
W_KA = HKV * HEAD_DIM
W_VA = HKV * HEAD_DIM
W_QB = HB * QK_B
MIX_A = HA * HEAD_DIM
MIX_B = HB * V_DIM
D_FF = 4 * D_MODEL
N_MOD = 6

LOG2_E = math.log2(math.e)
LANES = 128
ROW_CHUNK = 16
MLP_TF = 1024
VMEM_LIMIT = 56 * 1024 * 1024

F32 = jnp.float32
BF16 = jnp.bfloat16


def _rope_tables(seq_len, dim):
    rows = seq_len // GRID_W
    pos = np.arange(seq_len)
    row = (pos // GRID_W).astype(np.float64)
    col = (pos % GRID_W).astype(np.float64)
    half = dim // 2
    inv = ROPE_THETA ** (-np.arange(0, half, 2, dtype=np.float64) / half)
    ang_r = row[:, None] * inv[None, :]
    ang_c = col[:, None] * inv[None, :]
    ang = np.concatenate([ang_r, ang_r, ang_c, ang_c], axis=-1)
    cos, sin = np.cos(ang), np.sin(ang)
    quarter = dim // 4
    sign = np.where((np.arange(dim) % half) < quarter, -1.0, 1.0)
    sin = sin * sign[None, :]
    reps = LANES // dim
    cos = np.tile(cos, (1, reps))
    sin = np.tile(sin, (1, reps))
    del rows
    return jnp.asarray(cos, F32), jnp.asarray(sin, F32)


def _unit_rms(x):
    ms = jnp.mean(x * x, axis=-1, keepdims=True)
    return x * lax.rsqrt(ms + EPS)


def _rms(x, g):
    return _unit_rms(x) * g


def _row_chunks(span):
    span = span if isinstance(span, slice) else slice(0, span)
    return [slice(r, r + ROW_CHUNK) for r in range(span.start, span.stop, ROW_CHUNK)]


def _norm_mod_rows(load, store, span, g, sc, sh):
    gain = g * (1.0 + sc)
    for rows in _row_chunks(span):
        store(rows, (_unit_rms(load(rows)) * gain + sh).astype(BF16))


def _gated_residual_rows(load_x, load_y, store, span, g, gt):
    gain = gt * g
    for rows in _row_chunks(span):
        store(rows, load_x(rows) + _unit_rms(load_y(rows)) * gain)


def _dot_nt(a, w_rows):
    return lax.dot_general(a, w_rows, (((1,), (1,)), ((), ())), preferred_element_type=F32)


def _rope(x, cos, sin_signed, quarter):
    lane = lax.broadcasted_iota(jnp.int32, x.shape, 1)
    take_up = (lane % (2 * quarter)) < quarter
    up = pltpu.roll(x, LANES - quarter, axis=1)
    down = pltpu.roll(x, quarter, axis=1)
    rot = jnp.where(take_up, up, down)
    return x * cos + rot * sin_signed


def _ada_kernel(c_ref, w_ref, b_ref, o_ref):
    c = c_ref[...]
    c_act = (c * jax.nn.sigmoid(c)).astype(BF16)
    w = w_ref[...].astype(BF16)
    o_ref[...] = jnp.dot(c_act, w, preferred_element_type=F32) + b_ref[...]


def _ada(c_pad, w_ada, b_ada):
    m, d = c_pad.shape
    n = w_ada.shape[1]
    tn = 1024
    return pl.pallas_call(
        _ada_kernel,
        grid=(n // tn,),
        in_specs=[
            pl.BlockSpec((m, d), lambda j: (0, 0)),
            pl.BlockSpec((d, tn), lambda j: (0, j)),
            pl.BlockSpec((1, tn), lambda j: (0, j)),
        ],
        out_specs=pl.BlockSpec((m, tn), lambda j: (0, j)),
        out_shape=jax.ShapeDtypeStruct((m, n), F32),
        compiler_params=pltpu.CompilerParams(
            dimension_semantics=("arbitrary",), vmem_limit_bytes=VMEM_LIMIT),
        name="ada_mod",
    )(c_pad, w_ada, b_ada)


def _inproj_kernel(x_ref, sc_ref, sh_ref, gpre_ref, w_ref, wkv_ref, gq_ref, gk_ref,
                   gckv_ref, cosa_ref, sina_ref, cosb_ref, sinb_ref,
                   qa_ref, ka_ref, va_ref, qb_ref, kb_ref, vb_ref, h_scr):
    tm = h_scr.shape[0]

    def store_h(rows, val):
        h_scr[rows, :] = val

    _norm_mod_rows(lambda rows: x_ref[0, rows, :], store_h, tm,
                   gpre_ref[...], sc_ref[0], sh_ref[0])
    hb = h_scr[...]
    cosa, sina = cosa_ref[...], sina_ref[...]
    cosb, sinb = cosb_ref[...], sinb_ref[...]
    scale_a = LOG2_E / math.sqrt(HEAD_DIM)
    scale_b = LOG2_E / math.sqrt(QK_B)

    qa = _dot_nt(hb, w_ref[0:W_QA, :])
    gq = gq_ref[...] * scale_a
    for hd in range(HA):
        sl = slice(hd * HEAD_DIM, (hd + 1) * HEAD_DIM)
        q = _rope(_rms(qa[:, sl], gq), cosa, sina, HEAD_DIM // 4)
        qa_ref[0, :, sl] = q.astype(BF16)

    kva = _dot_nt(hb, w_ref[W_QA:W_QA + W_KA + W_VA, :])
    for hd in range(HKV):
        sl = slice(hd * HEAD_DIM, (hd + 1) * HEAD_DIM)
        k = _rope(_rms(kva[:, sl], gk_ref[...]), cosa, sina, HEAD_DIM // 4)
        ka_ref[0, :, sl] = k.astype(BF16)
    va_ref[0] = kva[:, W_KA:].astype(BF16)

    o_qb = W_QA + W_KA + W_VA
    qb = _dot_nt(hb, w_ref[o_qb:o_qb + W_QB, :])
    lane = lax.broadcasted_iota(jnp.int32, (tm, LANES), 1)
    low = lane < QK_ROPE
    swap = lambda t: pltpu.roll(t, QK_ROPE, axis=1)
    for pair in range(HB // 2):
        t0, t1, t2 = (qb[:, (3 * pair + i) * LANES:(3 * pair + i + 1) * LANES] for i in range(3))
        r1 = _rope(t1, cosb, sinb, QK_ROPE // 4)
        r2 = _rope(t2, cosb, sinb, QK_ROPE // 4)
        heads = (
            (2 * pair, t0, jnp.where(low, r1, 0.0)),
            (2 * pair + 1, jnp.where(low, swap(t1), swap(t2)), jnp.where(low, swap(r2), 0.0)),
        )
        for hd, nope, pe in heads:
            base = hd * QK_B_PAD
            qb_ref[0, :, base:base + QK_NOPE] = (nope * scale_b).astype(BF16)
            qb_ref[0, :, base + QK_NOPE:base + QK_B_PAD] = (pe * scale_b).astype(BF16)

    o_ckv = o_qb + W_QB
    w_rest = jnp.concatenate(
        [w_ref[o_ckv:, :], jnp.zeros((LANES - QK_ROPE, w_ref.shape[1]), BF16)], axis=0)
    rest = _dot_nt(hb, w_rest)
    ckv = _rms(rest[:, :KV_RANK], gckv_ref[...]).astype(BF16)
    kv = jnp.dot(ckv, wkv_ref[...], preferred_element_type=F32)
    kpe = _rope(rest[:, KV_RANK:], cosb, sinb, QK_ROPE // 4).astype(BF16)
    for hd in range(HB):
        base = hd * (QK_NOPE + V_DIM)
        kb_ref[0, :, hd * QK_B_PAD:hd * QK_B_PAD + QK_NOPE] = (
            kv[:, base:base + QK_NOPE].astype(BF16))
        kb_ref[0, :, hd * QK_B_PAD + QK_NOPE:(hd + 1) * QK_B_PAD] = kpe
        vb_ref[0, :, hd * V_DIM:(hd + 1) * V_DIM] = (
            kv[:, base + QK_NOPE:base + QK_NOPE + V_DIM].astype(BF16))


def _inproj(x, sc, sh, g_pre, w_in_p, w_kv, g_q, g_k, g_ckv, tabs, tm):
    b, s, d = x.shape
    n_in = w_in_p.shape[0]
    cosa, sina, cosb, sinb = tabs
    row = lambda bi, i: (bi, i, 0)
    per_b = lambda bi, i: (bi, 0, 0)
    const = lambda bi, i: (0, 0)
    tab = lambda bi, i: (i, 0)
    vec = lambda n: pl.BlockSpec((1, n), const)
    out_widths = (MIX_A, W_KA, W_VA, HB * QK_B_PAD, HB * QK_B_PAD, MIX_B)
    return pl.pallas_call(
        _inproj_kernel,
        grid=(b, s // tm),
        in_specs=[
            pl.BlockSpec((1, tm, d), row),
            pl.BlockSpec((1, 1, d), per_b),
            pl.BlockSpec((1, 1, d), per_b),
            vec(d),
            pl.BlockSpec((n_in, d), const, pipeline_mode=pl.Buffered(1)),
            pl.BlockSpec((KV_RANK, HB * (QK_NOPE + V_DIM)), const,
                         pipeline_mode=pl.Buffered(1)),
            vec(HEAD_DIM), vec(HEAD_DIM), vec(KV_RANK),
            pl.BlockSpec((tm, LANES), tab), pl.BlockSpec((tm, LANES), tab),
            pl.BlockSpec((tm, LANES), tab), pl.BlockSpec((tm, LANES), tab),
        ],
        out_specs=[pl.BlockSpec((1, tm, w), row) for w in out_widths],
        out_shape=[jax.ShapeDtypeStruct((b, s, w), BF16) for w in out_widths],
        scratch_shapes=[pltpu.VMEM((tm, d), BF16)],
        compiler_params=pltpu.CompilerParams(
            dimension_semantics=("arbitrary", "arbitrary"),
            vmem_limit_bytes=VMEM_LIMIT),
        name="in_proj",
    )(x, sc, sh, g_pre, w_in_p, w_kv, g_q, g_k, g_ckv, cosa, sina, cosb, sinb)


def _attn_kernel(*refs, n_heads, dk, shared_kv, tq, n_blocks, n_side):
    q0_ref, q_refs = refs[0], refs[1:1 + n_blocks]
    k_ref, kn_ref, v_ref = refs[1 + n_blocks:4 + n_blocks]
    rest = refs[4 + n_blocks:]
    side_in = rest[:n_side]
    o_ref = rest[n_side]
    side_out = rest[n_side + 1:2 * n_side + 1]
    s_scr, m_scr = rest[2 * n_side + 1:]

    def scores(q_ref, keys_ref, slot):
        for hd in range(n_heads):
            kv_hd = 0 if shared_kv else hd
            k = keys_ref[0, :, kv_hd * dk:(kv_hd + 1) * dk]
            q = q_ref[0, :, hd * dk:(hd + 1) * dk]
            s = lax.dot_general(q, k, (((1,), (1,)), ((), ())), preferred_element_type=F32)
            s_scr[slot, hd] = s
            m_scr[slot, hd] = jnp.max(s, axis=-1, keepdims=True)

    def softmax_pv(slot, row0):
        for hd in range(n_heads):
            kv_hd = 0 if shared_kv else hd
            v = v_ref[0, :, kv_hd * V_DIM:(kv_hd + 1) * V_DIM]
            p = jnp.exp2(s_scr[slot, hd] - m_scr[slot, hd]).astype(BF16)
            v_ext = jnp.concatenate([v, jnp.ones_like(v)], axis=1)
            o_ext = jnp.dot(p, v_ext, preferred_element_type=F32)
            o = o_ext[:, :V_DIM] / o_ext[:, V_DIM:]
            o_ref[0, row0:row0 + tq, hd * V_DIM:(hd + 1) * V_DIM] = o.astype(BF16)

    @pl.when(pl.program_id(0) == 0)
    def _():
        scores(q0_ref, k_ref, 0)

    for i in range(n_blocks):
        keys_ref = kn_ref if i == n_blocks - 1 else k_ref
        scores(q_refs[i], keys_ref, (i + 1) % 2)
        for w_ref, wb_ref in zip(side_in, side_out):
            n = w_ref.shape[0] // n_blocks
            rows = slice(i * n, (i + 1) * n)
            if len(wb_ref.shape) == 2:
                wb_ref[rows, :] = w_ref[rows, :].astype(BF16)
            else:
                cb = wb_ref.shape[2]
                for f in range(wb_ref.shape[0]):
                    wb_ref[f, rows, :] = w_ref[rows, f * cb:(f + 1) * cb].astype(BF16)
        softmax_pv(i % 2, i * tq)


def _attention(q, k, v, side, *, n_groups, n_heads, dk, shared_kv, tq, n_blocks, name):
    b, s, _ = q.shape
    n_kv = 1 if shared_kv else n_heads
    n_blk = s // tq
    assert n_blocks % 2 == 0 and n_blk % n_blocks == 0
    n_total = b * n_groups * n_blk
    n_steps = n_total // n_blocks

    def q_map(t):
        return (t // (n_groups * n_blk), t % n_blk, (t // n_blk) % n_groups)

    def kv_map(t):
        return (t // (n_groups * n_blk), 0, (t // n_blk) % n_groups)

    def out_map(j):
        bi, i, g = q_map(n_blocks * j)
        return (bi, i // n_blocks, g)

    def ahead(i):
        return lambda j: q_map(jnp.minimum(n_blocks * j + i + 1, n_total - 1))

    q_spec = lambda f: pl.BlockSpec((1, tq, n_heads * dk), f)
    side_in_specs, side_out_specs, side_shapes = [], [], []
    for w, col_block in side:
        rows, cols = w.shape[0] // n_steps, w.shape[1]
        side_in_specs.append(pl.BlockSpec((rows, cols), lambda j: (j, 0)))
        if col_block is None:
            side_out_specs.append(pl.BlockSpec((rows, cols), lambda j: (j, 0)))
            side_shapes.append(jax.ShapeDtypeStruct(w.shape, BF16))
        else:
            n_cb = cols // col_block
            side_out_specs.append(
                pl.BlockSpec((n_cb, rows, col_block), lambda j: (0, j, 0)))
            side_shapes.append(jax.ShapeDtypeStruct((n_cb, w.shape[0], col_block), BF16))
    return pl.pallas_call(
        functools.partial(_attn_kernel, n_heads=n_heads, dk=dk, shared_kv=shared_kv,
                          tq=tq, n_blocks=n_blocks, n_side=len(side)),
        grid=(n_steps,),
        in_specs=[q_spec(lambda j: (0, 0, 0))]
        + [q_spec(ahead(i)) for i in range(n_blocks)]
        + [
            pl.BlockSpec((1, s, n_kv * dk), lambda j: kv_map(n_blocks * j)),
            pl.BlockSpec((1, s, n_kv * dk),
                         lambda j: kv_map(jnp.minimum(n_blocks * (j + 1), n_total - 1))),
            pl.BlockSpec((1, s, n_kv * V_DIM), lambda j: kv_map(n_blocks * j)),
        ] + side_in_specs,
        out_specs=[pl.BlockSpec((1, n_blocks * tq, n_heads * V_DIM), out_map)]
        + side_out_specs,
        out_shape=[jax.ShapeDtypeStruct((b, s, n_groups * n_heads * V_DIM), BF16)]
        + side_shapes,
        scratch_shapes=[pltpu.VMEM((2, n_heads, tq, s), F32),
                        pltpu.VMEM((2, n_heads, tq, 1), F32)],
        compiler_params=pltpu.CompilerParams(
            dimension_semantics=("arbitrary",), vmem_limit_bytes=VMEM_LIMIT),
        name=name,
    )(*([q] * (n_blocks + 1)), k, k, v, *[w for w, _ in side])


def _outproj_kernel(oa_ref, ob_ref, x_ref, gt_ref, ga_ref, gb_ref, w_ref, gpost_ref, o_ref,
                    n_scr, y_scr):
    n_rows = n_scr.shape[0]

    def store_o(rows, val):
        o_ref[0, rows, :] = val

    for rows in _row_chunks(n_rows):
        n_scr[rows, 0:MIX_A] = _rms(oa_ref[0, rows, :].astype(F32), ga_ref[...]).astype(BF16)
        n_scr[rows, MIX_A:] = _rms(ob_ref[0, rows, :].astype(F32), gb_ref[...]).astype(BF16)
    y_scr[...] = jnp.dot(n_scr[...], w_ref[...], preferred_element_type=F32)
    _gated_residual_rows(lambda rows: x_ref[0, rows, :], lambda rows: y_scr[rows, :],
                         store_o, n_rows, gpost_ref[...], gt_ref[0])


def _outproj(o_a, o_b, x, gt, g_a, g_b, w_out, g_post, tm):
    b, s, d = x.shape
    row = lambda bi, i: (bi, i, 0)
    per_b = lambda bi, i: (bi, 0, 0)
    const = lambda bi, i: (0, 0)
    return pl.pallas_call(
        _outproj_kernel,
        grid=(b, s // tm),
        in_specs=[
            pl.BlockSpec((1, tm, MIX_A), row),
            pl.BlockSpec((1, tm, MIX_B), row),
            pl.BlockSpec((1, tm, d), row),
            pl.BlockSpec((1, 1, d), per_b),
            pl.BlockSpec((1, MIX_A), const),
            pl.BlockSpec((1, MIX_B), const),
            pl.BlockSpec((MIX_A + MIX_B, d), const, pipeline_mode=pl.Buffered(1)),
            pl.BlockSpec((1, d), const),
        ],
        out_specs=pl.BlockSpec((1, tm, d), row),
        out_shape=jax.ShapeDtypeStruct((b, s, d), F32),
        scratch_shapes=[pltpu.VMEM((tm, MIX_A + MIX_B), BF16), pltpu.VMEM((tm, d), F32)],
        compiler_params=pltpu.CompilerParams(
            dimension_semantics=("arbitrary", "arbitrary"),
            vmem_limit_bytes=VMEM_LIMIT),
        name="out_proj",
    )(o_a, o_b, x, gt, g_a, g_b, w_out, g_post)


def _mlp_kernel(x_ref, sc_ref, sh_ref, gt_ref, gpre_ref, w1_ref, w2_ref, gpost_ref,
                o_ref, h_scr, acc_scr, *, n_split):
    f = pl.program_id(2)
    last = pl.num_programs(2) - 1
    n_rows = h_scr.shape[0]
    parts = [slice(r, r + n_rows // n_split) for r in range(0, n_rows, n_rows // n_split)]

    def store_h(rows, val):
        h_scr[rows, :] = val

    def store_o(rows, val):
        o_ref[0, rows, :] = val

    def ffn(rows):
        u = jnp.dot(h_scr[rows, :], w1_ref[0], preferred_element_type=F32)
        u = jnp.maximum(u, 0.0)
        return jnp.dot((u * u).astype(BF16), w2_ref[...], preferred_element_type=F32)

    @pl.when(f == 0)
    def _():
        for part in parts:
            _norm_mod_rows(lambda rows: x_ref[0, rows, :], store_h, part,
                           gpre_ref[...], sc_ref[0], sh_ref[0])
            acc_scr[part, :] = ffn(part)

    @pl.when((f > 0) & (f < last))
    def _():
        acc_scr[...] += ffn(slice(0, n_rows))

    @pl.when(f == last)
    def _():
        for part in parts:
            acc_scr[part, :] += ffn(part)
            _gated_residual_rows(lambda rows: x_ref[0, rows, :], lambda rows: acc_scr[rows, :],
                                 store_o, part, gpost_ref[...], gt_ref[0])


def _mlp(x, sc, sh, gt, g_pre, w1, w2, g_post, tm):
    b, s, d = x.shape
    n_f, _, tf = w1.shape
    row = lambda bi, i, f: (bi, i, 0)
    per_b = lambda bi, i, f: (bi, 0, 0)
    const = lambda bi, i, f: (0, 0)
    return pl.pallas_call(
        functools.partial(_mlp_kernel, n_split=2),
        grid=(b, s // tm, n_f),
        in_specs=[
            pl.BlockSpec((1, tm, d), row),
            pl.BlockSpec((1, 1, d), per_b),
            pl.BlockSpec((1, 1, d), per_b),
            pl.BlockSpec((1, 1, d), per_b),
            pl.BlockSpec((1, d), const),
            pl.BlockSpec((1, d, tf), lambda bi, i, f: (f, 0, 0)),
            pl.BlockSpec((tf, d), lambda bi, i, f: (f, 0)),
            pl.BlockSpec((1, d), const),
        ],
        out_specs=pl.BlockSpec((1, tm, d), row),
        out_shape=jax.ShapeDtypeStruct((b, s, d), F32),
        scratch_shapes=[pltpu.VMEM((tm, d), BF16), pltpu.VMEM((tm, d), F32)],
        compiler_params=pltpu.CompilerParams(
            dimension_semantics=("arbitrary", "arbitrary", "arbitrary"),
            vmem_limit_bytes=VMEM_LIMIT),
        name="mlp",
    )(x, sc, sh, gt, g_pre, w1, w2, g_post)


def _cast_pad_kernel(w_ref, o_ref):
    n = w_ref.shape[1]
    o_ref[:, :n] = w_ref[...].astype(BF16)
    if o_ref.shape[1] > n:
        o_ref[:, n:] = jnp.zeros((o_ref.shape[0], o_ref.shape[1] - n), BF16)


def _cast_pad(w, tr):
    r, n = w.shape
    n_pad = -(-n // LANES) * LANES
    return pl.pallas_call(
        _cast_pad_kernel,
        grid=(r // tr,),
        in_specs=[pl.BlockSpec((tr, n), lambda i: (i, 0))],
        out_specs=pl.BlockSpec((tr, n_pad), lambda i: (i, 0)),
        out_shape=jax.ShapeDtypeStruct((r, n_pad), BF16),
        compiler_params=pltpu.CompilerParams(
            dimension_semantics=("arbitrary",), vmem_limit_bytes=VMEM_LIMIT),
        name="cast_pad",
    )(w)


def kernel(x, c, w_ada, b_ada, g_pre_attn, w_in, g_q_a, g_k_a, g_ckv, w_kv_b, g_out_a,
           g_out_b, w_out, g_post_attn, g_pre_mlp, w_mlp_in, w_mlp_out, g_post_mlp):
    b, s, d = x.shape
    depth = w_ada.shape[0]
    tabs = _rope_tables(s, HEAD_DIM) + _rope_tables(s, QK_ROPE)
    c_pad = jnp.pad(c, ((0, 8 - b), (0, 0)))

    for l in range(depth):
        mod = _ada(c_pad, w_ada[l], b_ada[l][None, :])[:b]
        sh_a, sc_a, gt_a, sh_m, sc_m, gt_m = [
            m.reshape(b, 1, d) for m in jnp.split(mod, N_MOD, axis=-1)]

        q_a, k_a, v_a, q_b, k_b, v_b = _inproj(
            x, sc_a, sh_a, g_pre_attn[l][None, :], _cast_pad(w_in[l].T, tr=608),
            _cast_pad(w_kv_b[l], tr=256), g_q_a[l][None, :], g_k_a[l][None, :],
            g_ckv[l][None, :], tabs, tm=512)

        o_a, w1_bf, w2_bf = _attention(
            q_a, k_a, v_a, [(w_mlp_in[l], MLP_TF), (w_mlp_out[l], None)],
            n_groups=HKV, n_heads=G_A,
            dk=HEAD_DIM, shared_kv=True, tq=256, n_blocks=4, name="attn_gqa")
        o_b, wo_bf = _attention(
            q_b, k_b, v_b, [(w_out[l], None)], n_groups=HB // 4, n_heads=4,
            dk=QK_B_PAD, shared_kv=False, tq=256, n_blocks=4, name="attn_mla")

        x = _outproj(o_a, o_b, x, gt_a, g_out_a[l][None, :], g_out_b[l][None, :],
                     wo_bf, g_post_attn[l][None, :], tm=512)

        x = _mlp(x, sc_m, sh_m, gt_m, g_pre_mlp[l][None, :], w1_bf, w2_bf,
                 g_post_mlp[l][None, :], tm=512)
    return x
```

```python
import functools
import math

import numpy as np
import jax
import jax.numpy as jnp
from jax import lax
from jax.experimental import pallas as pl
from jax.experimental.pallas import tpu as pltpu

D_MODEL = 2048
GRID_W = 64
ROPE_THETA = 10000.0
EPS = 1e-6

HEAD_DIM = 128
HA = 8
HKV = 2
G_A = HA // HKV

HB = 8
QK_NOPE = 128
QK_ROPE = 64
V_DIM = 128
KV_RANK = 512
QK_B = QK_NOPE + QK_ROPE
QK_B_PAD = 256

W_QA = HA * HEAD_DIM
W_KA = HKV * HEAD_DIM
W_VA = HKV * HEAD_DIM
W_QB = HB * QK_B
MIX_A = HA * HEAD_DIM
MIX_B = HB * V_DIM
D_FF = 4 * D_MODEL
N_MOD = 6
MOD_SH_A, MOD_SC_A, MOD_GT_A, MOD_SH_M, MOD_SC_M, MOD_GT_M = range(N_MOD)

LOG2_E = math.log2(math.e)
LANES = 128
ROW_CHUNK = 16
MLP_TF = 1024
VMEM_LIMIT = 56 * 1024 * 1024

F32 = jnp.float32
BF16 = jnp.bfloat16


def _rope_tables(seq_len):
    parts = _rope_table(seq_len, HEAD_DIM) + _rope_table(seq_len, QK_ROPE)
    return jnp.asarray(np.concatenate(parts, axis=1), F32)


def _rope_table(seq_len, dim):
    pos = np.arange(seq_len)
    row = (pos // GRID_W).astype(np.float64)
    col = (pos % GRID_W).astype(np.float64)
    half = dim // 2
    inv = ROPE_THETA ** (-np.arange(0, half, 2, dtype=np.float64) / half)
    ang_r = row[:, None] * inv[None, :]
    ang_c = col[:, None] * inv[None, :]
    ang = np.concatenate([ang_r, ang_r, ang_c, ang_c], axis=-1)
    cos, sin = np.cos(ang), np.sin(ang)
    quarter = dim // 4
    sign = np.where((np.arange(dim) % half) < quarter, -1.0, 1.0)
    sin = sin * sign[None, :]
    reps = LANES // dim
    return [np.tile(cos, (1, reps)), np.tile(sin, (1, reps))]


def _mod_row(mod_ref, idx):
    return mod_ref[0, idx:idx + 1, :]


def _unit_rms(x):
    ms = jnp.mean(x * x, axis=-1, keepdims=True)
    return x * lax.rsqrt(ms + EPS)


def _rms(x, g):
    return _unit_rms(x) * g


def _row_chunks(span):
    span = span if isinstance(span, slice) else slice(0, span)
    return [slice(r, r + ROW_CHUNK) for r in range(span.start, span.stop, ROW_CHUNK)]


def _norm_mod_rows(load, store, span, g, sc, sh):
    gain = g * (1.0 + sc)
    for rows in _row_chunks(span):
        store(rows, (_unit_rms(load(rows)) * gain + sh).astype(BF16))


def _gated_residual_rows(load_x, load_y, store, span, g, gt):
    gain = gt * g
    for rows in _row_chunks(span):
        store(rows, load_x(rows) + _unit_rms(load_y(rows)) * gain)


def _dot_nt(a, w_rows):
    return lax.dot_general(a, w_rows, (((1,), (1,)), ((), ())), preferred_element_type=F32)


def _rope(x, cos, sin_signed, quarter):
    lane = lax.broadcasted_iota(jnp.int32, x.shape, 1)
    take_up = (lane % (2 * quarter)) < quarter
    up = pltpu.roll(x, LANES - quarter, axis=1)
    down = pltpu.roll(x, quarter, axis=1)
    rot = jnp.where(take_up, up, down)
    return x * cos + rot * sin_signed


def _ada_kernel(c_ref, w_ref, b_ref, o_ref):
    c = c_ref[...]
    c_act = (c * jax.nn.sigmoid(c)).astype(BF16)
    w = w_ref[...].astype(BF16)
    o_ref[...] = jnp.dot(c_act, w, preferred_element_type=F32) + b_ref[...]


def _ada(c_pad, w_ada, b_ada):
    m, d = c_pad.shape
    n = w_ada.shape[1]
    tn = 1024
    return pl.pallas_call(
        _ada_kernel,
        grid=(n // tn,),
        in_specs=[
            pl.BlockSpec((m, d), lambda j: (0, 0)),
            pl.BlockSpec((d, tn), lambda j: (0, j)),
            pl.BlockSpec((1, tn), lambda j: (0, j)),
        ],
        out_specs=pl.BlockSpec((m, tn), lambda j: (0, j)),
        out_shape=jax.ShapeDtypeStruct((m, n), F32),
        compiler_params=pltpu.CompilerParams(
            dimension_semantics=("arbitrary",), vmem_limit_bytes=VMEM_LIMIT),
        name="ada_mod",
    )(c_pad, w_ada, b_ada)


def _inproj_kernel(x_ref, mod_ref, gpre_ref, w_ref, wkv_ref, gq_ref, gk_ref,
                   gckv_ref, tab_ref,
                   qa_ref, ka_ref, va_ref, qb_ref, kb_ref, vb_ref, h_scr):
    tm = h_scr.shape[0]

    def store_h(rows, val):
        h_scr[rows, :] = val

    _norm_mod_rows(lambda rows: x_ref[0, rows, :], store_h, tm,
                   gpre_ref[...], _mod_row(mod_ref, MOD_SC_A), _mod_row(mod_ref, MOD_SH_A))
    hb = h_scr[...]
    cosa, sina, cosb, sinb = (tab_ref[:, i * LANES:(i + 1) * LANES] for i in range(4))
    scale_a = LOG2_E / math.sqrt(HEAD_DIM)
    scale_b = LOG2_E / math.sqrt(QK_B)

    qa = _dot_nt(hb, w_ref[0:W_QA, :])
    gq = gq_ref[...] * scale_a
    for hd in range(HA):
        sl = slice(hd * HEAD_DIM, (hd + 1) * HEAD_DIM)
        q = _rope(_rms(qa[:, sl], gq), cosa, sina, HEAD_DIM // 4)
        qa_ref[0, :, sl] = q.astype(BF16)

    kva = _dot_nt(hb, w_ref[W_QA:W_QA + W_KA + W_VA, :])
    for hd in range(HKV):
        sl = slice(hd * HEAD_DIM, (hd + 1) * HEAD_DIM)
        k = _rope(_rms(kva[:, sl], gk_ref[...]), cosa, sina, HEAD_DIM // 4)
        ka_ref[0, :, sl] = k.astype(BF16)
    va_ref[0] = kva[:, W_KA:].astype(BF16)

    o_qb = W_QA + W_KA + W_VA
    qb = _dot_nt(hb, w_ref[o_qb:o_qb + W_QB, :])
    lane = lax.broadcasted_iota(jnp.int32, (tm, LANES), 1)
    low = lane < QK_ROPE
    swap = lambda t: pltpu.roll(t, QK_ROPE, axis=1)
    for pair in range(HB // 2):
        t0, t1, t2 = (qb[:, (3 * pair + i) * LANES:(3 * pair + i + 1) * LANES] for i in range(3))
        r1 = _rope(t1, cosb, sinb, QK_ROPE // 4)
        r2 = _rope(t2, cosb, sinb, QK_ROPE // 4)
        heads = (
            (2 * pair, t0, jnp.where(low, r1, 0.0)),
            (2 * pair + 1, jnp.where(low, swap(t1), swap(t2)), jnp.where(low, swap(r2), 0.0)),
        )
        for hd, nope, pe in heads:
            base = hd * QK_B_PAD
            qb_ref[0, :, base:base + QK_NOPE] = (nope * scale_b).astype(BF16)
            qb_ref[0, :, base + QK_NOPE:base + QK_B_PAD] = (pe * scale_b).astype(BF16)

    o_ckv = o_qb + W_QB
    w_rest = jnp.concatenate(
        [w_ref[o_ckv:, :], jnp.zeros((LANES - QK_ROPE, w_ref.shape[1]), BF16)], axis=0)
    rest = _dot_nt(hb, w_rest)
    ckv = _rms(rest[:, :KV_RANK], gckv_ref[...]).astype(BF16)
    kv = jnp.dot(ckv, wkv_ref[...], preferred_element_type=F32)
    kpe = _rope(rest[:, KV_RANK:], cosb, sinb, QK_ROPE // 4).astype(BF16)
    for hd in range(HB):
        base = hd * (QK_NOPE + V_DIM)
        kb_ref[0, :, hd * QK_B_PAD:hd * QK_B_PAD + QK_NOPE] = (
            kv[:, base:base + QK_NOPE].astype(BF16))
        kb_ref[0, :, hd * QK_B_PAD + QK_NOPE:(hd + 1) * QK_B_PAD] = kpe
        vb_ref[0, :, hd * V_DIM:(hd + 1) * V_DIM] = (
            kv[:, base + QK_NOPE:base + QK_NOPE + V_DIM].astype(BF16))


def _inproj(x, mod, g_pre, w_in_p, w_kv, g_q, g_k, g_ckv, tabs, tm):
    b, s, d = x.shape
    n_in = w_in_p.shape[0]
    row = lambda bi, i: (bi, i, 0)
    per_b = lambda bi, i: (bi, 0, 0)
    const = lambda bi, i: (0, 0)
    tab = lambda bi, i: (i, 0)
    vec = lambda n: pl.BlockSpec((1, n), const)
    out_widths = (MIX_A, W_KA, W_VA, HB * QK_B_PAD, HB * QK_B_PAD, MIX_B)
    return pl.pallas_call(
        _inproj_kernel,
        grid=(b, s // tm),
        in_specs=[
            pl.BlockSpec((1, tm, d), row),
            pl.BlockSpec((1, N_MOD, d), per_b),
            vec(d),
            pl.BlockSpec((n_in, d), const, pipeline_mode=pl.Buffered(1)),
            pl.BlockSpec((KV_RANK, HB * (QK_NOPE + V_DIM)), const,
                         pipeline_mode=pl.Buffered(1)),
            vec(HEAD_DIM), vec(HEAD_DIM), vec(KV_RANK),
            pl.BlockSpec((tm, 4 * LANES), tab),
        ],
        out_specs=[pl.BlockSpec((1, tm, w), row) for w in out_widths],
        out_shape=[jax.ShapeDtypeStruct((b, s, w), BF16) for w in out_widths],
        scratch_shapes=[pltpu.VMEM((tm, d), BF16)],
        compiler_params=pltpu.CompilerParams(
            dimension_semantics=("arbitrary", "arbitrary"),
            vmem_limit_bytes=VMEM_LIMIT),
        name="in_proj",
    )(x, mod, g_pre, w_in_p, w_kv, g_q, g_k, g_ckv, tabs)


def _attn_kernel(*refs, n_heads, dk, shared_kv, tq, n_blocks, n_side):
    q0_ref, q_refs = refs[0], refs[1:1 + n_blocks]
    k_ref, kn_ref, v_ref = refs[1 + n_blocks:4 + n_blocks]
    rest = refs[4 + n_blocks:]
    side_in = rest[:n_side]
    o_ref = rest[n_side]
    side_out = rest[n_side + 1:2 * n_side + 1]
    s_scr, m_scr = rest[2 * n_side + 1:]

    def scores(q_ref, keys_ref, slot):
        for hd in range(n_heads):
            kv_hd = 0 if shared_kv else hd
            k = keys_ref[0, :, kv_hd * dk:(kv_hd + 1) * dk]
            q = q_ref[0, :, hd * dk:(hd + 1) * dk]
            s = lax.dot_general(q, k, (((1,), (1,)), ((), ())), preferred_element_type=F32)
            s_scr[slot, hd] = s
            m_scr[slot, hd] = jnp.max(s, axis=-1, keepdims=True)

    def softmax_pv(slot, row0):
        for hd in range(n_heads):
            kv_hd = 0 if shared_kv else hd
            v = v_ref[0, :, kv_hd * V_DIM:(kv_hd + 1) * V_DIM]
            p = jnp.exp2(s_scr[slot, hd] - m_scr[slot, hd]).astype(BF16)
            v_ext = jnp.concatenate([v, jnp.ones_like(v)], axis=1)
            o_ext = jnp.dot(p, v_ext, preferred_element_type=F32)
            o = o_ext[:, :V_DIM] / o_ext[:, V_DIM:]
            o_ref[0, row0:row0 + tq, hd * V_DIM:(hd + 1) * V_DIM] = o.astype(BF16)

    @pl.when(pl.program_id(0) == 0)
    def _():
        scores(q0_ref, k_ref, 0)

    for i in range(n_blocks):
        keys_ref = kn_ref if i == n_blocks - 1 else k_ref
        scores(q_refs[i], keys_ref, (i + 1) % 2)
        for w_ref, wb_ref in zip(side_in, side_out):
            n = w_ref.shape[0] // n_blocks
            rows = slice(i * n, (i + 1) * n)
            if len(wb_ref.shape) == 2:
                wb_ref[rows, :] = w_ref[rows, :].astype(BF16)
            else:
                cb = wb_ref.shape[2]
                for f in range(wb_ref.shape[0]):
                    wb_ref[f, rows, :] = w_ref[rows, f * cb:(f + 1) * cb].astype(BF16)
        softmax_pv(i % 2, i * tq)


def _attention(q, k, v, side, *, n_groups, n_heads, dk, shared_kv, tq, n_blocks, name):
    b, s, _ = q.shape
    n_kv = 1 if shared_kv else n_heads
    n_blk = s // tq
    assert n_blocks % 2 == 0 and n_blk % n_blocks == 0
    n_total = b * n_groups * n_blk
    n_steps = n_total // n_blocks

    def q_map(t):
        return (t // (n_groups * n_blk), t % n_blk, (t // n_blk) % n_groups)

    def kv_map(t):
        return (t // (n_groups * n_blk), 0, (t // n_blk) % n_groups)

    def out_map(j):
        bi, i, g = q_map(n_blocks * j)
        return (bi, i // n_blocks, g)

    def ahead(i):
        return lambda j: q_map(jnp.minimum(n_blocks * j + i + 1, n_total - 1))

    q_spec = lambda f: pl.BlockSpec((1, tq, n_heads * dk), f)
    side_in_specs, side_out_specs, side_shapes = [], [], []
    for w, col_block in side:
        rows, cols = w.shape[0] // n_steps, w.shape[1]
        side_in_specs.append(pl.BlockSpec((rows, cols), lambda j: (j, 0)))
        if col_block is None:
            side_out_specs.append(pl.BlockSpec((rows, cols), lambda j: (j, 0)))
            side_shapes.append(jax.ShapeDtypeStruct(w.shape, BF16))
        else:
            n_cb = cols // col_block
            side_out_specs.append(
                pl.BlockSpec((n_cb, rows, col_block), lambda j: (0, j, 0)))
            side_shapes.append(jax.ShapeDtypeStruct((n_cb, w.shape[0], col_block), BF16))
    return pl.pallas_call(
        functools.partial(_attn_kernel, n_heads=n_heads, dk=dk, shared_kv=shared_kv,
                          tq=tq, n_blocks=n_blocks, n_side=len(side)),
        grid=(n_steps,),
        in_specs=[q_spec(lambda j: (0, 0, 0))]
        + [q_spec(ahead(i)) for i in range(n_blocks)]
        + [
            pl.BlockSpec((1, s, n_kv * dk), lambda j: kv_map(n_blocks * j)),
            pl.BlockSpec((1, s, n_kv * dk),
                         lambda j: kv_map(jnp.minimum(n_blocks * (j + 1), n_total - 1))),
            pl.BlockSpec((1, s, n_kv * V_DIM), lambda j: kv_map(n_blocks * j)),
        ] + side_in_specs,
        out_specs=[pl.BlockSpec((1, n_blocks * tq, n_heads * V_DIM), out_map)]
        + side_out_specs,
        out_shape=[jax.ShapeDtypeStruct((b, s, n_groups * n_heads * V_DIM), BF16)]
        + side_shapes,
        scratch_shapes=[pltpu.VMEM((2, n_heads, tq, s), F32),
                        pltpu.VMEM((2, n_heads, tq, 1), F32)],
        compiler_params=pltpu.CompilerParams(
            dimension_semantics=("arbitrary",), vmem_limit_bytes=VMEM_LIMIT),
        name=name,
    )(*([q] * (n_blocks + 1)), k, k, v, *[w for w, _ in side])


def _outproj_kernel(oa_ref, ob_ref, x_ref, mod_ref, ga_ref, gb_ref, w_ref, gpost_ref, o_ref,
                    n_scr, y_scr):
    n_rows = n_scr.shape[0]

    def store_o(rows, val):
        o_ref[0, rows, :] = val

    for rows in _row_chunks(n_rows):
        n_scr[rows, 0:MIX_A] = _rms(oa_ref[0, rows, :].astype(F32), ga_ref[...]).astype(BF16)
        n_scr[rows, MIX_A:] = _rms(ob_ref[0, rows, :].astype(F32), gb_ref[...]).astype(BF16)
    y_scr[...] = jnp.dot(n_scr[...], w_ref[...], preferred_element_type=F32)
    _gated_residual_rows(lambda rows: x_ref[0, rows, :], lambda rows: y_scr[rows, :],
                         store_o, n_rows, gpost_ref[...], _mod_row(mod_ref, MOD_GT_A))


def _outproj(o_a, o_b, x, mod, g_a, g_b, w_out, g_post, tm):
    b, s, d = x.shape
    row = lambda bi, i: (bi, i, 0)
    per_b = lambda bi, i: (bi, 0, 0)
    const = lambda bi, i: (0, 0)
    return pl.pallas_call(
        _outproj_kernel,
        grid=(b, s // tm),
        in_specs=[
            pl.BlockSpec((1, tm, MIX_A), row),
            pl.BlockSpec((1, tm, MIX_B), row),
            pl.BlockSpec((1, tm, d), row),
            pl.BlockSpec((1, N_MOD, d), per_b),
            pl.BlockSpec((1, MIX_A), const),
            pl.BlockSpec((1, MIX_B), const),
            pl.BlockSpec((MIX_A + MIX_B, d), const, pipeline_mode=pl.Buffered(1)),
            pl.BlockSpec((1, d), const),
        ],
        out_specs=pl.BlockSpec((1, tm, d), row),
        out_shape=jax.ShapeDtypeStruct((b, s, d), F32),
        scratch_shapes=[pltpu.VMEM((tm, MIX_A + MIX_B), BF16), pltpu.VMEM((tm, d), F32)],
        compiler_params=pltpu.CompilerParams(
            dimension_semantics=("arbitrary", "arbitrary"),
            vmem_limit_bytes=VMEM_LIMIT),
        name="out_proj",
    )(o_a, o_b, x, mod, g_a, g_b, w_out, g_post)


def _mlp_kernel(x_ref, mod_ref, gpre_ref, w1_ref, w2_ref, gpost_ref,
                o_ref, h_scr, acc_scr, *, n_split):
    f = pl.program_id(2)
    last = pl.num_programs(2) - 1
    n_rows = h_scr.shape[0]
    parts = [slice(r, r + n_rows // n_split) for r in range(0, n_rows, n_rows // n_split)]

    def store_h(rows, val):
        h_scr[rows, :] = val

    def store_o(rows, val):
        o_ref[0, rows, :] = val

    def ffn(rows):
        u = jnp.dot(h_scr[rows, :], w1_ref[0], preferred_element_type=F32)
        u = jnp.maximum(u, 0.0)
        return jnp.dot((u * u).astype(BF16), w2_ref[...], preferred_element_type=F32)

    @pl.when(f == 0)
    def _():
        for part in parts:
            _norm_mod_rows(lambda rows: x_ref[0, rows, :], store_h, part,
                           gpre_ref[...], _mod_row(mod_ref, MOD_SC_M),
                           _mod_row(mod_ref, MOD_SH_M))
            acc_scr[part, :] = ffn(part)

    @pl.when((f > 0) & (f < last))
    def _():
        acc_scr[...] += ffn(slice(0, n_rows))

    @pl.when(f == last)
    def _():
        for part in parts:
            acc_scr[part, :] += ffn(part)
            _gated_residual_rows(lambda rows: x_ref[0, rows, :], lambda rows: acc_scr[rows, :],
                                 store_o, part, gpost_ref[...], _mod_row(mod_ref, MOD_GT_M))


def _mlp(x, mod, g_pre, w1, w2, g_post, tm):
    b, s, d = x.shape
    n_f, _, tf = w1.shape
    row = lambda bi, i, f: (bi, i, 0)
    per_b = lambda bi, i, f: (bi, 0, 0)
    const = lambda bi, i, f: (0, 0)
    return pl.pallas_call(
        functools.partial(_mlp_kernel, n_split=2),
        grid=(b, s // tm, n_f),
        in_specs=[
            pl.BlockSpec((1, tm, d), row),
            pl.BlockSpec((1, N_MOD, d), per_b),
            pl.BlockSpec((1, d), const),
            pl.BlockSpec((1, d, tf), lambda bi, i, f: (f, 0, 0)),
            pl.BlockSpec((tf, d), lambda bi, i, f: (f, 0)),
            pl.BlockSpec((1, d), const),
        ],
        out_specs=pl.BlockSpec((1, tm, d), row),
        out_shape=jax.ShapeDtypeStruct((b, s, d), F32),
        scratch_shapes=[pltpu.VMEM((tm, d), BF16), pltpu.VMEM((tm, d), F32)],
        compiler_params=pltpu.CompilerParams(
            dimension_semantics=("arbitrary", "arbitrary", "arbitrary"),
            vmem_limit_bytes=VMEM_LIMIT),
        name="mlp",
    )(x, mod, g_pre, w1, w2, g_post)


def _cast_pad_kernel(w_ref, o_ref):
    n = w_ref.shape[1]
    o_ref[:, :n] = w_ref[...].astype(BF16)
    if o_ref.shape[1] > n:
        o_ref[:, n:] = jnp.zeros((o_ref.shape[0], o_ref.shape[1] - n), BF16)


def _cast_pad(w, tr):
    r, n = w.shape
    n_pad = -(-n // LANES) * LANES
    return pl.pallas_call(
        _cast_pad_kernel,
        grid=(r // tr,),
        in_specs=[pl.BlockSpec((tr, n), lambda i: (i, 0))],
        out_specs=pl.BlockSpec((tr, n_pad), lambda i: (i, 0)),
        out_shape=jax.ShapeDtypeStruct((r, n_pad), BF16),
        compiler_params=pltpu.CompilerParams(
            dimension_semantics=("arbitrary",), vmem_limit_bytes=VMEM_LIMIT),
        name="cast_pad",
    )(w)


def kernel(x, c, w_ada, b_ada, g_pre_attn, w_in, g_q_a, g_k_a, g_ckv, w_kv_b, g_out_a,
           g_out_b, w_out, g_post_attn, g_pre_mlp, w_mlp_in, w_mlp_out, g_post_mlp):
    b, s, d = x.shape
    depth = w_ada.shape[0]
    tabs = _rope_tables(s)
    c_pad = jnp.pad(c, ((0, 8 - b), (0, 0)))

    for l in range(depth):
        mod = _ada(c_pad, w_ada[l], b_ada[l][None, :]).reshape(8, N_MOD, d)

        q_a, k_a, v_a, q_b, k_b, v_b = _inproj(
            x, mod, g_pre_attn[l][None, :], _cast_pad(w_in[l].T, tr=608),
            _cast_pad(w_kv_b[l], tr=256), g_q_a[l][None, :], g_k_a[l][None, :],
            g_ckv[l][None, :], tabs, tm=512)

        o_a, w1_bf, w2_bf = _attention(
            q_a, k_a, v_a, [(w_mlp_in[l], MLP_TF), (w_mlp_out[l], None)],
            n_groups=HKV, n_heads=G_A,
            dk=HEAD_DIM, shared_kv=True, tq=256, n_blocks=4, name="attn_gqa")
        o_b, wo_bf = _attention(
            q_b, k_b, v_b, [(w_out[l], None)], n_groups=HB // 4, n_heads=4,
            dk=QK_B_PAD, shared_kv=False, tq=256, n_blocks=4, name="attn_mla")

        x = _outproj(o_a, o_b, x, mod, g_out_a[l][None, :], g_out_b[l][None, :],
                     wo_bf, g_post_attn[l][None, :], tm=512)

        x = _mlp(x, mod, g_pre_mlp[l][None, :], w1_bf, w2_bf,
                 g_post_mlp[l][None, :], tm=512)
    return x
```

```python
import functools
import math

import numpy as np
import jax
import jax.numpy as jnp
from jax import lax
from jax.experimental import pallas as pl
from jax.experimental.pallas import tpu as pltpu

D_MODEL = 2048
GRID_W = 64
ROPE_THETA = 10000.0
EPS = 1e-6

HEAD_DIM = 128
HA = 8
HKV = 2
G_A = HA // HKV

HB = 8
QK_NOPE = 128
QK_ROPE = 64
V_DIM = 128
KV_RANK = 512
QK_B = QK_NOPE + QK_ROPE
QK_B_PAD = 256

W_QA = HA * HEAD_DIM
W_KA = HKV * HEAD_DIM
W_VA = HKV * HEAD_DIM
W_QB = HB * QK_B
MIX_A = HA * HEAD_DIM
MIX_B = HB * V_DIM
D_FF = 4 * D_MODEL
N_MOD = 6
MOD_SH_A, MOD_SC_A, MOD_GT_A, MOD_SH_M, MOD_SC_M, MOD_GT_M = range(N_MOD)

LOG2_E = math.log2(math.e)
LANES = 128
ROW_CHUNK = 16
MLP_TF = 2048
MLP_SUB = 1024
VMEM_LIMIT = 56 * 1024 * 1024
VMEM_LIMIT_MLP = 62 * 1024 * 1024

F32 = jnp.float32
BF16 = jnp.bfloat16


def _rope_tables(seq_len):
    parts = _rope_table(seq_len, HEAD_DIM) + _rope_table(seq_len, QK_ROPE)
    return jnp.asarray(np.concatenate(parts, axis=1), F32)


def _rope_table(seq_len, dim):
    pos = np.arange(seq_len)
    row = (pos // GRID_W).astype(np.float64)
    col = (pos % GRID_W).astype(np.float64)
    half = dim // 2
    inv = ROPE_THETA ** (-np.arange(0, half, 2, dtype=np.float64) / half)
    ang_r = row[:, None] * inv[None, :]
    ang_c = col[:, None] * inv[None, :]
    ang = np.concatenate([ang_r, ang_r, ang_c, ang_c], axis=-1)
    cos, sin = np.cos(ang), np.sin(ang)
    quarter = dim // 4
    sign = np.where((np.arange(dim) % half) < quarter, -1.0, 1.0)
    sin = sin * sign[None, :]
    reps = LANES // dim
    return [np.tile(cos, (1, reps)), np.tile(sin, (1, reps))]


def _mod_row(mod_ref, idx):
    return mod_ref[0, idx:idx + 1, :]


def _unit_rms(x):
    ms = jnp.mean(x * x, axis=-1, keepdims=True)
    return x * lax.rsqrt(ms + EPS)


def _rms(x, g):
    return _unit_rms(x) * g


def _row_chunks(span):
    span = span if isinstance(span, slice) else slice(0, span)
    return [slice(r, r + ROW_CHUNK) for r in range(span.start, span.stop, ROW_CHUNK)]


def _norm_mod_rows(load, store, span, g, sc, sh):
    gain = g * (1.0 + sc)
    for rows in _row_chunks(span):
        store(rows, (_unit_rms(load(rows)) * gain + sh).astype(BF16))


def _gated_residual_rows(load_x, load_y, store, span, g, gt):
    gain = gt * g
    for rows in _row_chunks(span):
        store(rows, load_x(rows) + _unit_rms(load_y(rows)) * gain)


def _dot_nt(a, w_rows):
    return lax.dot_general(a, w_rows, (((1,), (1,)), ((), ())), preferred_element_type=F32)


def _rope(x, cos, sin_signed, quarter):
    lane = lax.broadcasted_iota(jnp.int32, x.shape, 1)
    take_up = (lane % (2 * quarter)) < quarter
    up = pltpu.roll(x, LANES - quarter, axis=1)
    down = pltpu.roll(x, quarter, axis=1)
    rot = jnp.where(take_up, up, down)
    return x * cos + rot * sin_signed


def _ada_kernel(c_ref, w_ref, b_ref, o_ref):
    c = c_ref[...]
    c_act = (c * jax.nn.sigmoid(c)).astype(BF16)
    w = w_ref[...].astype(BF16)
    o_ref[...] = jnp.dot(c_act, w, preferred_element_type=F32) + b_ref[...]


def _ada(c_pad, w_ada, b_ada):
    m, d = c_pad.shape
    n = w_ada.shape[1]
    tn = 1024
    return pl.pallas_call(
        _ada_kernel,
        grid=(n // tn,),
        in_specs=[
            pl.BlockSpec((m, d), lambda j: (0, 0)),
            pl.BlockSpec((d, tn), lambda j: (0, j)),
            pl.BlockSpec((1, tn), lambda j: (0, j)),
        ],
        out_specs=pl.BlockSpec((m, tn), lambda j: (0, j)),
        out_shape=jax.ShapeDtypeStruct((m, n), F32),
        compiler_params=pltpu.CompilerParams(
            dimension_semantics=("arbitrary",), vmem_limit_bytes=VMEM_LIMIT),
        name="ada_mod",
    )(c_pad, w_ada, b_ada)


def _inproj_kernel(x_ref, mod_ref, gpre_ref, w_ref, wkv_ref, gq_ref, gk_ref,
                   gckv_ref, tab_ref,
                   qa_ref, ka_ref, va_ref, qb_ref, kb_ref, vb_ref, h_scr):
    tm = h_scr.shape[0]

    def store_h(rows, val):
        h_scr[rows, :] = val

    _norm_mod_rows(lambda rows: x_ref[0, rows, :], store_h, tm,
                   gpre_ref[...], _mod_row(mod_ref, MOD_SC_A), _mod_row(mod_ref, MOD_SH_A))
    hb = h_scr[...]
    cosa, sina, cosb, sinb = (tab_ref[:, i * LANES:(i + 1) * LANES] for i in range(4))
    scale_a = LOG2_E / math.sqrt(HEAD_DIM)
    scale_b = LOG2_E / math.sqrt(QK_B)

    qa = _dot_nt(hb, w_ref[0:W_QA, :])
    gq = gq_ref[...] * scale_a
    for hd in range(HA):
        sl = slice(hd * HEAD_DIM, (hd + 1) * HEAD_DIM)
        q = _rope(_rms(qa[:, sl], gq), cosa, sina, HEAD_DIM // 4)
        qa_ref[0, :, sl] = q.astype(BF16)

    kva = _dot_nt(hb, w_ref[W_QA:W_QA + W_KA + W_VA, :])
    for hd in range(HKV):
        sl = slice(hd * HEAD_DIM, (hd + 1) * HEAD_DIM)
        k = _rope(_rms(kva[:, sl], gk_ref[...]), cosa, sina, HEAD_DIM // 4)
        ka_ref[0, :, sl] = k.astype(BF16)
    va_ref[0] = kva[:, W_KA:].astype(BF16)

    o_qb = W_QA + W_KA + W_VA
    qb = _dot_nt(hb, w_ref[o_qb:o_qb + W_QB, :])
    lane = lax.broadcasted_iota(jnp.int32, (tm, LANES), 1)
    low = lane < QK_ROPE
    swap = lambda t: pltpu.roll(t, QK_ROPE, axis=1)
    for pair in range(HB // 2):
        t0, t1, t2 = (qb[:, (3 * pair + i) * LANES:(3 * pair + i + 1) * LANES] for i in range(3))
        r1 = _rope(t1, cosb, sinb, QK_ROPE // 4)
        r2 = _rope(t2, cosb, sinb, QK_ROPE // 4)
        heads = (
            (2 * pair, t0, jnp.where(low, r1, 0.0)),
            (2 * pair + 1, jnp.where(low, swap(t1), swap(t2)), jnp.where(low, swap(r2), 0.0)),
        )
        for hd, nope, pe in heads:
            base = hd * QK_B_PAD
            qb_ref[0, :, base:base + QK_NOPE] = (nope * scale_b).astype(BF16)
            qb_ref[0, :, base + QK_NOPE:base + QK_B_PAD] = (pe * scale_b).astype(BF16)

    o_ckv = o_qb + W_QB
    w_rest = jnp.concatenate(
        [w_ref[o_ckv:, :], jnp.zeros((LANES - QK_ROPE, w_ref.shape[1]), BF16)], axis=0)
    rest = _dot_nt(hb, w_rest)
    ckv = _rms(rest[:, :KV_RANK], gckv_ref[...]).astype(BF16)
    kv = jnp.dot(ckv, wkv_ref[...], preferred_element_type=F32)
    kpe = _rope(rest[:, KV_RANK:], cosb, sinb, QK_ROPE // 4).astype(BF16)
    for hd in range(HB):
        base = hd * (QK_NOPE + V_DIM)
        kb_ref[0, :, hd * QK_B_PAD:hd * QK_B_PAD + QK_NOPE] = (
            kv[:, base:base + QK_NOPE].astype(BF16))
        kb_ref[0, :, hd * QK_B_PAD + QK_NOPE:(hd + 1) * QK_B_PAD] = kpe
        vb_ref[0, :, hd * V_DIM:(hd + 1) * V_DIM] = (
            kv[:, base + QK_NOPE:base + QK_NOPE + V_DIM].astype(BF16))


def _inproj(x, mod, g_pre, w_in_p, w_kv, g_q, g_k, g_ckv, tabs, tm):
    b, s, d = x.shape
    n_in = w_in_p.shape[0]
    row = lambda bi, i: (bi, i, 0)
    per_b = lambda bi, i: (bi, 0, 0)
    const = lambda bi, i: (0, 0)
    tab = lambda bi, i: (i, 0)
    vec = lambda n: pl.BlockSpec((1, n), const)
    out_widths = (MIX_A, W_KA, W_VA, HB * QK_B_PAD, HB * QK_B_PAD, MIX_B)
    return pl.pallas_call(
        _inproj_kernel,
        grid=(b, s // tm),
        in_specs=[
            pl.BlockSpec((1, tm, d), row),
            pl.BlockSpec((1, N_MOD, d), per_b),
            vec(d),
            pl.BlockSpec((n_in, d), const, pipeline_mode=pl.Buffered(1)),
            pl.BlockSpec((KV_RANK, HB * (QK_NOPE + V_DIM)), const,
                         pipeline_mode=pl.Buffered(1)),
            vec(HEAD_DIM), vec(HEAD_DIM), vec(KV_RANK),
            pl.BlockSpec((tm, 4 * LANES), tab),
        ],
        out_specs=[pl.BlockSpec((1, tm, w), row) for w in out_widths],
        out_shape=[jax.ShapeDtypeStruct((b, s, w), BF16) for w in out_widths],
        scratch_shapes=[pltpu.VMEM((tm, d), BF16)],
        compiler_params=pltpu.CompilerParams(
            dimension_semantics=("arbitrary", "arbitrary"),
            vmem_limit_bytes=VMEM_LIMIT),
        name="in_proj",
    )(x, mod, g_pre, w_in_p, w_kv, g_q, g_k, g_ckv, tabs)


def _attn_kernel(*refs, n_heads, dk, shared_kv, tq, n_blocks, n_side):
    q0_ref, q_refs = refs[0], refs[1:1 + n_blocks]
    k_ref, kn_ref, v_ref = refs[1 + n_blocks:4 + n_blocks]
    rest = refs[4 + n_blocks:]
    side_in = rest[:n_side]
    o_ref = rest[n_side]
    side_out = rest[n_side + 1:2 * n_side + 1]
    s_scr, m_scr = rest[2 * n_side + 1:]

    def scores(q_ref, keys_ref, slot):
        for hd in range(n_heads):
            kv_hd = 0 if shared_kv else hd
            k = keys_ref[0, :, kv_hd * dk:(kv_hd + 1) * dk]
            q = q_ref[0, :, hd * dk:(hd + 1) * dk]
            s = lax.dot_general(q, k, (((1,), (1,)), ((), ())), preferred_element_type=F32)
            s_scr[slot, hd] = s
            m_scr[slot, hd] = jnp.max(s, axis=-1, keepdims=True)

    def softmax_pv(slot, row0):
        for hd in range(n_heads):
            kv_hd = 0 if shared_kv else hd
            v = v_ref[0, :, kv_hd * V_DIM:(kv_hd + 1) * V_DIM]
            p = jnp.exp2(s_scr[slot, hd] - m_scr[slot, hd]).astype(BF16)
            v_ext = jnp.concatenate([v, jnp.ones_like(v)], axis=1)
            o_ext = jnp.dot(p, v_ext, preferred_element_type=F32)
            o = o_ext[:, :V_DIM] / o_ext[:, V_DIM:]
            o_ref[0, row0:row0 + tq, hd * V_DIM:(hd + 1) * V_DIM] = o.astype(BF16)

    @pl.when(pl.program_id(0) == 0)
    def _():
        scores(q0_ref, k_ref, 0)

    for i in range(n_blocks):
        keys_ref = kn_ref if i == n_blocks - 1 else k_ref
        scores(q_refs[i], keys_ref, (i + 1) % 2)
        for w_ref, wb_ref in zip(side_in, side_out):
            n = w_ref.shape[0] // n_blocks
            rows = slice(i * n, (i + 1) * n)
            if len(wb_ref.shape) == 2:
                wb_ref[rows, :] = w_ref[rows, :].astype(BF16)
            else:
                cb = wb_ref.shape[2]
                for f in range(wb_ref.shape[0]):
                    wb_ref[f, rows, :] = w_ref[rows, f * cb:(f + 1) * cb].astype(BF16)
        softmax_pv(i % 2, i * tq)


def _attention(q, k, v, side, *, n_groups, n_heads, dk, shared_kv, tq, n_blocks, name):
    b, s, _ = q.shape
    n_kv = 1 if shared_kv else n_heads
    n_blk = s // tq
    assert n_blocks % 2 == 0 and n_blk % n_blocks == 0
    n_total = b * n_groups * n_blk
    n_steps = n_total // n_blocks

    def q_map(t):
        return (t // (n_groups * n_blk), t % n_blk, (t // n_blk) % n_groups)

    def kv_map(t):
        return (t // (n_groups * n_blk), 0, (t // n_blk) % n_groups)

    def out_map(j):
        bi, i, g = q_map(n_blocks * j)
        return (bi, i // n_blocks, g)

    def ahead(i):
        return lambda j: q_map(jnp.minimum(n_blocks * j + i + 1, n_total - 1))

    q_spec = lambda f: pl.BlockSpec((1, tq, n_heads * dk), f)
    side_in_specs, side_out_specs, side_shapes = [], [], []
    for w, col_block in side:
        rows, cols = w.shape[0] // n_steps, w.shape[1]
        side_in_specs.append(pl.BlockSpec((rows, cols), lambda j: (j, 0)))
        if col_block is None:
            side_out_specs.append(pl.BlockSpec((rows, cols), lambda j: (j, 0)))
            side_shapes.append(jax.ShapeDtypeStruct(w.shape, BF16))
        else:
            n_cb = cols // col_block
            side_out_specs.append(
                pl.BlockSpec((n_cb, rows, col_block), lambda j: (0, j, 0)))
            side_shapes.append(jax.ShapeDtypeStruct((n_cb, w.shape[0], col_block), BF16))
    return pl.pallas_call(
        functools.partial(_attn_kernel, n_heads=n_heads, dk=dk, shared_kv=shared_kv,
                          tq=tq, n_blocks=n_blocks, n_side=len(side)),
        grid=(n_steps,),
        in_specs=[q_spec(lambda j: (0, 0, 0))]
        + [q_spec(ahead(i)) for i in range(n_blocks)]
        + [
            pl.BlockSpec((1, s, n_kv * dk), lambda j: kv_map(n_blocks * j)),
            pl.BlockSpec((1, s, n_kv * dk),
                         lambda j: kv_map(jnp.minimum(n_blocks * (j + 1), n_total - 1))),
            pl.BlockSpec((1, s, n_kv * V_DIM), lambda j: kv_map(n_blocks * j)),
        ] + side_in_specs,
        out_specs=[pl.BlockSpec((1, n_blocks * tq, n_heads * V_DIM), out_map)]
        + side_out_specs,
        out_shape=[jax.ShapeDtypeStruct((b, s, n_groups * n_heads * V_DIM), BF16)]
        + side_shapes,
        scratch_shapes=[pltpu.VMEM((2, n_heads, tq, s), F32),
                        pltpu.VMEM((2, n_heads, tq, 1), F32)],
        compiler_params=pltpu.CompilerParams(
            dimension_semantics=("arbitrary",), vmem_limit_bytes=VMEM_LIMIT),
        name=name,
    )(*([q] * (n_blocks + 1)), k, k, v, *[w for w, _ in side])


def _outproj_kernel(oa_ref, ob_ref, x_ref, mod_ref, ga_ref, gb_ref, w_ref, gpost_ref, o_ref,
                    n_scr, y_scr):
    n_rows = n_scr.shape[0]

    def store_o(rows, val):
        o_ref[0, rows, :] = val

    for rows in _row_chunks(n_rows):
        n_scr[rows, 0:MIX_A] = _rms(oa_ref[0, rows, :].astype(F32), ga_ref[...]).astype(BF16)
        n_scr[rows, MIX_A:] = _rms(ob_ref[0, rows, :].astype(F32), gb_ref[...]).astype(BF16)
    y_scr[...] = jnp.dot(n_scr[...], w_ref[...], preferred_element_type=F32)
    _gated_residual_rows(lambda rows: x_ref[0, rows, :], lambda rows: y_scr[rows, :],
                         store_o, n_rows, gpost_ref[...], _mod_row(mod_ref, MOD_GT_A))


def _outproj(o_a, o_b, x, mod, g_a, g_b, w_out, g_post, tm):
    b, s, d = x.shape
    row = lambda bi, i: (bi, i, 0)
    per_b = lambda bi, i: (bi, 0, 0)
    const = lambda bi, i: (0, 0)
    return pl.pallas_call(
        _outproj_kernel,
        grid=(b, s // tm),
        in_specs=[
            pl.BlockSpec((1, tm, MIX_A), row),
            pl.BlockSpec((1, tm, MIX_B), row),
            pl.BlockSpec((1, tm, d), row),
            pl.BlockSpec((1, N_MOD, d), per_b),
            pl.BlockSpec((1, MIX_A), const),
            pl.BlockSpec((1, MIX_B), const),
            pl.BlockSpec((MIX_A + MIX_B, d), const, pipeline_mode=pl.Buffered(1)),
            pl.BlockSpec((1, d), const),
        ],
        out_specs=pl.BlockSpec((1, tm, d), row),
        out_shape=jax.ShapeDtypeStruct((b, s, d), F32),
        scratch_shapes=[pltpu.VMEM((tm, MIX_A + MIX_B), BF16), pltpu.VMEM((tm, d), F32)],
        compiler_params=pltpu.CompilerParams(
            dimension_semantics=("arbitrary", "arbitrary"),
            vmem_limit_bytes=VMEM_LIMIT),
        name="out_proj",
    )(o_a, o_b, x, mod, g_a, g_b, w_out, g_post)


def _mlp_kernel(x_ref, mod_ref, gpre_ref, w1_ref, w2_ref, gpost_ref,
                o_ref, h_scr, acc_scr, *, n_split):
    f = pl.program_id(2)
    last = pl.num_programs(2) - 1
    n_rows = h_scr.shape[0]
    parts = [slice(r, r + n_rows // n_split) for r in range(0, n_rows, n_rows // n_split)]

    def store_h(rows, val):
        h_scr[rows, :] = val

    def store_o(rows, val):
        o_ref[0, rows, :] = val

    def ffn(rows, assign_first=False):
        for c in range(0, w2_ref.shape[0], MLP_SUB):
            u = jnp.dot(h_scr[rows, :], w1_ref[0, :, c:c + MLP_SUB],
                        preferred_element_type=F32)
            u = jnp.maximum(u, 0.0)
            y = jnp.dot((u * u).astype(BF16), w2_ref[c:c + MLP_SUB, :],
                        preferred_element_type=F32)
            if assign_first and c == 0:
                acc_scr[rows, :] = y
            else:
                acc_scr[rows, :] += y

    @pl.when(f == 0)
    def _():
        for part in parts:
            _norm_mod_rows(lambda rows: x_ref[0, rows, :], store_h, part,
                           gpre_ref[...], _mod_row(mod_ref, MOD_SC_M),
                           _mod_row(mod_ref, MOD_SH_M))
            ffn(part, assign_first=True)

    @pl.when((f > 0) & (f < last))
    def _():
        ffn(slice(0, n_rows))

    @pl.when(f == last)
    def _():
        for part in parts:
            ffn(part)
            _gated_residual_rows(lambda rows: x_ref[0, rows, :], lambda rows: acc_scr[rows, :],
                                 store_o, part, gpost_ref[...], _mod_row(mod_ref, MOD_GT_M))


def _mlp(x, mod, g_pre, w1, w2, g_post, tm):
    b, s, d = x.shape
    n_f, _, tf = w1.shape
    row = lambda bi, i, f: (bi, i, 0)
    per_b = lambda bi, i, f: (bi, 0, 0)
    const = lambda bi, i, f: (0, 0)
    return pl.pallas_call(
        functools.partial(_mlp_kernel, n_split=2),
        grid=(b, s // tm, n_f),
        in_specs=[
            pl.BlockSpec((1, tm, d), row),
            pl.BlockSpec((1, N_MOD, d), per_b),
            pl.BlockSpec((1, d), const),
            pl.BlockSpec((1, d, tf), lambda bi, i, f: (f, 0, 0)),
            pl.BlockSpec((tf, d), lambda bi, i, f: (f, 0)),
            pl.BlockSpec((1, d), const),
        ],
        out_specs=pl.BlockSpec((1, tm, d), row),
        out_shape=jax.ShapeDtypeStruct((b, s, d), F32),
        scratch_shapes=[pltpu.VMEM((tm, d), BF16), pltpu.VMEM((tm, d), F32)],
        compiler_params=pltpu.CompilerParams(
            dimension_semantics=("arbitrary", "arbitrary", "arbitrary"),
            vmem_limit_bytes=VMEM_LIMIT_MLP),
        name="mlp",
    )(x, mod, g_pre, w1, w2, g_post)


def _cast_pad_kernel(w_ref, o_ref):
    n = w_ref.shape[1]
    o_ref[:, :n] = w_ref[...].astype(BF16)
    if o_ref.shape[1] > n:
        o_ref[:, n:] = jnp.zeros((o_ref.shape[0], o_ref.shape[1] - n), BF16)


def _cast_pad(w, tr):
    r, n = w.shape
    n_pad = -(-n // LANES) * LANES
    return pl.pallas_call(
        _cast_pad_kernel,
        grid=(r // tr,),
        in_specs=[pl.BlockSpec((tr, n), lambda i: (i, 0))],
        out_specs=pl.BlockSpec((tr, n_pad), lambda i: (i, 0)),
        out_shape=jax.ShapeDtypeStruct((r, n_pad), BF16),
        compiler_params=pltpu.CompilerParams(
            dimension_semantics=("arbitrary",), vmem_limit_bytes=VMEM_LIMIT),
        name="cast_pad",
    )(w)


def kernel(x, c, w_ada, b_ada, g_pre_attn, w_in, g_q_a, g_k_a, g_ckv, w_kv_b, g_out_a,
           g_out_b, w_out, g_post_attn, g_pre_mlp, w_mlp_in, w_mlp_out, g_post_mlp):
    b, s, d = x.shape
    depth = w_ada.shape[0]
    tabs = _rope_tables(s)
    c_pad = jnp.pad(c, ((0, 8 - b), (0, 0)))

    for l in range(depth):
        mod = _ada(c_pad, w_ada[l], b_ada[l][None, :]).reshape(8, N_MOD, d)

        q_a, k_a, v_a, q_b, k_b, v_b = _inproj(
            x, mod, g_pre_attn[l][None, :], _cast_pad(w_in[l].T, tr=608),
            _cast_pad(w_kv_b[l], tr=256), g_q_a[l][None, :], g_k_a[l][None, :],
            g_ckv[l][None, :], tabs, tm=512)

        o_a, w1_bf, w2_bf = _attention(
            q_a, k_a, v_a, [(w_mlp_in[l], MLP_TF), (w_mlp_out[l], None)],
            n_groups=HKV, n_heads=G_A,
            dk=HEAD_DIM, shared_kv=True, tq=256, n_blocks=4, name="attn_gqa")
        o_b, wo_bf = _attention(
            q_b, k_b, v_b, [(w_out[l], None)], n_groups=HB // 4, n_heads=4,
            dk=QK_B_PAD, shared_kv=False, tq=256, n_blocks=4, name="attn_mla")

        x = _outproj(o_a, o_b, x, mod, g_out_a[l][None, :], g_out_b[l][None, :],
                     wo_bf, g_post_attn[l][None, :], tm=512)

        x = _mlp(x, mod, g_pre_mlp[l][None, :], w1_bf, w2_bf,
                 g_post_mlp[l][None, :], tm=512)
    return x
```

```python
import functools
import math

import numpy as np
import jax
import jax.numpy as jnp
from jax import lax
from jax.experimental import pallas as pl
from jax.experimental.pallas import tpu as pltpu

D_MODEL = 2048
GRID_W = 64
ROPE_THETA = 10000.0
EPS = 1e-6

HEAD_DIM = 128
HA = 8
HKV = 2
G_A = HA // HKV

HB = 8
QK_NOPE = 128
QK_ROPE = 64
V_DIM = 128
KV_RANK = 512
QK_B = QK_NOPE + QK_ROPE
QK_B_PAD = 256

W_QA = HA * HEAD_DIM
W_KA = HKV * HEAD_DIM
W_VA = HKV * HEAD_DIM
W_QB = HB * QK_B
MIX_A = HA * HEAD_DIM
MIX_B = HB * V_DIM
D_FF = 4 * D_MODEL
N_MOD = 6
MOD_SH_A, MOD_SC_A, MOD_GT_A, MOD_SH_M, MOD_SC_M, MOD_GT_M = range(N_MOD)

LOG2_E = math.log2(math.e)
LANES = 128
ROW_CHUNK = 16
MLP_TF = 2048
MLP_SUB = 1024
VMEM_LIMIT = 56 * 1024 * 1024
VMEM_LIMIT_MLP = 62 * 1024 * 1024

F32 = jnp.float32
BF16 = jnp.bfloat16


def _rope_tables(seq_len):
    parts = _rope_table(seq_len, HEAD_DIM) + _rope_table(seq_len, QK_ROPE)
    return jnp.asarray(np.concatenate(parts, axis=1), F32)


def _rope_table(seq_len, dim):
    pos = np.arange(seq_len)
    row = (pos // GRID_W).astype(np.float64)
    col = (pos % GRID_W).astype(np.float64)
    half = dim // 2
    inv = ROPE_THETA ** (-np.arange(0, half, 2, dtype=np.float64) / half)
    ang_r = row[:, None] * inv[None, :]
    ang_c = col[:, None] * inv[None, :]
    ang = np.concatenate([ang_r, ang_r, ang_c, ang_c], axis=-1)
    cos, sin = np.cos(ang), np.sin(ang)
    quarter = dim // 4
    sign = np.where((np.arange(dim) % half) < quarter, -1.0, 1.0)
    sin = sin * sign[None, :]
    reps = LANES // dim
    return [np.tile(cos, (1, reps)), np.tile(sin, (1, reps))]


def _mod_row(mod_ref, idx):
    return mod_ref[0, idx:idx + 1, :]


def _unit_rms(x):
    ms = jnp.mean(x * x, axis=-1, keepdims=True)
    return x * lax.rsqrt(ms + EPS)


def _rms(x, g):
    return _unit_rms(x) * g


def _row_chunks(span):
    span = span if isinstance(span, slice) else slice(0, span)
    return [slice(r, r + ROW_CHUNK) for r in range(span.start, span.stop, ROW_CHUNK)]


def _norm_mod_rows(load, store, span, g, sc, sh):
    gain = g * (1.0 + sc)
    for rows in _row_chunks(span):
        store(rows, (_unit_rms(load(rows)) * gain + sh).astype(BF16))


def _gated_residual_rows(load_x, load_y, store, span, g, gt):
    gain = gt * g
    for rows in _row_chunks(span):
        store(rows, load_x(rows) + _unit_rms(load_y(rows)) * gain)


def _dot_nt(a, w_rows):
    return lax.dot_general(a, w_rows, (((1,), (1,)), ((), ())), preferred_element_type=F32)


def _rope(x, cos, sin_signed, quarter):
    lane = lax.broadcasted_iota(jnp.int32, x.shape, 1)
    take_up = (lane % (2 * quarter)) < quarter
    up = pltpu.roll(x, LANES - quarter, axis=1)
    down = pltpu.roll(x, quarter, axis=1)
    rot = jnp.where(take_up, up, down)
    return x * cos + rot * sin_signed


def _ada_kernel(c_ref, w_ref, b_ref, o_ref):
    c = c_ref[...]
    c_act = (c * jax.nn.sigmoid(c)).astype(BF16)
    w = w_ref[...].astype(BF16)
    o_ref[...] = jnp.dot(c_act, w, preferred_element_type=F32) + b_ref[...]


def _ada(c_pad, w_ada, b_ada):
    m, d = c_pad.shape
    n = w_ada.shape[1]
    tn = 1024
    return pl.pallas_call(
        _ada_kernel,
        grid=(n // tn,),
        in_specs=[
            pl.BlockSpec((m, d), lambda j: (0, 0)),
            pl.BlockSpec((d, tn), lambda j: (0, j)),
            pl.BlockSpec((1, tn), lambda j: (0, j)),
        ],
        out_specs=pl.BlockSpec((m, tn), lambda j: (0, j)),
        out_shape=jax.ShapeDtypeStruct((m, n), F32),
        compiler_params=pltpu.CompilerParams(
            dimension_semantics=("arbitrary",), vmem_limit_bytes=VMEM_LIMIT),
        name="ada_mod",
    )(c_pad, w_ada, b_ada)


def _inproj_kernel(x_ref, mod_ref, gpre_ref, w_ref, wkv_ref, gq_ref, gk_ref,
                   gckv_ref, tab_ref,
                   qa_ref, ka_ref, va_ref, qb_ref, kb_ref, vb_ref, h_scr):
    tm = h_scr.shape[0]

    def store_h(rows, val):
        h_scr[rows, :] = val

    _norm_mod_rows(lambda rows: x_ref[0, rows, :], store_h, tm,
                   gpre_ref[...], _mod_row(mod_ref, MOD_SC_A), _mod_row(mod_ref, MOD_SH_A))
    hb = h_scr[...]
    cosa, sina, cosb, sinb = (tab_ref[:, i * LANES:(i + 1) * LANES] for i in range(4))
    scale_a = LOG2_E / math.sqrt(HEAD_DIM)
    scale_b = LOG2_E / math.sqrt(QK_B)

    qa = _dot_nt(hb, w_ref[0:W_QA, :])
    gq = gq_ref[...] * scale_a
    for hd in range(HA):
        sl = slice(hd * HEAD_DIM, (hd + 1) * HEAD_DIM)
        q = _rope(_rms(qa[:, sl], gq), cosa, sina, HEAD_DIM // 4)
        qa_ref[0, :, sl] = q.astype(BF16)

    kva = _dot_nt(hb, w_ref[W_QA:W_QA + W_KA + W_VA, :])
    for hd in range(HKV):
        sl = slice(hd * HEAD_DIM, (hd + 1) * HEAD_DIM)
        k = _rope(_rms(kva[:, sl], gk_ref[...]), cosa, sina, HEAD_DIM // 4)
        ka_ref[0, :, sl] = k.astype(BF16)
    va_ref[0] = kva[:, W_KA:].astype(BF16)

    o_qb = W_QA + W_KA + W_VA
    qb = _dot_nt(hb, w_ref[o_qb:o_qb + W_QB, :])
    lane = lax.broadcasted_iota(jnp.int32, (tm, LANES), 1)
    low = lane < QK_ROPE
    swap = lambda t: pltpu.roll(t, QK_ROPE, axis=1)
    for pair in range(HB // 2):
        t0, t1, t2 = (qb[:, (3 * pair + i) * LANES:(3 * pair + i + 1) * LANES] for i in range(3))
        r1 = _rope(t1, cosb, sinb, QK_ROPE // 4)
        r2 = _rope(t2, cosb, sinb, QK_ROPE // 4)
        heads = (
            (2 * pair, t0, jnp.where(low, r1, 0.0)),
            (2 * pair + 1, jnp.where(low, swap(t1), swap(t2)), jnp.where(low, swap(r2), 0.0)),
        )
        for hd, nope, pe in heads:
            base = hd * QK_B_PAD
            qb_ref[0, :, base:base + QK_NOPE] = (nope * scale_b).astype(BF16)
            qb_ref[0, :, base + QK_NOPE:base + QK_B_PAD] = (pe * scale_b).astype(BF16)

    o_ckv = o_qb + W_QB
    w_rest = jnp.concatenate(
        [w_ref[o_ckv:, :], jnp.zeros((LANES - QK_ROPE, w_ref.shape[1]), BF16)], axis=0)
    rest = _dot_nt(hb, w_rest)
    ckv = _rms(rest[:, :KV_RANK], gckv_ref[...]).astype(BF16)
    kv = jnp.dot(ckv, wkv_ref[...], preferred_element_type=F32)
    kpe = _rope(rest[:, KV_RANK:], cosb, sinb, QK_ROPE // 4).astype(BF16)
    for hd in range(HB):
        base = hd * (QK_NOPE + V_DIM)
        kb_ref[0, :, hd * QK_B_PAD:hd * QK_B_PAD + QK_NOPE] = (
            kv[:, base:base + QK_NOPE].astype(BF16))
        kb_ref[0, :, hd * QK_B_PAD + QK_NOPE:(hd + 1) * QK_B_PAD] = kpe
        vb_ref[0, :, hd * V_DIM:(hd + 1) * V_DIM] = (
            kv[:, base + QK_NOPE:base + QK_NOPE + V_DIM].astype(BF16))


def _inproj(x, mod, g_pre, w_in_p, w_kv, g_q, g_k, g_ckv, tabs, tm):
    b, s, d = x.shape
    n_in = w_in_p.shape[0]
    row = lambda bi, i: (bi, i, 0)
    per_b = lambda bi, i: (bi, 0, 0)
    const = lambda bi, i: (0, 0)
    tab = lambda bi, i: (i, 0)
    vec = lambda n: pl.BlockSpec((1, n), const)
    out_widths = (MIX_A, W_KA, W_VA, HB * QK_B_PAD, HB * QK_B_PAD, MIX_B)
    return pl.pallas_call(
        _inproj_kernel,
        grid=(b, s // tm),
        in_specs=[
            pl.BlockSpec((1, tm, d), row),
            pl.BlockSpec((1, N_MOD, d), per_b),
            vec(d),
            pl.BlockSpec((n_in, d), const, pipeline_mode=pl.Buffered(1)),
            pl.BlockSpec((KV_RANK, HB * (QK_NOPE + V_DIM)), const,
                         pipeline_mode=pl.Buffered(1)),
            vec(HEAD_DIM), vec(HEAD_DIM), vec(KV_RANK),
            pl.BlockSpec((tm, 4 * LANES), tab),
        ],
        out_specs=[pl.BlockSpec((1, tm, w), row) for w in out_widths],
        out_shape=[jax.ShapeDtypeStruct((b, s, w), BF16) for w in out_widths],
        scratch_shapes=[pltpu.VMEM((tm, d), BF16)],
        compiler_params=pltpu.CompilerParams(
            dimension_semantics=("arbitrary", "arbitrary"),
            vmem_limit_bytes=VMEM_LIMIT),
        name="in_proj",
    )(x, mod, g_pre, w_in_p, w_kv, g_q, g_k, g_ckv, tabs)


def _attn_kernel(*refs, n_heads, dk, shared_kv, tq, n_blocks, n_side):
    q0_ref, q_refs = refs[0], refs[1:1 + n_blocks]
    k_ref, kn_ref, v_ref = refs[1 + n_blocks:4 + n_blocks]
    rest = refs[4 + n_blocks:]
    side_in = rest[:n_side]
    o_ref = rest[n_side]
    side_out = rest[n_side + 1:2 * n_side + 1]
    s_scr, m_scr = rest[2 * n_side + 1:]

    def scores(q_ref, keys_ref, slot):
        for hd in range(n_heads):
            kv_hd = 0 if shared_kv else hd
            k = keys_ref[0, :, kv_hd * dk:(kv_hd + 1) * dk]
            q = q_ref[0, :, hd * dk:(hd + 1) * dk]
            s = lax.dot_general(q, k, (((1,), (1,)), ((), ())), preferred_element_type=F32)
            s_scr[slot, hd] = s
            m_scr[slot, hd] = jnp.max(s, axis=-1, keepdims=True)

    def softmax_pv(slot, row0):
        for hd in range(n_heads):
            kv_hd = 0 if shared_kv else hd
            v = v_ref[0, :, kv_hd * V_DIM:(kv_hd + 1) * V_DIM]
            p = jnp.exp2(s_scr[slot, hd] - m_scr[slot, hd]).astype(BF16)
            v_ext = jnp.concatenate([v, jnp.ones_like(v)], axis=1)
            o_ext = jnp.dot(p, v_ext, preferred_element_type=F32)
            o = o_ext[:, :V_DIM] / o_ext[:, V_DIM:]
            o_ref[0, row0:row0 + tq, hd * V_DIM:(hd + 1) * V_DIM] = o.astype(BF16)

    @pl.when(pl.program_id(0) == 0)
    def _():
        scores(q0_ref, k_ref, 0)

    for i in range(n_blocks):
        keys_ref = kn_ref if i == n_blocks - 1 else k_ref
        scores(q_refs[i], keys_ref, (i + 1) % 2)
        for w_ref, wb_ref in zip(side_in, side_out):
            n = w_ref.shape[0] // n_blocks
            rows = slice(i * n, (i + 1) * n)
            if len(wb_ref.shape) == 2:
                wb_ref[rows, :] = w_ref[rows, :].astype(BF16)
            else:
                cb = wb_ref.shape[2]
                for f in range(wb_ref.shape[0]):
                    wb_ref[f, rows, :] = w_ref[rows, f * cb:(f + 1) * cb].astype(BF16)
        softmax_pv(i % 2, i * tq)


def _attention(q, k, v, side, *, n_groups, n_heads, dk, shared_kv, tq, n_blocks, name):
    b, s, _ = q.shape
    n_kv = 1 if shared_kv else n_heads
    n_blk = s // tq
    assert n_blocks % 2 == 0 and n_blk % n_blocks == 0
    n_total = b * n_groups * n_blk
    n_steps = n_total // n_blocks

    def q_map(t):
        return (t // (n_groups * n_blk), t % n_blk, (t // n_blk) % n_groups)

    def kv_map(t):
        return (t // (n_groups * n_blk), 0, (t // n_blk) % n_groups)

    def out_map(j):
        bi, i, g = q_map(n_blocks * j)
        return (bi, i // n_blocks, g)

    def ahead(i):
        return lambda j: q_map(jnp.minimum(n_blocks * j + i + 1, n_total - 1))

    q_spec = lambda f: pl.BlockSpec((1, tq, n_heads * dk), f)
    side_in_specs, side_out_specs, side_shapes = [], [], []
    for w, col_block in side:
        rows, cols = w.shape[0] // n_steps, w.shape[1]
        side_in_specs.append(pl.BlockSpec((rows, cols), lambda j: (j, 0)))
        if col_block is None:
            side_out_specs.append(pl.BlockSpec((rows, cols), lambda j: (j, 0)))
            side_shapes.append(jax.ShapeDtypeStruct(w.shape, BF16))
        else:
            n_cb = cols // col_block
            side_out_specs.append(
                pl.BlockSpec((n_cb, rows, col_block), lambda j: (0, j, 0)))
            side_shapes.append(jax.ShapeDtypeStruct((n_cb, w.shape[0], col_block), BF16))
    return pl.pallas_call(
        functools.partial(_attn_kernel, n_heads=n_heads, dk=dk, shared_kv=shared_kv,
                          tq=tq, n_blocks=n_blocks, n_side=len(side)),
        grid=(n_steps,),
        in_specs=[q_spec(lambda j: (0, 0, 0))]
        + [q_spec(ahead(i)) for i in range(n_blocks)]
        + [
            pl.BlockSpec((1, s, n_kv * dk), lambda j: kv_map(n_blocks * j)),
            pl.BlockSpec((1, s, n_kv * dk),
                         lambda j: kv_map(jnp.minimum(n_blocks * (j + 1), n_total - 1))),
            pl.BlockSpec((1, s, n_kv * V_DIM), lambda j: kv_map(n_blocks * j)),
        ] + side_in_specs,
        out_specs=[pl.BlockSpec((1, n_blocks * tq, n_heads * V_DIM), out_map)]
        + side_out_specs,
        out_shape=[jax.ShapeDtypeStruct((b, s, n_groups * n_heads * V_DIM), BF16)]
        + side_shapes,
        scratch_shapes=[pltpu.VMEM((2, n_heads, tq, s), F32),
                        pltpu.VMEM((2, n_heads, tq, 1), F32)],
        compiler_params=pltpu.CompilerParams(
            dimension_semantics=("arbitrary",), vmem_limit_bytes=VMEM_LIMIT),
        name=name,
    )(*([q] * (n_blocks + 1)), k, k, v, *[w for w, _ in side])


def _outproj_kernel(oa_ref, ob_ref, x_ref, mod_ref, ga_ref, gb_ref, w_ref, gpost_ref, o_ref,
                    n_scr, y_scr):
    n_rows = n_scr.shape[0]

    def store_o(rows, val):
        o_ref[0, rows, :] = val

    for rows in _row_chunks(n_rows):
        n_scr[rows, 0:MIX_A] = _rms(oa_ref[0, rows, :].astype(F32), ga_ref[...]).astype(BF16)
        n_scr[rows, MIX_A:] = _rms(ob_ref[0, rows, :].astype(F32), gb_ref[...]).astype(BF16)
    y_scr[...] = jnp.dot(n_scr[...], w_ref[...], preferred_element_type=F32)
    _gated_residual_rows(lambda rows: x_ref[0, rows, :], lambda rows: y_scr[rows, :],
                         store_o, n_rows, gpost_ref[...], _mod_row(mod_ref, MOD_GT_A))


def _outproj(o_a, o_b, x, mod, g_a, g_b, w_out, g_post, tm):
    b, s, d = x.shape
    row = lambda bi, i: (bi, i, 0)
    per_b = lambda bi, i: (bi, 0, 0)
    const = lambda bi, i: (0, 0)
    return pl.pallas_call(
        _outproj_kernel,
        grid=(b, s // tm),
        in_specs=[
            pl.BlockSpec((1, tm, MIX_A), row),
            pl.BlockSpec((1, tm, MIX_B), row),
            pl.BlockSpec((1, tm, d), row),
            pl.BlockSpec((1, N_MOD, d), per_b),
            pl.BlockSpec((1, MIX_A), const),
            pl.BlockSpec((1, MIX_B), const),
            pl.BlockSpec((MIX_A + MIX_B, d), const, pipeline_mode=pl.Buffered(1)),
            pl.BlockSpec((1, d), const),
        ],
        out_specs=pl.BlockSpec((1, tm, d), row),
        out_shape=jax.ShapeDtypeStruct((b, s, d), F32),
        scratch_shapes=[pltpu.VMEM((tm, MIX_A + MIX_B), BF16), pltpu.VMEM((tm, d), F32)],
        compiler_params=pltpu.CompilerParams(
            dimension_semantics=("arbitrary", "arbitrary"),
            vmem_limit_bytes=VMEM_LIMIT),
        name="out_proj",
    )(o_a, o_b, x, mod, g_a, g_b, w_out, g_post)


def _mlp_kernel(x_ref, mod_ref, gpre_ref, w1_ref, w2_ref, gpost_ref,
                o_ref, h_scr, acc_scr, *, n_split):
    f = pl.program_id(2)
    last = pl.num_programs(2) - 1
    n_rows = h_scr.shape[0]
    parts = [slice(r, r + n_rows // n_split) for r in range(0, n_rows, n_rows // n_split)]

    def store_h(rows, val):
        h_scr[rows, :] = val

    def store_o(rows, val):
        o_ref[0, rows, :] = val

    def ffn(rows, assign_first=False):
        n_sub, _, sub = w1_ref.shape
        for c in range(n_sub):
            u = jnp.dot(h_scr[rows, :], w1_ref[c], preferred_element_type=F32)
            u = jnp.maximum(u, 0.0)
            y = jnp.dot((u * u).astype(BF16), w2_ref[c * sub:(c + 1) * sub, :],
                        preferred_element_type=F32)
            if assign_first and c == 0:
                acc_scr[rows, :] = y
            else:
                acc_scr[rows, :] += y

    @pl.when(f == 0)
    def _():
        for part in parts:
            _norm_mod_rows(lambda rows: x_ref[0, rows, :], store_h, part,
                           gpre_ref[...], _mod_row(mod_ref, MOD_SC_M),
                           _mod_row(mod_ref, MOD_SH_M))
            ffn(part, assign_first=True)

    @pl.when((f > 0) & (f < last))
    def _():
        ffn(slice(0, n_rows))

    @pl.when(f == last)
    def _():
        for part in parts:
            ffn(part)
            _gated_residual_rows(lambda rows: x_ref[0, rows, :], lambda rows: acc_scr[rows, :],
                                 store_o, part, gpost_ref[...], _mod_row(mod_ref, MOD_GT_M))


def _mlp(x, mod, g_pre, w1, w2, g_post, tm):
    b, s, d = x.shape
    n_sub = MLP_TF // MLP_SUB
    n_f, tf = w1.shape[0] // n_sub, MLP_TF
    row = lambda bi, i, f: (bi, i, 0)
    per_b = lambda bi, i, f: (bi, 0, 0)
    const = lambda bi, i, f: (0, 0)
    return pl.pallas_call(
        functools.partial(_mlp_kernel, n_split=2),
        grid=(b, s // tm, n_f),
        in_specs=[
            pl.BlockSpec((1, tm, d), row),
            pl.BlockSpec((1, N_MOD, d), per_b),
            pl.BlockSpec((1, d), const),
            pl.BlockSpec((n_sub, d, MLP_SUB), lambda bi, i, f: (f, 0, 0)),
            pl.BlockSpec((tf, d), lambda bi, i, f: (f, 0)),
            pl.BlockSpec((1, d), const),
        ],
        out_specs=pl.BlockSpec((1, tm, d), row),
        out_shape=jax.ShapeDtypeStruct((b, s, d), F32),
        scratch_shapes=[pltpu.VMEM((tm, d), BF16), pltpu.VMEM((tm, d), F32)],
        compiler_params=pltpu.CompilerParams(
            dimension_semantics=("arbitrary", "arbitrary", "arbitrary"),
            vmem_limit_bytes=VMEM_LIMIT_MLP),
        name="mlp",
    )(x, mod, g_pre, w1, w2, g_post)


def _cast_pad_kernel(w_ref, o_ref):
    n = w_ref.shape[1]
    o_ref[:, :n] = w_ref[...].astype(BF16)
    if o_ref.shape[1] > n:
        o_ref[:, n:] = jnp.zeros((o_ref.shape[0], o_ref.shape[1] - n), BF16)


def _cast_pad(w, tr):
    r, n = w.shape
    n_pad = -(-n // LANES) * LANES
    return pl.pallas_call(
        _cast_pad_kernel,
        grid=(r // tr,),
        in_specs=[pl.BlockSpec((tr, n), lambda i: (i, 0))],
        out_specs=pl.BlockSpec((tr, n_pad), lambda i: (i, 0)),
        out_shape=jax.ShapeDtypeStruct((r, n_pad), BF16),
        compiler_params=pltpu.CompilerParams(
            dimension_semantics=("arbitrary",), vmem_limit_bytes=VMEM_LIMIT),
        name="cast_pad",
    )(w)


def kernel(x, c, w_ada, b_ada, g_pre_attn, w_in, g_q_a, g_k_a, g_ckv, w_kv_b, g_out_a,
           g_out_b, w_out, g_post_attn, g_pre_mlp, w_mlp_in, w_mlp_out, g_post_mlp):
    b, s, d = x.shape
    depth = w_ada.shape[0]
    tabs = _rope_tables(s)
    c_pad = jnp.pad(c, ((0, 8 - b), (0, 0)))

    for l in range(depth):
        mod = _ada(c_pad, w_ada[l], b_ada[l][None, :]).reshape(8, N_MOD, d)

        q_a, k_a, v_a, q_b, k_b, v_b = _inproj(
            x, mod, g_pre_attn[l][None, :], _cast_pad(w_in[l].T, tr=608),
            _cast_pad(w_kv_b[l], tr=256), g_q_a[l][None, :], g_k_a[l][None, :],
            g_ckv[l][None, :], tabs, tm=512)

        o_a, w1_bf, w2_bf = _attention(
            q_a, k_a, v_a, [(w_mlp_in[l], MLP_SUB), (w_mlp_out[l], None)],
            n_groups=HKV, n_heads=G_A,
            dk=HEAD_DIM, shared_kv=True, tq=256, n_blocks=4, name="attn_gqa")
        o_b, wo_bf = _attention(
            q_b, k_b, v_b, [(w_out[l], None)], n_groups=HB // 4, n_heads=4,
            dk=QK_B_PAD, shared_kv=False, tq=256, n_blocks=4, name="attn_mla")

        x = _outproj(o_a, o_b, x, mod, g_out_a[l][None, :], g_out_b[l][None, :],
                     wo_bf, g_post_attn[l][None, :], tm=512)

        x = _mlp(x, mod, g_pre_mlp[l][None, :], w1_bf, w2_bf,
                 g_post_mlp[l][None, :], tm=512)
    return x
```

```python
import functools
import math

import numpy as np
import jax
import jax.numpy as jnp
from jax import lax
from jax.experimental import pallas as pl
from jax.experimental.pallas import tpu as pltpu

D_MODEL = 2048
GRID_W = 64
ROPE_THETA = 10000.0
EPS = 1e-6

HEAD_DIM = 128
HA = 8
HKV = 2
G_A = HA // HKV

HB = 8
QK_NOPE = 128
QK_ROPE = 64
V_DIM = 128
KV_RANK = 512
QK_B = QK_NOPE + QK_ROPE
QK_B_PAD = 256

W_QA = HA * HEAD_DIM
W_KA = HKV * HEAD_DIM
W_VA = HKV * HEAD_DIM
W_QB = HB * QK_B
MIX_A = HA * HEAD_DIM
MIX_B = HB * V_DIM
D_FF = 4 * D_MODEL
N_MOD = 6
MOD_SH_A, MOD_SC_A, MOD_GT_A, MOD_SH_M, MOD_SC_M, MOD_GT_M = range(N_MOD)

LOG2_E = math.log2(math.e)
LANES = 128
ROW_CHUNK = 16
MLP_TF = 2048
MLP_SUB = 1024
VMEM_LIMIT = 56 * 1024 * 1024

F32 = jnp.float32
BF16 = jnp.bfloat16


def _rope_tables(seq_len):
    parts = _rope_table(seq_len, HEAD_DIM) + _rope_table(seq_len, QK_ROPE)
    return jnp.asarray(np.concatenate(parts, axis=1), F32)


def _rope_table(seq_len, dim):
    pos = np.arange(seq_len)
    row = (pos // GRID_W).astype(np.float64)
    col = (pos % GRID_W).astype(np.float64)
    half = dim // 2
    inv = ROPE_THETA ** (-np.arange(0, half, 2, dtype=np.float64) / half)
    ang_r = row[:, None] * inv[None, :]
    ang_c = col[:, None] * inv[None, :]
    ang = np.concatenate([ang_r, ang_r, ang_c, ang_c], axis=-1)
    cos, sin = np.cos(ang), np.sin(ang)
    quarter = dim // 4
    sign = np.where((np.arange(dim) % half) < quarter, -1.0, 1.0)
    sin = sin * sign[None, :]
    reps = LANES // dim
    return [np.tile(cos, (1, reps)), np.tile(sin, (1, reps))]


def _mod_row(mod_ref, idx):
    return mod_ref[0, idx:idx + 1, :]


def _unit_rms(x):
    ms = jnp.mean(x * x, axis=-1, keepdims=True)
    return x * lax.rsqrt(ms + EPS)


def _rms(x, g):
    return _unit_rms(x) * g


def _row_chunks(span):
    span = span if isinstance(span, slice) else slice(0, span)
    return [slice(r, r + ROW_CHUNK) for r in range(span.start, span.stop, ROW_CHUNK)]


def _norm_mod_rows(load, store, span, g, sc, sh):
    gain = g * (1.0 + sc)
    for rows in _row_chunks(span):
        store(rows, (_unit_rms(load(rows)) * gain + sh).astype(BF16))


def _gated_residual_rows(load_x, load_y, store, span, g, gt):
    gain = gt * g
    for rows in _row_chunks(span):
        store(rows, load_x(rows) + _unit_rms(load_y(rows)) * gain)


def _dot_nt(a, w_rows):
    return lax.dot_general(a, w_rows, (((1,), (1,)), ((), ())), preferred_element_type=F32)


def _rope(x, cos, sin_signed, quarter):
    lane = lax.broadcasted_iota(jnp.int32, x.shape, 1)
    take_up = (lane % (2 * quarter)) < quarter
    up = pltpu.roll(x, LANES - quarter, axis=1)
    down = pltpu.roll(x, quarter, axis=1)
    rot = jnp.where(take_up, up, down)
    return x * cos + rot * sin_signed


def _ada_kernel(c_ref, w_ref, b_ref, o_ref):
    c = c_ref[...]
    c_act = (c * jax.nn.sigmoid(c)).astype(BF16)
    w = w_ref[...].astype(BF16)
    o_ref[...] = jnp.dot(c_act, w, preferred_element_type=F32) + b_ref[...]


def _ada(c_pad, w_ada, b_ada):
    m, d = c_pad.shape
    n = w_ada.shape[1]
    tn = 1024
    return pl.pallas_call(
        _ada_kernel,
        grid=(n // tn,),
        in_specs=[
            pl.BlockSpec((m, d), lambda j: (0, 0)),
            pl.BlockSpec((d, tn), lambda j: (0, j)),
            pl.BlockSpec((1, tn), lambda j: (0, j)),
        ],
        out_specs=pl.BlockSpec((m, tn), lambda j: (0, j)),
        out_shape=jax.ShapeDtypeStruct((m, n), F32),
        compiler_params=pltpu.CompilerParams(
            dimension_semantics=("arbitrary",), vmem_limit_bytes=VMEM_LIMIT),
        name="ada_mod",
    )(c_pad, w_ada, b_ada)


def _inproj_kernel(x_ref, mod_ref, gpre_ref, w_ref, wkv_ref, gq_ref, gk_ref,
                   gckv_ref, tab_ref,
                   qa_ref, ka_ref, va_ref, qb_ref, kb_ref, vb_ref, h_scr):
    tm = h_scr.shape[0]

    def store_h(rows, val):
        h_scr[rows, :] = val

    _norm_mod_rows(lambda rows: x_ref[0, rows, :], store_h, tm,
                   gpre_ref[...], _mod_row(mod_ref, MOD_SC_A), _mod_row(mod_ref, MOD_SH_A))
    hb = h_scr[...]
    cosa, sina, cosb, sinb = (tab_ref[:, i * LANES:(i + 1) * LANES] for i in range(4))
    scale_a = LOG2_E / math.sqrt(HEAD_DIM)
    scale_b = LOG2_E / math.sqrt(QK_B)

    qa = _dot_nt(hb, w_ref[0:W_QA, :])
    gq = gq_ref[...] * scale_a
    for hd in range(HA):
        sl = slice(hd * HEAD_DIM, (hd + 1) * HEAD_DIM)
        q = _rope(_rms(qa[:, sl], gq), cosa, sina, HEAD_DIM // 4)
        qa_ref[0, :, sl] = q.astype(BF16)

    kva = _dot_nt(hb, w_ref[W_QA:W_QA + W_KA + W_VA, :])
    for hd in range(HKV):
        sl = slice(hd * HEAD_DIM, (hd + 1) * HEAD_DIM)
        k = _rope(_rms(kva[:, sl], gk_ref[...]), cosa, sina, HEAD_DIM // 4)
        ka_ref[0, :, sl] = k.astype(BF16)
    va_ref[0] = kva[:, W_KA:].astype(BF16)

    o_qb = W_QA + W_KA + W_VA
    qb = _dot_nt(hb, w_ref[o_qb:o_qb + W_QB, :])
    lane = lax.broadcasted_iota(jnp.int32, (tm, LANES), 1)
    low = lane < QK_ROPE
    swap = lambda t: pltpu.roll(t, QK_ROPE, axis=1)
    for pair in range(HB // 2):
        t0, t1, t2 = (qb[:, (3 * pair + i) * LANES:(3 * pair + i + 1) * LANES] for i in range(3))
        r1 = _rope(t1, cosb, sinb, QK_ROPE // 4)
        r2 = _rope(t2, cosb, sinb, QK_ROPE // 4)
        heads = (
            (2 * pair, t0, jnp.where(low, r1, 0.0)),
            (2 * pair + 1, jnp.where(low, swap(t1), swap(t2)), jnp.where(low, swap(r2), 0.0)),
        )
        for hd, nope, pe in heads:
            base = hd * QK_B_PAD
            qb_ref[0, :, base:base + QK_NOPE] = (nope * scale_b).astype(BF16)
            qb_ref[0, :, base + QK_NOPE:base + QK_B_PAD] = (pe * scale_b).astype(BF16)

    o_ckv = o_qb + W_QB
    w_rest = jnp.concatenate(
        [w_ref[o_ckv:, :], jnp.zeros((LANES - QK_ROPE, w_ref.shape[1]), BF16)], axis=0)
    rest = _dot_nt(hb, w_rest)
    ckv = _rms(rest[:, :KV_RANK], gckv_ref[...]).astype(BF16)
    kv = jnp.dot(ckv, wkv_ref[...], preferred_element_type=F32)
    kpe = _rope(rest[:, KV_RANK:], cosb, sinb, QK_ROPE // 4).astype(BF16)
    for hd in range(HB):
        base = hd * (QK_NOPE + V_DIM)
        kb_ref[0, :, hd * QK_B_PAD:hd * QK_B_PAD + QK_NOPE] = (
            kv[:, base:base + QK_NOPE].astype(BF16))
        kb_ref[0, :, hd * QK_B_PAD + QK_NOPE:(hd + 1) * QK_B_PAD] = kpe
        vb_ref[0, :, hd * V_DIM:(hd + 1) * V_DIM] = (
            kv[:, base + QK_NOPE:base + QK_NOPE + V_DIM].astype(BF16))


def _inproj(x, mod, g_pre, w_in_p, w_kv, g_q, g_k, g_ckv, tabs, tm):
    b, s, d = x.shape
    n_in = w_in_p.shape[0]
    row = lambda bi, i: (bi, i, 0)
    per_b = lambda bi, i: (bi, 0, 0)
    const = lambda bi, i: (0, 0)
    tab = lambda bi, i: (i, 0)
    vec = lambda n: pl.BlockSpec((1, n), const)
    out_widths = (MIX_A, W_KA, W_VA, HB * QK_B_PAD, HB * QK_B_PAD, MIX_B)
    return pl.pallas_call(
        _inproj_kernel,
        grid=(b, s // tm),
        in_specs=[
            pl.BlockSpec((1, tm, d), row),
            pl.BlockSpec((1, N_MOD, d), per_b),
            vec(d),
            pl.BlockSpec((n_in, d), const, pipeline_mode=pl.Buffered(1)),
            pl.BlockSpec((KV_RANK, HB * (QK_NOPE + V_DIM)), const,
                         pipeline_mode=pl.Buffered(1)),
            vec(HEAD_DIM), vec(HEAD_DIM), vec(KV_RANK),
            pl.BlockSpec((tm, 4 * LANES), tab),
        ],
        out_specs=[pl.BlockSpec((1, tm, w), row) for w in out_widths],
        out_shape=[jax.ShapeDtypeStruct((b, s, w), BF16) for w in out_widths],
        scratch_shapes=[pltpu.VMEM((tm, d), BF16)],
        compiler_params=pltpu.CompilerParams(
            dimension_semantics=("arbitrary", "arbitrary"),
            vmem_limit_bytes=VMEM_LIMIT),
        name="in_proj",
    )(x, mod, g_pre, w_in_p, w_kv, g_q, g_k, g_ckv, tabs)


def _attn_kernel(*refs, n_heads, dk, shared_kv, tq, n_blocks, n_side):
    q0_ref, q_refs = refs[0], refs[1:1 + n_blocks]
    k_ref, kn_ref, v_ref = refs[1 + n_blocks:4 + n_blocks]
    rest = refs[4 + n_blocks:]
    side_in = rest[:n_side]
    o_ref = rest[n_side]
    side_out = rest[n_side + 1:2 * n_side + 1]
    s_scr, m_scr = rest[2 * n_side + 1:]

    def scores(q_ref, keys_ref, slot):
        for hd in range(n_heads):
            kv_hd = 0 if shared_kv else hd
            k = keys_ref[0, :, kv_hd * dk:(kv_hd + 1) * dk]
            q = q_ref[0, :, hd * dk:(hd + 1) * dk]
            s = lax.dot_general(q, k, (((1,), (1,)), ((), ())), preferred_element_type=F32)
            s_scr[slot, hd] = s
            m_scr[slot, hd] = jnp.max(s, axis=-1, keepdims=True)

    def softmax_pv(slot, row0):
        for hd in range(n_heads):
            kv_hd = 0 if shared_kv else hd
            v = v_ref[0, :, kv_hd * V_DIM:(kv_hd + 1) * V_DIM]
            p = jnp.exp2(s_scr[slot, hd] - m_scr[slot, hd]).astype(BF16)
            v_ext = jnp.concatenate([v, jnp.ones_like(v)], axis=1)
            o_ext = jnp.dot(p, v_ext, preferred_element_type=F32)
            o = o_ext[:, :V_DIM] / o_ext[:, V_DIM:]
            o_ref[0, row0:row0 + tq, hd * V_DIM:(hd + 1) * V_DIM] = o.astype(BF16)

    @pl.when(pl.program_id(0) == 0)
    def _():
        scores(q0_ref, k_ref, 0)

    for i in range(n_blocks):
        keys_ref = kn_ref if i == n_blocks - 1 else k_ref
        scores(q_refs[i], keys_ref, (i + 1) % 2)
        for w_ref, wb_ref in zip(side_in, side_out):
            n = w_ref.shape[0] // n_blocks
            rows = slice(i * n, (i + 1) * n)
            if len(wb_ref.shape) == 2:
                wb_ref[rows, :] = w_ref[rows, :].astype(BF16)
            else:
                cb = wb_ref.shape[2]
                for f in range(wb_ref.shape[0]):
                    wb_ref[f, rows, :] = w_ref[rows, f * cb:(f + 1) * cb].astype(BF16)
        softmax_pv(i % 2, i * tq)


def _attention(q, k, v, side, *, n_groups, n_heads, dk, shared_kv, tq, n_blocks, name):
    b, s, _ = q.shape
    n_kv = 1 if shared_kv else n_heads
    n_blk = s // tq
    assert n_blocks % 2 == 0 and n_blk % n_blocks == 0
    n_total = b * n_groups * n_blk
    n_steps = n_total // n_blocks

    def q_map(t):
        return (t // (n_groups * n_blk), t % n_blk, (t // n_blk) % n_groups)

    def kv_map(t):
        return (t // (n_groups * n_blk), 0, (t // n_blk) % n_groups)

    def out_map(j):
        bi, i, g = q_map(n_blocks * j)
        return (bi, i // n_blocks, g)

    def ahead(i):
        return lambda j: q_map(jnp.minimum(n_blocks * j + i + 1, n_total - 1))

    q_spec = lambda f: pl.BlockSpec((1, tq, n_heads * dk), f)
    side_in_specs, side_out_specs, side_shapes = [], [], []
    for w, col_block in side:
        rows, cols = w.shape[0] // n_steps, w.shape[1]
        side_in_specs.append(pl.BlockSpec((rows, cols), lambda j: (j, 0)))
        if col_block is None:
            side_out_specs.append(pl.BlockSpec((rows, cols), lambda j: (j, 0)))
            side_shapes.append(jax.ShapeDtypeStruct(w.shape, BF16))
        else:
            n_cb = cols // col_block
            side_out_specs.append(
                pl.BlockSpec((n_cb, rows, col_block), lambda j: (0, j, 0)))
            side_shapes.append(jax.ShapeDtypeStruct((n_cb, w.shape[0], col_block), BF16))
    return pl.pallas_call(
        functools.partial(_attn_kernel, n_heads=n_heads, dk=dk, shared_kv=shared_kv,
                          tq=tq, n_blocks=n_blocks, n_side=len(side)),
        grid=(n_steps,),
        in_specs=[q_spec(lambda j: (0, 0, 0))]
        + [q_spec(ahead(i)) for i in range(n_blocks)]
        + [
            pl.BlockSpec((1, s, n_kv * dk), lambda j: kv_map(n_blocks * j)),
            pl.BlockSpec((1, s, n_kv * dk),
                         lambda j: kv_map(jnp.minimum(n_blocks * (j + 1), n_total - 1))),
            pl.BlockSpec((1, s, n_kv * V_DIM), lambda j: kv_map(n_blocks * j)),
        ] + side_in_specs,
        out_specs=[pl.BlockSpec((1, n_blocks * tq, n_heads * V_DIM), out_map)]
        + side_out_specs,
        out_shape=[jax.ShapeDtypeStruct((b, s, n_groups * n_heads * V_DIM), BF16)]
        + side_shapes,
        scratch_shapes=[pltpu.VMEM((2, n_heads, tq, s), F32),
                        pltpu.VMEM((2, n_heads, tq, 1), F32)],
        compiler_params=pltpu.CompilerParams(
            dimension_semantics=("arbitrary",), vmem_limit_bytes=VMEM_LIMIT),
        name=name,
    )(*([q] * (n_blocks + 1)), k, k, v, *[w for w, _ in side])


def _outproj_kernel(oa_ref, ob_ref, x_ref, mod_ref, ga_ref, gb_ref, w_ref, gpost_ref, o_ref,
                    n_scr, y_scr):
    n_rows = n_scr.shape[0]

    def store_o(rows, val):
        o_ref[0, rows, :] = val

    for rows in _row_chunks(n_rows):
        n_scr[rows, 0:MIX_A] = _rms(oa_ref[0, rows, :].astype(F32), ga_ref[...]).astype(BF16)
        n_scr[rows, MIX_A:] = _rms(ob_ref[0, rows, :].astype(F32), gb_ref[...]).astype(BF16)
    y_scr[...] = jnp.dot(n_scr[...], w_ref[...], preferred_element_type=F32)
    _gated_residual_rows(lambda rows: x_ref[0, rows, :], lambda rows: y_scr[rows, :],
                         store_o, n_rows, gpost_ref[...], _mod_row(mod_ref, MOD_GT_A))


def _outproj(o_a, o_b, x, mod, g_a, g_b, w_out, g_post, tm):
    b, s, d = x.shape
    row = lambda bi, i: (bi, i, 0)
    per_b = lambda bi, i: (bi, 0, 0)
    const = lambda bi, i: (0, 0)
    return pl.pallas_call(
        _outproj_kernel,
        grid=(b, s // tm),
        in_specs=[
            pl.BlockSpec((1, tm, MIX_A), row),
            pl.BlockSpec((1, tm, MIX_B), row),
            pl.BlockSpec((1, tm, d), row),
            pl.BlockSpec((1, N_MOD, d), per_b),
            pl.BlockSpec((1, MIX_A), const),
            pl.BlockSpec((1, MIX_B), const),
            pl.BlockSpec((MIX_A + MIX_B, d), const, pipeline_mode=pl.Buffered(1)),
            pl.BlockSpec((1, d), const),
        ],
        out_specs=pl.BlockSpec((1, tm, d), row),
        out_shape=jax.ShapeDtypeStruct((b, s, d), F32),
        scratch_shapes=[pltpu.VMEM((tm, MIX_A + MIX_B), BF16), pltpu.VMEM((tm, d), F32)],
        compiler_params=pltpu.CompilerParams(
            dimension_semantics=("arbitrary", "arbitrary"),
            vmem_limit_bytes=VMEM_LIMIT),
        name="out_proj",
    )(o_a, o_b, x, mod, g_a, g_b, w_out, g_post)


def _mlp_kernel(x_ref, mod_ref, gpre_ref, w1_ref, w2_ref, gpost_ref,
                o_ref, h_scr, *, n_split):
    f = pl.program_id(2)
    last = pl.num_programs(2) - 1
    n_rows = h_scr.shape[0]
    parts = [slice(r, r + n_rows // n_split) for r in range(0, n_rows, n_rows // n_split)]

    def store_h(rows, val):
        h_scr[rows, :] = val

    def store_o(rows, val):
        o_ref[0, rows, :] = val

    def ffn(rows, assign_first=False):
        n_sub, _, sub = w1_ref.shape
        for c in range(n_sub):
            u = jnp.dot(h_scr[rows, :], w1_ref[c], preferred_element_type=F32)
            u = jnp.maximum(u, 0.0)
            y = jnp.dot((u * u).astype(BF16), w2_ref[c * sub:(c + 1) * sub, :],
                        preferred_element_type=F32)
            if assign_first and c == 0:
                o_ref[0, rows, :] = y
            else:
                o_ref[0, rows, :] += y

    @pl.when(f == 0)
    def _():
        for part in parts:
            _norm_mod_rows(lambda rows: x_ref[0, rows, :], store_h, part,
                           gpre_ref[...], _mod_row(mod_ref, MOD_SC_M),
                           _mod_row(mod_ref, MOD_SH_M))
            ffn(part, assign_first=True)

    @pl.when((f > 0) & (f < last))
    def _():
        ffn(slice(0, n_rows))

    @pl.when(f == last)
    def _():
        for part in parts:
            ffn(part)
            _gated_residual_rows(lambda rows: x_ref[0, rows, :], lambda rows: o_ref[0, rows, :],
                                 store_o, part, gpost_ref[...], _mod_row(mod_ref, MOD_GT_M))


def _mlp(x, mod, g_pre, w1, w2, g_post, tm):
    b, s, d = x.shape
    n_sub = MLP_TF // MLP_SUB
    n_f, tf = w1.shape[0] // n_sub, MLP_TF
    row = lambda bi, i, f: (bi, i, 0)
    per_b = lambda bi, i, f: (bi, 0, 0)
    const = lambda bi, i, f: (0, 0)
    return pl.pallas_call(
        functools.partial(_mlp_kernel, n_split=2),
        grid=(b, s // tm, n_f),
        in_specs=[
            pl.BlockSpec((1, tm, d), row),
            pl.BlockSpec((1, N_MOD, d), per_b),
            pl.BlockSpec((1, d), const),
            pl.BlockSpec((n_sub, d, MLP_SUB), lambda bi, i, f: (f, 0, 0)),
            pl.BlockSpec((tf, d), lambda bi, i, f: (f, 0)),
            pl.BlockSpec((1, d), const),
        ],
        out_specs=pl.BlockSpec((1, tm, d), row),
        out_shape=jax.ShapeDtypeStruct((b, s, d), F32),
        scratch_shapes=[pltpu.VMEM((tm, d), BF16)],
        compiler_params=pltpu.CompilerParams(
            dimension_semantics=("arbitrary", "arbitrary", "arbitrary"),
            vmem_limit_bytes=VMEM_LIMIT),
        name="mlp",
    )(x, mod, g_pre, w1, w2, g_post)


def _cast_pad_kernel(w_ref, o_ref):
    n = w_ref.shape[1]
    o_ref[:, :n] = w_ref[...].astype(BF16)
    if o_ref.shape[1] > n:
        o_ref[:, n:] = jnp.zeros((o_ref.shape[0], o_ref.shape[1] - n), BF16)


def _cast_pad(w, tr):
    r, n = w.shape
    n_pad = -(-n // LANES) * LANES
    return pl.pallas_call(
        _cast_pad_kernel,
        grid=(r // tr,),
        in_specs=[pl.BlockSpec((tr, n), lambda i: (i, 0))],
        out_specs=pl.BlockSpec((tr, n_pad), lambda i: (i, 0)),
        out_shape=jax.ShapeDtypeStruct((r, n_pad), BF16),
        compiler_params=pltpu.CompilerParams(
            dimension_semantics=("arbitrary",), vmem_limit_bytes=VMEM_LIMIT),
        name="cast_pad",
    )(w)


def kernel(x, c, w_ada, b_ada, g_pre_attn, w_in, g_q_a, g_k_a, g_ckv, w_kv_b, g_out_a,
           g_out_b, w_out, g_post_attn, g_pre_mlp, w_mlp_in, w_mlp_out, g_post_mlp):
    b, s, d = x.shape
    depth = w_ada.shape[0]
    tabs = _rope_tables(s)
    c_pad = jnp.pad(c, ((0, 8 - b), (0, 0)))

    for l in range(depth):
        mod = _ada(c_pad, w_ada[l], b_ada[l][None, :]).reshape(8, N_MOD, d)

        q_a, k_a, v_a, q_b, k_b, v_b = _inproj(
            x, mod, g_pre_attn[l][None, :], _cast_pad(w_in[l].T, tr=608),
            _cast_pad(w_kv_b[l], tr=256), g_q_a[l][None, :], g_k_a[l][None, :],
            g_ckv[l][None, :], tabs, tm=512)

        o_a, w1_bf, w2_bf = _attention(
            q_a, k_a, v_a, [(w_mlp_in[l], MLP_SUB), (w_mlp_out[l], None)],
            n_groups=HKV, n_heads=G_A,
            dk=HEAD_DIM, shared_kv=True, tq=256, n_blocks=4, name="attn_gqa")
        o_b, wo_bf = _attention(
            q_b, k_b, v_b, [(w_out[l], None)], n_groups=HB // 4, n_heads=4,
            dk=QK_B_PAD, shared_kv=False, tq=256, n_blocks=4, name="attn_mla")

        x = _outproj(o_a, o_b, x, mod, g_out_a[l][None, :], g_out_b[l][None, :],
                     wo_bf, g_post_attn[l][None, :], tm=512)

        x = _mlp(x, mod, g_pre_mlp[l][None, :], w1_bf, w2_bf,
                 g_post_mlp[l][None, :], tm=512)
    return x
```

```python
import functools
import math

import numpy as np
import jax
import jax.numpy as jnp
from jax import lax
from jax.experimental import pallas as pl
from jax.experimental.pallas import tpu as pltpu

D_MODEL = 2048
GRID_W = 64
ROPE_THETA = 10000.0
EPS = 1e-6

HEAD_DIM = 128
HA = 8
HKV = 2
G_A = HA // HKV

HB = 8
QK_NOPE = 128
QK_ROPE = 64
V_DIM = 128
KV_RANK = 512
QK_B = QK_NOPE + QK_ROPE
QK_B_PAD = 256

W_QA = HA * HEAD_DIM
W_KA = HKV * HEAD_DIM
W_VA = HKV * HEAD_DIM
W_QB = HB * QK_B
MIX_A = HA * HEAD_DIM
MIX_B = HB * V_DIM
D_FF = 4 * D_MODEL
N_MOD = 6
N_MOD_EARLY = 2
MOD_SH_A, MOD_SC_A = range(N_MOD_EARLY)
MOD_GT_A, MOD_SH_M, MOD_SC_M, MOD_GT_M = range(N_MOD - N_MOD_EARLY)

LOG2_E = math.log2(math.e)
LANES = 128
ROW_CHUNK = 16
MLP_TF = 2048
MLP_SUB = 1024
VMEM_LIMIT = 56 * 1024 * 1024
VMEM_LIMIT_MLA = 60 * 1024 * 1024

F32 = jnp.float32
BF16 = jnp.bfloat16


def _rope_tables(seq_len):
    parts = _rope_table(seq_len, HEAD_DIM) + _rope_table(seq_len, QK_ROPE)
    return jnp.asarray(np.concatenate(parts, axis=1), F32)


def _rope_table(seq_len, dim):
    pos = np.arange(seq_len)
    row = (pos // GRID_W).astype(np.float64)
    col = (pos % GRID_W).astype(np.float64)
    half = dim // 2
    inv = ROPE_THETA ** (-np.arange(0, half, 2, dtype=np.float64) / half)
    ang_r = row[:, None] * inv[None, :]
    ang_c = col[:, None] * inv[None, :]
    ang = np.concatenate([ang_r, ang_r, ang_c, ang_c], axis=-1)
    cos, sin = np.cos(ang), np.sin(ang)
    quarter = dim // 4
    sign = np.where((np.arange(dim) % half) < quarter, -1.0, 1.0)
    sin = sin * sign[None, :]
    reps = LANES // dim
    return [np.tile(cos, (1, reps)), np.tile(sin, (1, reps))]


def _mod_row(mod_ref, idx):
    return mod_ref[0, idx:idx + 1, :]


def _unit_rms(x):
    ms = jnp.mean(x * x, axis=-1, keepdims=True)
    return x * lax.rsqrt(ms + EPS)


def _rms(x, g):
    return _unit_rms(x) * g


def _row_chunks(span):
    span = span if isinstance(span, slice) else slice(0, span)
    return [slice(r, r + ROW_CHUNK) for r in range(span.start, span.stop, ROW_CHUNK)]


def _norm_mod_rows(load, store, span, g, sc, sh):
    gain = g * (1.0 + sc)
    for rows in _row_chunks(span):
        store(rows, (_unit_rms(load(rows)) * gain + sh).astype(BF16))


def _gated_residual_rows(load_x, load_y, store, span, g, gt):
    gain = gt * g
    for rows in _row_chunks(span):
        store(rows, load_x(rows) + _unit_rms(load_y(rows)) * gain)


def _dot_nt(a, w_rows):
    return lax.dot_general(a, w_rows, (((1,), (1,)), ((), ())), preferred_element_type=F32)


def _rope(x, cos, sin_signed, quarter):
    lane = lax.broadcasted_iota(jnp.int32, x.shape, 1)
    take_up = (lane % (2 * quarter)) < quarter
    up = pltpu.roll(x, LANES - quarter, axis=1)
    down = pltpu.roll(x, quarter, axis=1)
    rot = jnp.where(take_up, up, down)
    return x * cos + rot * sin_signed


def _ada_columns(c, w, b):
    c_act = (c * jax.nn.sigmoid(c)).astype(BF16)
    return jnp.dot(c_act, w.astype(BF16), preferred_element_type=F32) + b


def _ada_kernel(c_ref, w_ref, b_ref, o_ref):
    o_ref[...] = _ada_columns(c_ref[...], w_ref[...], b_ref[...])


def _ada(c_pad, w_ada, b_ada, n):
    m, d = c_pad.shape
    tn = 1024
    return pl.pallas_call(
        _ada_kernel,
        grid=(n // tn,),
        in_specs=[
            pl.BlockSpec((m, d), lambda j: (0, 0)),
            pl.BlockSpec((d, tn), lambda j: (0, j)),
            pl.BlockSpec((1, tn), lambda j: (0, j)),
        ],
        out_specs=pl.BlockSpec((m, tn), lambda j: (0, j)),
        out_shape=jax.ShapeDtypeStruct((m, n), F32),
        compiler_params=pltpu.CompilerParams(
            dimension_semantics=("arbitrary",), vmem_limit_bytes=VMEM_LIMIT),
        name="ada_mod",
    )(c_pad, w_ada, b_ada)


def _inproj_kernel(x_ref, mod_ref, gpre_ref, w_ref, wkv_ref, gq_ref, gk_ref,
                   gckv_ref, tab_ref,
                   qa_ref, ka_ref, va_ref, qb_ref, kb_ref, vb_ref, h_scr):
    tm = h_scr.shape[0]

    def store_h(rows, val):
        h_scr[rows, :] = val

    _norm_mod_rows(lambda rows: x_ref[0, rows, :], store_h, tm,
                   gpre_ref[...], _mod_row(mod_ref, MOD_SC_A), _mod_row(mod_ref, MOD_SH_A))
    hb = h_scr[...]
    cosa, sina, cosb, sinb = (tab_ref[:, i * LANES:(i + 1) * LANES] for i in range(4))
    scale_a = LOG2_E / math.sqrt(HEAD_DIM)
    scale_b = LOG2_E / math.sqrt(QK_B)

    qa = _dot_nt(hb, w_ref[0:W_QA, :])
    gq = gq_ref[...] * scale_a
    for hd in range(HA):
        sl = slice(hd * HEAD_DIM, (hd + 1) * HEAD_DIM)
        q = _rope(_rms(qa[:, sl], gq), cosa, sina, HEAD_DIM // 4)
        qa_ref[0, :, sl] = q.astype(BF16)

    kva = _dot_nt(hb, w_ref[W_QA:W_QA + W_KA + W_VA, :])
    for hd in range(HKV):
        sl = slice(hd * HEAD_DIM, (hd + 1) * HEAD_DIM)
        k = _rope(_rms(kva[:, sl], gk_ref[...]), cosa, sina, HEAD_DIM // 4)
        ka_ref[0, :, sl] = k.astype(BF16)
    va_ref[0] = kva[:, W_KA:].astype(BF16)

    o_qb = W_QA + W_KA + W_VA
    qb = _dot_nt(hb, w_ref[o_qb:o_qb + W_QB, :])
    lane = lax.broadcasted_iota(jnp.int32, (tm, LANES), 1)
    low = lane < QK_ROPE
    swap = lambda t: pltpu.roll(t, QK_ROPE, axis=1)
    for pair in range(HB // 2):
        t0, t1, t2 = (qb[:, (3 * pair + i) * LANES:(3 * pair + i + 1) * LANES] for i in range(3))
        r1 = _rope(t1, cosb, sinb, QK_ROPE // 4)
        r2 = _rope(t2, cosb, sinb, QK_ROPE // 4)
        heads = (
            (2 * pair, t0, jnp.where(low, r1, 0.0)),
            (2 * pair + 1, jnp.where(low, swap(t1), swap(t2)), jnp.where(low, swap(r2), 0.0)),
        )
        for hd, nope, pe in heads:
            base = hd * QK_B_PAD
            qb_ref[0, :, base:base + QK_NOPE] = (nope * scale_b).astype(BF16)
            qb_ref[0, :, base + QK_NOPE:base + QK_B_PAD] = (pe * scale_b).astype(BF16)

    o_ckv = o_qb + W_QB
    w_rest = jnp.concatenate(
        [w_ref[o_ckv:, :], jnp.zeros((LANES - QK_ROPE, w_ref.shape[1]), BF16)], axis=0)
    rest = _dot_nt(hb, w_rest)
    ckv = _rms(rest[:, :KV_RANK], gckv_ref[...]).astype(BF16)
    kv = jnp.dot(ckv, wkv_ref[...], preferred_element_type=F32)
    kpe = _rope(rest[:, KV_RANK:], cosb, sinb, QK_ROPE // 4).astype(BF16)
    for hd in range(HB):
        base = hd * (QK_NOPE + V_DIM)
        kb_ref[0, :, hd * QK_B_PAD:hd * QK_B_PAD + QK_NOPE] = (
            kv[:, base:base + QK_NOPE].astype(BF16))
        kb_ref[0, :, hd * QK_B_PAD + QK_NOPE:(hd + 1) * QK_B_PAD] = kpe
        vb_ref[0, :, hd * V_DIM:(hd + 1) * V_DIM] = (
            kv[:, base + QK_NOPE:base + QK_NOPE + V_DIM].astype(BF16))


def _inproj(x, mod, g_pre, w_in_p, w_kv, g_q, g_k, g_ckv, tabs, tm):
    b, s, d = x.shape
    n_in = w_in_p.shape[0]
    row = lambda bi, i: (bi, i, 0)
    per_b = lambda bi, i: (bi, 0, 0)
    const = lambda bi, i: (0, 0)
    tab = lambda bi, i: (i, 0)
    vec = lambda n: pl.BlockSpec((1, n), const)
    out_widths = (MIX_A, W_KA, W_VA, HB * QK_B_PAD, HB * QK_B_PAD, MIX_B)
    return pl.pallas_call(
        _inproj_kernel,
        grid=(b, s // tm),
        in_specs=[
            pl.BlockSpec((1, tm, d), row),
            pl.BlockSpec((1, N_MOD_EARLY, d), per_b),
            vec(d),
            pl.BlockSpec((n_in, d), const, pipeline_mode=pl.Buffered(1)),
            pl.BlockSpec((KV_RANK, HB * (QK_NOPE + V_DIM)), const,
                         pipeline_mode=pl.Buffered(1)),
            vec(HEAD_DIM), vec(HEAD_DIM), vec(KV_RANK),
            pl.BlockSpec((tm, 4 * LANES), tab),
        ],
        out_specs=[pl.BlockSpec((1, tm, w), row) for w in out_widths],
        out_shape=[jax.ShapeDtypeStruct((b, s, w), BF16) for w in out_widths],
        scratch_shapes=[pltpu.VMEM((tm, d), BF16)],
        compiler_params=pltpu.CompilerParams(
            dimension_semantics=("arbitrary", "arbitrary"),
            vmem_limit_bytes=VMEM_LIMIT),
        name="in_proj",
    )(x, mod, g_pre, w_in_p, w_kv, g_q, g_k, g_ckv, tabs)


def _attn_kernel(*refs, n_heads, dk, shared_kv, tq, n_blocks, n_side, with_ada):
    q0_ref, q_refs = refs[0], refs[1:1 + n_blocks]
    k_ref, kn_ref, v_ref = refs[1 + n_blocks:4 + n_blocks]
    rest = list(refs[4 + n_blocks:])
    side_in = [rest.pop(0) for _ in range(n_side)]
    ada_in = [rest.pop(0) for _ in range(3 if with_ada else 0)]
    o_ref = rest.pop(0)
    side_out = [rest.pop(0) for _ in range(n_side)]
    ada_out = [rest.pop(0) for _ in range(1 if with_ada else 0)]
    s_scr, m_scr = rest

    def scores(q_ref, keys_ref, slot):
        for hd in range(n_heads):
            kv_hd = 0 if shared_kv else hd
            k = keys_ref[0, :, kv_hd * dk:(kv_hd + 1) * dk]
            q = q_ref[0, :, hd * dk:(hd + 1) * dk]
            s = lax.dot_general(q, k, (((1,), (1,)), ((), ())), preferred_element_type=F32)
            s_scr[slot, hd] = s
            m_scr[slot, hd] = jnp.max(s, axis=-1, keepdims=True)

    def softmax_pv(slot, row0):
        for hd in range(n_heads):
            kv_hd = 0 if shared_kv else hd
            v = v_ref[0, :, kv_hd * V_DIM:(kv_hd + 1) * V_DIM]
            p = jnp.exp2(s_scr[slot, hd] - m_scr[slot, hd]).astype(BF16)
            v_ext = jnp.concatenate([v, jnp.ones_like(v)], axis=1)
            o_ext = jnp.dot(p, v_ext, preferred_element_type=F32)
            o = o_ext[:, :V_DIM] / o_ext[:, V_DIM:]
            o_ref[0, row0:row0 + tq, hd * V_DIM:(hd + 1) * V_DIM] = o.astype(BF16)

    @pl.when(pl.program_id(0) == 0)
    def _():
        scores(q0_ref, k_ref, 0)

    for i in range(n_blocks):
        keys_ref = kn_ref if i == n_blocks - 1 else k_ref
        scores(q_refs[i], keys_ref, (i + 1) % 2)
        for w_ref, wb_ref in zip(side_in, side_out):
            n = w_ref.shape[0] // n_blocks
            rows = slice(i * n, (i + 1) * n)
            if len(wb_ref.shape) == 2:
                wb_ref[rows, :] = w_ref[rows, :].astype(BF16)
            else:
                cb = wb_ref.shape[2]
                for f in range(wb_ref.shape[0]):
                    wb_ref[f, rows, :] = w_ref[rows, f * cb:(f + 1) * cb].astype(BF16)
        if with_ada and i == 0:
            c_ref, wada_ref, bada_ref = ada_in
            ada_out[0][...] = _ada_columns(c_ref[...], wada_ref[...], bada_ref[...])
        softmax_pv(i % 2, i * tq)


def _attention(q, k, v, side, *, n_groups, n_heads, dk, shared_kv, tq, n_blocks, name,
               ada=None, vmem_limit=VMEM_LIMIT):
    b, s, _ = q.shape
    n_kv = 1 if shared_kv else n_heads
    n_blk = s // tq
    assert n_blocks % 2 == 0 and n_blk % n_blocks == 0
    n_total = b * n_groups * n_blk
    n_steps = n_total // n_blocks

    def q_map(t):
        return (t // (n_groups * n_blk), t % n_blk, (t // n_blk) % n_groups)

    def kv_map(t):
        return (t // (n_groups * n_blk), 0, (t // n_blk) % n_groups)

    def out_map(j):
        bi, i, g = q_map(n_blocks * j)
        return (bi, i // n_blocks, g)

    def ahead(i):
        return lambda j: q_map(jnp.minimum(n_blocks * j + i + 1, n_total - 1))

    q_spec = lambda f: pl.BlockSpec((1, tq, n_heads * dk), f)
    side_in_specs, side_out_specs, side_shapes = [], [], []
    for w, col_block in side:
        rows, cols = w.shape[0] // n_steps, w.shape[1]
        side_in_specs.append(pl.BlockSpec((rows, cols), lambda j: (j, 0)))
        if col_block is None:
            side_out_specs.append(pl.BlockSpec((rows, cols), lambda j: (j, 0)))
            side_shapes.append(jax.ShapeDtypeStruct(w.shape, BF16))
        else:
            n_cb = cols // col_block
            side_out_specs.append(
                pl.BlockSpec((n_cb, rows, col_block), lambda j: (0, j, 0)))
            side_shapes.append(jax.ShapeDtypeStruct((n_cb, w.shape[0], col_block), BF16))
    ada_args, ada_in_specs, ada_out_specs, ada_shapes = [], [], [], []
    if ada is not None:
        c, w_ada, b_ada, first_col = ada
        n_cols = w_ada.shape[1] - first_col
        tn = n_cols // n_steps
        skip = first_col // tn
        ada_args = [c, w_ada, b_ada]
        ada_in_specs = [
            pl.BlockSpec(c.shape, lambda j: (0, 0)),
            pl.BlockSpec((w_ada.shape[0], tn), lambda j: (0, skip + j)),
            pl.BlockSpec((1, tn), lambda j: (0, skip + j)),
        ]
        ada_out_specs = [pl.BlockSpec((c.shape[0], tn), lambda j: (0, j))]
        ada_shapes = [jax.ShapeDtypeStruct((c.shape[0], n_cols), F32)]
    return pl.pallas_call(
        functools.partial(_attn_kernel, n_heads=n_heads, dk=dk, shared_kv=shared_kv,
                          tq=tq, n_blocks=n_blocks, n_side=len(side),
                          with_ada=ada is not None),
        grid=(n_steps,),
        in_specs=[q_spec(lambda j: (0, 0, 0))]
        + [q_spec(ahead(i)) for i in range(n_blocks)]
        + [
            pl.BlockSpec((1, s, n_kv * dk), lambda j: kv_map(n_blocks * j)),
            pl.BlockSpec((1, s, n_kv * dk),
                         lambda j: kv_map(jnp.minimum(n_blocks * (j + 1), n_total - 1))),
            pl.BlockSpec((1, s, n_kv * V_DIM), lambda j: kv_map(n_blocks * j)),
        ] + side_in_specs + ada_in_specs,
        out_specs=[pl.BlockSpec((1, n_blocks * tq, n_heads * V_DIM), out_map)]
        + side_out_specs + ada_out_specs,
        out_shape=[jax.ShapeDtypeStruct((b, s, n_groups * n_heads * V_DIM), BF16)]
        + side_shapes + ada_shapes,
        scratch_shapes=[pltpu.VMEM((2, n_heads, tq, s), F32),
                        pltpu.VMEM((2, n_heads, tq, 1), F32)],
        compiler_params=pltpu.CompilerParams(
            dimension_semantics=("arbitrary",), vmem_limit_bytes=vmem_limit),
        name=name,
    )(*([q] * (n_blocks + 1)), k, k, v, *[w for w, _ in side], *ada_args)


def _outproj_kernel(oa_ref, ob_ref, x_ref, mod_ref, ga_ref, gb_ref, w_ref, gpost_ref, o_ref,
                    n_scr, y_scr):
    n_rows = n_scr.shape[0]

    def store_o(rows, val):
        o_ref[0, rows, :] = val

    for rows in _row_chunks(n_rows):
        n_scr[rows, 0:MIX_A] = _rms(oa_ref[0, rows, :].astype(F32), ga_ref[...]).astype(BF16)
        n_scr[rows, MIX_A:] = _rms(ob_ref[0, rows, :].astype(F32), gb_ref[...]).astype(BF16)
    y_scr[...] = jnp.dot(n_scr[...], w_ref[...], preferred_element_type=F32)
    _gated_residual_rows(lambda rows: x_ref[0, rows, :], lambda rows: y_scr[rows, :],
                         store_o, n_rows, gpost_ref[...], _mod_row(mod_ref, MOD_GT_A))


def _outproj(o_a, o_b, x, mod, g_a, g_b, w_out, g_post, tm):
    b, s, d = x.shape
    row = lambda bi, i: (bi, i, 0)
    per_b = lambda bi, i: (bi, 0, 0)
    const = lambda bi, i: (0, 0)
    return pl.pallas_call(
        _outproj_kernel,
        grid=(b, s // tm),
        in_specs=[
            pl.BlockSpec((1, tm, MIX_A), row),
            pl.BlockSpec((1, tm, MIX_B), row),
            pl.BlockSpec((1, tm, d), row),
            pl.BlockSpec((1, N_MOD - N_MOD_EARLY, d), per_b),
            pl.BlockSpec((1, MIX_A), const),
            pl.BlockSpec((1, MIX_B), const),
            pl.BlockSpec((MIX_A + MIX_B, d), const, pipeline_mode=pl.Buffered(1)),
            pl.BlockSpec((1, d), const),
        ],
        out_specs=pl.BlockSpec((1, tm, d), row),
        out_shape=jax.ShapeDtypeStruct((b, s, d), F32),
        scratch_shapes=[pltpu.VMEM((tm, MIX_A + MIX_B), BF16), pltpu.VMEM((tm, d), F32)],
        compiler_params=pltpu.CompilerParams(
            dimension_semantics=("arbitrary", "arbitrary"),
            vmem_limit_bytes=VMEM_LIMIT),
        name="out_proj",
    )(o_a, o_b, x, mod, g_a, g_b, w_out, g_post)


def _mlp_kernel(x_ref, mod_ref, gpre_ref, w1_ref, w2_ref, gpost_ref,
                o_ref, h_scr, *, n_split):
    f = pl.program_id(2)
    last = pl.num_programs(2) - 1
    n_rows = h_scr.shape[0]
    parts = [slice(r, r + n_rows // n_split) for r in range(0, n_rows, n_rows // n_split)]

    def store_h(rows, val):
        h_scr[rows, :] = val

    def store_o(rows, val):
        o_ref[0, rows, :] = val

    def ffn(rows, assign_first=False):
        n_sub, _, sub = w1_ref.shape
        for c in range(n_sub):
            u = jnp.dot(h_scr[rows, :], w1_ref[c], preferred_element_type=F32)
            u = jnp.maximum(u, 0.0)
            y = jnp.dot((u * u).astype(BF16), w2_ref[c * sub:(c + 1) * sub, :],
                        preferred_element_type=F32)
            if assign_first and c == 0:
                o_ref[0, rows, :] = y
            else:
                o_ref[0, rows, :] += y

    @pl.when(f == 0)
    def _():
        for part in parts:
            _norm_mod_rows(lambda rows: x_ref[0, rows, :], store_h, part,
                           gpre_ref[...], _mod_row(mod_ref, MOD_SC_M),
                           _mod_row(mod_ref, MOD_SH_M))
            ffn(part, assign_first=True)

    @pl.when((f > 0) & (f < last))
    def _():
        ffn(slice(0, n_rows))

    @pl.when(f == last)
    def _():
        for part in parts:
            ffn(part)
            _gated_residual_rows(lambda rows: x_ref[0, rows, :], lambda rows: o_ref[0, rows, :],
                                 store_o, part, gpost_ref[...], _mod_row(mod_ref, MOD_GT_M))


def _mlp(x, mod, g_pre, w1, w2, g_post, tm):
    b, s, d = x.shape
    n_sub = MLP_TF // MLP_SUB
    n_f, tf = w1.shape[0] // n_sub, MLP_TF
    row = lambda bi, i, f: (bi, i, 0)
    per_b = lambda bi, i, f: (bi, 0, 0)
    const = lambda bi, i, f: (0, 0)
    return pl.pallas_call(
        functools.partial(_mlp_kernel, n_split=2),
        grid=(b, s // tm, n_f),
        in_specs=[
            pl.BlockSpec((1, tm, d), row),
            pl.BlockSpec((1, N_MOD - N_MOD_EARLY, d), per_b),
            pl.BlockSpec((1, d), const),
            pl.BlockSpec((n_sub, d, MLP_SUB), lambda bi, i, f: (f, 0, 0)),
            pl.BlockSpec((tf, d), lambda bi, i, f: (f, 0)),
            pl.BlockSpec((1, d), const),
        ],
        out_specs=pl.BlockSpec((1, tm, d), row),
        out_shape=jax.ShapeDtypeStruct((b, s, d), F32),
        scratch_shapes=[pltpu.VMEM((tm, d), BF16)],
        compiler_params=pltpu.CompilerParams(
            dimension_semantics=("arbitrary", "arbitrary", "arbitrary"),
            vmem_limit_bytes=VMEM_LIMIT),
        name="mlp",
    )(x, mod, g_pre, w1, w2, g_post)


def _cast_pad_kernel(w_ref, o_ref):
    n = w_ref.shape[1]
    o_ref[:, :n] = w_ref[...].astype(BF16)
    if o_ref.shape[1] > n:
        o_ref[:, n:] = jnp.zeros((o_ref.shape[0], o_ref.shape[1] - n), BF16)


def _cast_pad(w, tr):
    r, n = w.shape
    n_pad = -(-n // LANES) * LANES
    return pl.pallas_call(
        _cast_pad_kernel,
        grid=(r // tr,),
        in_specs=[pl.BlockSpec((tr, n), lambda i: (i, 0))],
        out_specs=pl.BlockSpec((tr, n_pad), lambda i: (i, 0)),
        out_shape=jax.ShapeDtypeStruct((r, n_pad), BF16),
        compiler_params=pltpu.CompilerParams(
            dimension_semantics=("arbitrary",), vmem_limit_bytes=VMEM_LIMIT),
        name="cast_pad",
    )(w)


def kernel(x, c, w_ada, b_ada, g_pre_attn, w_in, g_q_a, g_k_a, g_ckv, w_kv_b, g_out_a,
           g_out_b, w_out, g_post_attn, g_pre_mlp, w_mlp_in, w_mlp_out, g_post_mlp):
    b, s, d = x.shape
    depth = w_ada.shape[0]
    tabs = _rope_tables(s)
    c_pad = jnp.pad(c, ((0, 8 - b), (0, 0)))

    for l in range(depth):
        b_ada_l = b_ada[l][None, :]
        n_early = N_MOD_EARLY * d
        mod_early = _ada(c_pad, w_ada[l], b_ada_l, n_early).reshape(8, N_MOD_EARLY, d)

        q_a, k_a, v_a, q_b, k_b, v_b = _inproj(
            x, mod_early, g_pre_attn[l][None, :], _cast_pad(w_in[l].T, tr=608),
            _cast_pad(w_kv_b[l], tr=256), g_q_a[l][None, :], g_k_a[l][None, :],
            g_ckv[l][None, :], tabs, tm=512)

        o_a, w1_bf, w2_bf = _attention(
            q_a, k_a, v_a, [(w_mlp_in[l], MLP_SUB), (w_mlp_out[l], None)],
            n_groups=HKV, n_heads=G_A,
            dk=HEAD_DIM, shared_kv=True, tq=256, n_blocks=4, name="attn_gqa")
        o_b, wo_bf, mod_late = _attention(
            q_b, k_b, v_b, [(w_out[l], None)], n_groups=HB // 4, n_heads=4,
            dk=QK_B_PAD, shared_kv=False, tq=256, n_blocks=4, name="attn_mla",
            ada=(c_pad, w_ada[l], b_ada_l, n_early), vmem_limit=VMEM_LIMIT_MLA)
        mod_late = mod_late.reshape(8, N_MOD - N_MOD_EARLY, d)

        x = _outproj(o_a, o_b, x, mod_late, g_out_a[l][None, :], g_out_b[l][None, :],
                     wo_bf, g_post_attn[l][None, :], tm=512)

        x = _mlp(x, mod_late, g_pre_mlp[l][None, :], w1_bf, w2_bf,
                 g_post_mlp[l][None, :], tm=512)
    return x
```

```python
import functools
import math

import numpy as np
import jax
import jax.numpy as jnp
from jax import lax
from jax.experimental import pallas as pl
from jax.experimental.pallas import tpu as pltpu

D_MODEL = 2048
GRID_W = 64
ROPE_THETA = 10000.0
EPS = 1e-6

HEAD_DIM = 128
HA = 8
HKV = 2
G_A = HA // HKV

HB = 8
QK_NOPE = 128
QK_ROPE = 64
V_DIM = 128
KV_RANK = 512
QK_B = QK_NOPE + QK_ROPE
QK_B_PAD = 256

W_QA = HA * HEAD_DIM
W_KA = HKV * HEAD_DIM
W_VA = HKV * HEAD_DIM
W_QB = HB * QK_B
MIX_A = HA * HEAD_DIM
MIX_B = HB * V_DIM
D_FF = 4 * D_MODEL
N_MOD = 6
N_MOD_EARLY = 2
MOD_SH_A, MOD_SC_A = range(N_MOD_EARLY)
MOD_GT_A, MOD_SH_M, MOD_SC_M, MOD_GT_M = range(N_MOD - N_MOD_EARLY)

LOG2_E = math.log2(math.e)
LANES = 128
ROW_CHUNK = 16
MLP_TF = 2048
MLP_SUB = 1024
VMEM_LIMIT = 56 * 1024 * 1024
VMEM_LIMIT_MLA = 60 * 1024 * 1024

F32 = jnp.float32
BF16 = jnp.bfloat16


def _rope_tables(seq_len):
    parts = _rope_table(seq_len, HEAD_DIM) + _rope_table(seq_len, QK_ROPE)
    return jnp.asarray(np.concatenate(parts, axis=1), F32)


def _rope_table(seq_len, dim):
    pos = np.arange(seq_len)
    row = (pos // GRID_W).astype(np.float64)
    col = (pos % GRID_W).astype(np.float64)
    half = dim // 2
    inv = ROPE_THETA ** (-np.arange(0, half, 2, dtype=np.float64) / half)
    ang_r = row[:, None] * inv[None, :]
    ang_c = col[:, None] * inv[None, :]
    ang = np.concatenate([ang_r, ang_r, ang_c, ang_c], axis=-1)
    cos, sin = np.cos(ang), np.sin(ang)
    quarter = dim // 4
    sign = np.where((np.arange(dim) % half) < quarter, -1.0, 1.0)
    sin = sin * sign[None, :]
    reps = LANES // dim
    return [np.tile(cos, (1, reps)), np.tile(sin, (1, reps))]


def _mod_row(mod_ref, bi, idx):
    return mod_ref[pl.ds(bi, 1), idx * D_MODEL:(idx + 1) * D_MODEL]


def _unit_rms(x):
    ms = jnp.mean(x * x, axis=-1, keepdims=True)
    return x * lax.rsqrt(ms + EPS)


def _rms(x, g):
    return _unit_rms(x) * g


def _row_chunks(span):
    span = span if isinstance(span, slice) else slice(0, span)
    return [slice(r, r + ROW_CHUNK) for r in range(span.start, span.stop, ROW_CHUNK)]


def _norm_mod_rows(load, store, span, g, sc, sh):
    gain = g * (1.0 + sc)
    for rows in _row_chunks(span):
        store(rows, (_unit_rms(load(rows)) * gain + sh).astype(BF16))


def _gated_residual_rows(load_x, load_y, store, span, g, gt):
    gain = gt * g
    for rows in _row_chunks(span):
        store(rows, load_x(rows) + _unit_rms(load_y(rows)) * gain)


def _dot_nt(a, w_rows):
    return lax.dot_general(a, w_rows, (((1,), (1,)), ((), ())), preferred_element_type=F32)


def _rope(x, cos, sin_signed, quarter):
    lane = lax.broadcasted_iota(jnp.int32, x.shape, 1)
    take_up = (lane % (2 * quarter)) < quarter
    up = pltpu.roll(x, LANES - quarter, axis=1)
    down = pltpu.roll(x, quarter, axis=1)
    rot = jnp.where(take_up, up, down)
    return x * cos + rot * sin_signed


def _ada_columns(c, w, b):
    c_act = (c * jax.nn.sigmoid(c)).astype(BF16)
    return jnp.dot(c_act, w.astype(BF16), preferred_element_type=F32) + b


def _ada_kernel(c_ref, w_ref, b_ref, o_ref):
    o_ref[...] = _ada_columns(c_ref[...], w_ref[...], b_ref[...])


def _ada(c_pad, w_ada, b_ada, n):
    m, d = c_pad.shape
    tn = 512
    return pl.pallas_call(
        _ada_kernel,
        grid=(n // tn,),
        in_specs=[
            pl.BlockSpec((m, d), lambda j: (0, 0)),
            pl.BlockSpec((d, tn), lambda j: (0, j)),
            pl.BlockSpec((1, tn), lambda j: (0, j)),
        ],
        out_specs=pl.BlockSpec((m, tn), lambda j: (0, j)),
        out_shape=jax.ShapeDtypeStruct((m, n), F32),
        compiler_params=pltpu.CompilerParams(
            dimension_semantics=("arbitrary",), vmem_limit_bytes=VMEM_LIMIT),
        name="ada_mod",
    )(c_pad, w_ada, b_ada)


def _inproj_kernel(x_ref, mod_ref, gpre_ref, w_ref, wkv_ref, gq_ref, gk_ref,
                   gckv_ref, tab_ref,
                   qa_ref, ka_ref, va_ref, qb_ref, kb_ref, vb_ref, h_scr):
    tm = h_scr.shape[0]
    bi = pl.program_id(0)

    def store_h(rows, val):
        h_scr[rows, :] = val

    _norm_mod_rows(lambda rows: x_ref[0, rows, :], store_h, tm, gpre_ref[...],
                   _mod_row(mod_ref, bi, MOD_SC_A), _mod_row(mod_ref, bi, MOD_SH_A))
    hb = h_scr[...]
    cosa, sina, cosb, sinb = (tab_ref[:, i * LANES:(i + 1) * LANES] for i in range(4))
    scale_a = LOG2_E / math.sqrt(HEAD_DIM)
    scale_b = LOG2_E / math.sqrt(QK_B)

    qa = _dot_nt(hb, w_ref[0:W_QA, :])
    gq = gq_ref[...] * scale_a
    for hd in range(HA):
        sl = slice(hd * HEAD_DIM, (hd + 1) * HEAD_DIM)
        q = _rope(_rms(qa[:, sl], gq), cosa, sina, HEAD_DIM // 4)
        qa_ref[0, :, sl] = q.astype(BF16)

    kva = _dot_nt(hb, w_ref[W_QA:W_QA + W_KA + W_VA, :])
    for hd in range(HKV):
        sl = slice(hd * HEAD_DIM, (hd + 1) * HEAD_DIM)
        k = _rope(_rms(kva[:, sl], gk_ref[...]), cosa, sina, HEAD_DIM // 4)
        ka_ref[0, :, sl] = k.astype(BF16)
    va_ref[0] = kva[:, W_KA:].astype(BF16)

    o_qb = W_QA + W_KA + W_VA
    qb = _dot_nt(hb, w_ref[o_qb:o_qb + W_QB, :])
    lane = lax.broadcasted_iota(jnp.int32, (tm, LANES), 1)
    low = lane < QK_ROPE
    swap = lambda t: pltpu.roll(t, QK_ROPE, axis=1)
    for pair in range(HB // 2):
        t0, t1, t2 = (qb[:, (3 * pair + i) * LANES:(3 * pair + i + 1) * LANES] for i in range(3))
        r1 = _rope(t1, cosb, sinb, QK_ROPE // 4)
        r2 = _rope(t2, cosb, sinb, QK_ROPE // 4)
        heads = (
            (2 * pair, t0, jnp.where(low, r1, 0.0)),
            (2 * pair + 1, jnp.where(low, swap(t1), swap(t2)), jnp.where(low, swap(r2), 0.0)),
        )
        for hd, nope, pe in heads:
            base = hd * QK_B_PAD
            qb_ref[0, :, base:base + QK_NOPE] = (nope * scale_b).astype(BF16)
            qb_ref[0, :, base + QK_NOPE:base + QK_B_PAD] = (pe * scale_b).astype(BF16)

    o_ckv = o_qb + W_QB
    w_rest = jnp.concatenate(
        [w_ref[o_ckv:, :], jnp.zeros((LANES - QK_ROPE, w_ref.shape[1]), BF16)], axis=0)
    rest = _dot_nt(hb, w_rest)
    ckv = _rms(rest[:, :KV_RANK], gckv_ref[...]).astype(BF16)
    kv = jnp.dot(ckv, wkv_ref[...], preferred_element_type=F32)
    kpe = _rope(rest[:, KV_RANK:], cosb, sinb, QK_ROPE // 4).astype(BF16)
    for hd in range(HB):
        base = hd * (QK_NOPE + V_DIM)
        kb_ref[0, :, hd * QK_B_PAD:hd * QK_B_PAD + QK_NOPE] = (
            kv[:, base:base + QK_NOPE].astype(BF16))
        kb_ref[0, :, hd * QK_B_PAD + QK_NOPE:(hd + 1) * QK_B_PAD] = kpe
        vb_ref[0, :, hd * V_DIM:(hd + 1) * V_DIM] = (
            kv[:, base + QK_NOPE:base + QK_NOPE + V_DIM].astype(BF16))


def _inproj(x, mod, g_pre, w_in_p, w_kv, g_q, g_k, g_ckv, tabs, tm):
    b, s, d = x.shape
    n_in = w_in_p.shape[0]
    row = lambda bi, i: (bi, i, 0)
    const = lambda bi, i: (0, 0)
    tab = lambda bi, i: (i, 0)
    vec = lambda n: pl.BlockSpec((1, n), const)
    out_widths = (MIX_A, W_KA, W_VA, HB * QK_B_PAD, HB * QK_B_PAD, MIX_B)
    return pl.pallas_call(
        _inproj_kernel,
        grid=(b, s // tm),
        in_specs=[
            pl.BlockSpec((1, tm, d), row),
            pl.BlockSpec(mod.shape, const),
            vec(d),
            pl.BlockSpec((n_in, d), const, pipeline_mode=pl.Buffered(1)),
            pl.BlockSpec((KV_RANK, HB * (QK_NOPE + V_DIM)), const,
                         pipeline_mode=pl.Buffered(1)),
            vec(HEAD_DIM), vec(HEAD_DIM), vec(KV_RANK),
            pl.BlockSpec((tm, 4 * LANES), tab),
        ],
        out_specs=[pl.BlockSpec((1, tm, w), row) for w in out_widths],
        out_shape=[jax.ShapeDtypeStruct((b, s, w), BF16) for w in out_widths],
        scratch_shapes=[pltpu.VMEM((tm, d), BF16)],
        compiler_params=pltpu.CompilerParams(
            dimension_semantics=("arbitrary", "arbitrary"),
            vmem_limit_bytes=VMEM_LIMIT),
        name="in_proj",
    )(x, mod, g_pre, w_in_p, w_kv, g_q, g_k, g_ckv, tabs)


def _attn_kernel(*refs, n_heads, dk, shared_kv, tq, n_blocks, n_side, with_ada):
    q0_ref, q_refs = refs[0], refs[1:1 + n_blocks]
    k_ref, kn_ref, v_ref = refs[1 + n_blocks:4 + n_blocks]
    rest = list(refs[4 + n_blocks:])
    side_in = [rest.pop(0) for _ in range(n_side)]
    ada_in = [rest.pop(0) for _ in range(3 if with_ada else 0)]
    o_ref = rest.pop(0)
    side_out = [rest.pop(0) for _ in range(n_side)]
    ada_out = [rest.pop(0) for _ in range(1 if with_ada else 0)]
    s_scr, m_scr = rest

    def scores(q_ref, keys_ref, slot):
        for hd in range(n_heads):
            kv_hd = 0 if shared_kv else hd
            k = keys_ref[0, :, kv_hd * dk:(kv_hd + 1) * dk]
            q = q_ref[0, :, hd * dk:(hd + 1) * dk]
            s = lax.dot_general(q, k, (((1,), (1,)), ((), ())), preferred_element_type=F32)
            s_scr[slot, hd] = s
            m_scr[slot, hd] = jnp.max(s, axis=-1, keepdims=True)

    def softmax_pv(slot, row0):
        for hd in range(n_heads):
            kv_hd = 0 if shared_kv else hd
            v = v_ref[0, :, kv_hd * V_DIM:(kv_hd + 1) * V_DIM]
            p = jnp.exp2(s_scr[slot, hd] - m_scr[slot, hd]).astype(BF16)
            v_ext = jnp.concatenate([v, jnp.ones_like(v)], axis=1)
            o_ext = jnp.dot(p, v_ext, preferred_element_type=F32)
            o = o_ext[:, :V_DIM] / o_ext[:, V_DIM:]
            o_ref[0, row0:row0 + tq, hd * V_DIM:(hd + 1) * V_DIM] = o.astype(BF16)

    @pl.when(pl.program_id(0) == 0)
    def _():
        scores(q0_ref, k_ref, 0)

    for i in range(n_blocks):
        keys_ref = kn_ref if i == n_blocks - 1 else k_ref
        scores(q_refs[i], keys_ref, (i + 1) % 2)
        for w_ref, wb_ref in zip(side_in, side_out):
            n = w_ref.shape[0] // n_blocks
            rows = slice(i * n, (i + 1) * n)
            if len(wb_ref.shape) == 2:
                wb_ref[rows, :] = w_ref[rows, :].astype(BF16)
            else:
                cb = wb_ref.shape[2]
                for f in range(wb_ref.shape[0]):
                    wb_ref[f, rows, :] = w_ref[rows, f * cb:(f + 1) * cb].astype(BF16)
        if with_ada and i == 0:
            c_ref, wada_ref, bada_ref = ada_in
            ada_out[0][...] = _ada_columns(c_ref[...], wada_ref[...], bada_ref[...])
        softmax_pv(i % 2, i * tq)


def _attention(q, k, v, side, *, n_groups, n_heads, dk, shared_kv, tq, n_blocks, name,
               ada=None, vmem_limit=VMEM_LIMIT):
    b, s, _ = q.shape
    n_kv = 1 if shared_kv else n_heads
    n_blk = s // tq
    assert n_blocks % 2 == 0 and n_blk % n_blocks == 0
    n_total = b * n_groups * n_blk
    n_steps = n_total // n_blocks

    def q_map(t):
        return (t // (n_groups * n_blk), t % n_blk, (t // n_blk) % n_groups)

    def kv_map(t):
        return (t // (n_groups * n_blk), 0, (t // n_blk) % n_groups)

    def out_map(j):
        bi, i, g = q_map(n_blocks * j)
        return (bi, i // n_blocks, g)

    def ahead(i):
        return lambda j: q_map(jnp.minimum(n_blocks * j + i + 1, n_total - 1))

    q_spec = lambda f: pl.BlockSpec((1, tq, n_heads * dk), f)
    side_in_specs, side_out_specs, side_shapes = [], [], []
    for w, col_block in side:
        rows, cols = w.shape[0] // n_steps, w.shape[1]
        side_in_specs.append(pl.BlockSpec((rows, cols), lambda j: (j, 0)))
        if col_block is None:
            side_out_specs.append(pl.BlockSpec((rows, cols), lambda j: (j, 0)))
            side_shapes.append(jax.ShapeDtypeStruct(w.shape, BF16))
        else:
            n_cb = cols // col_block
            side_out_specs.append(
                pl.BlockSpec((n_cb, rows, col_block), lambda j: (0, j, 0)))
            side_shapes.append(jax.ShapeDtypeStruct((n_cb, w.shape[0], col_block), BF16))
    ada_args, ada_in_specs, ada_out_specs, ada_shapes = [], [], [], []
    if ada is not None:
        c, w_ada, b_ada, first_col = ada
        n_cols = w_ada.shape[1] - first_col
        tn = n_cols // n_steps
        skip = first_col // tn
        ada_args = [c, w_ada, b_ada]
        ada_in_specs = [
            pl.BlockSpec(c.shape, lambda j: (0, 0)),
            pl.BlockSpec((w_ada.shape[0], tn), lambda j: (0, skip + j)),
            pl.BlockSpec((1, tn), lambda j: (0, skip + j)),
        ]
        ada_out_specs = [pl.BlockSpec((c.shape[0], tn), lambda j: (0, j))]
        ada_shapes = [jax.ShapeDtypeStruct((c.shape[0], n_cols), F32)]
    return pl.pallas_call(
        functools.partial(_attn_kernel, n_heads=n_heads, dk=dk, shared_kv=shared_kv,
                          tq=tq, n_blocks=n_blocks, n_side=len(side),
                          with_ada=ada is not None),
        grid=(n_steps,),
        in_specs=[q_spec(lambda j: (0, 0, 0))]
        + [q_spec(ahead(i)) for i in range(n_blocks)]
        + [
            pl.BlockSpec((1, s, n_kv * dk), lambda j: kv_map(n_blocks * j)),
            pl.BlockSpec((1, s, n_kv * dk),
                         lambda j: kv_map(jnp.minimum(n_blocks * (j + 1), n_total - 1))),
            pl.BlockSpec((1, s, n_kv * V_DIM), lambda j: kv_map(n_blocks * j)),
        ] + side_in_specs + ada_in_specs,
        out_specs=[pl.BlockSpec((1, n_blocks * tq, n_heads * V_DIM), out_map)]
        + side_out_specs + ada_out_specs,
        out_shape=[jax.ShapeDtypeStruct((b, s, n_groups * n_heads * V_DIM), BF16)]
        + side_shapes + ada_shapes,
        scratch_shapes=[pltpu.VMEM((2, n_heads, tq, s), F32),
                        pltpu.VMEM((2, n_heads, tq, 1), F32)],
        compiler_params=pltpu.CompilerParams(
            dimension_semantics=("arbitrary",), vmem_limit_bytes=vmem_limit),
        name=name,
    )(*([q] * (n_blocks + 1)), k, k, v, *[w for w, _ in side], *ada_args)


def _outproj_kernel(oa_ref, ob_ref, x_ref, mod_ref, ga_ref, gb_ref, w_ref, gpost_ref, o_ref,
                    n_scr, y_scr):
    n_rows = n_scr.shape[0]

    def store_o(rows, val):
        o_ref[0, rows, :] = val

    for rows in _row_chunks(n_rows):
        n_scr[rows, 0:MIX_A] = _rms(oa_ref[0, rows, :].astype(F32), ga_ref[...]).astype(BF16)
        n_scr[rows, MIX_A:] = _rms(ob_ref[0, rows, :].astype(F32), gb_ref[...]).astype(BF16)
    y_scr[...] = jnp.dot(n_scr[...], w_ref[...], preferred_element_type=F32)
    _gated_residual_rows(lambda rows: x_ref[0, rows, :], lambda rows: y_scr[rows, :],
                         store_o, n_rows, gpost_ref[...],
                         _mod_row(mod_ref, pl.program_id(0), MOD_GT_A))


def _outproj(o_a, o_b, x, mod, g_a, g_b, w_out, g_post, tm):
    b, s, d = x.shape
    row = lambda bi, i: (bi, i, 0)
    const = lambda bi, i: (0, 0)
    return pl.pallas_call(
        _outproj_kernel,
        grid=(b, s // tm),
        in_specs=[
            pl.BlockSpec((1, tm, MIX_A), row),
            pl.BlockSpec((1, tm, MIX_B), row),
            pl.BlockSpec((1, tm, d), row),
            pl.BlockSpec(mod.shape, const),
            pl.BlockSpec((1, MIX_A), const),
            pl.BlockSpec((1, MIX_B), const),
            pl.BlockSpec((MIX_A + MIX_B, d), const, pipeline_mode=pl.Buffered(1)),
            pl.BlockSpec((1, d), const),
        ],
        out_specs=pl.BlockSpec((1, tm, d), row),
        out_shape=jax.ShapeDtypeStruct((b, s, d), F32),
        scratch_shapes=[pltpu.VMEM((tm, MIX_A + MIX_B), BF16), pltpu.VMEM((tm, d), F32)],
        compiler_params=pltpu.CompilerParams(
            dimension_semantics=("arbitrary", "arbitrary"),
            vmem_limit_bytes=VMEM_LIMIT),
        name="out_proj",
    )(o_a, o_b, x, mod, g_a, g_b, w_out, g_post)


def _mlp_kernel(x_ref, mod_ref, gpre_ref, w1_ref, w2_ref, gpost_ref,
                o_ref, h_scr, *, n_split):
    bi, f = pl.program_id(0), pl.program_id(2)
    last = pl.num_programs(2) - 1
    n_rows = h_scr.shape[0]
    parts = [slice(r, r + n_rows // n_split) for r in range(0, n_rows, n_rows // n_split)]

    def store_h(rows, val):
        h_scr[rows, :] = val

    def store_o(rows, val):
        o_ref[0, rows, :] = val

    def ffn(rows, assign_first=False):
        n_sub, _, sub = w1_ref.shape
        for c in range(n_sub):
            u = jnp.dot(h_scr[rows, :], w1_ref[c], preferred_element_type=F32)
            u = jnp.maximum(u, 0.0)
            y = jnp.dot((u * u).astype(BF16), w2_ref[c * sub:(c + 1) * sub, :],
                        preferred_element_type=F32)
            if assign_first and c == 0:
                o_ref[0, rows, :] = y
            else:
                o_ref[0, rows, :] += y

    @pl.when(f == 0)
    def _():
        for part in parts:
            _norm_mod_rows(lambda rows: x_ref[0, rows, :], store_h, part,
                           gpre_ref[...], _mod_row(mod_ref, bi, MOD_SC_M),
                           _mod_row(mod_ref, bi, MOD_SH_M))
            ffn(part, assign_first=True)

    @pl.when((f > 0) & (f < last))
    def _():
        ffn(slice(0, n_rows))

    @pl.when(f == last)
    def _():
        for part in parts:
            ffn(part)
            _gated_residual_rows(lambda rows: x_ref[0, rows, :], lambda rows: o_ref[0, rows, :],
                                 store_o, part, gpost_ref[...],
                                 _mod_row(mod_ref, bi, MOD_GT_M))


def _mlp(x, mod, g_pre, w1, w2, g_post, tm):
    b, s, d = x.shape
    n_sub = MLP_TF // MLP_SUB
    n_f, tf = w1.shape[0] // n_sub, MLP_TF
    row = lambda bi, i, f: (bi, i, 0)
    const = lambda bi, i, f: (0, 0)
    return pl.pallas_call(
        functools.partial(_mlp_kernel, n_split=2),
        grid=(b, s // tm, n_f),
        in_specs=[
            pl.BlockSpec((1, tm, d), row),
            pl.BlockSpec(mod.shape, const),
            pl.BlockSpec((1, d), const),
            pl.BlockSpec((n_sub, d, MLP_SUB), lambda bi, i, f: (f, 0, 0)),
            pl.BlockSpec((tf, d), lambda bi, i, f: (f, 0)),
            pl.BlockSpec((1, d), const),
        ],
        out_specs=pl.BlockSpec((1, tm, d), row),
        out_shape=jax.ShapeDtypeStruct((b, s, d), F32),
        scratch_shapes=[pltpu.VMEM((tm, d), BF16)],
        compiler_params=pltpu.CompilerParams(
            dimension_semantics=("arbitrary", "arbitrary", "arbitrary"),
            vmem_limit_bytes=VMEM_LIMIT),
        name="mlp",
    )(x, mod, g_pre, w1, w2, g_post)


def _cast_pad_kernel(w_ref, o_ref):
    n = w_ref.shape[1]
    o_ref[:, :n] = w_ref[...].astype(BF16)
    if o_ref.shape[1] > n:
        o_ref[:, n:] = jnp.zeros((o_ref.shape[0], o_ref.shape[1] - n), BF16)


def _cast_pad(w, tr):
    r, n = w.shape
    n_pad = -(-n // LANES) * LANES
    return pl.pallas_call(
        _cast_pad_kernel,
        grid=(r // tr,),
        in_specs=[pl.BlockSpec((tr, n), lambda i: (i, 0))],
        out_specs=pl.BlockSpec((tr, n_pad), lambda i: (i, 0)),
        out_shape=jax.ShapeDtypeStruct((r, n_pad), BF16),
        compiler_params=pltpu.CompilerParams(
            dimension_semantics=("arbitrary",), vmem_limit_bytes=VMEM_LIMIT),
        name="cast_pad",
    )(w)


def kernel(x, c, w_ada, b_ada, g_pre_attn, w_in, g_q_a, g_k_a, g_ckv, w_kv_b, g_out_a,
           g_out_b, w_out, g_post_attn, g_pre_mlp, w_mlp_in, w_mlp_out, g_post_mlp):
    b, s, d = x.shape
    depth = w_ada.shape[0]
    tabs = _rope_tables(s)
    c_pad = jnp.pad(c, ((0, 8 - b), (0, 0)))

    for l in range(depth):
        b_ada_l = b_ada[l][None, :]
        n_early = N_MOD_EARLY * d
        mod_early = _ada(c_pad, w_ada[l], b_ada_l, n_early)

        q_a, k_a, v_a, q_b, k_b, v_b = _inproj(
            x, mod_early, g_pre_attn[l][None, :], _cast_pad(w_in[l].T, tr=608),
            _cast_pad(w_kv_b[l], tr=256), g_q_a[l][None, :], g_k_a[l][None, :],
            g_ckv[l][None, :], tabs, tm=512)

        o_a, w1_bf, w2_bf = _attention(
            q_a, k_a, v_a, [(w_mlp_in[l], MLP_SUB), (w_mlp_out[l], None)],
            n_groups=HKV, n_heads=G_A,
            dk=HEAD_DIM, shared_kv=True, tq=256, n_blocks=4, name="attn_gqa")
        o_b, wo_bf, mod_late = _attention(
            q_b, k_b, v_b, [(w_out[l], None)], n_groups=HB // 4, n_heads=4,
            dk=QK_B_PAD, shared_kv=False, tq=256, n_blocks=4, name="attn_mla",
            ada=(c_pad, w_ada[l], b_ada_l, n_early), vmem_limit=VMEM_LIMIT_MLA)

        x = _outproj(o_a, o_b, x, mod_late, g_out_a[l][None, :], g_out_b[l][None, :],
                     wo_bf, g_post_attn[l][None, :], tm=512)

        x = _mlp(x, mod_late, g_pre_mlp[l][None, :], w1_bf, w2_bf,
                 g_post_mlp[l][None, :], tm=512)
    return x
```

```python
import functools
import math

import numpy as np
import jax
import jax.numpy as jnp
from jax import lax
from jax.experimental import pallas as pl
from jax.experimental.pallas import tpu as pltpu

D_MODEL = 2048
GRID_W = 64
ROPE_THETA = 10000.0
EPS = 1e-6

HEAD_DIM = 128
HA = 8
HKV = 2
G_A = HA // HKV

HB = 8
QK_NOPE = 128
QK_ROPE = 64
V_DIM = 128
KV_RANK = 512
QK_B = QK_NOPE + QK_ROPE
QK_B_PAD = 256

W_QA = HA * HEAD_DIM
W_KA = HKV * HEAD_DIM
W_VA = HKV * HEAD_DIM
W_QB = HB * QK_B
MIX_A = HA * HEAD_DIM
MIX_B = HB * V_DIM
D_FF = 4 * D_MODEL
N_MOD = 6
N_MOD_EARLY = 2
MOD_SH_A, MOD_SC_A = range(N_MOD_EARLY)
MOD_GT_A, MOD_SH_M, MOD_SC_M, MOD_GT_M = range(N_MOD - N_MOD_EARLY)

LOG2_E = math.log2(math.e)
LANES = 128
ROW_CHUNK = 16
MLP_TF = 2048
MLP_SUB = 1024
VMEM_LIMIT = 56 * 1024 * 1024
VMEM_LIMIT_MLA = 60 * 1024 * 1024

F32 = jnp.float32
BF16 = jnp.bfloat16


def _rope_tables(seq_len):
    parts = _rope_table(seq_len, HEAD_DIM) + _rope_table(seq_len, QK_ROPE)
    return jnp.asarray(np.concatenate(parts, axis=1), F32)


def _rope_table(seq_len, dim):
    pos = np.arange(seq_len)
    row = (pos // GRID_W).astype(np.float64)
    col = (pos % GRID_W).astype(np.float64)
    half = dim // 2
    inv = ROPE_THETA ** (-np.arange(0, half, 2, dtype=np.float64) / half)
    ang_r = row[:, None] * inv[None, :]
    ang_c = col[:, None] * inv[None, :]
    ang = np.concatenate([ang_r, ang_r, ang_c, ang_c], axis=-1)
    cos, sin = np.cos(ang), np.sin(ang)
    quarter = dim // 4
    sign = np.where((np.arange(dim) % half) < quarter, -1.0, 1.0)
    sin = sin * sign[None, :]
    reps = LANES // dim
    return [np.tile(cos, (1, reps)), np.tile(sin, (1, reps))]


def _log2(n):
    assert n > 0 and n & (n - 1) == 0, n
    return n.bit_length() - 1


def _mod_row(mod_ref, bi, idx):
    return mod_ref[pl.ds(bi, 1), idx * D_MODEL:(idx + 1) * D_MODEL]


def _unit_rms(x):
    ms = jnp.mean(x * x, axis=-1, keepdims=True)
    return x * lax.rsqrt(ms + EPS)


def _rms(x, g):
    return _unit_rms(x) * g


def _row_chunks(span):
    span = span if isinstance(span, slice) else slice(0, span)
    return [slice(r, r + ROW_CHUNK) for r in range(span.start, span.stop, ROW_CHUNK)]


def _norm_mod_rows(load, store, span, g, sc, sh):
    gain = g * (1.0 + sc)
    for rows in _row_chunks(span):
        store(rows, (_unit_rms(load(rows)) * gain + sh).astype(BF16))


def _gated_residual_rows(load_x, load_y, store, span, g, gt):
    gain = gt * g
    for rows in _row_chunks(span):
        store(rows, load_x(rows) + _unit_rms(load_y(rows)) * gain)


def _dot_nt(a, w_rows):
    return lax.dot_general(a, w_rows, (((1,), (1,)), ((), ())), preferred_element_type=F32)


def _rope(x, cos, sin_signed, quarter):
    lane = lax.broadcasted_iota(jnp.int32, x.shape, 1)
    take_up = (lane % (2 * quarter)) < quarter
    up = pltpu.roll(x, LANES - quarter, axis=1)
    down = pltpu.roll(x, quarter, axis=1)
    rot = jnp.where(take_up, up, down)
    return x * cos + rot * sin_signed


def _ada_columns(c, w, b):
    c_act = (c * jax.nn.sigmoid(c)).astype(BF16)
    return jnp.dot(c_act, w.astype(BF16), preferred_element_type=F32) + b


def _ada_kernel(c_ref, w_ref, b_ref, o_ref):
    o_ref[...] = _ada_columns(c_ref[...], w_ref[...], b_ref[...])


def _ada(c_pad, w_ada, b_ada, n):
    m, d = c_pad.shape
    tn = 512
    return pl.pallas_call(
        _ada_kernel,
        grid=(n // tn,),
        in_specs=[
            pl.BlockSpec((m, d), lambda j: (0, 0)),
            pl.BlockSpec((d, tn), lambda j: (0, j)),
            pl.BlockSpec((1, tn), lambda j: (0, j)),
        ],
        out_specs=pl.BlockSpec((m, tn), lambda j: (0, j)),
        out_shape=jax.ShapeDtypeStruct((m, n), F32),
        compiler_params=pltpu.CompilerParams(
            dimension_semantics=("arbitrary",), vmem_limit_bytes=VMEM_LIMIT),
        name="ada_mod",
    )(c_pad, w_ada, b_ada)


def _inproj_kernel(x_ref, mod_ref, gpre_ref, w_ref, wkv_ref, gq_ref, gk_ref,
                   gckv_ref, tab_ref,
                   qa_ref, ka_ref, va_ref, qb_ref, kb_ref, vb_ref, h_scr):
    tm = h_scr.shape[0]
    bi = pl.program_id(0)

    def store_h(rows, val):
        h_scr[rows, :] = val

    _norm_mod_rows(lambda rows: x_ref[0, rows, :], store_h, tm, gpre_ref[...],
                   _mod_row(mod_ref, bi, MOD_SC_A), _mod_row(mod_ref, bi, MOD_SH_A))
    hb = h_scr[...]
    cosa, sina, cosb, sinb = (tab_ref[:, i * LANES:(i + 1) * LANES] for i in range(4))
    scale_a = LOG2_E / math.sqrt(HEAD_DIM)
    scale_b = LOG2_E / math.sqrt(QK_B)

    qa = _dot_nt(hb, w_ref[0:W_QA, :])
    gq = gq_ref[...] * scale_a
    for hd in range(HA):
        sl = slice(hd * HEAD_DIM, (hd + 1) * HEAD_DIM)
        q = _rope(_rms(qa[:, sl], gq), cosa, sina, HEAD_DIM // 4)
        qa_ref[0, :, sl] = q.astype(BF16)

    kva = _dot_nt(hb, w_ref[W_QA:W_QA + W_KA + W_VA, :])
    for hd in range(HKV):
        sl = slice(hd * HEAD_DIM, (hd + 1) * HEAD_DIM)
        k = _rope(_rms(kva[:, sl], gk_ref[...]), cosa, sina, HEAD_DIM // 4)
        ka_ref[0, :, sl] = k.astype(BF16)
    va_ref[0] = kva[:, W_KA:].astype(BF16)

    o_qb = W_QA + W_KA + W_VA
    qb = _dot_nt(hb, w_ref[o_qb:o_qb + W_QB, :])
    lane = lax.broadcasted_iota(jnp.int32, (tm, LANES), 1)
    low = lane < QK_ROPE
    swap = lambda t: pltpu.roll(t, QK_ROPE, axis=1)
    for pair in range(HB // 2):
        t0, t1, t2 = (qb[:, (3 * pair + i) * LANES:(3 * pair + i + 1) * LANES] for i in range(3))
        r1 = _rope(t1, cosb, sinb, QK_ROPE // 4)
        r2 = _rope(t2, cosb, sinb, QK_ROPE // 4)
        heads = (
            (2 * pair, t0, jnp.where(low, r1, 0.0)),
            (2 * pair + 1, jnp.where(low, swap(t1), swap(t2)), jnp.where(low, swap(r2), 0.0)),
        )
        for hd, nope, pe in heads:
            base = hd * QK_B_PAD
            qb_ref[0, :, base:base + QK_NOPE] = (nope * scale_b).astype(BF16)
            qb_ref[0, :, base + QK_NOPE:base + QK_B_PAD] = (pe * scale_b).astype(BF16)

    o_ckv = o_qb + W_QB
    w_rest = jnp.concatenate(
        [w_ref[o_ckv:, :], jnp.zeros((LANES - QK_ROPE, w_ref.shape[1]), BF16)], axis=0)
    rest = _dot_nt(hb, w_rest)
    ckv = _rms(rest[:, :KV_RANK], gckv_ref[...]).astype(BF16)
    kv = jnp.dot(ckv, wkv_ref[...], preferred_element_type=F32)
    kpe = _rope(rest[:, KV_RANK:], cosb, sinb, QK_ROPE // 4).astype(BF16)
    for hd in range(HB):
        base = hd * (QK_NOPE + V_DIM)
        kb_ref[0, :, hd * QK_B_PAD:hd * QK_B_PAD + QK_NOPE] = (
            kv[:, base:base + QK_NOPE].astype(BF16))
        kb_ref[0, :, hd * QK_B_PAD + QK_NOPE:(hd + 1) * QK_B_PAD] = kpe
        vb_ref[0, :, hd * V_DIM:(hd + 1) * V_DIM] = (
            kv[:, base + QK_NOPE:base + QK_NOPE + V_DIM].astype(BF16))


def _inproj(x, mod, g_pre, w_in_p, w_kv, g_q, g_k, g_ckv, tabs, tm):
    b, s, d = x.shape
    n_in = w_in_p.shape[0]
    row = lambda bi, i: (bi, i, 0)
    const = lambda bi, i: (0, 0)
    tab = lambda bi, i: (i, 0)
    vec = lambda n: pl.BlockSpec((1, n), const)
    out_widths = (MIX_A, W_KA, W_VA, HB * QK_B_PAD, HB * QK_B_PAD, MIX_B)
    return pl.pallas_call(
        _inproj_kernel,
        grid=(b, s // tm),
        in_specs=[
            pl.BlockSpec((1, tm, d), row),
            pl.BlockSpec(mod.shape, const),
            vec(d),
            pl.BlockSpec((n_in, d), const, pipeline_mode=pl.Buffered(1)),
            pl.BlockSpec((KV_RANK, HB * (QK_NOPE + V_DIM)), const,
                         pipeline_mode=pl.Buffered(1)),
            vec(HEAD_DIM), vec(HEAD_DIM), vec(KV_RANK),
            pl.BlockSpec((tm, 4 * LANES), tab),
        ],
        out_specs=[pl.BlockSpec((1, tm, w), row) for w in out_widths],
        out_shape=[jax.ShapeDtypeStruct((b, s, w), BF16) for w in out_widths],
        scratch_shapes=[pltpu.VMEM((tm, d), BF16)],
        compiler_params=pltpu.CompilerParams(
            dimension_semantics=("arbitrary", "arbitrary"),
            vmem_limit_bytes=VMEM_LIMIT),
        name="in_proj",
    )(x, mod, g_pre, w_in_p, w_kv, g_q, g_k, g_ckv, tabs)


def _attn_kernel(*refs, n_heads, dk, shared_kv, tq, n_blocks, n_side, with_ada):
    q0_ref, q_refs = refs[0], refs[1:1 + n_blocks]
    k_ref, kn_ref, v_ref = refs[1 + n_blocks:4 + n_blocks]
    rest = list(refs[4 + n_blocks:])
    side_in = [rest.pop(0) for _ in range(n_side)]
    ada_in = [rest.pop(0) for _ in range(3 if with_ada else 0)]
    o_ref = rest.pop(0)
    side_out = [rest.pop(0) for _ in range(n_side)]
    ada_out = [rest.pop(0) for _ in range(1 if with_ada else 0)]
    s_scr, m_scr = rest

    def scores(q_ref, keys_ref, slot):
        for hd in range(n_heads):
            kv_hd = 0 if shared_kv else hd
            k = keys_ref[0, :, kv_hd * dk:(kv_hd + 1) * dk]
            q = q_ref[0, :, hd * dk:(hd + 1) * dk]
            s = lax.dot_general(q, k, (((1,), (1,)), ((), ())), preferred_element_type=F32)
            s_scr[slot, hd] = s
            m_scr[slot, hd] = jnp.max(s, axis=-1, keepdims=True)

    def softmax_pv(slot, row0):
        for hd in range(n_heads):
            kv_hd = 0 if shared_kv else hd
            v = v_ref[0, :, kv_hd * V_DIM:(kv_hd + 1) * V_DIM]
            p = jnp.exp2(s_scr[slot, hd] - m_scr[slot, hd]).astype(BF16)
            v_ext = jnp.concatenate([v, jnp.ones_like(v)], axis=1)
            o_ext = jnp.dot(p, v_ext, preferred_element_type=F32)
            o = o_ext[:, :V_DIM] / o_ext[:, V_DIM:]
            o_ref[0, row0:row0 + tq, hd * V_DIM:(hd + 1) * V_DIM] = o.astype(BF16)

    @pl.when(pl.program_id(0) == 0)
    def _():
        scores(q0_ref, k_ref, 0)

    for i in range(n_blocks):
        keys_ref = kn_ref if i == n_blocks - 1 else k_ref
        scores(q_refs[i], keys_ref, (i + 1) % 2)
        for w_ref, wb_ref in zip(side_in, side_out):
            n = w_ref.shape[0] // n_blocks
            rows = slice(i * n, (i + 1) * n)
            if len(wb_ref.shape) == 2:
                wb_ref[rows, :] = w_ref[rows, :].astype(BF16)
            else:
                cb = wb_ref.shape[2]
                for f in range(wb_ref.shape[0]):
                    wb_ref[f, rows, :] = w_ref[rows, f * cb:(f + 1) * cb].astype(BF16)
        if with_ada and i == 0:
            c_ref, wada_ref, bada_ref = ada_in
            ada_out[0][...] = _ada_columns(c_ref[...], wada_ref[...], bada_ref[...])
        softmax_pv(i % 2, i * tq)


def _attention(q, k, v, side, *, n_groups, n_heads, dk, shared_kv, tq, n_blocks, name,
               ada=None, vmem_limit=VMEM_LIMIT):
    b, s, _ = q.shape
    n_kv = 1 if shared_kv else n_heads
    n_blk = s // tq
    assert n_blocks % 2 == 0 and n_blk % n_blocks == 0
    n_total = b * n_groups * n_blk
    n_steps = n_total // n_blocks

    sh_blk, sh_grp, sh_nb = (_log2(n) for n in (n_blk, n_groups, n_blocks))

    def q_map(t):
        grp = jnp.right_shift(t, sh_blk)
        return (jnp.right_shift(grp, sh_grp), t & (n_blk - 1), grp & (n_groups - 1))

    def kv_map(t):
        bi, _, g = q_map(t)
        return (bi, 0, g)

    def out_map(j):
        bi, i, g = q_map(n_blocks * j)
        return (bi, jnp.right_shift(i, sh_nb), g)

    def ahead(i):
        return lambda j: q_map(jnp.minimum(n_blocks * j + i + 1, n_total - 1))

    q_spec = lambda f: pl.BlockSpec((1, tq, n_heads * dk), f)
    side_in_specs, side_out_specs, side_shapes = [], [], []
    for w, col_block in side:
        rows, cols = w.shape[0] // n_steps, w.shape[1]
        side_in_specs.append(pl.BlockSpec((rows, cols), lambda j: (j, 0)))
        if col_block is None:
            side_out_specs.append(pl.BlockSpec((rows, cols), lambda j: (j, 0)))
            side_shapes.append(jax.ShapeDtypeStruct(w.shape, BF16))
        else:
            n_cb = cols // col_block
            side_out_specs.append(
                pl.BlockSpec((n_cb, rows, col_block), lambda j: (0, j, 0)))
            side_shapes.append(jax.ShapeDtypeStruct((n_cb, w.shape[0], col_block), BF16))
    ada_args, ada_in_specs, ada_out_specs, ada_shapes = [], [], [], []
    if ada is not None:
        c, w_ada, b_ada, first_col = ada
        n_cols = w_ada.shape[1] - first_col
        tn = n_cols // n_steps
        skip = first_col // tn
        ada_args = [c, w_ada, b_ada]
        ada_in_specs = [
            pl.BlockSpec(c.shape, lambda j: (0, 0)),
            pl.BlockSpec((w_ada.shape[0], tn), lambda j: (0, skip + j)),
            pl.BlockSpec((1, tn), lambda j: (0, skip + j)),
        ]
        ada_out_specs = [pl.BlockSpec((c.shape[0], tn), lambda j: (0, j))]
        ada_shapes = [jax.ShapeDtypeStruct((c.shape[0], n_cols), F32)]
    return pl.pallas_call(
        functools.partial(_attn_kernel, n_heads=n_heads, dk=dk, shared_kv=shared_kv,
                          tq=tq, n_blocks=n_blocks, n_side=len(side),
                          with_ada=ada is not None),
        grid=(n_steps,),
        in_specs=[q_spec(lambda j: (0, 0, 0))]
        + [q_spec(ahead(i)) for i in range(n_blocks)]
        + [
            pl.BlockSpec((1, s, n_kv * dk), lambda j: kv_map(n_blocks * j)),
            pl.BlockSpec((1, s, n_kv * dk),
                         lambda j: kv_map(jnp.minimum(n_blocks * (j + 1), n_total - 1))),
            pl.BlockSpec((1, s, n_kv * V_DIM), lambda j: kv_map(n_blocks * j)),
        ] + side_in_specs + ada_in_specs,
        out_specs=[pl.BlockSpec((1, n_blocks * tq, n_heads * V_DIM), out_map)]
        + side_out_specs + ada_out_specs,
        out_shape=[jax.ShapeDtypeStruct((b, s, n_groups * n_heads * V_DIM), BF16)]
        + side_shapes + ada_shapes,
        scratch_shapes=[pltpu.VMEM((2, n_heads, tq, s), F32),
                        pltpu.VMEM((2, n_heads, tq, 1), F32)],
        compiler_params=pltpu.CompilerParams(
            dimension_semantics=("arbitrary",), vmem_limit_bytes=vmem_limit),
        name=name,
    )(*([q] * (n_blocks + 1)), k, k, v, *[w for w, _ in side], *ada_args)


def _outproj_kernel(oa_ref, ob_ref, x_ref, mod_ref, ga_ref, gb_ref, w_ref, gpost_ref, o_ref,
                    n_scr, y_scr):
    n_rows = n_scr.shape[0]

    def store_o(rows, val):
        o_ref[0, rows, :] = val

    for rows in _row_chunks(n_rows):
        n_scr[rows, 0:MIX_A] = _rms(oa_ref[0, rows, :].astype(F32), ga_ref[...]).astype(BF16)
        n_scr[rows, MIX_A:] = _rms(ob_ref[0, rows, :].astype(F32), gb_ref[...]).astype(BF16)
    y_scr[...] = jnp.dot(n_scr[...], w_ref[...], preferred_element_type=F32)
    _gated_residual_rows(lambda rows: x_ref[0, rows, :], lambda rows: y_scr[rows, :],
                         store_o, n_rows, gpost_ref[...],
                         _mod_row(mod_ref, pl.program_id(0), MOD_GT_A))


def _outproj(o_a, o_b, x, mod, g_a, g_b, w_out, g_post, tm):
    b, s, d = x.shape
    row = lambda bi, i: (bi, i, 0)
    const = lambda bi, i: (0, 0)
    return pl.pallas_call(
        _outproj_kernel,
        grid=(b, s // tm),
        in_specs=[
            pl.BlockSpec((1, tm, MIX_A), row),
            pl.BlockSpec((1, tm, MIX_B), row),
            pl.BlockSpec((1, tm, d), row),
            pl.BlockSpec(mod.shape, const),
            pl.BlockSpec((1, MIX_A), const),
            pl.BlockSpec((1, MIX_B), const),
            pl.BlockSpec((MIX_A + MIX_B, d), const, pipeline_mode=pl.Buffered(1)),
            pl.BlockSpec((1, d), const),
        ],
        out_specs=pl.BlockSpec((1, tm, d), row),
        out_shape=jax.ShapeDtypeStruct((b, s, d), F32),
        scratch_shapes=[pltpu.VMEM((tm, MIX_A + MIX_B), BF16), pltpu.VMEM((tm, d), F32)],
        compiler_params=pltpu.CompilerParams(
            dimension_semantics=("arbitrary", "arbitrary"),
            vmem_limit_bytes=VMEM_LIMIT),
        name="out_proj",
    )(o_a, o_b, x, mod, g_a, g_b, w_out, g_post)


def _mlp_kernel(x_ref, mod_ref, gpre_ref, w1_ref, w2_ref, gpost_ref,
                o_ref, h_scr, *, n_split):
    bi, f = pl.program_id(0), pl.program_id(2)
    last = pl.num_programs(2) - 1
    n_rows = h_scr.shape[0]
    parts = [slice(r, r + n_rows // n_split) for r in range(0, n_rows, n_rows // n_split)]

    def store_h(rows, val):
        h_scr[rows, :] = val

    def store_o(rows, val):
        o_ref[0, rows, :] = val

    def ffn(rows, assign_first=False):
        n_sub, _, sub = w1_ref.shape
        for c in range(n_sub):
            u = jnp.dot(h_scr[rows, :], w1_ref[c], preferred_element_type=F32)
            u = jnp.maximum(u, 0.0)
            y = jnp.dot((u * u).astype(BF16), w2_ref[c * sub:(c + 1) * sub, :],
                        preferred_element_type=F32)
            if assign_first and c == 0:
                o_ref[0, rows, :] = y
            else:
                o_ref[0, rows, :] += y

    @pl.when(f == 0)
    def _():
        for part in parts:
            _norm_mod_rows(lambda rows: x_ref[0, rows, :], store_h, part,
                           gpre_ref[...], _mod_row(mod_ref, bi, MOD_SC_M),
                           _mod_row(mod_ref, bi, MOD_SH_M))
            ffn(part, assign_first=True)

    @pl.when((f > 0) & (f < last))
    def _():
        ffn(slice(0, n_rows))

    @pl.when(f == last)
    def _():
        for part in parts:
            ffn(part)
            _gated_residual_rows(lambda rows: x_ref[0, rows, :], lambda rows: o_ref[0, rows, :],
                                 store_o, part, gpost_ref[...],
                                 _mod_row(mod_ref, bi, MOD_GT_M))


def _mlp(x, mod, g_pre, w1, w2, g_post, tm):
    b, s, d = x.shape
    n_sub = MLP_TF // MLP_SUB
    n_f, tf = w1.shape[0] // n_sub, MLP_TF
    row = lambda bi, i, f: (bi, i, 0)
    const = lambda bi, i, f: (0, 0)
    return pl.pallas_call(
        functools.partial(_mlp_kernel, n_split=2),
        grid=(b, s // tm, n_f),
        in_specs=[
            pl.BlockSpec((1, tm, d), row),
            pl.BlockSpec(mod.shape, const),
            pl.BlockSpec((1, d), const),
            pl.BlockSpec((n_sub, d, MLP_SUB), lambda bi, i, f: (f, 0, 0)),
            pl.BlockSpec((tf, d), lambda bi, i, f: (f, 0)),
            pl.BlockSpec((1, d), const),
        ],
        out_specs=pl.BlockSpec((1, tm, d), row),
        out_shape=jax.ShapeDtypeStruct((b, s, d), F32),
        scratch_shapes=[pltpu.VMEM((tm, d), BF16)],
        compiler_params=pltpu.CompilerParams(
            dimension_semantics=("arbitrary", "arbitrary", "arbitrary"),
            vmem_limit_bytes=VMEM_LIMIT),
        name="mlp",
    )(x, mod, g_pre, w1, w2, g_post)


def _cast_pad_kernel(w_ref, o_ref):
    n = w_ref.shape[1]
    o_ref[:, :n] = w_ref[...].astype(BF16)
    if o_ref.shape[1] > n:
        o_ref[:, n:] = jnp.zeros((o_ref.shape[0], o_ref.shape[1] - n), BF16)


def _cast_pad(w, tr):
    r, n = w.shape
    n_pad = -(-n // LANES) * LANES
    return pl.pallas_call(
        _cast_pad_kernel,
        grid=(r // tr,),
        in_specs=[pl.BlockSpec((tr, n), lambda i: (i, 0))],
        out_specs=pl.BlockSpec((tr, n_pad), lambda i: (i, 0)),
        out_shape=jax.ShapeDtypeStruct((r, n_pad), BF16),
        compiler_params=pltpu.CompilerParams(
            dimension_semantics=("arbitrary",), vmem_limit_bytes=VMEM_LIMIT),
        name="cast_pad",
    )(w)


def kernel(x, c, w_ada, b_ada, g_pre_attn, w_in, g_q_a, g_k_a, g_ckv, w_kv_b, g_out_a,
           g_out_b, w_out, g_post_attn, g_pre_mlp, w_mlp_in, w_mlp_out, g_post_mlp):
    b, s, d = x.shape
    depth = w_ada.shape[0]
    tabs = _rope_tables(s)
    c_pad = jnp.pad(c, ((0, 8 - b), (0, 0)))

    for l in range(depth):
        b_ada_l = b_ada[l][None, :]
        n_early = N_MOD_EARLY * d
        mod_early = _ada(c_pad, w_ada[l], b_ada_l, n_early)

        q_a, k_a, v_a, q_b, k_b, v_b = _inproj(
            x, mod_early, g_pre_attn[l][None, :], _cast_pad(w_in[l].T, tr=608),
            _cast_pad(w_kv_b[l], tr=256), g_q_a[l][None, :], g_k_a[l][None, :],
            g_ckv[l][None, :], tabs, tm=512)

        o_a, w1_bf, w2_bf = _attention(
            q_a, k_a, v_a, [(w_mlp_in[l], MLP_SUB), (w_mlp_out[l], None)],
            n_groups=HKV, n_heads=G_A,
            dk=HEAD_DIM, shared_kv=True, tq=256, n_blocks=4, name="attn_gqa")
        o_b, wo_bf, mod_late = _attention(
            q_b, k_b, v_b, [(w_out[l], None)], n_groups=HB // 4, n_heads=4,
            dk=QK_B_PAD, shared_kv=False, tq=256, n_blocks=4, name="attn_mla",
            ada=(c_pad, w_ada[l], b_ada_l, n_early), vmem_limit=VMEM_LIMIT_MLA)

        x = _outproj(o_a, o_b, x, mod_late, g_out_a[l][None, :], g_out_b[l][None, :],
                     wo_bf, g_post_attn[l][None, :], tm=512)

        x = _mlp(x, mod_late, g_pre_mlp[l][None, :], w1_bf, w2_bf,
                 g_post_mlp[l][None, :], tm=512)
    return x
```

```python
import functools
import math

import numpy as np
import jax
import jax.numpy as jnp
from jax import lax
from jax.experimental import pallas as pl
from jax.experimental.pallas import tpu as pltpu

D_MODEL = 2048
GRID_W = 64
ROPE_THETA = 10000.0
EPS = 1e-6

HEAD_DIM = 128
HA = 8
HKV = 2
G_A = HA // HKV

HB = 8
QK_NOPE = 128
QK_ROPE = 64
V_DIM = 128
KV_RANK = 512
QK_B = QK_NOPE + QK_ROPE
QK_B_PAD = 256

W_QA = HA * HEAD_DIM
W_KA = HKV * HEAD_DIM
W_VA = HKV * HEAD_DIM
W_QB = HB * QK_B
MIX_A = HA * HEAD_DIM
MIX_B = HB * V_DIM
D_FF = 4 * D_MODEL
N_MOD = 6
N_MOD_EARLY = 2
MOD_SH_A, MOD_SC_A = range(N_MOD_EARLY)
MOD_GT_A, MOD_SH_M, MOD_SC_M, MOD_GT_M = range(N_MOD - N_MOD_EARLY)

LOG2_E = math.log2(math.e)
LANES = 128
ROW_CHUNK = 16
MLP_TF = 2048
MLP_SUB = 1024
VMEM_LIMIT = 56 * 1024 * 1024
VMEM_LIMIT_MLA = 60 * 1024 * 1024

F32 = jnp.float32
BF16 = jnp.bfloat16


def _rope_tables(seq_len):
    parts = _rope_table(seq_len, HEAD_DIM) + _rope_table(seq_len, QK_ROPE)
    return jnp.asarray(np.concatenate(parts, axis=1), F32)


def _rope_table(seq_len, dim):
    pos = np.arange(seq_len)
    row = (pos // GRID_W).astype(np.float64)
    col = (pos % GRID_W).astype(np.float64)
    half = dim // 2
    inv = ROPE_THETA ** (-np.arange(0, half, 2, dtype=np.float64) / half)
    ang_r = row[:, None] * inv[None, :]
    ang_c = col[:, None] * inv[None, :]
    ang = np.concatenate([ang_r, ang_r, ang_c, ang_c], axis=-1)
    cos, sin = np.cos(ang), np.sin(ang)
    quarter = dim // 4
    sign = np.where((np.arange(dim) % half) < quarter, -1.0, 1.0)
    sin = sin * sign[None, :]
    reps = LANES // dim
    return [np.tile(cos, (1, reps)), np.tile(sin, (1, reps))]


def _log2(n):
    assert n > 0 and n & (n - 1) == 0, n
    return n.bit_length() - 1


def _mod_row(mod_ref, bi, idx):
    return mod_ref[pl.ds(bi, 1), idx * D_MODEL:(idx + 1) * D_MODEL]


def _unit_rms(x):
    ms = jnp.mean(x * x, axis=-1, keepdims=True)
    return x * lax.rsqrt(ms + EPS)


def _rms(x, g):
    return _unit_rms(x) * g


def _row_chunks(span):
    span = span if isinstance(span, slice) else slice(0, span)
    return [slice(r, r + ROW_CHUNK) for r in range(span.start, span.stop, ROW_CHUNK)]


def _norm_mod_rows(load, store, span, g, sc, sh):
    gain = g * (1.0 + sc)
    for rows in _row_chunks(span):
        store(rows, (_unit_rms(load(rows)) * gain + sh).astype(BF16))


def _gated_residual_rows(load_x, load_y, store, span, g, gt):
    gain = gt * g
    for rows in _row_chunks(span):
        store(rows, load_x(rows) + _unit_rms(load_y(rows)) * gain)


def _dot_nt(a, w_rows):
    return lax.dot_general(a, w_rows, (((1,), (1,)), ((), ())), preferred_element_type=F32)


def _rope(x, cos, sin_signed, quarter):
    lane = lax.broadcasted_iota(jnp.int32, x.shape, 1)
    take_up = (lane % (2 * quarter)) < quarter
    up = pltpu.roll(x, LANES - quarter, axis=1)
    down = pltpu.roll(x, quarter, axis=1)
    rot = jnp.where(take_up, up, down)
    return x * cos + rot * sin_signed


def _ada_columns(c, w, b):
    c_act = (c * jax.nn.sigmoid(c)).astype(BF16)
    return jnp.dot(c_act, w.astype(BF16), preferred_element_type=F32) + b


def _ada_kernel(c_ref, w_ref, b_ref, o_ref):
    o_ref[...] = _ada_columns(c_ref[...], w_ref[...], b_ref[...])


def _ada(c_pad, w_ada, b_ada, n):
    m, d = c_pad.shape
    tn = 512
    return pl.pallas_call(
        _ada_kernel,
        grid=(n // tn,),
        in_specs=[
            pl.BlockSpec((m, d), lambda j: (0, 0)),
            pl.BlockSpec((d, tn), lambda j: (0, j)),
            pl.BlockSpec((1, tn), lambda j: (0, j)),
        ],
        out_specs=pl.BlockSpec((m, tn), lambda j: (0, j)),
        out_shape=jax.ShapeDtypeStruct((m, n), F32),
        compiler_params=pltpu.CompilerParams(
            dimension_semantics=("arbitrary",), vmem_limit_bytes=VMEM_LIMIT),
        name="ada_mod",
    )(c_pad, w_ada, b_ada)


def _inproj_kernel(x_ref, mod_ref, gpre_ref, w_ref, wkv_ref, gq_ref, gk_ref,
                   gckv_ref, tab_ref,
                   qa_ref, ka_ref, va_ref, qb_ref, kb_ref, vb_ref, h_scr):
    tm = h_scr.shape[0]
    bi = pl.program_id(0)

    def store_h(rows, val):
        h_scr[rows, :] = val

    _norm_mod_rows(lambda rows: x_ref[0, rows, :], store_h, tm, gpre_ref[...],
                   _mod_row(mod_ref, bi, MOD_SC_A), _mod_row(mod_ref, bi, MOD_SH_A))
    hb = h_scr[...]
    cosa, sina, cosb, sinb = (tab_ref[:, i * LANES:(i + 1) * LANES] for i in range(4))
    scale_a = LOG2_E / math.sqrt(HEAD_DIM)
    scale_b = LOG2_E / math.sqrt(QK_B)

    qa = _dot_nt(hb, w_ref[0:W_QA, :])
    gq = gq_ref[...] * scale_a
    for hd in range(HA):
        sl = slice(hd * HEAD_DIM, (hd + 1) * HEAD_DIM)
        q = _rope(_rms(qa[:, sl], gq), cosa, sina, HEAD_DIM // 4)
        qa_ref[0, :, sl] = q.astype(BF16)

    kva = _dot_nt(hb, w_ref[W_QA:W_QA + W_KA + W_VA, :])
    for hd in range(HKV):
        sl = slice(hd * HEAD_DIM, (hd + 1) * HEAD_DIM)
        k = _rope(_rms(kva[:, sl], gk_ref[...]), cosa, sina, HEAD_DIM // 4)
        ka_ref[0, :, sl] = k.astype(BF16)
    va_ref[0] = kva[:, W_KA:].astype(BF16)

    o_qb = W_QA + W_KA + W_VA
    qb = _dot_nt(hb, w_ref[o_qb:o_qb + W_QB, :])
    lane = lax.broadcasted_iota(jnp.int32, (tm, LANES), 1)
    low = lane < QK_ROPE
    swap = lambda t: pltpu.roll(t, QK_ROPE, axis=1)
    for pair in range(HB // 2):
        t0, t1, t2 = (qb[:, (3 * pair + i) * LANES:(3 * pair + i + 1) * LANES] for i in range(3))
        r1 = _rope(t1, cosb, sinb, QK_ROPE // 4)
        r2 = _rope(t2, cosb, sinb, QK_ROPE // 4)
        heads = (
            (2 * pair, t0, jnp.where(low, r1, 0.0)),
            (2 * pair + 1, jnp.where(low, swap(t1), swap(t2)), jnp.where(low, swap(r2), 0.0)),
        )
        for hd, nope, pe in heads:
            base = hd * QK_B_PAD
            qb_ref[0, :, base:base + QK_NOPE] = (nope * scale_b).astype(BF16)
            qb_ref[0, :, base + QK_NOPE:base + QK_B_PAD] = (pe * scale_b).astype(BF16)

    o_ckv = o_qb + W_QB
    w_rest = jnp.concatenate(
        [w_ref[o_ckv:, :], jnp.zeros((LANES - QK_ROPE, w_ref.shape[1]), BF16)], axis=0)
    rest = _dot_nt(hb, w_rest)
    ckv = _rms(rest[:, :KV_RANK], gckv_ref[...]).astype(BF16)
    kv = jnp.dot(ckv, wkv_ref[...], preferred_element_type=F32)
    kpe = _rope(rest[:, KV_RANK:], cosb, sinb, QK_ROPE // 4).astype(BF16)
    for hd in range(HB):
        base = hd * (QK_NOPE + V_DIM)
        kb_ref[0, :, hd * QK_B_PAD:hd * QK_B_PAD + QK_NOPE] = (
            kv[:, base:base + QK_NOPE].astype(BF16))
        kb_ref[0, :, hd * QK_B_PAD + QK_NOPE:(hd + 1) * QK_B_PAD] = kpe
        vb_ref[0, :, hd * V_DIM:(hd + 1) * V_DIM] = (
            kv[:, base + QK_NOPE:base + QK_NOPE + V_DIM].astype(BF16))


def _inproj(x, mod, g_pre, w_in_p, w_kv, g_q, g_k, g_ckv, tabs, tm):
    b, s, d = x.shape
    n_in = w_in_p.shape[0]
    row = lambda bi, i: (bi, i, 0)
    const = lambda bi, i: (0, 0)
    tab = lambda bi, i: (i, 0)
    vec = lambda n: pl.BlockSpec((1, n), const)
    out_widths = (MIX_A, W_KA, W_VA, HB * QK_B_PAD, HB * QK_B_PAD, MIX_B)
    return pl.pallas_call(
        _inproj_kernel,
        grid=(b, s // tm),
        in_specs=[
            pl.BlockSpec((1, tm, d), row),
            pl.BlockSpec(mod.shape, const),
            vec(d),
            pl.BlockSpec((n_in, d), const, pipeline_mode=pl.Buffered(1)),
            pl.BlockSpec((KV_RANK, HB * (QK_NOPE + V_DIM)), const,
                         pipeline_mode=pl.Buffered(1)),
            vec(HEAD_DIM), vec(HEAD_DIM), vec(KV_RANK),
            pl.BlockSpec((tm, 4 * LANES), tab),
        ],
        out_specs=[pl.BlockSpec((1, tm, w), row) for w in out_widths],
        out_shape=[jax.ShapeDtypeStruct((b, s, w), BF16) for w in out_widths],
        scratch_shapes=[pltpu.VMEM((tm, d), BF16)],
        compiler_params=pltpu.CompilerParams(
            dimension_semantics=("arbitrary", "arbitrary"),
            vmem_limit_bytes=VMEM_LIMIT),
        name="in_proj",
    )(x, mod, g_pre, w_in_p, w_kv, g_q, g_k, g_ckv, tabs)


def _attn_kernel(*refs, n_heads, dk, shared_kv, tq, n_blocks, n_side, with_ada):
    q0_ref, q_refs = refs[0], refs[1:1 + n_blocks]
    k_ref, kn_ref, v_ref = refs[1 + n_blocks:4 + n_blocks]
    rest = list(refs[4 + n_blocks:])
    side_in = [rest.pop(0) for _ in range(n_side)]
    ada_in = [rest.pop(0) for _ in range(3 if with_ada else 0)]
    o_ref = rest.pop(0)
    side_out = [rest.pop(0) for _ in range(n_side)]
    ada_out = [rest.pop(0) for _ in range(1 if with_ada else 0)]
    s_scr, m_scr = rest

    def scores(q_ref, keys_ref, slot):
        for hd in range(n_heads):
            kv_hd = 0 if shared_kv else hd
            k = keys_ref[0, :, kv_hd * dk:(kv_hd + 1) * dk]
            q = q_ref[0, :, hd * dk:(hd + 1) * dk]
            s = lax.dot_general(q, k, (((1,), (1,)), ((), ())), preferred_element_type=F32)
            s_scr[slot, hd] = s
            m_scr[slot, hd] = jnp.max(s, axis=-1, keepdims=True)

    def softmax_pv(slot, row0):
        for hd in range(n_heads):
            kv_hd = 0 if shared_kv else hd
            v = v_ref[0, :, kv_hd * V_DIM:(kv_hd + 1) * V_DIM]
            p = jnp.exp2(s_scr[slot, hd] - m_scr[slot, hd]).astype(BF16)
            v_ext = jnp.concatenate([v, jnp.ones_like(v)], axis=1)
            o_ext = jnp.dot(p, v_ext, preferred_element_type=F32)
            o = o_ext[:, :V_DIM] / o_ext[:, V_DIM:]
            o_ref[0, row0:row0 + tq, hd * V_DIM:(hd + 1) * V_DIM] = o.astype(BF16)

    @pl.when(pl.program_id(0) == 0)
    def _():
        scores(q0_ref, k_ref, 0)

    for i in range(n_blocks):
        keys_ref = kn_ref if i == n_blocks - 1 else k_ref
        scores(q_refs[i], keys_ref, (i + 1) % 2)
        for w_ref, wb_ref in zip(side_in, side_out):
            n = w_ref.shape[0] // n_blocks
            rows = slice(i * n, (i + 1) * n)
            if len(wb_ref.shape) == 2:
                wb_ref[rows, :] = w_ref[rows, :].astype(BF16)
            else:
                cb = wb_ref.shape[2]
                for f in range(wb_ref.shape[0]):
                    wb_ref[f, rows, :] = w_ref[rows, f * cb:(f + 1) * cb].astype(BF16)
        if with_ada and i == 0:
            c_ref, wada_ref, bada_ref = ada_in
            ada_out[0][...] = _ada_columns(c_ref[...], wada_ref[...], bada_ref[...])
        softmax_pv(i % 2, i * tq)


def _attention(q, k, v, side, *, n_groups, n_heads, dk, shared_kv, tq, n_blocks, name,
               ada=None, vmem_limit=VMEM_LIMIT):
    b, s, _ = q.shape
    n_kv = 1 if shared_kv else n_heads
    n_blk = s // tq
    assert n_blocks % 2 == 0 and n_blk % n_blocks == 0
    n_total = b * n_groups * n_blk
    n_steps = n_total // n_blocks

    sh_blk, sh_grp, sh_nb = (_log2(n) for n in (n_blk, n_groups, n_blocks))

    def q_map(t):
        grp = jnp.right_shift(t, sh_blk)
        return (jnp.right_shift(grp, sh_grp), t & (n_blk - 1), grp & (n_groups - 1))

    def kv_map(t):
        bi, _, g = q_map(t)
        return (bi, 0, g)

    def out_map(j):
        bi, i, g = q_map(n_blocks * j)
        return (bi, jnp.right_shift(i, sh_nb), g)

    def ahead(i):
        return lambda j: q_map(jnp.minimum(n_blocks * j + i + 1, n_total - 1))

    q_spec = lambda f: pl.BlockSpec((1, tq, n_heads * dk), f)
    side_in_specs, side_out_specs, side_shapes = [], [], []
    for w, col_block in side:
        rows, cols = w.shape[0] // n_steps, w.shape[1]
        side_in_specs.append(pl.BlockSpec((rows, cols), lambda j: (j, 0)))
        if col_block is None:
            side_out_specs.append(pl.BlockSpec((rows, cols), lambda j: (j, 0)))
            side_shapes.append(jax.ShapeDtypeStruct(w.shape, BF16))
        else:
            n_cb = cols // col_block
            side_out_specs.append(
                pl.BlockSpec((n_cb, rows, col_block), lambda j: (0, j, 0)))
            side_shapes.append(jax.ShapeDtypeStruct((n_cb, w.shape[0], col_block), BF16))
    ada_args, ada_in_specs, ada_out_specs, ada_shapes = [], [], [], []
    if ada is not None:
        c, w_ada, b_ada, first_col = ada
        n_cols = w_ada.shape[1] - first_col
        tn = n_cols // n_steps
        skip = first_col // tn
        ada_args = [c, w_ada, b_ada]
        ada_in_specs = [
            pl.BlockSpec(c.shape, lambda j: (0, 0)),
            pl.BlockSpec((w_ada.shape[0], tn), lambda j: (0, skip + j)),
            pl.BlockSpec((1, tn), lambda j: (0, skip + j)),
        ]
        ada_out_specs = [pl.BlockSpec((c.shape[0], tn), lambda j: (0, j))]
        ada_shapes = [jax.ShapeDtypeStruct((c.shape[0], n_cols), F32)]
    return pl.pallas_call(
        functools.partial(_attn_kernel, n_heads=n_heads, dk=dk, shared_kv=shared_kv,
                          tq=tq, n_blocks=n_blocks, n_side=len(side),
                          with_ada=ada is not None),
        grid=(n_steps,),
        in_specs=[q_spec(lambda j: (0, 0, 0))]
        + [q_spec(ahead(i)) for i in range(n_blocks)]
        + [
            pl.BlockSpec((1, s, n_kv * dk), lambda j: kv_map(n_blocks * j)),
            pl.BlockSpec((1, s, n_kv * dk),
                         lambda j: kv_map(jnp.minimum(n_blocks * (j + 1), n_total - 1))),
            pl.BlockSpec((1, s, n_kv * V_DIM), lambda j: kv_map(n_blocks * j)),
        ] + side_in_specs + ada_in_specs,
        out_specs=[pl.BlockSpec((1, n_blocks * tq, n_heads * V_DIM), out_map)]
        + side_out_specs + ada_out_specs,
        out_shape=[jax.ShapeDtypeStruct((b, s, n_groups * n_heads * V_DIM), BF16)]
        + side_shapes + ada_shapes,
        scratch_shapes=[pltpu.VMEM((2, n_heads, tq, s), F32),
                        pltpu.VMEM((2, n_heads, tq, 1), F32)],
        compiler_params=pltpu.CompilerParams(
            dimension_semantics=("arbitrary",), vmem_limit_bytes=vmem_limit),
        name=name,
    )(*([q] * (n_blocks + 1)), k, k, v, *[w for w, _ in side], *ada_args)


def _outproj_kernel(oa_ref, ob_ref, x_ref, mod_ref, ga_ref, gb_ref, w_ref, gpost_ref, o_ref,
                    n_scr, y_scr):
    n_rows = n_scr.shape[0]

    def store_o(rows, val):
        o_ref[0, rows, :] = val

    for rows in _row_chunks(n_rows):
        n_scr[rows, 0:MIX_A] = _rms(oa_ref[0, rows, :].astype(F32), ga_ref[...]).astype(BF16)
        n_scr[rows, MIX_A:] = _rms(ob_ref[0, rows, :].astype(F32), gb_ref[...]).astype(BF16)
    y_scr[...] = jnp.dot(n_scr[...], w_ref[...], preferred_element_type=F32)
    _gated_residual_rows(lambda rows: x_ref[0, rows, :], lambda rows: y_scr[rows, :],
                         store_o, n_rows, gpost_ref[...],
                         _mod_row(mod_ref, pl.program_id(0), MOD_GT_A))


def _outproj(o_a, o_b, x, mod, g_a, g_b, w_out, g_post, tm):
    b, s, d = x.shape
    row = lambda bi, i: (bi, i, 0)
    const = lambda bi, i: (0, 0)
    return pl.pallas_call(
        _outproj_kernel,
        grid=(b, s // tm),
        in_specs=[
            pl.BlockSpec((1, tm, MIX_A), row),
            pl.BlockSpec((1, tm, MIX_B), row),
            pl.BlockSpec((1, tm, d), row),
            pl.BlockSpec(mod.shape, const),
            pl.BlockSpec((1, MIX_A), const),
            pl.BlockSpec((1, MIX_B), const),
            pl.BlockSpec((MIX_A + MIX_B, d), const, pipeline_mode=pl.Buffered(1)),
            pl.BlockSpec((1, d), const),
        ],
        out_specs=pl.BlockSpec((1, tm, d), row),
        out_shape=jax.ShapeDtypeStruct((b, s, d), F32),
        scratch_shapes=[pltpu.VMEM((tm, MIX_A + MIX_B), BF16), pltpu.VMEM((tm, d), F32)],
        compiler_params=pltpu.CompilerParams(
            dimension_semantics=("arbitrary", "arbitrary"),
            vmem_limit_bytes=VMEM_LIMIT),
        name="out_proj",
    )(o_a, o_b, x, mod, g_a, g_b, w_out, g_post)


def _mlp_kernel(x_ref, mod_ref, gpre_ref, w1_ref, w2_ref, gpost_ref,
                o_ref, h_scr, *, n_split):
    bi, f = pl.program_id(0), pl.program_id(2)
    last = pl.num_programs(2) - 1
    n_rows = h_scr.shape[0]
    parts = [slice(r, r + n_rows // n_split) for r in range(0, n_rows, n_rows // n_split)]

    def store_h(rows, val):
        h_scr[rows, :] = val

    def store_o(rows, val):
        o_ref[0, rows, :] = val

    def ffn(rows, assign_first=False):
        n_sub, _, sub = w1_ref.shape
        for c in range(n_sub):
            u = jnp.dot(h_scr[rows, :], w1_ref[c], preferred_element_type=F32)
            u = jnp.maximum(u, 0.0)
            y = jnp.dot((u * u).astype(BF16), w2_ref[c * sub:(c + 1) * sub, :],
                        preferred_element_type=F32)
            if assign_first and c == 0:
                o_ref[0, rows, :] = y
            else:
                o_ref[0, rows, :] += y

    del parts

    @pl.when(f == 0)
    def _():
        _norm_mod_rows(lambda rows: x_ref[0, rows, :], store_h, n_rows,
                       gpre_ref[...], _mod_row(mod_ref, bi, MOD_SC_M),
                       _mod_row(mod_ref, bi, MOD_SH_M))
        o_ref[...] = jnp.zeros_like(o_ref)

    ffn(slice(0, n_rows))

    @pl.when(f == last)
    def _():
        _gated_residual_rows(lambda rows: x_ref[0, rows, :], lambda rows: o_ref[0, rows, :],
                             store_o, n_rows, gpost_ref[...],
                             _mod_row(mod_ref, bi, MOD_GT_M))


def _mlp(x, mod, g_pre, w1, w2, g_post, tm):
    b, s, d = x.shape
    n_sub = MLP_TF // MLP_SUB
    n_f, tf = w1.shape[0] // n_sub, MLP_TF
    row = lambda bi, i, f: (bi, i, 0)
    const = lambda bi, i, f: (0, 0)
    return pl.pallas_call(
        functools.partial(_mlp_kernel, n_split=2),
        grid=(b, s // tm, n_f),
        in_specs=[
            pl.BlockSpec((1, tm, d), row),
            pl.BlockSpec(mod.shape, const),
            pl.BlockSpec((1, d), const),
            pl.BlockSpec((n_sub, d, MLP_SUB), lambda bi, i, f: (f, 0, 0)),
            pl.BlockSpec((tf, d), lambda bi, i, f: (f, 0)),
            pl.BlockSpec((1, d), const),
        ],
        out_specs=pl.BlockSpec((1, tm, d), row),
        out_shape=jax.ShapeDtypeStruct((b, s, d), F32),
        scratch_shapes=[pltpu.VMEM((tm, d), BF16)],
        compiler_params=pltpu.CompilerParams(
            dimension_semantics=("arbitrary", "arbitrary", "arbitrary"),
            vmem_limit_bytes=VMEM_LIMIT),
        name="mlp",
    )(x, mod, g_pre, w1, w2, g_post)


def _cast_pad_kernel(w_ref, o_ref):
    n = w_ref.shape[1]
    o_ref[:, :n] = w_ref[...].astype(BF16)
    if o_ref.shape[1] > n:
        o_ref[:, n:] = jnp.zeros((o_ref.shape[0], o_ref.shape[1] - n), BF16)


def _cast_pad(w, tr):
    r, n = w.shape
    n_pad = -(-n // LANES) * LANES
    return pl.pallas_call(
        _cast_pad_kernel,
        grid=(r // tr,),
        in_specs=[pl.BlockSpec((tr, n), lambda i: (i, 0))],
        out_specs=pl.BlockSpec((tr, n_pad), lambda i: (i, 0)),
        out_shape=jax.ShapeDtypeStruct((r, n_pad), BF16),
        compiler_params=pltpu.CompilerParams(
            dimension_semantics=("arbitrary",), vmem_limit_bytes=VMEM_LIMIT),
        name="cast_pad",
    )(w)


def kernel(x, c, w_ada, b_ada, g_pre_attn, w_in, g_q_a, g_k_a, g_ckv, w_kv_b, g_out_a,
           g_out_b, w_out, g_post_attn, g_pre_mlp, w_mlp_in, w_mlp_out, g_post_mlp):
    b, s, d = x.shape
    depth = w_ada.shape[0]
    tabs = _rope_tables(s)
    c_pad = jnp.pad(c, ((0, 8 - b), (0, 0)))

    for l in range(depth):
        b_ada_l = b_ada[l][None, :]
        n_early = N_MOD_EARLY * d
        mod_early = _ada(c_pad, w_ada[l], b_ada_l, n_early)

        q_a, k_a, v_a, q_b, k_b, v_b = _inproj(
            x, mod_early, g_pre_attn[l][None, :], _cast_pad(w_in[l].T, tr=608),
            _cast_pad(w_kv_b[l], tr=256), g_q_a[l][None, :], g_k_a[l][None, :],
            g_ckv[l][None, :], tabs, tm=512)

        o_a, w1_bf, w2_bf = _attention(
            q_a, k_a, v_a, [(w_mlp_in[l], MLP_SUB), (w_mlp_out[l], None)],
            n_groups=HKV, n_heads=G_A,
            dk=HEAD_DIM, shared_kv=True, tq=256, n_blocks=4, name="attn_gqa")
        o_b, wo_bf, mod_late = _attention(
            q_b, k_b, v_b, [(w_out[l], None)], n_groups=HB // 4, n_heads=4,
            dk=QK_B_PAD, shared_kv=False, tq=256, n_blocks=4, name="attn_mla",
            ada=(c_pad, w_ada[l], b_ada_l, n_early), vmem_limit=VMEM_LIMIT_MLA)

        x = _outproj(o_a, o_b, x, mod_late, g_out_a[l][None, :], g_out_b[l][None, :],
                     wo_bf, g_post_attn[l][None, :], tm=512)

        x = _mlp(x, mod_late, g_pre_mlp[l][None, :], w1_bf, w2_bf,
                 g_post_mlp[l][None, :], tm=512)
    return x
```

```python
import functools
import math

import numpy as np
import jax
import jax.numpy as jnp
from jax import lax
from jax.experimental import pallas as pl
from jax.experimental.pallas import tpu as pltpu

D_MODEL = 2048
GRID_W = 64
ROPE_THETA = 10000.0
EPS = 1e-6

HEAD_DIM = 128
HA = 8
HKV = 2
G_A = HA // HKV

HB = 8
QK_NOPE = 128
QK_ROPE = 64
V_DIM = 128
KV_RANK = 512
QK_B = QK_NOPE + QK_ROPE
QK_B_PAD = 256

W_QA = HA * HEAD_DIM
W_KA = HKV * HEAD_DIM
W_VA = HKV * HEAD_DIM
W_QB = HB * QK_B
MIX_A = HA * HEAD_DIM
MIX_B = HB * V_DIM
D_FF = 4 * D_MODEL
N_MOD = 6
N_MOD_EARLY = 2
MOD_SH_A, MOD_SC_A = range(N_MOD_EARLY)
MOD_GT_A, MOD_SH_M, MOD_SC_M, MOD_GT_M = range(N_MOD - N_MOD_EARLY)

LOG2_E = math.log2(math.e)
LANES = 128
ROW_CHUNK = 16
MLP_TF = 2048
MLP_SUB = 1024
VMEM_LIMIT = 56 * 1024 * 1024
VMEM_LIMIT_MLA = 60 * 1024 * 1024

F32 = jnp.float32
BF16 = jnp.bfloat16


def _rope_tables(seq_len):
    parts = _rope_table(seq_len, HEAD_DIM) + _rope_table(seq_len, QK_ROPE)
    return jnp.asarray(np.concatenate(parts, axis=1), F32)


def _rope_table(seq_len, dim):
    pos = np.arange(seq_len)
    row = (pos // GRID_W).astype(np.float64)
    col = (pos % GRID_W).astype(np.float64)
    half = dim // 2
    inv = ROPE_THETA ** (-np.arange(0, half, 2, dtype=np.float64) / half)
    ang_r = row[:, None] * inv[None, :]
    ang_c = col[:, None] * inv[None, :]
    ang = np.concatenate([ang_r, ang_r, ang_c, ang_c], axis=-1)
    cos, sin = np.cos(ang), np.sin(ang)
    quarter = dim // 4
    sign = np.where((np.arange(dim) % half) < quarter, -1.0, 1.0)
    sin = sin * sign[None, :]
    reps = LANES // dim
    return [np.tile(cos, (1, reps)), np.tile(sin, (1, reps))]


def _log2(n):
    assert n > 0 and n & (n - 1) == 0, n
    return n.bit_length() - 1


def _mod_row(mod_ref, bi, idx):
    return mod_ref[pl.ds(bi, 1), idx * D_MODEL:(idx + 1) * D_MODEL]


def _unit_rms(x):
    ms = jnp.mean(x * x, axis=-1, keepdims=True)
    return x * lax.rsqrt(ms + EPS)


def _rms(x, g):
    return _unit_rms(x) * g


def _row_chunks(span):
    span = span if isinstance(span, slice) else slice(0, span)
    return [slice(r, r + ROW_CHUNK) for r in range(span.start, span.stop, ROW_CHUNK)]


def _norm_mod_rows(load, store, span, g, sc, sh):
    gain = g * (1.0 + sc)
    for rows in _row_chunks(span):
        store(rows, (_unit_rms(load(rows)) * gain + sh).astype(BF16))


def _gated_residual_rows(load_x, load_y, store, span, g, gt):
    gain = gt * g
    for rows in _row_chunks(span):
        store(rows, load_x(rows) + _unit_rms(load_y(rows)) * gain)


def _dot_nt(a, w_rows):
    return lax.dot_general(a, w_rows, (((1,), (1,)), ((), ())), preferred_element_type=F32)


def _rope(x, cos, sin_signed, quarter):
    lane = lax.broadcasted_iota(jnp.int32, x.shape, 1)
    take_up = (lane % (2 * quarter)) < quarter
    up = pltpu.roll(x, LANES - quarter, axis=1)
    down = pltpu.roll(x, quarter, axis=1)
    rot = jnp.where(take_up, up, down)
    return x * cos + rot * sin_signed


def _ada_columns(c, w, b):
    c_act = (c * jax.nn.sigmoid(c)).astype(BF16)
    return jnp.dot(c_act, w.astype(BF16), preferred_element_type=F32) + b


def _ada_kernel(c_ref, w_ref, b_ref, o_ref):
    o_ref[...] = _ada_columns(c_ref[...], w_ref[...], b_ref[...])


def _ada(c_pad, w_ada, b_ada, n):
    m, d = c_pad.shape
    tn = 512
    return pl.pallas_call(
        _ada_kernel,
        grid=(n // tn,),
        in_specs=[
            pl.BlockSpec((m, d), lambda j: (0, 0)),
            pl.BlockSpec((d, tn), lambda j: (0, j)),
            pl.BlockSpec((1, tn), lambda j: (0, j)),
        ],
        out_specs=pl.BlockSpec((m, tn), lambda j: (0, j)),
        out_shape=jax.ShapeDtypeStruct((m, n), F32),
        compiler_params=pltpu.CompilerParams(
            dimension_semantics=("arbitrary",), vmem_limit_bytes=VMEM_LIMIT),
        name="ada_mod",
    )(c_pad, w_ada, b_ada)


def _inproj_kernel(x_ref, mod_ref, gpre_ref, w_ref, wkv_ref, gq_ref, gk_ref,
                   gckv_ref, tab_ref,
                   qa_ref, ka_ref, va_ref, qb_ref, kb_ref, vb_ref, h_scr):
    tm = h_scr.shape[0]
    bi = pl.program_id(0)

    def store_h(rows, val):
        h_scr[rows, :] = val

    _norm_mod_rows(lambda rows: x_ref[0, rows, :], store_h, tm, gpre_ref[...],
                   _mod_row(mod_ref, bi, MOD_SC_A), _mod_row(mod_ref, bi, MOD_SH_A))
    hb = h_scr[...]
    cosa, sina, cosb, sinb = (tab_ref[:, i * LANES:(i + 1) * LANES] for i in range(4))
    scale_a = LOG2_E / math.sqrt(HEAD_DIM)
    scale_b = LOG2_E / math.sqrt(QK_B)

    qa = _dot_nt(hb, w_ref[0:W_QA, :])
    gq = gq_ref[...] * scale_a
    for hd in range(HA):
        sl = slice(hd * HEAD_DIM, (hd + 1) * HEAD_DIM)
        q = _rope(_rms(qa[:, sl], gq), cosa, sina, HEAD_DIM // 4)
        qa_ref[0, :, sl] = q.astype(BF16)

    kva = _dot_nt(hb, w_ref[W_QA:W_QA + W_KA + W_VA, :])
    for hd in range(HKV):
        sl = slice(hd * HEAD_DIM, (hd + 1) * HEAD_DIM)
        k = _rope(_rms(kva[:, sl], gk_ref[...]), cosa, sina, HEAD_DIM // 4)
        ka_ref[0, :, sl] = k.astype(BF16)
    va_ref[0] = kva[:, W_KA:].astype(BF16)

    o_qb = W_QA + W_KA + W_VA
    qb = _dot_nt(hb, w_ref[o_qb:o_qb + W_QB, :])
    lane = lax.broadcasted_iota(jnp.int32, (tm, LANES), 1)
    low = lane < QK_ROPE
    swap = lambda t: pltpu.roll(t, QK_ROPE, axis=1)
    for pair in range(HB // 2):
        t0, t1, t2 = (qb[:, (3 * pair + i) * LANES:(3 * pair + i + 1) * LANES] for i in range(3))
        r1 = _rope(t1, cosb, sinb, QK_ROPE // 4)
        r2 = _rope(t2, cosb, sinb, QK_ROPE // 4)
        heads = (
            (2 * pair, t0, jnp.where(low, r1, 0.0)),
            (2 * pair + 1, jnp.where(low, swap(t1), swap(t2)), jnp.where(low, swap(r2), 0.0)),
        )
        for hd, nope, pe in heads:
            base = hd * QK_B_PAD
            qb_ref[0, :, base:base + QK_NOPE] = (nope * scale_b).astype(BF16)
            qb_ref[0, :, base + QK_NOPE:base + QK_B_PAD] = (pe * scale_b).astype(BF16)

    o_ckv = o_qb + W_QB
    w_rest = jnp.concatenate(
        [w_ref[o_ckv:, :], jnp.zeros((LANES - QK_ROPE, w_ref.shape[1]), BF16)], axis=0)
    rest = _dot_nt(hb, w_rest)
    ckv = _rms(rest[:, :KV_RANK], gckv_ref[...]).astype(BF16)
    kv = jnp.dot(ckv, wkv_ref[...], preferred_element_type=F32)
    kpe = _rope(rest[:, KV_RANK:], cosb, sinb, QK_ROPE // 4).astype(BF16)
    for hd in range(HB):
        base = hd * (QK_NOPE + V_DIM)
        kb_ref[0, :, hd * QK_B_PAD:hd * QK_B_PAD + QK_NOPE] = (
            kv[:, base:base + QK_NOPE].astype(BF16))
        kb_ref[0, :, hd * QK_B_PAD + QK_NOPE:(hd + 1) * QK_B_PAD] = kpe
        vb_ref[0, :, hd * V_DIM:(hd + 1) * V_DIM] = (
            kv[:, base + QK_NOPE:base + QK_NOPE + V_DIM].astype(BF16))


def _inproj(x, mod, g_pre, w_in_p, w_kv, g_q, g_k, g_ckv, tabs, tm):
    b, s, d = x.shape
    n_in = w_in_p.shape[0]
    row = lambda bi, i: (bi, i, 0)
    const = lambda bi, i: (0, 0)
    tab = lambda bi, i: (i, 0)
    vec = lambda n: pl.BlockSpec((1, n), const)
    out_widths = (MIX_A, W_KA, W_VA, HB * QK_B_PAD, HB * QK_B_PAD, MIX_B)
    return pl.pallas_call(
        _inproj_kernel,
        grid=(b, s // tm),
        in_specs=[
            pl.BlockSpec((1, tm, d), row),
            pl.BlockSpec(mod.shape, const),
            vec(d),
            pl.BlockSpec((n_in, d), const, pipeline_mode=pl.Buffered(1)),
            pl.BlockSpec((KV_RANK, HB * (QK_NOPE + V_DIM)), const,
                         pipeline_mode=pl.Buffered(1)),
            vec(HEAD_DIM), vec(HEAD_DIM), vec(KV_RANK),
            pl.BlockSpec((tm, 4 * LANES), tab),
        ],
        out_specs=[pl.BlockSpec((1, tm, w), row) for w in out_widths],
        out_shape=[jax.ShapeDtypeStruct((b, s, w), BF16) for w in out_widths],
        scratch_shapes=[pltpu.VMEM((tm, d), BF16)],
        compiler_params=pltpu.CompilerParams(
            dimension_semantics=("arbitrary", "arbitrary"),
            vmem_limit_bytes=VMEM_LIMIT),
        name="in_proj",
    )(x, mod, g_pre, w_in_p, w_kv, g_q, g_k, g_ckv, tabs)


def _attn_kernel(*refs, n_heads, dk, shared_kv, tq, n_blocks, n_side, with_ada):
    q_ref, qn_ref, k_ref, kn_ref, v_ref = refs[:5]
    rest = list(refs[5:])
    side_in = [rest.pop(0) for _ in range(n_side)]
    ada_in = [rest.pop(0) for _ in range(3 if with_ada else 0)]
    o_ref = rest.pop(0)
    side_out = [rest.pop(0) for _ in range(n_side)]
    ada_out = [rest.pop(0) for _ in range(1 if with_ada else 0)]
    s_scr, m_scr = rest

    def scores(i, slot):
        nxt = i == n_blocks
        keys_ref = kn_ref if nxt else k_ref
        for hd in range(n_heads):
            kv_hd = 0 if shared_kv else hd
            k = keys_ref[0, :, kv_hd * dk:(kv_hd + 1) * dk]
            if nxt:
                q = qn_ref[0, :, hd * dk:(hd + 1) * dk]
            else:
                q = q_ref[0, i * tq:(i + 1) * tq, hd * dk:(hd + 1) * dk]
            s = lax.dot_general(q, k, (((1,), (1,)), ((), ())), preferred_element_type=F32)
            s_scr[slot, hd] = s
            m_scr[slot, hd] = jnp.max(s, axis=-1, keepdims=True)

    def softmax_pv(slot, row0):
        for hd in range(n_heads):
            kv_hd = 0 if shared_kv else hd
            v = v_ref[0, :, kv_hd * V_DIM:(kv_hd + 1) * V_DIM]
            p = jnp.exp2(s_scr[slot, hd] - m_scr[slot, hd]).astype(BF16)
            v_ext = jnp.concatenate([v, jnp.ones_like(v)], axis=1)
            o_ext = jnp.dot(p, v_ext, preferred_element_type=F32)
            o = o_ext[:, :V_DIM] / o_ext[:, V_DIM:]
            o_ref[0, row0:row0 + tq, hd * V_DIM:(hd + 1) * V_DIM] = o.astype(BF16)

    @pl.when(pl.program_id(0) == 0)
    def _():
        scores(0, 0)

    for i in range(n_blocks):
        scores(i + 1, (i + 1) % 2)
        for w_ref, wb_ref in zip(side_in, side_out):
            n = w_ref.shape[0] // n_blocks
            rows = slice(i * n, (i + 1) * n)
            if len(wb_ref.shape) == 2:
                wb_ref[rows, :] = w_ref[rows, :].astype(BF16)
            else:
                cb = wb_ref.shape[2]
                for f in range(wb_ref.shape[0]):
                    wb_ref[f, rows, :] = w_ref[rows, f * cb:(f + 1) * cb].astype(BF16)
        if with_ada and i == 0:
            c_ref, wada_ref, bada_ref = ada_in
            ada_out[0][...] = _ada_columns(c_ref[...], wada_ref[...], bada_ref[...])
        softmax_pv(i % 2, i * tq)


def _attention(q, k, v, side, *, n_groups, n_heads, dk, shared_kv, tq, n_blocks, name,
               ada=None, vmem_limit=VMEM_LIMIT):
    b, s, _ = q.shape
    n_kv = 1 if shared_kv else n_heads
    n_blk = s // tq
    assert n_blocks % 2 == 0 and n_blk % n_blocks == 0
    n_total = b * n_groups * n_blk
    n_steps = n_total // n_blocks

    sh_blk, sh_grp, sh_nb = (_log2(n) for n in (n_blk, n_groups, n_blocks))

    def q_map(t):
        grp = jnp.right_shift(t, sh_blk)
        return (jnp.right_shift(grp, sh_grp), t & (n_blk - 1), grp & (n_groups - 1))

    def kv_map(t):
        bi, _, g = q_map(t)
        return (bi, 0, g)

    def step_map(j):
        bi, i, g = q_map(n_blocks * j)
        return (bi, jnp.right_shift(i, sh_nb), g)

    def next_first(j):
        return jnp.minimum(n_blocks * (j + 1), n_total - 1)

    side_in_specs, side_out_specs, side_shapes = [], [], []
    for w, col_block in side:
        rows, cols = w.shape[0] // n_steps, w.shape[1]
        side_in_specs.append(pl.BlockSpec((rows, cols), lambda j: (j, 0)))
        if col_block is None:
            side_out_specs.append(pl.BlockSpec((rows, cols), lambda j: (j, 0)))
            side_shapes.append(jax.ShapeDtypeStruct(w.shape, BF16))
        else:
            n_cb = cols // col_block
            side_out_specs.append(
                pl.BlockSpec((n_cb, rows, col_block), lambda j: (0, j, 0)))
            side_shapes.append(jax.ShapeDtypeStruct((n_cb, w.shape[0], col_block), BF16))
    ada_args, ada_in_specs, ada_out_specs, ada_shapes = [], [], [], []
    if ada is not None:
        c, w_ada, b_ada, first_col = ada
        n_cols = w_ada.shape[1] - first_col
        tn = n_cols // n_steps
        skip = first_col // tn
        ada_args = [c, w_ada, b_ada]
        ada_in_specs = [
            pl.BlockSpec(c.shape, lambda j: (0, 0)),
            pl.BlockSpec((w_ada.shape[0], tn), lambda j: (0, skip + j)),
            pl.BlockSpec((1, tn), lambda j: (0, skip + j)),
        ]
        ada_out_specs = [pl.BlockSpec((c.shape[0], tn), lambda j: (0, j))]
        ada_shapes = [jax.ShapeDtypeStruct((c.shape[0], n_cols), F32)]
    return pl.pallas_call(
        functools.partial(_attn_kernel, n_heads=n_heads, dk=dk, shared_kv=shared_kv,
                          tq=tq, n_blocks=n_blocks, n_side=len(side),
                          with_ada=ada is not None),
        grid=(n_steps,),
        in_specs=[
            pl.BlockSpec((1, n_blocks * tq, n_heads * dk), step_map),
            pl.BlockSpec((1, tq, n_heads * dk), lambda j: q_map(next_first(j))),
            pl.BlockSpec((1, s, n_kv * dk), lambda j: kv_map(n_blocks * j)),
            pl.BlockSpec((1, s, n_kv * dk), lambda j: kv_map(next_first(j))),
            pl.BlockSpec((1, s, n_kv * V_DIM), lambda j: kv_map(n_blocks * j)),
        ] + side_in_specs + ada_in_specs,
        out_specs=[pl.BlockSpec((1, n_blocks * tq, n_heads * V_DIM), step_map)]
        + side_out_specs + ada_out_specs,
        out_shape=[jax.ShapeDtypeStruct((b, s, n_groups * n_heads * V_DIM), BF16)]
        + side_shapes + ada_shapes,
        scratch_shapes=[pltpu.VMEM((2, n_heads, tq, s), F32),
                        pltpu.VMEM((2, n_heads, tq, 1), F32)],
        compiler_params=pltpu.CompilerParams(
            dimension_semantics=("arbitrary",), vmem_limit_bytes=vmem_limit),
        name=name,
    )(q, q, k, k, v, *[w for w, _ in side], *ada_args)


def _outproj_kernel(oa_ref, ob_ref, x_ref, mod_ref, ga_ref, gb_ref, w_ref, gpost_ref, o_ref,
                    n_scr, y_scr):
    n_rows = n_scr.shape[0]

    def store_o(rows, val):
        o_ref[0, rows, :] = val

    for rows in _row_chunks(n_rows):
        n_scr[rows, 0:MIX_A] = _rms(oa_ref[0, rows, :].astype(F32), ga_ref[...]).astype(BF16)
        n_scr[rows, MIX_A:] = _rms(ob_ref[0, rows, :].astype(F32), gb_ref[...]).astype(BF16)
    y_scr[...] = jnp.dot(n_scr[...], w_ref[...], preferred_element_type=F32)
    _gated_residual_rows(lambda rows: x_ref[0, rows, :], lambda rows: y_scr[rows, :],
                         store_o, n_rows, gpost_ref[...],
                         _mod_row(mod_ref, pl.program_id(0), MOD_GT_A))


def _outproj(o_a, o_b, x, mod, g_a, g_b, w_out, g_post, tm):
    b, s, d = x.shape
    row = lambda bi, i: (bi, i, 0)
    const = lambda bi, i: (0, 0)
    return pl.pallas_call(
        _outproj_kernel,
        grid=(b, s // tm),
        in_specs=[
            pl.BlockSpec((1, tm, MIX_A), row),
            pl.BlockSpec((1, tm, MIX_B), row),
            pl.BlockSpec((1, tm, d), row),
            pl.BlockSpec(mod.shape, const),
            pl.BlockSpec((1, MIX_A), const),
            pl.BlockSpec((1, MIX_B), const),
            pl.BlockSpec((MIX_A + MIX_B, d), const, pipeline_mode=pl.Buffered(1)),
            pl.BlockSpec((1, d), const),
        ],
        out_specs=pl.BlockSpec((1, tm, d), row),
        out_shape=jax.ShapeDtypeStruct((b, s, d), F32),
        scratch_shapes=[pltpu.VMEM((tm, MIX_A + MIX_B), BF16), pltpu.VMEM((tm, d), F32)],
        compiler_params=pltpu.CompilerParams(
            dimension_semantics=("arbitrary", "arbitrary"),
            vmem_limit_bytes=VMEM_LIMIT),
        name="out_proj",
    )(o_a, o_b, x, mod, g_a, g_b, w_out, g_post)


def _mlp_kernel(x_ref, mod_ref, gpre_ref, w1_ref, w2_ref, gpost_ref,
                o_ref, h_scr, *, n_split):
    bi, f = pl.program_id(0), pl.program_id(2)
    last = pl.num_programs(2) - 1
    n_rows = h_scr.shape[0]
    parts = [slice(r, r + n_rows // n_split) for r in range(0, n_rows, n_rows // n_split)]

    def store_h(rows, val):
        h_scr[rows, :] = val

    def store_o(rows, val):
        o_ref[0, rows, :] = val

    def ffn(rows, assign_first=False):
        n_sub, _, sub = w1_ref.shape
        for c in range(n_sub):
            u = jnp.dot(h_scr[rows, :], w1_ref[c], preferred_element_type=F32)
            u = jnp.maximum(u, 0.0)
            y = jnp.dot((u * u).astype(BF16), w2_ref[c * sub:(c + 1) * sub, :],
                        preferred_element_type=F32)
            if assign_first and c == 0:
                o_ref[0, rows, :] = y
            else:
                o_ref[0, rows, :] += y

    @pl.when(f == 0)
    def _():
        for part in parts:
            _norm_mod_rows(lambda rows: x_ref[0, rows, :], store_h, part,
                           gpre_ref[...], _mod_row(mod_ref, bi, MOD_SC_M),
                           _mod_row(mod_ref, bi, MOD_SH_M))
            ffn(part, assign_first=True)

    @pl.when((f > 0) & (f < last))
    def _():
        ffn(slice(0, n_rows))

    @pl.when(f == last)
    def _():
        for part in parts:
            ffn(part)
            _gated_residual_rows(lambda rows: x_ref[0, rows, :], lambda rows: o_ref[0, rows, :],
                                 store_o, part, gpost_ref[...],
                                 _mod_row(mod_ref, bi, MOD_GT_M))


def _mlp(x, mod, g_pre, w1, w2, g_post, tm):
    b, s, d = x.shape
    n_sub = MLP_TF // MLP_SUB
    n_f, tf = w1.shape[0] // n_sub, MLP_TF
    row = lambda bi, i, f: (bi, i, 0)
    const = lambda bi, i, f: (0, 0)
    return pl.pallas_call(
        functools.partial(_mlp_kernel, n_split=2),
        grid=(b, s // tm, n_f),
        in_specs=[
            pl.BlockSpec((1, tm, d), row),
            pl.BlockSpec(mod.shape, const),
            pl.BlockSpec((1, d), const),
            pl.BlockSpec((n_sub, d, MLP_SUB), lambda bi, i, f: (f, 0, 0)),
            pl.BlockSpec((tf, d), lambda bi, i, f: (f, 0)),
            pl.BlockSpec((1, d), const),
        ],
        out_specs=pl.BlockSpec((1, tm, d), row),
        out_shape=jax.ShapeDtypeStruct((b, s, d), F32),
        scratch_shapes=[pltpu.VMEM((tm, d), BF16)],
        compiler_params=pltpu.CompilerParams(
            dimension_semantics=("arbitrary", "arbitrary", "arbitrary"),
            vmem_limit_bytes=VMEM_LIMIT),
        name="mlp",
    )(x, mod, g_pre, w1, w2, g_post)


def _cast_pad_kernel(w_ref, o_ref):
    n = w_ref.shape[1]
    o_ref[:, :n] = w_ref[...].astype(BF16)
    if o_ref.shape[1] > n:
        o_ref[:, n:] = jnp.zeros((o_ref.shape[0], o_ref.shape[1] - n), BF16)


def _cast_pad(w, tr):
    r, n = w.shape
    n_pad = -(-n // LANES) * LANES
    return pl.pallas_call(
        _cast_pad_kernel,
        grid=(r // tr,),
        in_specs=[pl.BlockSpec((tr, n), lambda i: (i, 0))],
        out_specs=pl.BlockSpec((tr, n_pad), lambda i: (i, 0)),
        out_shape=jax.ShapeDtypeStruct((r, n_pad), BF16),
        compiler_params=pltpu.CompilerParams(
            dimension_semantics=("arbitrary",), vmem_limit_bytes=VMEM_LIMIT),
        name="cast_pad",
    )(w)


def kernel(x, c, w_ada, b_ada, g_pre_attn, w_in, g_q_a, g_k_a, g_ckv, w_kv_b, g_out_a,
           g_out_b, w_out, g_post_attn, g_pre_mlp, w_mlp_in, w_mlp_out, g_post_mlp):
    b, s, d = x.shape
    depth = w_ada.shape[0]
    tabs = _rope_tables(s)
    c_pad = jnp.pad(c, ((0, 8 - b), (0, 0)))

    for l in range(depth):
        b_ada_l = b_ada[l][None, :]
        n_early = N_MOD_EARLY * d
        mod_early = _ada(c_pad, w_ada[l], b_ada_l, n_early)

        q_a, k_a, v_a, q_b, k_b, v_b = _inproj(
            x, mod_early, g_pre_attn[l][None, :], _cast_pad(w_in[l].T, tr=608),
            _cast_pad(w_kv_b[l], tr=256), g_q_a[l][None, :], g_k_a[l][None, :],
            g_ckv[l][None, :], tabs, tm=512)

        o_a, w1_bf, w2_bf = _attention(
            q_a, k_a, v_a, [(w_mlp_in[l], MLP_SUB), (w_mlp_out[l], None)],
            n_groups=HKV, n_heads=G_A,
            dk=HEAD_DIM, shared_kv=True, tq=256, n_blocks=4, name="attn_gqa")
        o_b, wo_bf, mod_late = _attention(
            q_b, k_b, v_b, [(w_out[l], None)], n_groups=HB // 4, n_heads=4,
            dk=QK_B_PAD, shared_kv=False, tq=256, n_blocks=4, name="attn_mla",
            ada=(c_pad, w_ada[l], b_ada_l, n_early), vmem_limit=VMEM_LIMIT_MLA)

        x = _outproj(o_a, o_b, x, mod_late, g_out_a[l][None, :], g_out_b[l][None, :],
                     wo_bf, g_post_attn[l][None, :], tm=512)

        x = _mlp(x, mod_late, g_pre_mlp[l][None, :], w1_bf, w2_bf,
                 g_post_mlp[l][None, :], tm=512)
    return x
```

```python
import functools
import math

import numpy as np
import jax
import jax.numpy as jnp
from jax import lax
from jax.experimental import pallas as pl
from jax.experimental.pallas import tpu as pltpu

D_MODEL = 2048
GRID_W = 64
ROPE_THETA = 10000.0
EPS = 1e-6

HEAD_DIM = 128
HA = 8
HKV = 2
G_A = HA // HKV

HB = 8
QK_NOPE = 128
QK_ROPE = 64
V_DIM = 128
KV_RANK = 512
QK_B = QK_NOPE + QK_ROPE
QK_B_PAD = 256

W_QA = HA * HEAD_DIM
W_KA = HKV * HEAD_DIM
W_VA = HKV * HEAD_DIM
W_QB = HB * QK_B
MIX_A = HA * HEAD_DIM
MIX_B = HB * V_DIM
D_FF = 4 * D_MODEL
N_MOD = 6
N_MOD_EARLY = 2
MOD_SH_A, MOD_SC_A = range(N_MOD_EARLY)
MOD_GT_A, MOD_SH_M, MOD_SC_M, MOD_GT_M = range(N_MOD - N_MOD_EARLY)

LOG2_E = math.log2(math.e)
LANES = 128
BF16_ROWS = 16
ROW_CHUNK = BF16_ROWS
W_STAGE_ROWS = 304
MLP_TF = 2048
MLP_SUB = 1024
VMEM_LIMIT = 56 * 1024 * 1024
VMEM_LIMIT_MLA = 60 * 1024 * 1024

F32 = jnp.float32
BF16 = jnp.bfloat16


def _rope_tables(seq_len):
    parts = _rope_table(seq_len, HEAD_DIM) + _rope_table(seq_len, QK_ROPE)
    return jnp.asarray(np.concatenate(parts, axis=1), F32)


def _rope_table(seq_len, dim):
    pos = np.arange(seq_len)
    row = (pos // GRID_W).astype(np.float64)
    col = (pos % GRID_W).astype(np.float64)
    half = dim // 2
    inv = ROPE_THETA ** (-np.arange(0, half, 2, dtype=np.float64) / half)
    ang_r = row[:, None] * inv[None, :]
    ang_c = col[:, None] * inv[None, :]
    ang = np.concatenate([ang_r, ang_r, ang_c, ang_c], axis=-1)
    cos, sin = np.cos(ang), np.sin(ang)
    quarter = dim // 4
    sign = np.where((np.arange(dim) % half) < quarter, -1.0, 1.0)
    sin = sin * sign[None, :]
    reps = LANES // dim
    return [np.tile(cos, (1, reps)), np.tile(sin, (1, reps))]


def _largest_divisor(n, at_most, multiple_of):
    return max(k for k in range(multiple_of, at_most + 1, multiple_of) if n % k == 0)


def _log2(n):
    assert n > 0 and n & (n - 1) == 0, n
    return n.bit_length() - 1


def _mod_row(mod_ref, bi, idx):
    return mod_ref[pl.ds(bi, 1), idx * D_MODEL:(idx + 1) * D_MODEL]


def _unit_rms(x):
    ms = jnp.mean(x * x, axis=-1, keepdims=True)
    return x * lax.rsqrt(ms + EPS)


def _rms(x, g):
    return _unit_rms(x) * g


def _row_chunks(span):
    span = span if isinstance(span, slice) else slice(0, span)
    return [slice(r, r + ROW_CHUNK) for r in range(span.start, span.stop, ROW_CHUNK)]


def _norm_mod_rows(load, store, span, g, sc, sh):
    gain = g * (1.0 + sc)
    for rows in _row_chunks(span):
        store(rows, (_unit_rms(load(rows)) * gain + sh).astype(BF16))


def _gated_residual_rows(load_x, load_y, store, span, g, gt):
    gain = gt * g
    for rows in _row_chunks(span):
        store(rows, load_x(rows) + _unit_rms(load_y(rows)) * gain)


def _dot_nt(a, w_rows):
    return lax.dot_general(a, w_rows, (((1,), (1,)), ((), ())), preferred_element_type=F32)


def _rope(x, cos, sin_signed, quarter):
    lane = lax.broadcasted_iota(jnp.int32, x.shape, 1)
    take_up = (lane % (2 * quarter)) < quarter
    up = pltpu.roll(x, LANES - quarter, axis=1)
    down = pltpu.roll(x, quarter, axis=1)
    rot = jnp.where(take_up, up, down)
    return x * cos + rot * sin_signed


def _ada_columns(c, w, b):
    c_act = (c * jax.nn.sigmoid(c)).astype(BF16)
    return jnp.dot(c_act, w.astype(BF16), preferred_element_type=F32) + b


def _ada_kernel(c_ref, w_ref, b_ref, o_ref):
    o_ref[...] = _ada_columns(c_ref[...], w_ref[...], b_ref[...])


def _ada(c_pad, w_ada, b_ada, n):
    m, d = c_pad.shape
    tn = 512
    return pl.pallas_call(
        _ada_kernel,
        grid=(n // tn,),
        in_specs=[
            pl.BlockSpec((m, d), lambda j: (0, 0)),
            pl.BlockSpec((d, tn), lambda j: (0, j)),
            pl.BlockSpec((1, tn), lambda j: (0, j)),
        ],
        out_specs=pl.BlockSpec((m, tn), lambda j: (0, j)),
        out_shape=jax.ShapeDtypeStruct((m, n), F32),
        compiler_params=pltpu.CompilerParams(
            dimension_semantics=("arbitrary",), vmem_limit_bytes=VMEM_LIMIT),
        name="ada_mod",
    )(c_pad, w_ada, b_ada)


def _load_as_bf16(chunks, stage, sem):
    def copy(c):
        src, row0, n, _ = chunks[c]
        slot = c % 2
        return pltpu.make_async_copy(
            src.at[row0:row0 + n, :], stage.at[slot, 0:n, :], sem.at[slot])

    copy(0).start()
    for c, (_, row0, n, dst) in enumerate(chunks):
        if c + 1 < len(chunks):
            copy(c + 1).start()
        copy(c).wait()
        dst[row0:row0 + n, :] = stage[c % 2, 0:n, :].astype(BF16)


def _inproj_kernel(x_ref, mod_ref, gpre_ref, w_hbm, wkv_hbm, gq_ref, gk_ref,
                   gckv_ref, tab_ref,
                   qa_ref, ka_ref, va_ref, qb_ref, kb_ref, vb_ref,
                   h_scr, w_ref, wkv_ref, stage, sem):
    tm = h_scr.shape[0]
    bi = pl.program_id(0)

    @pl.when((bi == 0) & (pl.program_id(1) == 0))
    def _():
        chunks = []
        for src, dst in ((w_hbm, w_ref), (wkv_hbm, wkv_ref)):
            n = _largest_divisor(dst.shape[0], stage.shape[1], BF16_ROWS)
            chunks += [(src, r, n, dst) for r in range(0, dst.shape[0], n)]
        _load_as_bf16(chunks, stage, sem)

    def store_h(rows, val):
        h_scr[rows, :] = val

    _norm_mod_rows(lambda rows: x_ref[0, rows, :], store_h, tm, gpre_ref[...],
                   _mod_row(mod_ref, bi, MOD_SC_A), _mod_row(mod_ref, bi, MOD_SH_A))
    hb = h_scr[...]
    cosa, sina, cosb, sinb = (tab_ref[:, i * LANES:(i + 1) * LANES] for i in range(4))
    scale_a = LOG2_E / math.sqrt(HEAD_DIM)
    scale_b = LOG2_E / math.sqrt(QK_B)

    qa = _dot_nt(hb, w_ref[0:W_QA, :])
    gq = gq_ref[...] * scale_a
    for hd in range(HA):
        sl = slice(hd * HEAD_DIM, (hd + 1) * HEAD_DIM)
        q = _rope(_rms(qa[:, sl], gq), cosa, sina, HEAD_DIM // 4)
        qa_ref[0, :, sl] = q.astype(BF16)

    kva = _dot_nt(hb, w_ref[W_QA:W_QA + W_KA + W_VA, :])
    for hd in range(HKV):
        sl = slice(hd * HEAD_DIM, (hd + 1) * HEAD_DIM)
        k = _rope(_rms(kva[:, sl], gk_ref[...]), cosa, sina, HEAD_DIM // 4)
        ka_ref[0, :, sl] = k.astype(BF16)
    va_ref[0] = kva[:, W_KA:].astype(BF16)

    o_qb = W_QA + W_KA + W_VA
    qb = _dot_nt(hb, w_ref[o_qb:o_qb + W_QB, :])
    lane = lax.broadcasted_iota(jnp.int32, (tm, LANES), 1)
    low = lane < QK_ROPE
    swap = lambda t: pltpu.roll(t, QK_ROPE, axis=1)
    for pair in range(HB // 2):
        t0, t1, t2 = (qb[:, (3 * pair + i) * LANES:(3 * pair + i + 1) * LANES] for i in range(3))
        r1 = _rope(t1, cosb, sinb, QK_ROPE // 4)
        r2 = _rope(t2, cosb, sinb, QK_ROPE // 4)
        heads = (
            (2 * pair, t0, jnp.where(low, r1, 0.0)),
            (2 * pair + 1, jnp.where(low, swap(t1), swap(t2)), jnp.where(low, swap(r2), 0.0)),
        )
        for hd, nope, pe in heads:
            base = hd * QK_B_PAD
            qb_ref[0, :, base:base + QK_NOPE] = (nope * scale_b).astype(BF16)
            qb_ref[0, :, base + QK_NOPE:base + QK_B_PAD] = (pe * scale_b).astype(BF16)

    o_ckv = o_qb + W_QB
    w_rest = jnp.concatenate(
        [w_ref[o_ckv:, :], jnp.zeros((LANES - QK_ROPE, w_ref.shape[1]), BF16)], axis=0)
    rest = _dot_nt(hb, w_rest)
    ckv = _rms(rest[:, :KV_RANK], gckv_ref[...]).astype(BF16)
    kv = jnp.dot(ckv, wkv_ref[...], preferred_element_type=F32)
    kpe = _rope(rest[:, KV_RANK:], cosb, sinb, QK_ROPE // 4).astype(BF16)
    for hd in range(HB):
        base = hd * (QK_NOPE + V_DIM)
        kb_ref[0, :, hd * QK_B_PAD:hd * QK_B_PAD + QK_NOPE] = (
            kv[:, base:base + QK_NOPE].astype(BF16))
        kb_ref[0, :, hd * QK_B_PAD + QK_NOPE:(hd + 1) * QK_B_PAD] = kpe
        vb_ref[0, :, hd * V_DIM:(hd + 1) * V_DIM] = (
            kv[:, base + QK_NOPE:base + QK_NOPE + V_DIM].astype(BF16))


def _inproj(x, mod, g_pre, w_in_p, w_kv, g_q, g_k, g_ckv, tabs, tm):
    b, s, d = x.shape
    n_in = w_in_p.shape[0]
    row = lambda bi, i: (bi, i, 0)
    const = lambda bi, i: (0, 0)
    tab = lambda bi, i: (i, 0)
    vec = lambda n: pl.BlockSpec((1, n), const)
    out_widths = (MIX_A, W_KA, W_VA, HB * QK_B_PAD, HB * QK_B_PAD, MIX_B)
    return pl.pallas_call(
        _inproj_kernel,
        grid=(b, s // tm),
        in_specs=[
            pl.BlockSpec((1, tm, d), row),
            pl.BlockSpec(mod.shape, const),
            vec(d),
            pl.BlockSpec(memory_space=pl.ANY),
            pl.BlockSpec(memory_space=pl.ANY),
            vec(HEAD_DIM), vec(HEAD_DIM), vec(KV_RANK),
            pl.BlockSpec((tm, 4 * LANES), tab),
        ],
        out_specs=[pl.BlockSpec((1, tm, w), row) for w in out_widths],
        out_shape=[jax.ShapeDtypeStruct((b, s, w), BF16) for w in out_widths],
        scratch_shapes=[
            pltpu.VMEM((tm, d), BF16),
            pltpu.VMEM((n_in, d), BF16),
            pltpu.VMEM(w_kv.shape, BF16),
            pltpu.VMEM((2, W_STAGE_ROWS, d), F32),
            pltpu.SemaphoreType.DMA((2,)),
        ],
        compiler_params=pltpu.CompilerParams(
            dimension_semantics=("arbitrary", "arbitrary"),
            vmem_limit_bytes=VMEM_LIMIT),
        name="in_proj",
    )(x, mod, g_pre, w_in_p, w_kv, g_q, g_k, g_ckv, tabs)


def _attn_kernel(*refs, n_heads, dk, shared_kv, tq, n_blocks, n_side, with_ada):
    q_ref, qn_ref, k_ref, kn_ref, v_ref = refs[:5]
    rest = list(refs[5:])
    side_in = [rest.pop(0) for _ in range(n_side)]
    ada_in = [rest.pop(0) for _ in range(3 if with_ada else 0)]
    o_ref = rest.pop(0)
    side_out = [rest.pop(0) for _ in range(n_side)]
    ada_out = [rest.pop(0) for _ in range(1 if with_ada else 0)]
    s_scr, m_scr = rest

    def scores(i, slot):
        nxt = i == n_blocks
        keys_ref = kn_ref if nxt else k_ref
        for hd in range(n_heads):
            kv_hd = 0 if shared_kv else hd
            k = keys_ref[0, :, kv_hd * dk:(kv_hd + 1) * dk]
            if nxt:
                q = qn_ref[0, :, hd * dk:(hd + 1) * dk]
            else:
                q = q_ref[0, i * tq:(i + 1) * tq, hd * dk:(hd + 1) * dk]
            s = lax.dot_general(q, k, (((1,), (1,)), ((), ())), preferred_element_type=F32)
            s_scr[slot, hd] = s
            m_scr[slot, hd] = jnp.max(s, axis=-1, keepdims=True)

    def softmax_pv(slot, row0):
        for hd in range(n_heads):
            kv_hd = 0 if shared_kv else hd
            v = v_ref[0, :, kv_hd * V_DIM:(kv_hd + 1) * V_DIM]
            p = jnp.exp2(s_scr[slot, hd] - m_scr[slot, hd]).astype(BF16)
            v_ext = jnp.concatenate([v, jnp.ones_like(v)], axis=1)
            o_ext = jnp.dot(p, v_ext, preferred_element_type=F32)
            o = o_ext[:, :V_DIM] / o_ext[:, V_DIM:]
            o_ref[0, row0:row0 + tq, hd * V_DIM:(hd + 1) * V_DIM] = o.astype(BF16)

    @pl.when(pl.program_id(0) == 0)
    def _():
        scores(0, 0)

    for i in range(n_blocks):
        scores(i + 1, (i + 1) % 2)
        for w_ref, wb_ref in zip(side_in, side_out):
            n = w_ref.shape[0] // n_blocks
            rows = slice(i * n, (i + 1) * n)
            if len(wb_ref.shape) == 2:
                wb_ref[rows, :] = w_ref[rows, :].astype(BF16)
            else:
                cb = wb_ref.shape[2]
                for f in range(wb_ref.shape[0]):
                    wb_ref[f, rows, :] = w_ref[rows, f * cb:(f + 1) * cb].astype(BF16)
        if with_ada and i == 0:
            c_ref, wada_ref, bada_ref = ada_in
            ada_out[0][...] = _ada_columns(c_ref[...], wada_ref[...], bada_ref[...])
        softmax_pv(i % 2, i * tq)


def _attention(q, k, v, side, *, n_groups, n_heads, dk, shared_kv, tq, n_blocks, name,
               ada=None, vmem_limit=VMEM_LIMIT):
    b, s, _ = q.shape
    n_kv = 1 if shared_kv else n_heads
    n_blk = s // tq
    assert n_blocks % 2 == 0 and n_blk % n_blocks == 0
    n_total = b * n_groups * n_blk
    n_steps = n_total // n_blocks

    sh_blk, sh_grp, sh_nb = (_log2(n) for n in (n_blk, n_groups, n_blocks))

    def q_map(t):
        grp = jnp.right_shift(t, sh_blk)
        return (jnp.right_shift(grp, sh_grp), t & (n_blk - 1), grp & (n_groups - 1))

    def kv_map(t):
        bi, _, g = q_map(t)
        return (bi, 0, g)

    def step_map(j):
        bi, i, g = q_map(n_blocks * j)
        return (bi, jnp.right_shift(i, sh_nb), g)

    def next_first(j):
        return jnp.minimum(n_blocks * (j + 1), n_total - 1)

    side_in_specs, side_out_specs, side_shapes = [], [], []
    for w, col_block in side:
        rows, cols = w.shape[0] // n_steps, w.shape[1]
        side_in_specs.append(pl.BlockSpec((rows, cols), lambda j: (j, 0)))
        if col_block is None:
            side_out_specs.append(pl.BlockSpec((rows, cols), lambda j: (j, 0)))
            side_shapes.append(jax.ShapeDtypeStruct(w.shape, BF16))
        else:
            n_cb = cols // col_block
            side_out_specs.append(
                pl.BlockSpec((n_cb, rows, col_block), lambda j: (0, j, 0)))
            side_shapes.append(jax.ShapeDtypeStruct((n_cb, w.shape[0], col_block), BF16))
    ada_args, ada_in_specs, ada_out_specs, ada_shapes = [], [], [], []
    if ada is not None:
        c, w_ada, b_ada, first_col = ada
        n_cols = w_ada.shape[1] - first_col
        tn = n_cols // n_steps
        skip = first_col // tn
        ada_args = [c, w_ada, b_ada]
        ada_in_specs = [
            pl.BlockSpec(c.shape, lambda j: (0, 0)),
            pl.BlockSpec((w_ada.shape[0], tn), lambda j: (0, skip + j)),
            pl.BlockSpec((1, tn), lambda j: (0, skip + j)),
        ]
        ada_out_specs = [pl.BlockSpec((c.shape[0], tn), lambda j: (0, j))]
        ada_shapes = [jax.ShapeDtypeStruct((c.shape[0], n_cols), F32)]
    return pl.pallas_call(
        functools.partial(_attn_kernel, n_heads=n_heads, dk=dk, shared_kv=shared_kv,
                          tq=tq, n_blocks=n_blocks, n_side=len(side),
                          with_ada=ada is not None),
        grid=(n_steps,),
        in_specs=[
            pl.BlockSpec((1, n_blocks * tq, n_heads * dk), step_map),
            pl.BlockSpec((1, tq, n_heads * dk), lambda j: q_map(next_first(j))),
            pl.BlockSpec((1, s, n_kv * dk), lambda j: kv_map(n_blocks * j)),
            pl.BlockSpec((1, s, n_kv * dk), lambda j: kv_map(next_first(j))),
            pl.BlockSpec((1, s, n_kv * V_DIM), lambda j: kv_map(n_blocks * j)),
        ] + side_in_specs + ada_in_specs,
        out_specs=[pl.BlockSpec((1, n_blocks * tq, n_heads * V_DIM), step_map)]
        + side_out_specs + ada_out_specs,
        out_shape=[jax.ShapeDtypeStruct((b, s, n_groups * n_heads * V_DIM), BF16)]
        + side_shapes + ada_shapes,
        scratch_shapes=[pltpu.VMEM((2, n_heads, tq, s), F32),
                        pltpu.VMEM((2, n_heads, tq, 1), F32)],
        compiler_params=pltpu.CompilerParams(
            dimension_semantics=("arbitrary",), vmem_limit_bytes=vmem_limit),
        name=name,
    )(q, q, k, k, v, *[w for w, _ in side], *ada_args)


def _outproj_kernel(oa_ref, ob_ref, x_ref, mod_ref, ga_ref, gb_ref, w_ref, gpost_ref, o_ref,
                    n_scr, y_scr):
    n_rows = n_scr.shape[0]

    def store_o(rows, val):
        o_ref[0, rows, :] = val

    for rows in _row_chunks(n_rows):
        n_scr[rows, 0:MIX_A] = _rms(oa_ref[0, rows, :].astype(F32), ga_ref[...]).astype(BF16)
        n_scr[rows, MIX_A:] = _rms(ob_ref[0, rows, :].astype(F32), gb_ref[...]).astype(BF16)
    y_scr[...] = jnp.dot(n_scr[...], w_ref[...], preferred_element_type=F32)
    _gated_residual_rows(lambda rows: x_ref[0, rows, :], lambda rows: y_scr[rows, :],
                         store_o, n_rows, gpost_ref[...],
                         _mod_row(mod_ref, pl.program_id(0), MOD_GT_A))


def _outproj(o_a, o_b, x, mod, g_a, g_b, w_out, g_post, tm):
    b, s, d = x.shape
    row = lambda bi, i: (bi, i, 0)
    const = lambda bi, i: (0, 0)
    return pl.pallas_call(
        _outproj_kernel,
        grid=(b, s // tm),
        in_specs=[
            pl.BlockSpec((1, tm, MIX_A), row),
            pl.BlockSpec((1, tm, MIX_B), row),
            pl.BlockSpec((1, tm, d), row),
            pl.BlockSpec(mod.shape, const),
            pl.BlockSpec((1, MIX_A), const),
            pl.BlockSpec((1, MIX_B), const),
            pl.BlockSpec((MIX_A + MIX_B, d), const, pipeline_mode=pl.Buffered(1)),
            pl.BlockSpec((1, d), const),
        ],
        out_specs=pl.BlockSpec((1, tm, d), row),
        out_shape=jax.ShapeDtypeStruct((b, s, d), F32),
        scratch_shapes=[pltpu.VMEM((tm, MIX_A + MIX_B), BF16), pltpu.VMEM((tm, d), F32)],
        compiler_params=pltpu.CompilerParams(
            dimension_semantics=("arbitrary", "arbitrary"),
            vmem_limit_bytes=VMEM_LIMIT),
        name="out_proj",
    )(o_a, o_b, x, mod, g_a, g_b, w_out, g_post)


def _mlp_kernel(x_ref, mod_ref, gpre_ref, w1_ref, w2_ref, gpost_ref,
                o_ref, h_scr, *, n_split):
    bi, f = pl.program_id(0), pl.program_id(2)
    last = pl.num_programs(2) - 1
    n_rows = h_scr.shape[0]
    parts = [slice(r, r + n_rows // n_split) for r in range(0, n_rows, n_rows // n_split)]

    def store_h(rows, val):
        h_scr[rows, :] = val

    def store_o(rows, val):
        o_ref[0, rows, :] = val

    def ffn(rows, assign_first=False):
        n_sub, _, sub = w1_ref.shape
        for c in range(n_sub):
            u = jnp.dot(h_scr[rows, :], w1_ref[c], preferred_element_type=F32)
            u = jnp.maximum(u, 0.0)
            y = jnp.dot((u * u).astype(BF16), w2_ref[c * sub:(c + 1) * sub, :],
                        preferred_element_type=F32)
            if assign_first and c == 0:
                o_ref[0, rows, :] = y
            else:
                o_ref[0, rows, :] += y

    @pl.when(f == 0)
    def _():
        for part in parts:
            _norm_mod_rows(lambda rows: x_ref[0, rows, :], store_h, part,
                           gpre_ref[...], _mod_row(mod_ref, bi, MOD_SC_M),
                           _mod_row(mod_ref, bi, MOD_SH_M))
            ffn(part, assign_first=True)

    @pl.when((f > 0) & (f < last))
    def _():
        ffn(slice(0, n_rows))

    @pl.when(f == last)
    def _():
        for part in parts:
            ffn(part)
            _gated_residual_rows(lambda rows: x_ref[0, rows, :], lambda rows: o_ref[0, rows, :],
                                 store_o, part, gpost_ref[...],
                                 _mod_row(mod_ref, bi, MOD_GT_M))


def _mlp(x, mod, g_pre, w1, w2, g_post, tm):
    b, s, d = x.shape
    n_sub = MLP_TF // MLP_SUB
    n_f, tf = w1.shape[0] // n_sub, MLP_TF
    row = lambda bi, i, f: (bi, i, 0)
    const = lambda bi, i, f: (0, 0)
    return pl.pallas_call(
        functools.partial(_mlp_kernel, n_split=2),
        grid=(b, s // tm, n_f),
        in_specs=[
            pl.BlockSpec((1, tm, d), row),
            pl.BlockSpec(mod.shape, const),
            pl.BlockSpec((1, d), const),
            pl.BlockSpec((n_sub, d, MLP_SUB), lambda bi, i, f: (f, 0, 0)),
            pl.BlockSpec((tf, d), lambda bi, i, f: (f, 0)),
            pl.BlockSpec((1, d), const),
        ],
        out_specs=pl.BlockSpec((1, tm, d), row),
        out_shape=jax.ShapeDtypeStruct((b, s, d), F32),
        scratch_shapes=[pltpu.VMEM((tm, d), BF16)],
        compiler_params=pltpu.CompilerParams(
            dimension_semantics=("arbitrary", "arbitrary", "arbitrary"),
            vmem_limit_bytes=VMEM_LIMIT),
        name="mlp",
    )(x, mod, g_pre, w1, w2, g_post)


def kernel(x, c, w_ada, b_ada, g_pre_attn, w_in, g_q_a, g_k_a, g_ckv, w_kv_b, g_out_a,
           g_out_b, w_out, g_post_attn, g_pre_mlp, w_mlp_in, w_mlp_out, g_post_mlp):
    b, s, d = x.shape
    depth = w_ada.shape[0]
    tabs = _rope_tables(s)
    c_pad = jnp.pad(c, ((0, 8 - b), (0, 0)))

    for l in range(depth):
        b_ada_l = b_ada[l][None, :]
        n_early = N_MOD_EARLY * d
        mod_early = _ada(c_pad, w_ada[l], b_ada_l, n_early)

        q_a, k_a, v_a, q_b, k_b, v_b = _inproj(
            x, mod_early, g_pre_attn[l][None, :], w_in[l].T, w_kv_b[l],
            g_q_a[l][None, :], g_k_a[l][None, :],
            g_ckv[l][None, :], tabs, tm=512)

        o_a, w1_bf, w2_bf = _attention(
            q_a, k_a, v_a, [(w_mlp_in[l], MLP_SUB), (w_mlp_out[l], None)],
            n_groups=HKV, n_heads=G_A,
            dk=HEAD_DIM, shared_kv=True, tq=256, n_blocks=4, name="attn_gqa")
        o_b, wo_bf, mod_late = _attention(
            q_b, k_b, v_b, [(w_out[l], None)], n_groups=HB // 4, n_heads=4,
            dk=QK_B_PAD, shared_kv=False, tq=256, n_blocks=4, name="attn_mla",
            ada=(c_pad, w_ada[l], b_ada_l, n_early), vmem_limit=VMEM_LIMIT_MLA)

        x = _outproj(o_a, o_b, x, mod_late, g_out_a[l][None, :], g_out_b[l][None, :],
                     wo_bf, g_post_attn[l][None, :], tm=512)

        x = _mlp(x, mod_late, g_pre_mlp[l][None, :], w1_bf, w2_bf,
                 g_post_mlp[l][None, :], tm=512)
    return x
```

```python
import functools
import math

import numpy as np
import jax
import jax.numpy as jnp
from jax import lax
from jax.experimental import pallas as pl
from jax.experimental.pallas import tpu as pltpu

D_MODEL = 2048
GRID_W = 64
ROPE_THETA = 10000.0
EPS = 1e-6

HEAD_DIM = 128
HA = 8
HKV = 2
G_A = HA // HKV

HB = 8
QK_NOPE = 128
QK_ROPE = 64
V_DIM = 128
KV_RANK = 512
QK_B = QK_NOPE + QK_ROPE
QK_B_PAD = 256

W_QA = HA * HEAD_DIM
W_KA = HKV * HEAD_DIM
W_VA = HKV * HEAD_DIM
W_QB = HB * QK_B
MIX_A = HA * HEAD_DIM
MIX_B = HB * V_DIM
D_FF = 4 * D_MODEL
N_MOD = 6
N_MOD_EARLY = 2
MOD_SH_A, MOD_SC_A = range(N_MOD_EARLY)
MOD_GT_A, MOD_SH_M, MOD_SC_M, MOD_GT_M = range(N_MOD - N_MOD_EARLY)

LOG2_E = math.log2(math.e)
LANES = 128
BF16_ROWS = 16
ROW_CHUNK = BF16_ROWS
W_STAGE_ROWS = 304
MLP_TF = 2048
MLP_SUB = 1024
VMEM_LIMIT = 56 * 1024 * 1024
VMEM_LIMIT_BIG = 60 * 1024 * 1024

F32 = jnp.float32
BF16 = jnp.bfloat16


def _rope_tables(seq_len):
    parts = _rope_table(seq_len, HEAD_DIM) + _rope_table(seq_len, QK_ROPE)
    return jnp.asarray(np.concatenate(parts, axis=1), F32)


def _rope_table(seq_len, dim):
    pos = np.arange(seq_len)
    row = (pos // GRID_W).astype(np.float64)
    col = (pos % GRID_W).astype(np.float64)
    half = dim // 2
    inv = ROPE_THETA ** (-np.arange(0, half, 2, dtype=np.float64) / half)
    ang_r = row[:, None] * inv[None, :]
    ang_c = col[:, None] * inv[None, :]
    ang = np.concatenate([ang_r, ang_r, ang_c, ang_c], axis=-1)
    cos, sin = np.cos(ang), np.sin(ang)
    quarter = dim // 4
    sign = np.where((np.arange(dim) % half) < quarter, -1.0, 1.0)
    sin = sin * sign[None, :]
    reps = LANES // dim
    return [np.tile(cos, (1, reps)), np.tile(sin, (1, reps))]


def _largest_divisor(n, at_most, multiple_of):
    return max(k for k in range(multiple_of, at_most + 1, multiple_of) if n % k == 0)


def _log2(n):
    assert n > 0 and n & (n - 1) == 0, n
    return n.bit_length() - 1


def _mod_row(mod_ref, bi, idx):
    return mod_ref[pl.ds(bi, 1), idx * D_MODEL:(idx + 1) * D_MODEL]


def _unit_rms(x):
    ms = jnp.mean(x * x, axis=-1, keepdims=True)
    return x * lax.rsqrt(ms + EPS)


def _rms(x, g):
    return _unit_rms(x) * g


def _row_chunks(span):
    span = span if isinstance(span, slice) else slice(0, span)
    return [slice(r, r + ROW_CHUNK) for r in range(span.start, span.stop, ROW_CHUNK)]


def _norm_mod_rows(load, store, span, g, sc, sh):
    gain = g * (1.0 + sc)
    for rows in _row_chunks(span):
        store(rows, (_unit_rms(load(rows)) * gain + sh).astype(BF16))


def _gated_residual_rows(load_x, load_y, store, span, g, gt):
    gain = gt * g
    for rows in _row_chunks(span):
        store(rows, load_x(rows) + _unit_rms(load_y(rows)) * gain)


def _dot_nt(a, w_rows):
    return lax.dot_general(a, w_rows, (((1,), (1,)), ((), ())), preferred_element_type=F32)


def _rope(x, cos, sin_signed, quarter):
    lane = lax.broadcasted_iota(jnp.int32, x.shape, 1)
    take_up = (lane % (2 * quarter)) < quarter
    up = pltpu.roll(x, LANES - quarter, axis=1)
    down = pltpu.roll(x, quarter, axis=1)
    rot = jnp.where(take_up, up, down)
    return x * cos + rot * sin_signed


def _ada_columns(c, w, b):
    c_act = (c * jax.nn.sigmoid(c)).astype(BF16)
    return jnp.dot(c_act, w.astype(BF16), preferred_element_type=F32) + b


def _ada_kernel(c_ref, w_ref, b_ref, o_ref):
    o_ref[...] = _ada_columns(c_ref[...], w_ref[...], b_ref[...])


def _ada(c_pad, w_ada, b_ada, n):
    m, d = c_pad.shape
    tn = 512
    return pl.pallas_call(
        _ada_kernel,
        grid=(n // tn,),
        in_specs=[
            pl.BlockSpec((m, d), lambda j: (0, 0)),
            pl.BlockSpec((d, tn), lambda j: (0, j)),
            pl.BlockSpec((1, tn), lambda j: (0, j)),
        ],
        out_specs=pl.BlockSpec((m, tn), lambda j: (0, j)),
        out_shape=jax.ShapeDtypeStruct((m, n), F32),
        compiler_params=pltpu.CompilerParams(
            dimension_semantics=("arbitrary",), vmem_limit_bytes=VMEM_LIMIT),
        name="ada_mod",
    )(c_pad, w_ada, b_ada)


def _load_as_bf16(chunks, stage, sem):
    def copy(c):
        src, row0, n, _ = chunks[c]
        slot = c % 2
        return pltpu.make_async_copy(
            src.at[row0:row0 + n, :], stage.at[slot, 0:n, :], sem.at[slot])

    copy(0).start()
    for c, (_, row0, n, dst) in enumerate(chunks):
        if c + 1 < len(chunks):
            copy(c + 1).start()
        copy(c).wait()
        dst[row0:row0 + n, :] = stage[c % 2, 0:n, :].astype(BF16)


def _inproj_kernel(x_ref, mod_ref, gpre_ref, w_hbm, wkv_hbm, gq_ref, gk_ref,
                   gckv_ref, tab_ref,
                   qa_ref, ka_ref, va_ref, qb_ref, kb_ref, vb_ref,
                   h_scr, w_ref, wkv_ref, stage, sem):
    tm = h_scr.shape[0]
    bi = pl.program_id(0)

    @pl.when((bi == 0) & (pl.program_id(1) == 0))
    def _():
        chunks = []
        for src, dst in ((w_hbm, w_ref), (wkv_hbm, wkv_ref)):
            n = _largest_divisor(dst.shape[0], stage.shape[1], BF16_ROWS)
            chunks += [(src, r, n, dst) for r in range(0, dst.shape[0], n)]
        _load_as_bf16(chunks, stage, sem)

    def store_h(rows, val):
        h_scr[rows, :] = val

    _norm_mod_rows(lambda rows: x_ref[0, rows, :], store_h, tm, gpre_ref[...],
                   _mod_row(mod_ref, bi, MOD_SC_A), _mod_row(mod_ref, bi, MOD_SH_A))
    hb = h_scr[...]
    cosa, sina, cosb, sinb = (tab_ref[:, i * LANES:(i + 1) * LANES] for i in range(4))
    scale_a = LOG2_E / math.sqrt(HEAD_DIM)
    scale_b = LOG2_E / math.sqrt(QK_B)

    qa = _dot_nt(hb, w_ref[0:W_QA, :])
    gq = gq_ref[...] * scale_a
    for hd in range(HA):
        sl = slice(hd * HEAD_DIM, (hd + 1) * HEAD_DIM)
        q = _rope(_rms(qa[:, sl], gq), cosa, sina, HEAD_DIM // 4)
        qa_ref[0, :, sl] = q.astype(BF16)

    kva = _dot_nt(hb, w_ref[W_QA:W_QA + W_KA + W_VA, :])
    for hd in range(HKV):
        sl = slice(hd * HEAD_DIM, (hd + 1) * HEAD_DIM)
        k = _rope(_rms(kva[:, sl], gk_ref[...]), cosa, sina, HEAD_DIM // 4)
        ka_ref[0, :, sl] = k.astype(BF16)
    va_ref[0] = kva[:, W_KA:].astype(BF16)

    o_qb = W_QA + W_KA + W_VA
    qb = _dot_nt(hb, w_ref[o_qb:o_qb + W_QB, :])
    lane = lax.broadcasted_iota(jnp.int32, (tm, LANES), 1)
    low = lane < QK_ROPE
    swap = lambda t: pltpu.roll(t, QK_ROPE, axis=1)
    for pair in range(HB // 2):
        t0, t1, t2 = (qb[:, (3 * pair + i) * LANES:(3 * pair + i + 1) * LANES] for i in range(3))
        r1 = _rope(t1, cosb, sinb, QK_ROPE // 4)
        r2 = _rope(t2, cosb, sinb, QK_ROPE // 4)
        heads = (
            (2 * pair, t0, jnp.where(low, r1, 0.0)),
            (2 * pair + 1, jnp.where(low, swap(t1), swap(t2)), jnp.where(low, swap(r2), 0.0)),
        )
        for hd, nope, pe in heads:
            base = hd * QK_B_PAD
            qb_ref[0, :, base:base + QK_NOPE] = (nope * scale_b).astype(BF16)
            qb_ref[0, :, base + QK_NOPE:base + QK_B_PAD] = (pe * scale_b).astype(BF16)

    o_ckv = o_qb + W_QB
    w_rest = jnp.concatenate(
        [w_ref[o_ckv:, :], jnp.zeros((LANES - QK_ROPE, w_ref.shape[1]), BF16)], axis=0)
    rest = _dot_nt(hb, w_rest)
    ckv = _rms(rest[:, :KV_RANK], gckv_ref[...]).astype(BF16)
    kv = jnp.dot(ckv, wkv_ref[...], preferred_element_type=F32)
    kpe = _rope(rest[:, KV_RANK:], cosb, sinb, QK_ROPE // 4).astype(BF16)
    for hd in range(HB):
        base = hd * (QK_NOPE + V_DIM)
        kb_ref[0, :, hd * QK_B_PAD:hd * QK_B_PAD + QK_NOPE] = (
            kv[:, base:base + QK_NOPE].astype(BF16))
        kb_ref[0, :, hd * QK_B_PAD + QK_NOPE:(hd + 1) * QK_B_PAD] = kpe
        vb_ref[0, :, hd * V_DIM:(hd + 1) * V_DIM] = (
            kv[:, base + QK_NOPE:base + QK_NOPE + V_DIM].astype(BF16))


def _inproj(x, mod, g_pre, w_in_p, w_kv, g_q, g_k, g_ckv, tabs, tm):
    b, s, d = x.shape
    n_in = w_in_p.shape[0]
    row = lambda bi, i: (bi, i, 0)
    const = lambda bi, i: (0, 0)
    tab = lambda bi, i: (i, 0)
    vec = lambda n: pl.BlockSpec((1, n), const)
    out_widths = (MIX_A, W_KA, W_VA, HB * QK_B_PAD, HB * QK_B_PAD, MIX_B)
    return pl.pallas_call(
        _inproj_kernel,
        grid=(b, s // tm),
        in_specs=[
            pl.BlockSpec((1, tm, d), row),
            pl.BlockSpec(mod.shape, const),
            vec(d),
            pl.BlockSpec(memory_space=pl.ANY),
            pl.BlockSpec(memory_space=pl.ANY),
            vec(HEAD_DIM), vec(HEAD_DIM), vec(KV_RANK),
            pl.BlockSpec((tm, 4 * LANES), tab),
        ],
        out_specs=[pl.BlockSpec((1, tm, w), row) for w in out_widths],
        out_shape=[jax.ShapeDtypeStruct((b, s, w), BF16) for w in out_widths],
        scratch_shapes=[
            pltpu.VMEM((tm, d), BF16),
            pltpu.VMEM((n_in, d), BF16),
            pltpu.VMEM(w_kv.shape, BF16),
            pltpu.VMEM((2, W_STAGE_ROWS, d), F32),
            pltpu.SemaphoreType.DMA((2,)),
        ],
        compiler_params=pltpu.CompilerParams(
            dimension_semantics=("arbitrary", "arbitrary"),
            vmem_limit_bytes=VMEM_LIMIT),
        name="in_proj",
    )(x, mod, g_pre, w_in_p, w_kv, g_q, g_k, g_ckv, tabs)


def _side_specs(side, n_steps, step_of):
    in_specs, out_specs, shapes = [], [], []
    for w, col_block in side:
        rows, cols = w.shape[0] // n_steps, w.shape[1]
        in_specs.append(pl.BlockSpec((rows, cols), lambda *g: (step_of(*g), 0)))
        if col_block is None:
            out_specs.append(pl.BlockSpec((rows, cols), lambda *g: (step_of(*g), 0)))
            shapes.append(jax.ShapeDtypeStruct(w.shape, BF16))
        else:
            n_cb = cols // col_block
            out_specs.append(
                pl.BlockSpec((n_cb, rows, col_block), lambda *g: (0, step_of(*g), 0)))
            shapes.append(jax.ShapeDtypeStruct((n_cb, w.shape[0], col_block), BF16))
    return in_specs, out_specs, shapes


def _side_cast(side_in, side_out, part, n_parts):
    for w_ref, wb_ref in zip(side_in, side_out):
        n = w_ref.shape[0] // n_parts
        rows = slice(part * n, (part + 1) * n)
        if len(wb_ref.shape) == 2:
            wb_ref[rows, :] = w_ref[rows, :].astype(BF16)
        else:
            cb = wb_ref.shape[2]
            for f in range(wb_ref.shape[0]):
                wb_ref[f, rows, :] = w_ref[rows, f * cb:(f + 1) * cb].astype(BF16)


def _attn_kernel(*refs, n_heads, dk, shared_kv, tq, n_blocks, n_side, with_ada):
    q_ref, qn_ref, k_ref, kn_ref, v_ref = refs[:5]
    rest = list(refs[5:])
    side_in = [rest.pop(0) for _ in range(n_side)]
    ada_in = [rest.pop(0) for _ in range(3 if with_ada else 0)]
    o_ref = rest.pop(0)
    side_out = [rest.pop(0) for _ in range(n_side)]
    ada_out = [rest.pop(0) for _ in range(1 if with_ada else 0)]
    s_scr, m_scr = rest

    def scores(i, slot):
        nxt = i == n_blocks
        keys_ref = kn_ref if nxt else k_ref
        for hd in range(n_heads):
            kv_hd = 0 if shared_kv else hd
            k = keys_ref[0, :, kv_hd * dk:(kv_hd + 1) * dk]
            if nxt:
                q = qn_ref[0, :, hd * dk:(hd + 1) * dk]
            else:
                q = q_ref[0, i * tq:(i + 1) * tq, hd * dk:(hd + 1) * dk]
            s = lax.dot_general(q, k, (((1,), (1,)), ((), ())), preferred_element_type=F32)
            s_scr[slot, hd] = s
            m_scr[slot, hd] = jnp.max(s, axis=-1, keepdims=True)

    def softmax_pv(slot, row0):
        for hd in range(n_heads):
            kv_hd = 0 if shared_kv else hd
            v = v_ref[0, :, kv_hd * V_DIM:(kv_hd + 1) * V_DIM]
            p = jnp.exp2(s_scr[slot, hd] - m_scr[slot, hd]).astype(BF16)
            v_ext = jnp.concatenate([v, jnp.ones_like(v)], axis=1)
            o_ext = jnp.dot(p, v_ext, preferred_element_type=F32)
            o = o_ext[:, :V_DIM] / o_ext[:, V_DIM:]
            o_ref[0, row0:row0 + tq, hd * V_DIM:(hd + 1) * V_DIM] = o.astype(BF16)

    @pl.when(pl.program_id(0) == 0)
    def _():
        scores(0, 0)

    for i in range(n_blocks):
        scores(i + 1, (i + 1) % 2)
        _side_cast(side_in, side_out, i, n_blocks)
        if with_ada and i == 0:
            c_ref, wada_ref, bada_ref = ada_in
            ada_out[0][...] = _ada_columns(c_ref[...], wada_ref[...], bada_ref[...])
        softmax_pv(i % 2, i * tq)


def _attention(q, k, v, side, *, n_groups, n_heads, dk, shared_kv, tq, n_blocks, name,
               ada=None, vmem_limit=VMEM_LIMIT):
    b, s, _ = q.shape
    n_kv = 1 if shared_kv else n_heads
    n_blk = s // tq
    assert n_blocks % 2 == 0 and n_blk % n_blocks == 0
    n_total = b * n_groups * n_blk
    n_steps = n_total // n_blocks

    sh_blk, sh_grp, sh_nb = (_log2(n) for n in (n_blk, n_groups, n_blocks))

    def q_map(t):
        grp = jnp.right_shift(t, sh_blk)
        return (jnp.right_shift(grp, sh_grp), t & (n_blk - 1), grp & (n_groups - 1))

    def kv_map(t):
        bi, _, g = q_map(t)
        return (bi, 0, g)

    def step_map(j):
        bi, i, g = q_map(n_blocks * j)
        return (bi, jnp.right_shift(i, sh_nb), g)

    def next_first(j):
        return jnp.minimum(n_blocks * (j + 1), n_total - 1)

    side_in_specs, side_out_specs, side_shapes = _side_specs(side, n_steps, lambda j: j)
    ada_args, ada_in_specs, ada_out_specs, ada_shapes = [], [], [], []
    if ada is not None:
        c, w_ada, b_ada, first_col = ada
        n_cols = w_ada.shape[1] - first_col
        tn = n_cols // n_steps
        skip = first_col // tn
        ada_args = [c, w_ada, b_ada]
        ada_in_specs = [
            pl.BlockSpec(c.shape, lambda j: (0, 0)),
            pl.BlockSpec((w_ada.shape[0], tn), lambda j: (0, skip + j)),
            pl.BlockSpec((1, tn), lambda j: (0, skip + j)),
        ]
        ada_out_specs = [pl.BlockSpec((c.shape[0], tn), lambda j: (0, j))]
        ada_shapes = [jax.ShapeDtypeStruct((c.shape[0], n_cols), F32)]
    return pl.pallas_call(
        functools.partial(_attn_kernel, n_heads=n_heads, dk=dk, shared_kv=shared_kv,
                          tq=tq, n_blocks=n_blocks, n_side=len(side),
                          with_ada=ada is not None),
        grid=(n_steps,),
        in_specs=[
            pl.BlockSpec((1, n_blocks * tq, n_heads * dk), step_map),
            pl.BlockSpec((1, tq, n_heads * dk), lambda j: q_map(next_first(j))),
            pl.BlockSpec((1, s, n_kv * dk), lambda j: kv_map(n_blocks * j)),
            pl.BlockSpec((1, s, n_kv * dk), lambda j: kv_map(next_first(j))),
            pl.BlockSpec((1, s, n_kv * V_DIM), lambda j: kv_map(n_blocks * j)),
        ] + side_in_specs + ada_in_specs,
        out_specs=[pl.BlockSpec((1, n_blocks * tq, n_heads * V_DIM), step_map)]
        + side_out_specs + ada_out_specs,
        out_shape=[jax.ShapeDtypeStruct((b, s, n_groups * n_heads * V_DIM), BF16)]
        + side_shapes + ada_shapes,
        scratch_shapes=[pltpu.VMEM((2, n_heads, tq, s), F32),
                        pltpu.VMEM((2, n_heads, tq, 1), F32)],
        compiler_params=pltpu.CompilerParams(
            dimension_semantics=("arbitrary",), vmem_limit_bytes=vmem_limit),
        name=name,
    )(q, q, k, k, v, *[w for w, _ in side], *ada_args)


def _outproj_kernel(oa_ref, ob_ref, x_ref, mod_ref, ga_ref, gb_ref, w_ref, gpost_ref, *rest,
                    n_side):
    rest = list(rest)
    side_in = [rest.pop(0) for _ in range(n_side)]
    o_ref = rest.pop(0)
    side_out = [rest.pop(0) for _ in range(n_side)]
    n_scr, y_scr = rest
    n_rows = n_scr.shape[0]

    def store_o(rows, val):
        o_ref[0, rows, :] = val

    for rows in _row_chunks(n_rows):
        n_scr[rows, 0:MIX_A] = _rms(oa_ref[0, rows, :].astype(F32), ga_ref[...]).astype(BF16)
        n_scr[rows, MIX_A:] = _rms(ob_ref[0, rows, :].astype(F32), gb_ref[...]).astype(BF16)
    y_scr[...] = jnp.dot(n_scr[...], w_ref[...], preferred_element_type=F32)
    _side_cast(side_in, side_out, 0, 1)
    _gated_residual_rows(lambda rows: x_ref[0, rows, :], lambda rows: y_scr[rows, :],
                         store_o, n_rows, gpost_ref[...],
                         _mod_row(mod_ref, pl.program_id(0), MOD_GT_A))


def _outproj(o_a, o_b, x, mod, g_a, g_b, w_out, g_post, side, tm):
    b, s, d = x.shape
    n_i = s // tm
    row = lambda bi, i: (bi, i, 0)
    const = lambda bi, i: (0, 0)
    side_in_specs, side_out_specs, side_shapes = _side_specs(
        side, b * n_i, lambda bi, i: bi * n_i + i)
    return pl.pallas_call(
        functools.partial(_outproj_kernel, n_side=len(side)),
        grid=(b, n_i),
        in_specs=[
            pl.BlockSpec((1, tm, MIX_A), row),
            pl.BlockSpec((1, tm, MIX_B), row),
            pl.BlockSpec((1, tm, d), row),
            pl.BlockSpec(mod.shape, const),
            pl.BlockSpec((1, MIX_A), const),
            pl.BlockSpec((1, MIX_B), const),
            pl.BlockSpec((MIX_A + MIX_B, d), const, pipeline_mode=pl.Buffered(1)),
            pl.BlockSpec((1, d), const),
        ] + side_in_specs,
        out_specs=[pl.BlockSpec((1, tm, d), row)] + side_out_specs,
        out_shape=[jax.ShapeDtypeStruct((b, s, d), F32)] + side_shapes,
        scratch_shapes=[pltpu.VMEM((tm, MIX_A + MIX_B), BF16), pltpu.VMEM((tm, d), F32)],
        compiler_params=pltpu.CompilerParams(
            dimension_semantics=("arbitrary", "arbitrary"),
            vmem_limit_bytes=VMEM_LIMIT_BIG if side else VMEM_LIMIT),
        name="out_proj",
    )(o_a, o_b, x, mod, g_a, g_b, w_out, g_post, *[w for w, _ in side])


def _mlp_kernel(x_ref, mod_ref, gpre_ref, w1_ref, w2_ref, gpost_ref,
                o_ref, h_scr, *, n_split):
    bi, f = pl.program_id(0), pl.program_id(2)
    last = pl.num_programs(2) - 1
    n_rows = h_scr.shape[0]
    parts = [slice(r, r + n_rows // n_split) for r in range(0, n_rows, n_rows // n_split)]

    def store_h(rows, val):
        h_scr[rows, :] = val

    def store_o(rows, val):
        o_ref[0, rows, :] = val

    def ffn(rows, assign_first=False):
        n_sub, _, sub = w1_ref.shape
        for c in range(n_sub):
            u = jnp.dot(h_scr[rows, :], w1_ref[c], preferred_element_type=F32)
            u = jnp.maximum(u, 0.0)
            y = jnp.dot((u * u).astype(BF16), w2_ref[c * sub:(c + 1) * sub, :],
                        preferred_element_type=F32)
            if assign_first and c == 0:
                o_ref[0, rows, :] = y
            else:
                o_ref[0, rows, :] += y

    @pl.when(f == 0)
    def _():
        for part in parts:
            _norm_mod_rows(lambda rows: x_ref[0, rows, :], store_h, part,
                           gpre_ref[...], _mod_row(mod_ref, bi, MOD_SC_M),
                           _mod_row(mod_ref, bi, MOD_SH_M))
            ffn(part, assign_first=True)

    @pl.when((f > 0) & (f < last))
    def _():
        ffn(slice(0, n_rows))

    @pl.when(f == last)
    def _():
        for part in parts:
            ffn(part)
            _gated_residual_rows(lambda rows: x_ref[0, rows, :], lambda rows: o_ref[0, rows, :],
                                 store_o, part, gpost_ref[...],
                                 _mod_row(mod_ref, bi, MOD_GT_M))


def _mlp(x, mod, g_pre, w1, w2, g_post, tm):
    b, s, d = x.shape
    n_sub = MLP_TF // MLP_SUB
    n_f, tf = w1.shape[0] // n_sub, MLP_TF
    row = lambda bi, i, f: (bi, i, 0)
    const = lambda bi, i, f: (0, 0)
    return pl.pallas_call(
        functools.partial(_mlp_kernel, n_split=2),
        grid=(b, s // tm, n_f),
        in_specs=[
            pl.BlockSpec((1, tm, d), row),
            pl.BlockSpec(mod.shape, const),
            pl.BlockSpec((1, d), const),
            pl.BlockSpec((n_sub, d, MLP_SUB), lambda bi, i, f: (f, 0, 0)),
            pl.BlockSpec((tf, d), lambda bi, i, f: (f, 0)),
            pl.BlockSpec((1, d), const),
        ],
        out_specs=pl.BlockSpec((1, tm, d), row),
        out_shape=jax.ShapeDtypeStruct((b, s, d), F32),
        scratch_shapes=[pltpu.VMEM((tm, d), BF16)],
        compiler_params=pltpu.CompilerParams(
            dimension_semantics=("arbitrary", "arbitrary", "arbitrary"),
            vmem_limit_bytes=VMEM_LIMIT),
        name="mlp",
    )(x, mod, g_pre, w1, w2, g_post)


def kernel(x, c, w_ada, b_ada, g_pre_attn, w_in, g_q_a, g_k_a, g_ckv, w_kv_b, g_out_a,
           g_out_b, w_out, g_post_attn, g_pre_mlp, w_mlp_in, w_mlp_out, g_post_mlp):
    b, s, d = x.shape
    depth = w_ada.shape[0]
    tabs = _rope_tables(s)
    c_pad = jnp.pad(c, ((0, 8 - b), (0, 0)))

    for l in range(depth):
        b_ada_l = b_ada[l][None, :]
        n_early = N_MOD_EARLY * d
        mod_early = _ada(c_pad, w_ada[l], b_ada_l, n_early)

        q_a, k_a, v_a, q_b, k_b, v_b = _inproj(
            x, mod_early, g_pre_attn[l][None, :], w_in[l].T, w_kv_b[l],
            g_q_a[l][None, :], g_k_a[l][None, :],
            g_ckv[l][None, :], tabs, tm=512)

        o_a, wo_bf = _attention(
            q_a, k_a, v_a, [(w_out[l], None)], n_groups=HKV, n_heads=G_A,
            dk=HEAD_DIM, shared_kv=True, tq=256, n_blocks=4, name="attn_gqa")
        o_b, mod_late = _attention(
            q_b, k_b, v_b, [], n_groups=HB // 4, n_heads=4,
            dk=QK_B_PAD, shared_kv=False, tq=256, n_blocks=4, name="attn_mla",
            ada=(c_pad, w_ada[l], b_ada_l, n_early), vmem_limit=VMEM_LIMIT_BIG)

        x, w1_bf, w2_bf = _outproj(
            o_a, o_b, x, mod_late, g_out_a[l][None, :], g_out_b[l][None, :], wo_bf,
            g_post_attn[l][None, :], [(w_mlp_in[l], MLP_SUB), (w_mlp_out[l], None)],
            tm=512)

        x = _mlp(x, mod_late, g_pre_mlp[l][None, :], w1_bf, w2_bf,
                 g_post_mlp[l][None, :], tm=512)
    return x
```

```python
import functools
import math

import numpy as np
import jax
import jax.numpy as jnp
from jax import lax
from jax.experimental import pallas as pl
from jax.experimental.pallas import tpu as pltpu

D_MODEL = 2048
GRID_W = 64
ROPE_THETA = 10000.0
EPS = 1e-6

HEAD_DIM = 128
HA = 8
HKV = 2
G_A = HA // HKV

HB = 8
N_GROUPS = 2
QK_NOPE = 128
QK_ROPE = 64
V_DIM = 128
KV_RANK = 512
QK_B = QK_NOPE + QK_ROPE
QK_B_PAD = 256

W_QA = HA * HEAD_DIM
W_KA = HKV * HEAD_DIM
W_VA = HKV * HEAD_DIM
W_QB = HB * QK_B
MIX_A = HA * HEAD_DIM
MIX_B = HB * V_DIM
D_FF = 4 * D_MODEL
N_MOD = 6
N_MOD_EARLY = 2
MOD_SH_A, MOD_SC_A = range(N_MOD_EARLY)
MOD_GT_A, MOD_SH_M, MOD_SC_M, MOD_GT_M = range(N_MOD - N_MOD_EARLY)

LOG2_E = math.log2(math.e)
LANES = 128
BF16_ROWS = 16
ROW_CHUNK = BF16_ROWS
W_STAGE_ROWS = 304
MLP_TF = 2048
MLP_SUB = 1024
VMEM_LIMIT = 56 * 1024 * 1024
VMEM_LIMIT_BIG = 60 * 1024 * 1024

F32 = jnp.float32
BF16 = jnp.bfloat16


def _rope_tables(seq_len):
    parts = _rope_table(seq_len, HEAD_DIM) + _rope_table(seq_len, QK_ROPE)
    return jnp.asarray(np.concatenate(parts, axis=1), F32)


def _rope_table(seq_len, dim):
    pos = np.arange(seq_len)
    row = (pos // GRID_W).astype(np.float64)
    col = (pos % GRID_W).astype(np.float64)
    half = dim // 2
    inv = ROPE_THETA ** (-np.arange(0, half, 2, dtype=np.float64) / half)
    ang_r = row[:, None] * inv[None, :]
    ang_c = col[:, None] * inv[None, :]
    ang = np.concatenate([ang_r, ang_r, ang_c, ang_c], axis=-1)
    cos, sin = np.cos(ang), np.sin(ang)
    quarter = dim // 4
    sign = np.where((np.arange(dim) % half) < quarter, -1.0, 1.0)
    sin = sin * sign[None, :]
    reps = LANES // dim
    return [np.tile(cos, (1, reps)), np.tile(sin, (1, reps))]


def _largest_divisor(n, at_most, multiple_of):
    return max(k for k in range(multiple_of, at_most + 1, multiple_of) if n % k == 0)


def _log2(n):
    assert n > 0 and n & (n - 1) == 0, n
    return n.bit_length() - 1


def _mod_row(mod_ref, bi, idx):
    return mod_ref[pl.ds(bi, 1), idx * D_MODEL:(idx + 1) * D_MODEL]


def _unit_rms(x):
    ms = jnp.mean(x * x, axis=-1, keepdims=True)
    return x * lax.rsqrt(ms + EPS)


def _rms(x, g):
    return _unit_rms(x) * g


def _row_chunks(span):
    span = span if isinstance(span, slice) else slice(0, span)
    return [slice(r, r + ROW_CHUNK) for r in range(span.start, span.stop, ROW_CHUNK)]


def _norm_mod_rows(load, store, span, g, sc, sh):
    gain = g * (1.0 + sc)
    for rows in _row_chunks(span):
        store(rows, (_unit_rms(load(rows)) * gain + sh).astype(BF16))


def _gated_residual_rows(load_x, load_y, store, span, g, gt):
    gain = gt * g
    for rows in _row_chunks(span):
        store(rows, load_x(rows) + _unit_rms(load_y(rows)) * gain)


def _dot_nt(a, w_rows):
    return lax.dot_general(a, w_rows, (((1,), (1,)), ((), ())), preferred_element_type=F32)


def _rope(x, cos, sin_signed, quarter):
    lane = lax.broadcasted_iota(jnp.int32, x.shape, 1)
    take_up = (lane % (2 * quarter)) < quarter
    up = pltpu.roll(x, LANES - quarter, axis=1)
    down = pltpu.roll(x, quarter, axis=1)
    rot = jnp.where(take_up, up, down)
    return x * cos + rot * sin_signed


def _ada_columns(c, w, b):
    c_act = (c * jax.nn.sigmoid(c)).astype(BF16)
    return jnp.dot(c_act, w.astype(BF16), preferred_element_type=F32) + b


def _ada_kernel(c_ref, w_ref, b_ref, o_ref):
    o_ref[...] = _ada_columns(c_ref[...], w_ref[...], b_ref[...])


def _ada(c_pad, w_ada, b_ada, n):
    m, d = c_pad.shape
    tn = 512
    return pl.pallas_call(
        _ada_kernel,
        grid=(n // tn,),
        in_specs=[
            pl.BlockSpec((m, d), lambda j: (0, 0)),
            pl.BlockSpec((d, tn), lambda j: (0, j)),
            pl.BlockSpec((1, tn), lambda j: (0, j)),
        ],
        out_specs=pl.BlockSpec((m, tn), lambda j: (0, j)),
        out_shape=jax.ShapeDtypeStruct((m, n), F32),
        compiler_params=pltpu.CompilerParams(
            dimension_semantics=("arbitrary",), vmem_limit_bytes=VMEM_LIMIT),
        name="ada_mod",
    )(c_pad, w_ada, b_ada)


def _store_cols(ref, col0, val):
    g, c = divmod(col0, ref.shape[3])
    ref[0, g, :, c:c + val.shape[1]] = val


def _load_as_bf16(chunks, stage, sem):
    def copy(c):
        src, row0, n, _ = chunks[c]
        slot = c % 2
        return pltpu.make_async_copy(
            src.at[row0:row0 + n, :], stage.at[slot, 0:n, :], sem.at[slot])

    copy(0).start()
    for c, (_, row0, n, dst) in enumerate(chunks):
        if c + 1 < len(chunks):
            copy(c + 1).start()
        copy(c).wait()
        dst[row0:row0 + n, :] = stage[c % 2, 0:n, :].astype(BF16)


def _inproj_kernel(x_ref, mod_ref, gpre_ref, w_hbm, wkv_hbm, gq_ref, gk_ref,
                   gckv_ref, tab_ref,
                   qa_ref, ka_ref, va_ref, qb_ref, kb_ref, vb_ref,
                   h_scr, w_ref, wkv_ref, stage, sem):
    tm = h_scr.shape[0]
    bi = pl.program_id(0)

    @pl.when((bi == 0) & (pl.program_id(1) == 0))
    def _():
        chunks = []
        for src, dst in ((w_hbm, w_ref), (wkv_hbm, wkv_ref)):
            n = _largest_divisor(dst.shape[0], stage.shape[1], BF16_ROWS)
            chunks += [(src, r, n, dst) for r in range(0, dst.shape[0], n)]
        _load_as_bf16(chunks, stage, sem)

    def store_h(rows, val):
        h_scr[rows, :] = val

    _norm_mod_rows(lambda rows: x_ref[0, rows, :], store_h, tm, gpre_ref[...],
                   _mod_row(mod_ref, bi, MOD_SC_A), _mod_row(mod_ref, bi, MOD_SH_A))
    hb = h_scr[...]
    cosa, sina, cosb, sinb = (tab_ref[:, i * LANES:(i + 1) * LANES] for i in range(4))
    scale_a = LOG2_E / math.sqrt(HEAD_DIM)
    scale_b = LOG2_E / math.sqrt(QK_B)

    qa = _dot_nt(hb, w_ref[0:W_QA, :])
    gq = gq_ref[...] * scale_a
    for hd in range(HA):
        sl = slice(hd * HEAD_DIM, (hd + 1) * HEAD_DIM)
        q = _rope(_rms(qa[:, sl], gq), cosa, sina, HEAD_DIM // 4)
        _store_cols(qa_ref, hd * HEAD_DIM, q.astype(BF16))

    kva = _dot_nt(hb, w_ref[W_QA:W_QA + W_KA + W_VA, :])
    for hd in range(HKV):
        sl = slice(hd * HEAD_DIM, (hd + 1) * HEAD_DIM)
        k = _rope(_rms(kva[:, sl], gk_ref[...]), cosa, sina, HEAD_DIM // 4)
        _store_cols(ka_ref, hd * HEAD_DIM, k.astype(BF16))
        _store_cols(va_ref, hd * HEAD_DIM,
                    kva[:, W_KA + hd * HEAD_DIM:W_KA + (hd + 1) * HEAD_DIM].astype(BF16))

    o_qb = W_QA + W_KA + W_VA
    qb = _dot_nt(hb, w_ref[o_qb:o_qb + W_QB, :])
    lane = lax.broadcasted_iota(jnp.int32, (tm, LANES), 1)
    low = lane < QK_ROPE
    swap = lambda t: pltpu.roll(t, QK_ROPE, axis=1)
    for pair in range(HB // 2):
        t0, t1, t2 = (qb[:, (3 * pair + i) * LANES:(3 * pair + i + 1) * LANES] for i in range(3))
        r1 = _rope(t1, cosb, sinb, QK_ROPE // 4)
        r2 = _rope(t2, cosb, sinb, QK_ROPE // 4)
        heads = (
            (2 * pair, t0, jnp.where(low, r1, 0.0)),
            (2 * pair + 1, jnp.where(low, swap(t1), swap(t2)), jnp.where(low, swap(r2), 0.0)),
        )
        for hd, nope, pe in heads:
            base = hd * QK_B_PAD
            _store_cols(qb_ref, base, (nope * scale_b).astype(BF16))
            _store_cols(qb_ref, base + QK_NOPE, (pe * scale_b).astype(BF16))

    o_ckv = o_qb + W_QB
    w_rest = jnp.concatenate(
        [w_ref[o_ckv:, :], jnp.zeros((LANES - QK_ROPE, w_ref.shape[1]), BF16)], axis=0)
    rest = _dot_nt(hb, w_rest)
    ckv = _rms(rest[:, :KV_RANK], gckv_ref[...]).astype(BF16)
    kv = jnp.dot(ckv, wkv_ref[...], preferred_element_type=F32)
    kpe = _rope(rest[:, KV_RANK:], cosb, sinb, QK_ROPE // 4).astype(BF16)
    for hd in range(HB):
        base = hd * (QK_NOPE + V_DIM)
        _store_cols(kb_ref, hd * QK_B_PAD, kv[:, base:base + QK_NOPE].astype(BF16))
        _store_cols(kb_ref, hd * QK_B_PAD + QK_NOPE, kpe)
        _store_cols(vb_ref, hd * V_DIM,
                    kv[:, base + QK_NOPE:base + QK_NOPE + V_DIM].astype(BF16))


def _inproj(x, mod, g_pre, w_in_p, w_kv, g_q, g_k, g_ckv, tabs, tm):
    b, s, d = x.shape
    n_in = w_in_p.shape[0]
    row = lambda bi, i: (bi, i, 0)
    const = lambda bi, i: (0, 0)
    tab = lambda bi, i: (i, 0)
    vec = lambda n: pl.BlockSpec((1, n), const)
    out_widths = (MIX_A, W_KA, W_VA, HB * QK_B_PAD, HB * QK_B_PAD, MIX_B)
    return pl.pallas_call(
        _inproj_kernel,
        grid=(b, s // tm),
        in_specs=[
            pl.BlockSpec((1, tm, d), row),
            pl.BlockSpec(mod.shape, const),
            vec(d),
            pl.BlockSpec(memory_space=pl.ANY),
            pl.BlockSpec(memory_space=pl.ANY),
            vec(HEAD_DIM), vec(HEAD_DIM), vec(KV_RANK),
            pl.BlockSpec((tm, 4 * LANES), tab),
        ],
        out_specs=[pl.BlockSpec((1, N_GROUPS, tm, w // N_GROUPS), lambda bi, i: (bi, 0, i, 0))
                   for w in out_widths],
        out_shape=[jax.ShapeDtypeStruct((b, N_GROUPS, s, w // N_GROUPS), BF16)
                   for w in out_widths],
        scratch_shapes=[
            pltpu.VMEM((tm, d), BF16),
            pltpu.VMEM((n_in, d), BF16),
            pltpu.VMEM(w_kv.shape, BF16),
            pltpu.VMEM((2, W_STAGE_ROWS, d), F32),
            pltpu.SemaphoreType.DMA((2,)),
        ],
        compiler_params=pltpu.CompilerParams(
            dimension_semantics=("arbitrary", "arbitrary"),
            vmem_limit_bytes=VMEM_LIMIT),
        name="in_proj",
    )(x, mod, g_pre, w_in_p, w_kv, g_q, g_k, g_ckv, tabs)


def _side_specs(side, n_steps, step_of):
    in_specs, out_specs, shapes = [], [], []
    for w, col_block in side:
        rows, cols = w.shape[0] // n_steps, w.shape[1]
        in_specs.append(pl.BlockSpec((rows, cols), lambda *g: (step_of(*g), 0)))
        if col_block is None:
            out_specs.append(pl.BlockSpec((rows, cols), lambda *g: (step_of(*g), 0)))
            shapes.append(jax.ShapeDtypeStruct(w.shape, BF16))
        else:
            n_cb = cols // col_block
            out_specs.append(
                pl.BlockSpec((n_cb, rows, col_block), lambda *g: (0, step_of(*g), 0)))
            shapes.append(jax.ShapeDtypeStruct((n_cb, w.shape[0], col_block), BF16))
    return in_specs, out_specs, shapes


def _side_cast(side_in, side_out, part, n_parts):
    for w_ref, wb_ref in zip(side_in, side_out):
        n = w_ref.shape[0] // n_parts
        rows = slice(part * n, (part + 1) * n)
        if len(wb_ref.shape) == 2:
            wb_ref[rows, :] = w_ref[rows, :].astype(BF16)
        else:
            cb = wb_ref.shape[2]
            for f in range(wb_ref.shape[0]):
                wb_ref[f, rows, :] = w_ref[rows, f * cb:(f + 1) * cb].astype(BF16)


def _attn_kernel(*refs, n_heads, dk, shared_kv, tq, n_blocks, n_side, with_ada):
    q_ref, qn_ref, k_ref, kn_ref, v_ref = refs[:5]
    rest = list(refs[5:])
    side_in = [rest.pop(0) for _ in range(n_side)]
    ada_in = [rest.pop(0) for _ in range(3 if with_ada else 0)]
    o_ref = rest.pop(0)
    side_out = [rest.pop(0) for _ in range(n_side)]
    ada_out = [rest.pop(0) for _ in range(1 if with_ada else 0)]
    s_scr, m_scr = rest

    def scores(i, slot):
        nxt = i == n_blocks
        keys_ref = kn_ref if nxt else k_ref
        for hd in range(n_heads):
            kv_hd = 0 if shared_kv else hd
            k = keys_ref[0, 0, :, kv_hd * dk:(kv_hd + 1) * dk]
            if nxt:
                q = qn_ref[0, 0, :, hd * dk:(hd + 1) * dk]
            else:
                q = q_ref[0, 0, i * tq:(i + 1) * tq, hd * dk:(hd + 1) * dk]
            s = lax.dot_general(q, k, (((1,), (1,)), ((), ())), preferred_element_type=F32)
            s_scr[slot, hd] = s
            m_scr[slot, hd] = jnp.max(s, axis=-1, keepdims=True)

    def softmax_pv(slot, row0):
        for hd in range(n_heads):
            kv_hd = 0 if shared_kv else hd
            v = v_ref[0, 0, :, kv_hd * V_DIM:(kv_hd + 1) * V_DIM]
            p = jnp.exp2(s_scr[slot, hd] - m_scr[slot, hd]).astype(BF16)
            v_ext = jnp.concatenate([v, jnp.ones_like(v)], axis=1)
            o_ext = jnp.dot(p, v_ext, preferred_element_type=F32)
            o = o_ext[:, :V_DIM] / o_ext[:, V_DIM:]
            o_ref[0, 0, row0:row0 + tq, hd * V_DIM:(hd + 1) * V_DIM] = o.astype(BF16)

    @pl.when(pl.program_id(0) == 0)
    def _():
        scores(0, 0)

    for i in range(n_blocks):
        scores(i + 1, (i + 1) % 2)
        _side_cast(side_in, side_out, i, n_blocks)
        if with_ada and i == 0:
            c_ref, wada_ref, bada_ref = ada_in
            ada_out[0][...] = _ada_columns(c_ref[...], wada_ref[...], bada_ref[...])
        softmax_pv(i % 2, i * tq)


def _attention(q, k, v, side, *, n_heads, dk, shared_kv, tq, n_blocks, name,
               ada=None, vmem_limit=VMEM_LIMIT):
    b, n_groups, s, _ = q.shape
    n_kv = 1 if shared_kv else n_heads
    n_blk = s // tq
    assert n_blocks % 2 == 0 and n_blk % n_blocks == 0
    n_total = b * n_groups * n_blk
    n_steps = n_total // n_blocks

    sh_blk, sh_grp, sh_nb = (_log2(n) for n in (n_blk, n_groups, n_blocks))

    def q_map(t):
        grp = jnp.right_shift(t, sh_blk)
        return (jnp.right_shift(grp, sh_grp), grp & (n_groups - 1), t & (n_blk - 1), 0)

    def kv_map(t):
        bi, g, _, _ = q_map(t)
        return (bi, g, 0, 0)

    def step_map(j):
        bi, g, i, _ = q_map(n_blocks * j)
        return (bi, g, jnp.right_shift(i, sh_nb), 0)

    def next_first(j):
        return jnp.minimum(n_blocks * (j + 1), n_total - 1)

    side_in_specs, side_out_specs, side_shapes = _side_specs(side, n_steps, lambda j: j)
    ada_args, ada_in_specs, ada_out_specs, ada_shapes = [], [], [], []
    if ada is not None:
        c, w_ada, b_ada, first_col = ada
        n_cols = w_ada.shape[1] - first_col
        tn = n_cols // n_steps
        skip = first_col // tn
        ada_args = [c, w_ada, b_ada]
        ada_in_specs = [
            pl.BlockSpec(c.shape, lambda j: (0, 0)),
            pl.BlockSpec((w_ada.shape[0], tn), lambda j: (0, skip + j)),
            pl.BlockSpec((1, tn), lambda j: (0, skip + j)),
        ]
        ada_out_specs = [pl.BlockSpec((c.shape[0], tn), lambda j: (0, j))]
        ada_shapes = [jax.ShapeDtypeStruct((c.shape[0], n_cols), F32)]
    return pl.pallas_call(
        functools.partial(_attn_kernel, n_heads=n_heads, dk=dk, shared_kv=shared_kv,
                          tq=tq, n_blocks=n_blocks, n_side=len(side),
                          with_ada=ada is not None),
        grid=(n_steps,),
        in_specs=[
            pl.BlockSpec((1, 1, n_blocks * tq, n_heads * dk), step_map),
            pl.BlockSpec((1, 1, tq, n_heads * dk), lambda j: q_map(next_first(j))),
            pl.BlockSpec((1, 1, s, n_kv * dk), lambda j: kv_map(n_blocks * j)),
            pl.BlockSpec((1, 1, s, n_kv * dk), lambda j: kv_map(next_first(j))),
            pl.BlockSpec((1, 1, s, n_kv * V_DIM), lambda j: kv_map(n_blocks * j)),
        ] + side_in_specs + ada_in_specs,
        out_specs=[pl.BlockSpec((1, 1, n_blocks * tq, n_heads * V_DIM), step_map)]
        + side_out_specs + ada_out_specs,
        out_shape=[jax.ShapeDtypeStruct((b, n_groups, s, n_heads * V_DIM), BF16)]
        + side_shapes + ada_shapes,
        scratch_shapes=[pltpu.VMEM((2, n_heads, tq, s), F32),
                        pltpu.VMEM((2, n_heads, tq, 1), F32)],
        compiler_params=pltpu.CompilerParams(
            dimension_semantics=("arbitrary",), vmem_limit_bytes=vmem_limit),
        name=name,
    )(q, q, k, k, v, *[w for w, _ in side], *ada_args)


def _outproj_kernel(oa_ref, ob_ref, x_ref, mod_ref, ga_ref, gb_ref, w_ref, gpost_ref, *rest,
                    n_side):
    rest = list(rest)
    side_in = [rest.pop(0) for _ in range(n_side)]
    o_ref = rest.pop(0)
    side_out = [rest.pop(0) for _ in range(n_side)]
    n_scr, y_scr = rest
    n_rows = n_scr.shape[0]

    def store_o(rows, val):
        o_ref[0, rows, :] = val

    def head_rows(ref, rows):
        return jnp.concatenate(
            [ref[0, g, rows, :] for g in range(ref.shape[1])], axis=-1).astype(F32)

    for rows in _row_chunks(n_rows):
        n_scr[rows, 0:MIX_A] = _rms(head_rows(oa_ref, rows), ga_ref[...]).astype(BF16)
        n_scr[rows, MIX_A:] = _rms(head_rows(ob_ref, rows), gb_ref[...]).astype(BF16)
    y_scr[...] = jnp.dot(n_scr[...], w_ref[...], preferred_element_type=F32)
    _side_cast(side_in, side_out, 0, 1)
    _gated_residual_rows(lambda rows: x_ref[0, rows, :], lambda rows: y_scr[rows, :],
                         store_o, n_rows, gpost_ref[...],
                         _mod_row(mod_ref, pl.program_id(0), MOD_GT_A))


def _outproj(o_a, o_b, x, mod, g_a, g_b, w_out, g_post, side, tm):
    b, s, d = x.shape
    n_i = s // tm
    row = lambda bi, i: (bi, i, 0)
    const = lambda bi, i: (0, 0)
    side_in_specs, side_out_specs, side_shapes = _side_specs(
        side, b * n_i, lambda bi, i: bi * n_i + i)
    return pl.pallas_call(
        functools.partial(_outproj_kernel, n_side=len(side)),
        grid=(b, n_i),
        in_specs=[
            pl.BlockSpec((1, N_GROUPS, tm, MIX_A // N_GROUPS), lambda bi, i: (bi, 0, i, 0)),
            pl.BlockSpec((1, N_GROUPS, tm, MIX_B // N_GROUPS), lambda bi, i: (bi, 0, i, 0)),
            pl.BlockSpec((1, tm, d), row),
            pl.BlockSpec(mod.shape, const),
            pl.BlockSpec((1, MIX_A), const),
            pl.BlockSpec((1, MIX_B), const),
            pl.BlockSpec((MIX_A + MIX_B, d), const, pipeline_mode=pl.Buffered(1)),
            pl.BlockSpec((1, d), const),
        ] + side_in_specs,
        out_specs=[pl.BlockSpec((1, tm, d), row)] + side_out_specs,
        out_shape=[jax.ShapeDtypeStruct((b, s, d), F32)] + side_shapes,
        scratch_shapes=[pltpu.VMEM((tm, MIX_A + MIX_B), BF16), pltpu.VMEM((tm, d), F32)],
        compiler_params=pltpu.CompilerParams(
            dimension_semantics=("arbitrary", "arbitrary"),
            vmem_limit_bytes=VMEM_LIMIT_BIG if side else VMEM_LIMIT),
        name="out_proj",
    )(o_a, o_b, x, mod, g_a, g_b, w_out, g_post, *[w for w, _ in side])


def _mlp_kernel(x_ref, mod_ref, gpre_ref, w1_ref, w2_ref, gpost_ref,
                o_ref, h_scr, *, n_split):
    bi, f = pl.program_id(0), pl.program_id(2)
    last = pl.num_programs(2) - 1
    n_rows = h_scr.shape[0]
    parts = [slice(r, r + n_rows // n_split) for r in range(0, n_rows, n_rows // n_split)]

    def store_h(rows, val):
        h_scr[rows, :] = val

    def store_o(rows, val):
        o_ref[0, rows, :] = val

    def ffn(rows, assign_first=False):
        n_sub, _, sub = w1_ref.shape
        for c in range(n_sub):
            u = jnp.dot(h_scr[rows, :], w1_ref[c], preferred_element_type=F32)
            u = jnp.maximum(u, 0.0)
            y = jnp.dot((u * u).astype(BF16), w2_ref[c * sub:(c + 1) * sub, :],
                        preferred_element_type=F32)
            if assign_first and c == 0:
                o_ref[0, rows, :] = y
            else:
                o_ref[0, rows, :] += y

    @pl.when(f == 0)
    def _():
        for part in parts:
            _norm_mod_rows(lambda rows: x_ref[0, rows, :], store_h, part,
                           gpre_ref[...], _mod_row(mod_ref, bi, MOD_SC_M),
                           _mod_row(mod_ref, bi, MOD_SH_M))
            ffn(part, assign_first=True)

    @pl.when((f > 0) & (f < last))
    def _():
        ffn(slice(0, n_rows))

    @pl.when(f == last)
    def _():
        for part in parts:
            ffn(part)
            _gated_residual_rows(lambda rows: x_ref[0, rows, :], lambda rows: o_ref[0, rows, :],
                                 store_o, part, gpost_ref[...],
                                 _mod_row(mod_ref, bi, MOD_GT_M))


def _mlp(x, mod, g_pre, w1, w2, g_post, tm):
    b, s, d = x.shape
    n_sub = MLP_TF // MLP_SUB
    n_f, tf = w1.shape[0] // n_sub, MLP_TF
    row = lambda bi, i, f: (bi, i, 0)
    const = lambda bi, i, f: (0, 0)
    return pl.pallas_call(
        functools.partial(_mlp_kernel, n_split=2),
        grid=(b, s // tm, n_f),
        in_specs=[
            pl.BlockSpec((1, tm, d), row),
            pl.BlockSpec(mod.shape, const),
            pl.BlockSpec((1, d), const),
            pl.BlockSpec((n_sub, d, MLP_SUB), lambda bi, i, f: (f, 0, 0)),
            pl.BlockSpec((tf, d), lambda bi, i, f: (f, 0)),
            pl.BlockSpec((1, d), const),
        ],
        out_specs=pl.BlockSpec((1, tm, d), row),
        out_shape=jax.ShapeDtypeStruct((b, s, d), F32),
        scratch_shapes=[pltpu.VMEM((tm, d), BF16)],
        compiler_params=pltpu.CompilerParams(
            dimension_semantics=("arbitrary", "arbitrary", "arbitrary"),
            vmem_limit_bytes=VMEM_LIMIT),
        name="mlp",
    )(x, mod, g_pre, w1, w2, g_post)


def kernel(x, c, w_ada, b_ada, g_pre_attn, w_in, g_q_a, g_k_a, g_ckv, w_kv_b, g_out_a,
           g_out_b, w_out, g_post_attn, g_pre_mlp, w_mlp_in, w_mlp_out, g_post_mlp):
    b, s, d = x.shape
    depth = w_ada.shape[0]
    tabs = _rope_tables(s)
    c_pad = jnp.pad(c, ((0, 8 - b), (0, 0)))

    for l in range(depth):
        b_ada_l = b_ada[l][None, :]
        n_early = N_MOD_EARLY * d
        mod_early = _ada(c_pad, w_ada[l], b_ada_l, n_early)

        q_a, k_a, v_a, q_b, k_b, v_b = _inproj(
            x, mod_early, g_pre_attn[l][None, :], w_in[l].T, w_kv_b[l],
            g_q_a[l][None, :], g_k_a[l][None, :],
            g_ckv[l][None, :], tabs, tm=512)

        o_a, w1_bf, w2_bf = _attention(
            q_a, k_a, v_a, [(w_mlp_in[l], MLP_SUB), (w_mlp_out[l], None)],
            n_heads=G_A,
            dk=HEAD_DIM, shared_kv=True, tq=256, n_blocks=4, name="attn_gqa")
        o_b, wo_bf, mod_late = _attention(
            q_b, k_b, v_b, [(w_out[l], None)], n_heads=HB // N_GROUPS,
            dk=QK_B_PAD, shared_kv=False, tq=256, n_blocks=4, name="attn_mla",
            ada=(c_pad, w_ada[l], b_ada_l, n_early), vmem_limit=VMEM_LIMIT_BIG)

        x, = _outproj(
            o_a, o_b, x, mod_late, g_out_a[l][None, :], g_out_b[l][None, :], wo_bf,
            g_post_attn[l][None, :], [], tm=512)

        x = _mlp(x, mod_late, g_pre_mlp[l][None, :], w1_bf, w2_bf,
                 g_post_mlp[l][None, :], tm=512)
    return x
```

```python
import functools
import math

import numpy as np
import jax
import jax.numpy as jnp
from jax import lax
from jax.experimental import pallas as pl
from jax.experimental.pallas import tpu as pltpu

D_MODEL = 2048
GRID_W = 64
ROPE_THETA = 10000.0
EPS = 1e-6

HEAD_DIM = 128
HA = 8
HKV = 2
G_A = HA // HKV

HB = 8
N_GROUPS = 2
QK_NOPE = 128
QK_ROPE = 64
V_DIM = 128
KV_RANK = 512
QK_B = QK_NOPE + QK_ROPE
QK_B_PAD = 256

W_QA = HA * HEAD_DIM
W_KA = HKV * HEAD_DIM
W_VA = HKV * HEAD_DIM
W_QB = HB * QK_B
MIX_A = HA * HEAD_DIM
MIX_B = HB * V_DIM
D_FF = 4 * D_MODEL
N_MOD = 6
N_MOD_EARLY = 2
MOD_SH_A, MOD_SC_A = range(N_MOD_EARLY)
MOD_GT_A, MOD_SH_M, MOD_SC_M, MOD_GT_M = range(N_MOD - N_MOD_EARLY)

LOG2_E = math.log2(math.e)
LANES = 128
BF16_ROWS = 16
ROW_CHUNK = BF16_ROWS

ADA_TN = 512
IN_PROJ_TM = 512
W_STAGE_ROWS = 304
ATTN_TQ = 256
ATTN_BLOCKS = 4
OUT_PROJ_TM = 512
MLP_TM = 512
MLP_TF = 2048
MLP_SUB = 1024
VMEM_LIMIT = 56 * 1024 * 1024
VMEM_LIMIT_BIG = 60 * 1024 * 1024

F32 = jnp.float32
BF16 = jnp.bfloat16


def _rope_tables(seq_len):
    parts = _rope_table(seq_len, HEAD_DIM) + _rope_table(seq_len, QK_ROPE)
    return jnp.asarray(np.concatenate(parts, axis=1), F32)


def _rope_table(seq_len, dim):
    pos = np.arange(seq_len)
    row = (pos // GRID_W).astype(np.float64)
    col = (pos % GRID_W).astype(np.float64)
    half = dim // 2
    inv = ROPE_THETA ** (-np.arange(0, half, 2, dtype=np.float64) / half)
    ang_r = row[:, None] * inv[None, :]
    ang_c = col[:, None] * inv[None, :]
    ang = np.concatenate([ang_r, ang_r, ang_c, ang_c], axis=-1)
    cos, sin = np.cos(ang), np.sin(ang)
    quarter = dim // 4
    sign = np.where((np.arange(dim) % half) < quarter, -1.0, 1.0)
    sin = sin * sign[None, :]
    reps = LANES // dim
    return [np.tile(cos, (1, reps)), np.tile(sin, (1, reps))]


def _largest_divisor(n, at_most, multiple_of):
    return max(k for k in range(multiple_of, at_most + 1, multiple_of) if n % k == 0)


def _log2(n):
    assert n > 0 and n & (n - 1) == 0, n
    return n.bit_length() - 1


def _mod_row(mod_ref, bi, idx):
    return mod_ref[pl.ds(bi, 1), idx * D_MODEL:(idx + 1) * D_MODEL]


def _unit_rms(x):
    ms = jnp.mean(x * x, axis=-1, keepdims=True)
    return x * lax.rsqrt(ms + EPS)


def _rms(x, g):
    return _unit_rms(x) * g


def _row_chunks(span):
    span = span if isinstance(span, slice) else slice(0, span)
    return [slice(r, r + ROW_CHUNK) for r in range(span.start, span.stop, ROW_CHUNK)]


def _norm_mod_rows(load, store, span, g, sc, sh):
    gain = g * (1.0 + sc)
    for rows in _row_chunks(span):
        store(rows, (_unit_rms(load(rows)) * gain + sh).astype(BF16))


def _gated_residual_rows(load_x, load_y, store, span, g, gt):
    gain = gt * g
    for rows in _row_chunks(span):
        store(rows, load_x(rows) + _unit_rms(load_y(rows)) * gain)


def _dot_nt(a, w_rows):
    return lax.dot_general(a, w_rows, (((1,), (1,)), ((), ())), preferred_element_type=F32)


def _rope(x, cos, sin_signed, quarter):
    lane = lax.broadcasted_iota(jnp.int32, x.shape, 1)
    take_up = (lane % (2 * quarter)) < quarter
    up = pltpu.roll(x, LANES - quarter, axis=1)
    down = pltpu.roll(x, quarter, axis=1)
    rot = jnp.where(take_up, up, down)
    return x * cos + rot * sin_signed


def _ada_columns(c, w, b):
    c_act = (c * jax.nn.sigmoid(c)).astype(BF16)
    return jnp.dot(c_act, w.astype(BF16), preferred_element_type=F32) + b


def _ada_kernel(c_ref, w_ref, b_ref, o_ref):
    o_ref[...] = _ada_columns(c_ref[...], w_ref[...], b_ref[...])


def _ada(c_pad, w_ada, b_ada, n):
    m, d = c_pad.shape
    tn = ADA_TN
    return pl.pallas_call(
        _ada_kernel,
        grid=(n // tn,),
        in_specs=[
            pl.BlockSpec((m, d), lambda j: (0, 0)),
            pl.BlockSpec((d, tn), lambda j: (0, j)),
            pl.BlockSpec((1, tn), lambda j: (0, j)),
        ],
        out_specs=pl.BlockSpec((m, tn), lambda j: (0, j)),
        out_shape=jax.ShapeDtypeStruct((m, n), F32),
        compiler_params=pltpu.CompilerParams(
            dimension_semantics=("arbitrary",), vmem_limit_bytes=VMEM_LIMIT),
        name="ada_mod",
    )(c_pad, w_ada, b_ada)


def _store_cols(ref, col0, val):
    g, c = divmod(col0, ref.shape[3])
    ref[0, g, :, c:c + val.shape[1]] = val


def _load_as_bf16(chunks, stage, sem):
    def copy(c):
        src, row0, n, _ = chunks[c]
        slot = c % 2
        return pltpu.make_async_copy(
            src.at[row0:row0 + n, :], stage.at[slot, 0:n, :], sem.at[slot])

    copy(0).start()
    for c, (_, row0, n, dst) in enumerate(chunks):
        if c + 1 < len(chunks):
            copy(c + 1).start()
        copy(c).wait()
        dst[row0:row0 + n, :] = stage[c % 2, 0:n, :].astype(BF16)


def _inproj_kernel(x_ref, mod_ref, gpre_ref, w_hbm, wkv_hbm, gq_ref, gk_ref,
                   gckv_ref, tab_ref,
                   qa_ref, ka_ref, va_ref, qb_ref, kb_ref, vb_ref,
                   h_scr, w_ref, wkv_ref, stage, sem):
    tm = h_scr.shape[0]
    bi = pl.program_id(0)

    @pl.when((bi == 0) & (pl.program_id(1) == 0))
    def _():
        chunks = []
        for src, dst in ((w_hbm, w_ref), (wkv_hbm, wkv_ref)):
            n = _largest_divisor(dst.shape[0], stage.shape[1], BF16_ROWS)
            chunks += [(src, r, n, dst) for r in range(0, dst.shape[0], n)]
        _load_as_bf16(chunks, stage, sem)

    def store_h(rows, val):
        h_scr[rows, :] = val

    _norm_mod_rows(lambda rows: x_ref[0, rows, :], store_h, tm, gpre_ref[...],
                   _mod_row(mod_ref, bi, MOD_SC_A), _mod_row(mod_ref, bi, MOD_SH_A))
    hb = h_scr[...]
    cosa, sina, cosb, sinb = (tab_ref[:, i * LANES:(i + 1) * LANES] for i in range(4))
    scale_a = LOG2_E / math.sqrt(HEAD_DIM)
    scale_b = LOG2_E / math.sqrt(QK_B)

    qa = _dot_nt(hb, w_ref[0:W_QA, :])
    gq = gq_ref[...] * scale_a
    for hd in range(HA):
        sl = slice(hd * HEAD_DIM, (hd + 1) * HEAD_DIM)
        q = _rope(_rms(qa[:, sl], gq), cosa, sina, HEAD_DIM // 4)
        _store_cols(qa_ref, hd * HEAD_DIM, q.astype(BF16))

    kva = _dot_nt(hb, w_ref[W_QA:W_QA + W_KA + W_VA, :])
    for hd in range(HKV):
        sl = slice(hd * HEAD_DIM, (hd + 1) * HEAD_DIM)
        k = _rope(_rms(kva[:, sl], gk_ref[...]), cosa, sina, HEAD_DIM // 4)
        _store_cols(ka_ref, hd * HEAD_DIM, k.astype(BF16))
        _store_cols(va_ref, hd * HEAD_DIM,
                    kva[:, W_KA + hd * HEAD_DIM:W_KA + (hd + 1) * HEAD_DIM].astype(BF16))

    o_qb = W_QA + W_KA + W_VA
    qb = _dot_nt(hb, w_ref[o_qb:o_qb + W_QB, :])
    lane = lax.broadcasted_iota(jnp.int32, (tm, LANES), 1)
    low = lane < QK_ROPE
    swap = lambda t: pltpu.roll(t, QK_ROPE, axis=1)
    for pair in range(HB // 2):
        t0, t1, t2 = (qb[:, (3 * pair + i) * LANES:(3 * pair + i + 1) * LANES] for i in range(3))
        r1 = _rope(t1, cosb, sinb, QK_ROPE // 4)
        r2 = _rope(t2, cosb, sinb, QK_ROPE // 4)
        heads = (
            (2 * pair, t0, jnp.where(low, r1, 0.0)),
            (2 * pair + 1, jnp.where(low, swap(t1), swap(t2)), jnp.where(low, swap(r2), 0.0)),
        )
        for hd, nope, pe in heads:
            base = hd * QK_B_PAD
            _store_cols(qb_ref, base, (nope * scale_b).astype(BF16))
            _store_cols(qb_ref, base + QK_NOPE, (pe * scale_b).astype(BF16))

    o_ckv = o_qb + W_QB
    w_rest = jnp.concatenate(
        [w_ref[o_ckv:, :], jnp.zeros((LANES - QK_ROPE, w_ref.shape[1]), BF16)], axis=0)
    rest = _dot_nt(hb, w_rest)
    ckv = _rms(rest[:, :KV_RANK], gckv_ref[...]).astype(BF16)
    kv = jnp.dot(ckv, wkv_ref[...], preferred_element_type=F32)
    kpe = _rope(rest[:, KV_RANK:], cosb, sinb, QK_ROPE // 4).astype(BF16)
    for hd in range(HB):
        base = hd * (QK_NOPE + V_DIM)
        _store_cols(kb_ref, hd * QK_B_PAD, kv[:, base:base + QK_NOPE].astype(BF16))
        _store_cols(kb_ref, hd * QK_B_PAD + QK_NOPE, kpe)
        _store_cols(vb_ref, hd * V_DIM,
                    kv[:, base + QK_NOPE:base + QK_NOPE + V_DIM].astype(BF16))


def _inproj(x, mod, g_pre, w_in_p, w_kv, g_q, g_k, g_ckv, tabs):
    b, s, d = x.shape
    tm = IN_PROJ_TM
    n_in = w_in_p.shape[0]
    row = lambda bi, i: (bi, i, 0)
    const = lambda bi, i: (0, 0)
    tab = lambda bi, i: (i, 0)
    vec = lambda n: pl.BlockSpec((1, n), const)
    out_widths = (MIX_A, W_KA, W_VA, HB * QK_B_PAD, HB * QK_B_PAD, MIX_B)
    return pl.pallas_call(
        _inproj_kernel,
        grid=(b, s // tm),
        in_specs=[
            pl.BlockSpec((1, tm, d), row),
            pl.BlockSpec(mod.shape, const),
            vec(d),
            pl.BlockSpec(memory_space=pl.ANY),
            pl.BlockSpec(memory_space=pl.ANY),
            vec(HEAD_DIM), vec(HEAD_DIM), vec(KV_RANK),
            pl.BlockSpec((tm, 4 * LANES), tab),
        ],
        out_specs=[pl.BlockSpec((1, N_GROUPS, tm, w // N_GROUPS), lambda bi, i: (bi, 0, i, 0))
                   for w in out_widths],
        out_shape=[jax.ShapeDtypeStruct((b, N_GROUPS, s, w // N_GROUPS), BF16)
                   for w in out_widths],
        scratch_shapes=[
            pltpu.VMEM((tm, d), BF16),
            pltpu.VMEM((n_in, d), BF16),
            pltpu.VMEM(w_kv.shape, BF16),
            pltpu.VMEM((2, W_STAGE_ROWS, d), F32),
            pltpu.SemaphoreType.DMA((2,)),
        ],
        compiler_params=pltpu.CompilerParams(
            dimension_semantics=("arbitrary", "arbitrary"),
            vmem_limit_bytes=VMEM_LIMIT),
        name="in_proj",
    )(x, mod, g_pre, w_in_p, w_kv, g_q, g_k, g_ckv, tabs)


def _side_specs(side, n_steps, step_of):
    in_specs, out_specs, shapes = [], [], []
    for w, col_block in side:
        rows, cols = w.shape[0] // n_steps, w.shape[1]
        in_specs.append(pl.BlockSpec((rows, cols), lambda *g: (step_of(*g), 0)))
        if col_block is None:
            out_specs.append(pl.BlockSpec((rows, cols), lambda *g: (step_of(*g), 0)))
            shapes.append(jax.ShapeDtypeStruct(w.shape, BF16))
        else:
            n_cb = cols // col_block
            out_specs.append(
                pl.BlockSpec((n_cb, rows, col_block), lambda *g: (0, step_of(*g), 0)))
            shapes.append(jax.ShapeDtypeStruct((n_cb, w.shape[0], col_block), BF16))
    return in_specs, out_specs, shapes


def _side_cast(side_in, side_out, part, n_parts):
    for w_ref, wb_ref in zip(side_in, side_out):
        n = w_ref.shape[0] // n_parts
        rows = slice(part * n, (part + 1) * n)
        if len(wb_ref.shape) == 2:
            wb_ref[rows, :] = w_ref[rows, :].astype(BF16)
        else:
            cb = wb_ref.shape[2]
            for f in range(wb_ref.shape[0]):
                wb_ref[f, rows, :] = w_ref[rows, f * cb:(f + 1) * cb].astype(BF16)


def _attn_kernel(*refs, n_heads, dk, shared_kv, tq, n_blocks, n_side, with_ada):
    q_ref, qn_ref, k_ref, kn_ref, v_ref = refs[:5]
    rest = list(refs[5:])
    side_in = [rest.pop(0) for _ in range(n_side)]
    ada_in = [rest.pop(0) for _ in range(3 if with_ada else 0)]
    o_ref = rest.pop(0)
    side_out = [rest.pop(0) for _ in range(n_side)]
    ada_out = [rest.pop(0) for _ in range(1 if with_ada else 0)]
    s_scr, m_scr = rest

    def scores(i, slot):
        nxt = i == n_blocks
        keys_ref = kn_ref if nxt else k_ref
        for hd in range(n_heads):
            kv_hd = 0 if shared_kv else hd
            k = keys_ref[0, 0, :, kv_hd * dk:(kv_hd + 1) * dk]
            if nxt:
                q = qn_ref[0, 0, :, hd * dk:(hd + 1) * dk]
            else:
                q = q_ref[0, 0, i * tq:(i + 1) * tq, hd * dk:(hd + 1) * dk]
            s = lax.dot_general(q, k, (((1,), (1,)), ((), ())), preferred_element_type=F32)
            s_scr[slot, hd] = s
            m_scr[slot, hd] = jnp.max(s, axis=-1, keepdims=True)

    def softmax_pv(slot, row0):
        for hd in range(n_heads):
            kv_hd = 0 if shared_kv else hd
            v = v_ref[0, 0, :, kv_hd * V_DIM:(kv_hd + 1) * V_DIM]
            p = jnp.exp2(s_scr[slot, hd] - m_scr[slot, hd]).astype(BF16)
            v_ext = jnp.concatenate([v, jnp.ones_like(v)], axis=1)
            o_ext = jnp.dot(p, v_ext, preferred_element_type=F32)
            o = o_ext[:, :V_DIM] / o_ext[:, V_DIM:]
            o_ref[0, 0, row0:row0 + tq, hd * V_DIM:(hd + 1) * V_DIM] = o.astype(BF16)

    @pl.when(pl.program_id(0) == 0)
    def _():
        scores(0, 0)

    for i in range(n_blocks):
        scores(i + 1, (i + 1) % 2)
        _side_cast(side_in, side_out, i, n_blocks)
        if with_ada and i == 0:
            c_ref, wada_ref, bada_ref = ada_in
            ada_out[0][...] = _ada_columns(c_ref[...], wada_ref[...], bada_ref[...])
        softmax_pv(i % 2, i * tq)


def _attention(q, k, v, side, *, n_heads, dk, shared_kv, name, ada=None,
               vmem_limit=VMEM_LIMIT):
    b, n_groups, s, _ = q.shape
    tq, n_blocks = ATTN_TQ, ATTN_BLOCKS
    n_kv = 1 if shared_kv else n_heads
    n_blk = s // tq
    assert n_blocks % 2 == 0 and n_blk % n_blocks == 0
    n_total = b * n_groups * n_blk
    n_steps = n_total // n_blocks

    sh_blk, sh_grp, sh_nb = (_log2(n) for n in (n_blk, n_groups, n_blocks))

    def q_map(t):
        grp = jnp.right_shift(t, sh_blk)
        return (jnp.right_shift(grp, sh_grp), grp & (n_groups - 1), t & (n_blk - 1), 0)

    def kv_map(t):
        bi, g, _, _ = q_map(t)
        return (bi, g, 0, 0)

    def step_map(j):
        bi, g, i, _ = q_map(n_blocks * j)
        return (bi, g, jnp.right_shift(i, sh_nb), 0)

    def next_first(j):
        return jnp.minimum(n_blocks * (j + 1), n_total - 1)

    side_in_specs, side_out_specs, side_shapes = _side_specs(side, n_steps, lambda j: j)
    ada_args, ada_in_specs, ada_out_specs, ada_shapes = [], [], [], []
    if ada is not None:
        c, w_ada, b_ada, first_col = ada
        n_cols = w_ada.shape[1] - first_col
        tn = n_cols // n_steps
        skip = first_col // tn
        ada_args = [c, w_ada, b_ada]
        ada_in_specs = [
            pl.BlockSpec(c.shape, lambda j: (0, 0)),
            pl.BlockSpec((w_ada.shape[0], tn), lambda j: (0, skip + j)),
            pl.BlockSpec((1, tn), lambda j: (0, skip + j)),
        ]
        ada_out_specs = [pl.BlockSpec((c.shape[0], tn), lambda j: (0, j))]
        ada_shapes = [jax.ShapeDtypeStruct((c.shape[0], n_cols), F32)]
    return pl.pallas_call(
        functools.partial(_attn_kernel, n_heads=n_heads, dk=dk, shared_kv=shared_kv,
                          tq=tq, n_blocks=n_blocks, n_side=len(side),
                          with_ada=ada is not None),
        grid=(n_steps,),
        in_specs=[
            pl.BlockSpec((1, 1, n_blocks * tq, n_heads * dk), step_map),
            pl.BlockSpec((1, 1, tq, n_heads * dk), lambda j: q_map(next_first(j))),
            pl.BlockSpec((1, 1, s, n_kv * dk), lambda j: kv_map(n_blocks * j)),
            pl.BlockSpec((1, 1, s, n_kv * dk), lambda j: kv_map(next_first(j))),
            pl.BlockSpec((1, 1, s, n_kv * V_DIM), lambda j: kv_map(n_blocks * j)),
        ] + side_in_specs + ada_in_specs,
        out_specs=[pl.BlockSpec((1, 1, n_blocks * tq, n_heads * V_DIM), step_map)]
        + side_out_specs + ada_out_specs,
        out_shape=[jax.ShapeDtypeStruct((b, n_groups, s, n_heads * V_DIM), BF16)]
        + side_shapes + ada_shapes,
        scratch_shapes=[pltpu.VMEM((2, n_heads, tq, s), F32),
                        pltpu.VMEM((2, n_heads, tq, 1), F32)],
        compiler_params=pltpu.CompilerParams(
            dimension_semantics=("arbitrary",), vmem_limit_bytes=vmem_limit),
        name=name,
    )(q, q, k, k, v, *[w for w, _ in side], *ada_args)


def _outproj_kernel(oa_ref, ob_ref, x_ref, mod_ref, ga_ref, gb_ref, w_ref, gpost_ref, o_ref,
                    n_scr, y_scr):
    n_rows = n_scr.shape[0]

    def store_o(rows, val):
        o_ref[0, rows, :] = val

    def head_rows(ref, rows):
        return jnp.concatenate(
            [ref[0, g, rows, :] for g in range(ref.shape[1])], axis=-1).astype(F32)

    for rows in _row_chunks(n_rows):
        n_scr[rows, 0:MIX_A] = _rms(head_rows(oa_ref, rows), ga_ref[...]).astype(BF16)
        n_scr[rows, MIX_A:] = _rms(head_rows(ob_ref, rows), gb_ref[...]).astype(BF16)
    y_scr[...] = jnp.dot(n_scr[...], w_ref[...], preferred_element_type=F32)
    _gated_residual_rows(lambda rows: x_ref[0, rows, :], lambda rows: y_scr[rows, :],
                         store_o, n_rows, gpost_ref[...],
                         _mod_row(mod_ref, pl.program_id(0), MOD_GT_A))


def _outproj(o_a, o_b, x, mod, g_a, g_b, w_out, g_post):
    b, s, d = x.shape
    tm = OUT_PROJ_TM
    row = lambda bi, i: (bi, i, 0)
    const = lambda bi, i: (0, 0)
    return pl.pallas_call(
        _outproj_kernel,
        grid=(b, s // tm),
        in_specs=[
            pl.BlockSpec((1, N_GROUPS, tm, MIX_A // N_GROUPS), lambda bi, i: (bi, 0, i, 0)),
            pl.BlockSpec((1, N_GROUPS, tm, MIX_B // N_GROUPS), lambda bi, i: (bi, 0, i, 0)),
            pl.BlockSpec((1, tm, d), row),
            pl.BlockSpec(mod.shape, const),
            pl.BlockSpec((1, MIX_A), const),
            pl.BlockSpec((1, MIX_B), const),
            pl.BlockSpec((MIX_A + MIX_B, d), const, pipeline_mode=pl.Buffered(1)),
            pl.BlockSpec((1, d), const),
        ],
        out_specs=pl.BlockSpec((1, tm, d), row),
        out_shape=jax.ShapeDtypeStruct((b, s, d), F32),
        scratch_shapes=[pltpu.VMEM((tm, MIX_A + MIX_B), BF16), pltpu.VMEM((tm, d), F32)],
        compiler_params=pltpu.CompilerParams(
            dimension_semantics=("arbitrary", "arbitrary"),
            vmem_limit_bytes=VMEM_LIMIT),
        name="out_proj",
    )(o_a, o_b, x, mod, g_a, g_b, w_out, g_post)


def _mlp_kernel(x_ref, mod_ref, gpre_ref, w1_ref, w2_ref, gpost_ref,
                o_ref, h_scr, *, n_split):
    bi, f = pl.program_id(0), pl.program_id(2)
    last = pl.num_programs(2) - 1
    n_rows = h_scr.shape[0]
    parts = [slice(r, r + n_rows // n_split) for r in range(0, n_rows, n_rows // n_split)]

    def store_h(rows, val):
        h_scr[rows, :] = val

    def store_o(rows, val):
        o_ref[0, rows, :] = val

    def ffn(rows, assign_first=False):
        n_sub, _, sub = w1_ref.shape
        for c in range(n_sub):
            u = jnp.dot(h_scr[rows, :], w1_ref[c], preferred_element_type=F32)
            u = jnp.maximum(u, 0.0)
            y = jnp.dot((u * u).astype(BF16), w2_ref[c * sub:(c + 1) * sub, :],
                        preferred_element_type=F32)
            if assign_first and c == 0:
                o_ref[0, rows, :] = y
            else:
                o_ref[0, rows, :] += y

    @pl.when(f == 0)
    def _():
        for part in parts:
            _norm_mod_rows(lambda rows: x_ref[0, rows, :], store_h, part,
                           gpre_ref[...], _mod_row(mod_ref, bi, MOD_SC_M),
                           _mod_row(mod_ref, bi, MOD_SH_M))
            ffn(part, assign_first=True)

    @pl.when((f > 0) & (f < last))
    def _():
        ffn(slice(0, n_rows))

    @pl.when(f == last)
    def _():
        for part in parts:
            ffn(part)
            _gated_residual_rows(lambda rows: x_ref[0, rows, :], lambda rows: o_ref[0, rows, :],
                                 store_o, part, gpost_ref[...],
                                 _mod_row(mod_ref, bi, MOD_GT_M))


def _mlp(x, mod, g_pre, w1, w2, g_post):
    b, s, d = x.shape
    tm = MLP_TM
    n_sub = MLP_TF // MLP_SUB
    n_f, tf = w1.shape[0] // n_sub, MLP_TF
    row = lambda bi, i, f: (bi, i, 0)
    const = lambda bi, i, f: (0, 0)
    return pl.pallas_call(
        functools.partial(_mlp_kernel, n_split=2),
        grid=(b, s // tm, n_f),
        in_specs=[
            pl.BlockSpec((1, tm, d), row),
            pl.BlockSpec(mod.shape, const),
            pl.BlockSpec((1, d), const),
            pl.BlockSpec((n_sub, d, MLP_SUB), lambda bi, i, f: (f, 0, 0)),
            pl.BlockSpec((tf, d), lambda bi, i, f: (f, 0)),
            pl.BlockSpec((1, d), const),
        ],
        out_specs=pl.BlockSpec((1, tm, d), row),
        out_shape=jax.ShapeDtypeStruct((b, s, d), F32),
        scratch_shapes=[pltpu.VMEM((tm, d), BF16)],
        compiler_params=pltpu.CompilerParams(
            dimension_semantics=("arbitrary", "arbitrary", "arbitrary"),
            vmem_limit_bytes=VMEM_LIMIT),
        name="mlp",
    )(x, mod, g_pre, w1, w2, g_post)


def kernel(x, c, w_ada, b_ada, g_pre_attn, w_in, g_q_a, g_k_a, g_ckv, w_kv_b, g_out_a,
           g_out_b, w_out, g_post_attn, g_pre_mlp, w_mlp_in, w_mlp_out, g_post_mlp):
    b, s, d = x.shape
    depth = w_ada.shape[0]
    tabs = _rope_tables(s)
    c_pad = jnp.pad(c, ((0, 8 - b), (0, 0)))

    for l in range(depth):
        b_ada_l = b_ada[l][None, :]
        n_early = N_MOD_EARLY * d
        mod_early = _ada(c_pad, w_ada[l], b_ada_l, n_early)

        q_a, k_a, v_a, q_b, k_b, v_b = _inproj(
            x, mod_early, g_pre_attn[l][None, :], w_in[l].T, w_kv_b[l],
            g_q_a[l][None, :], g_k_a[l][None, :], g_ckv[l][None, :], tabs)

        o_a, w2_bf, w1_bf = _attention(
            q_a, k_a, v_a, [(w_mlp_out[l], None), (w_mlp_in[l], MLP_SUB)],
            n_heads=G_A, dk=HEAD_DIM, shared_kv=True, name="attn_gqa")
        o_b, wo_bf, mod_late = _attention(
            q_b, k_b, v_b, [(w_out[l], None)], n_heads=HB // N_GROUPS, dk=QK_B_PAD,
            shared_kv=False, name="attn_mla",
            ada=(c_pad, w_ada[l], b_ada_l, n_early), vmem_limit=VMEM_LIMIT_BIG)

        x = _outproj(o_a, o_b, x, mod_late, g_out_a[l][None, :], g_out_b[l][None, :],
                     wo_bf, g_post_attn[l][None, :])

        x = _mlp(x, mod_late, g_pre_mlp[l][None, :], w1_bf, w2_bf, g_post_mlp[l][None, :])
    return x
```

```python
import functools
import math

import numpy as np
import jax
import jax.numpy as jnp
from jax import lax
from jax.experimental import pallas as pl
from jax.experimental.pallas import tpu as pltpu

D_MODEL = 2048
GRID_W = 64
ROPE_THETA = 10000.0
EPS = 1e-6

HEAD_DIM = 128
HA = 8
HKV = 2
G_A = HA // HKV

HB = 8
N_GROUPS = 2
QK_NOPE = 128
QK_ROPE = 64
V_DIM = 128
KV_RANK = 512
QK_B = QK_NOPE + QK_ROPE
QK_B_PAD = 256

W_QA = HA * HEAD_DIM
W_KA = HKV * HEAD_DIM
W_VA = HKV * HEAD_DIM
W_QB = HB * QK_B
MIX_A = HA * HEAD_DIM
MIX_B = HB * V_DIM
N_MOD = 6
N_MOD_EARLY = 2
MOD_SH_A, MOD_SC_A = range(N_MOD_EARLY)
MOD_GT_A, MOD_SH_M, MOD_SC_M, MOD_GT_M = range(N_MOD - N_MOD_EARLY)

LOG2_E = math.log2(math.e)
LANES = 128
BF16_ROWS = 16
ROW_CHUNK = BF16_ROWS

ADA_TN = 512
IN_PROJ_TM = 512
W_STAGE_ROWS = 304
ATTN_TQ = 256
ATTN_BLOCKS = 4
OUT_PROJ_TM = 512
MLP_TM = 512
MLP_TF = 2048
MLP_SUB = 1024
VMEM_LIMIT = 56 * 1024 * 1024
VMEM_LIMIT_BIG = 60 * 1024 * 1024

F32 = jnp.float32
BF16 = jnp.bfloat16


def _rope_tables(seq_len):
    parts = _rope_table(seq_len, HEAD_DIM) + _rope_table(seq_len, QK_ROPE)
    return jnp.asarray(np.concatenate(parts, axis=1), F32)


def _rope_table(seq_len, dim):
    pos = np.arange(seq_len)
    row = (pos // GRID_W).astype(np.float64)
    col = (pos % GRID_W).astype(np.float64)
    half = dim // 2
    inv = ROPE_THETA ** (-np.arange(0, half, 2, dtype=np.float64) / half)
    ang_r = row[:, None] * inv[None, :]
    ang_c = col[:, None] * inv[None, :]
    ang = np.concatenate([ang_r, ang_r, ang_c, ang_c], axis=-1)
    cos, sin = np.cos(ang), np.sin(ang)
    quarter = dim // 4
    sign = np.where((np.arange(dim) % half) < quarter, -1.0, 1.0)
    sin = sin * sign[None, :]
    reps = LANES // dim
    return [np.tile(cos, (1, reps)), np.tile(sin, (1, reps))]


def _largest_divisor(n, at_most, multiple_of):
    return max(k for k in range(multiple_of, at_most + 1, multiple_of) if n % k == 0)


def _log2(n):
    assert n > 0 and n & (n - 1) == 0, n
    return n.bit_length() - 1


def _mod_row(mod_ref, bi, idx):
    return mod_ref[pl.ds(bi, 1), idx * D_MODEL:(idx + 1) * D_MODEL]


def _unit_rms(x):
    ms = jnp.mean(x * x, axis=-1, keepdims=True)
    return x * lax.rsqrt(ms + EPS)


def _rms(x, g):
    return _unit_rms(x) * g


def _row_chunks(span):
    span = span if isinstance(span, slice) else slice(0, span)
    return [slice(r, r + ROW_CHUNK) for r in range(span.start, span.stop, ROW_CHUNK)]


def _norm_mod_rows(load, store, span, g, sc, sh):
    gain = g * (1.0 + sc)
    for rows in _row_chunks(span):
        store(rows, (_unit_rms(load(rows)) * gain + sh).astype(BF16))


def _gated_residual_rows(load_x, load_y, store, span, g, gt):
    gain = gt * g
    for rows in _row_chunks(span):
        store(rows, load_x(rows) + _unit_rms(load_y(rows)) * gain)


def _dot_nt(a, w_rows):
    return lax.dot_general(a, w_rows, (((1,), (1,)), ((), ())), preferred_element_type=F32)


def _rope(x, cos, sin_signed, quarter):
    lane = lax.broadcasted_iota(jnp.int32, x.shape, 1)
    take_up = (lane % (2 * quarter)) < quarter
    up = pltpu.roll(x, LANES - quarter, axis=1)
    down = pltpu.roll(x, quarter, axis=1)
    rot = jnp.where(take_up, up, down)
    return x * cos + rot * sin_signed


def _ada_columns(c, w, b):
    c_act = (c * jax.nn.sigmoid(c)).astype(BF16)
    return jnp.dot(c_act, w.astype(BF16), preferred_element_type=F32) + b


def _ada_kernel(c_ref, w_ref, b_ref, o_ref):
    o_ref[...] = _ada_columns(c_ref[...], w_ref[...], b_ref[...])


def _ada(c, w_ada, b_ada, n):
    m, d = c.shape
    tn = ADA_TN
    return pl.pallas_call(
        _ada_kernel,
        grid=(n // tn,),
        in_specs=[
            pl.BlockSpec((m, d), lambda j: (0, 0)),
            pl.BlockSpec((d, tn), lambda j: (0, j)),
            pl.BlockSpec((1, tn), lambda j: (0, j)),
        ],
        out_specs=pl.BlockSpec((m, tn), lambda j: (0, j)),
        out_shape=jax.ShapeDtypeStruct((m, n), F32),
        compiler_params=pltpu.CompilerParams(
            dimension_semantics=("arbitrary",), vmem_limit_bytes=VMEM_LIMIT),
        name="ada_mod",
    )(c, w_ada, b_ada)


def _store_cols(ref, col0, val):
    g, c = divmod(col0, ref.shape[3])
    ref[0, g, :, c:c + val.shape[1]] = val


def _load_as_bf16(chunks, stage, sem):
    def copy(c):
        src, row0, n, _ = chunks[c]
        slot = c % 2
        return pltpu.make_async_copy(
            src.at[row0:row0 + n, :], stage.at[slot, 0:n, :], sem.at[slot])

    copy(0).start()
    for c, (_, row0, n, dst) in enumerate(chunks):
        if c + 1 < len(chunks):
            copy(c + 1).start()
        copy(c).wait()
        dst[row0:row0 + n, :] = stage[c % 2, 0:n, :].astype(BF16)


def _inproj_kernel(x_ref, mod_ref, gpre_ref, w_hbm, wkv_hbm, gq_ref, gk_ref,
                   gckv_ref, tab_ref,
                   qa_ref, ka_ref, va_ref, qb_ref, kb_ref, vb_ref,
                   h_scr, w_ref, wkv_ref, stage, sem):
    tm = h_scr.shape[0]
    bi = pl.program_id(0)

    @pl.when((bi == 0) & (pl.program_id(1) == 0))
    def _():
        chunks = []
        for src, dst in ((w_hbm, w_ref), (wkv_hbm, wkv_ref)):
            n = _largest_divisor(dst.shape[0], stage.shape[1], BF16_ROWS)
            chunks += [(src, r, n, dst) for r in range(0, dst.shape[0], n)]
        _load_as_bf16(chunks, stage, sem)

    def store_h(rows, val):
        h_scr[rows, :] = val

    _norm_mod_rows(lambda rows: x_ref[0, rows, :], store_h, tm, gpre_ref[...],
                   _mod_row(mod_ref, bi, MOD_SC_A), _mod_row(mod_ref, bi, MOD_SH_A))
    hb = h_scr[...]
    cosa, sina, cosb, sinb = (tab_ref[:, i * LANES:(i + 1) * LANES] for i in range(4))
    scale_a = LOG2_E / math.sqrt(HEAD_DIM)
    scale_b = LOG2_E / math.sqrt(QK_B)

    qa = _dot_nt(hb, w_ref[0:W_QA, :])
    gq = gq_ref[...] * scale_a
    for hd in range(HA):
        sl = slice(hd * HEAD_DIM, (hd + 1) * HEAD_DIM)
        q = _rope(_rms(qa[:, sl], gq), cosa, sina, HEAD_DIM // 4)
        _store_cols(qa_ref, hd * HEAD_DIM, q.astype(BF16))

    kva = _dot_nt(hb, w_ref[W_QA:W_QA + W_KA + W_VA, :])
    for hd in range(HKV):
        sl = slice(hd * HEAD_DIM, (hd + 1) * HEAD_DIM)
        k = _rope(_rms(kva[:, sl], gk_ref[...]), cosa, sina, HEAD_DIM // 4)
        _store_cols(ka_ref, hd * HEAD_DIM, k.astype(BF16))
        _store_cols(va_ref, hd * HEAD_DIM,
                    kva[:, W_KA + hd * HEAD_DIM:W_KA + (hd + 1) * HEAD_DIM].astype(BF16))

    o_qb = W_QA + W_KA + W_VA
    qb = _dot_nt(hb, w_ref[o_qb:o_qb + W_QB, :])
    lane = lax.broadcasted_iota(jnp.int32, (tm, LANES), 1)
    low = lane < QK_ROPE
    swap = lambda t: pltpu.roll(t, QK_ROPE, axis=1)
    for pair in range(HB // 2):
        t0, t1, t2 = (qb[:, (3 * pair + i) * LANES:(3 * pair + i + 1) * LANES] for i in range(3))
        r1 = _rope(t1, cosb, sinb, QK_ROPE // 4)
        r2 = _rope(t2, cosb, sinb, QK_ROPE // 4)
        heads = (
            (2 * pair, t0, jnp.where(low, r1, 0.0)),
            (2 * pair + 1, jnp.where(low, swap(t1), swap(t2)), jnp.where(low, swap(r2), 0.0)),
        )
        for hd, nope, pe in heads:
            base = hd * QK_B_PAD
            _store_cols(qb_ref, base, (nope * scale_b).astype(BF16))
            _store_cols(qb_ref, base + QK_NOPE, (pe * scale_b).astype(BF16))

    o_ckv = o_qb + W_QB
    w_rest = jnp.concatenate(
        [w_ref[o_ckv:, :], jnp.zeros((LANES - QK_ROPE, w_ref.shape[1]), BF16)], axis=0)
    rest = _dot_nt(hb, w_rest)
    ckv = _rms(rest[:, :KV_RANK], gckv_ref[...]).astype(BF16)
    kv = jnp.dot(ckv, wkv_ref[...], preferred_element_type=F32)
    kpe = _rope(rest[:, KV_RANK:], cosb, sinb, QK_ROPE // 4).astype(BF16)
    for hd in range(HB):
        base = hd * (QK_NOPE + V_DIM)
        _store_cols(kb_ref, hd * QK_B_PAD, kv[:, base:base + QK_NOPE].astype(BF16))
        _store_cols(kb_ref, hd * QK_B_PAD + QK_NOPE, kpe)
        _store_cols(vb_ref, hd * V_DIM,
                    kv[:, base + QK_NOPE:base + QK_NOPE + V_DIM].astype(BF16))


def _inproj(x, mod, g_pre, w_in_p, w_kv, g_q, g_k, g_ckv, tabs):
    b, s, d = x.shape
    tm = IN_PROJ_TM
    n_in = w_in_p.shape[0]
    row = lambda bi, i: (bi, i, 0)
    const = lambda bi, i: (0, 0)
    tab = lambda bi, i: (i, 0)
    vec = lambda n: pl.BlockSpec((1, n), const)
    out_widths = (MIX_A, W_KA, W_VA, HB * QK_B_PAD, HB * QK_B_PAD, MIX_B)
    return pl.pallas_call(
        _inproj_kernel,
        grid=(b, s // tm),
        in_specs=[
            pl.BlockSpec((1, tm, d), row),
            pl.BlockSpec(mod.shape, const),
            vec(d),
            pl.BlockSpec(memory_space=pl.ANY),
            pl.BlockSpec(memory_space=pl.ANY),
            vec(HEAD_DIM), vec(HEAD_DIM), vec(KV_RANK),
            pl.BlockSpec((tm, 4 * LANES), tab),
        ],
        out_specs=[pl.BlockSpec((1, N_GROUPS, tm, w // N_GROUPS), lambda bi, i: (bi, 0, i, 0))
                   for w in out_widths],
        out_shape=[jax.ShapeDtypeStruct((b, N_GROUPS, s, w // N_GROUPS), BF16)
                   for w in out_widths],
        scratch_shapes=[
            pltpu.VMEM((tm, d), BF16),
            pltpu.VMEM((n_in, d), BF16),
            pltpu.VMEM(w_kv.shape, BF16),
            pltpu.VMEM((2, W_STAGE_ROWS, d), F32),
            pltpu.SemaphoreType.DMA((2,)),
        ],
        compiler_params=pltpu.CompilerParams(
            dimension_semantics=("arbitrary", "arbitrary"),
            vmem_limit_bytes=VMEM_LIMIT),
        name="in_proj",
    )(x, mod, g_pre, w_in_p, w_kv, g_q, g_k, g_ckv, tabs)


def _side_specs(side, n_steps, step_of):
    in_specs, out_specs, shapes = [], [], []
    for w, col_block in side:
        rows, cols = w.shape[0] // n_steps, w.shape[1]
        in_specs.append(pl.BlockSpec((rows, cols), lambda *g: (step_of(*g), 0)))
        if col_block is None:
            out_specs.append(pl.BlockSpec((rows, cols), lambda *g: (step_of(*g), 0)))
            shapes.append(jax.ShapeDtypeStruct(w.shape, BF16))
        else:
            n_cb = cols // col_block
            out_specs.append(
                pl.BlockSpec((n_cb, rows, col_block), lambda *g: (0, step_of(*g), 0)))
            shapes.append(jax.ShapeDtypeStruct((n_cb, w.shape[0], col_block), BF16))
    return in_specs, out_specs, shapes


def _side_cast(side_in, side_out, part, n_parts):
    for w_ref, wb_ref in zip(side_in, side_out):
        n = w_ref.shape[0] // n_parts
        rows = slice(part * n, (part + 1) * n)
        if len(wb_ref.shape) == 2:
            wb_ref[rows, :] = w_ref[rows, :].astype(BF16)
        else:
            cb = wb_ref.shape[2]
            for f in range(wb_ref.shape[0]):
                wb_ref[f, rows, :] = w_ref[rows, f * cb:(f + 1) * cb].astype(BF16)


def _attn_kernel(*refs, n_heads, dk, shared_kv, tq, n_blocks, n_side, with_ada):
    q_ref, qn_ref, k_ref, kn_ref, v_ref = refs[:5]
    rest = list(refs[5:])
    side_in = [rest.pop(0) for _ in range(n_side)]
    ada_in = [rest.pop(0) for _ in range(3 if with_ada else 0)]
    o_ref = rest.pop(0)
    side_out = [rest.pop(0) for _ in range(n_side)]
    ada_out = [rest.pop(0) for _ in range(1 if with_ada else 0)]
    s_scr, m_scr = rest

    def scores(i, slot):
        nxt = i == n_blocks
        keys_ref = kn_ref if nxt else k_ref
        for hd in range(n_heads):
            kv_hd = 0 if shared_kv else hd
            k = keys_ref[0, 0, :, kv_hd * dk:(kv_hd + 1) * dk]
            if nxt:
                q = qn_ref[0, 0, :, hd * dk:(hd + 1) * dk]
            else:
                q = q_ref[0, 0, i * tq:(i + 1) * tq, hd * dk:(hd + 1) * dk]
            s = lax.dot_general(q, k, (((1,), (1,)), ((), ())), preferred_element_type=F32)
            s_scr[slot, hd] = s
            m_scr[slot, hd] = jnp.max(s, axis=-1, keepdims=True)

    def softmax_pv(slot, row0):
        for hd in range(n_heads):
            kv_hd = 0 if shared_kv else hd
            v = v_ref[0, 0, :, kv_hd * V_DIM:(kv_hd + 1) * V_DIM]
            p = jnp.exp2(s_scr[slot, hd] - m_scr[slot, hd]).astype(BF16)
            v_ext = jnp.concatenate([v, jnp.ones_like(v)], axis=1)
            o_ext = jnp.dot(p, v_ext, preferred_element_type=F32)
            o = o_ext[:, :V_DIM] / o_ext[:, V_DIM:]
            o_ref[0, 0, row0:row0 + tq, hd * V_DIM:(hd + 1) * V_DIM] = o.astype(BF16)

    @pl.when(pl.program_id(0) == 0)
    def _():
        scores(0, 0)

    for i in range(n_blocks):
        scores(i + 1, (i + 1) % 2)
        _side_cast(side_in, side_out, i, n_blocks)
        if with_ada and i == 0:
            c_ref, wada_ref, bada_ref = ada_in
            ada_out[0][...] = _ada_columns(c_ref[...], wada_ref[...], bada_ref[...])
        softmax_pv(i % 2, i * tq)


def _attention(q, k, v, side, *, n_heads, dk, shared_kv, name, ada=None,
               vmem_limit=VMEM_LIMIT):
    b, n_groups, s, _ = q.shape
    tq, n_blocks = ATTN_TQ, ATTN_BLOCKS
    n_kv = 1 if shared_kv else n_heads
    n_blk = s // tq
    assert n_blocks % 2 == 0 and n_blk % n_blocks == 0
    n_total = b * n_groups * n_blk
    n_steps = n_total // n_blocks

    sh_blk, sh_grp, sh_nb = (_log2(n) for n in (n_blk, n_groups, n_blocks))

    def q_map(t):
        grp = jnp.right_shift(t, sh_blk)
        return (jnp.right_shift(grp, sh_grp), grp & (n_groups - 1), t & (n_blk - 1), 0)

    def kv_map(t):
        bi, g, _, _ = q_map(t)
        return (bi, g, 0, 0)

    def step_map(j):
        bi, g, i, _ = q_map(n_blocks * j)
        return (bi, g, jnp.right_shift(i, sh_nb), 0)

    def next_first(j):
        return jnp.minimum(n_blocks * (j + 1), n_total - 1)

    side_in_specs, side_out_specs, side_shapes = _side_specs(side, n_steps, lambda j: j)
    ada_args, ada_in_specs, ada_out_specs, ada_shapes = [], [], [], []
    if ada is not None:
        c, w_ada, b_ada, first_col = ada
        n_cols = w_ada.shape[1] - first_col
        tn = n_cols // n_steps
        skip = first_col // tn
        ada_args = [c, w_ada, b_ada]
        ada_in_specs = [
            pl.BlockSpec(c.shape, lambda j: (0, 0)),
            pl.BlockSpec((w_ada.shape[0], tn), lambda j: (0, skip + j)),
            pl.BlockSpec((1, tn), lambda j: (0, skip + j)),
        ]
        ada_out_specs = [pl.BlockSpec((c.shape[0], tn), lambda j: (0, j))]
        ada_shapes = [jax.ShapeDtypeStruct((c.shape[0], n_cols), F32)]
    return pl.pallas_call(
        functools.partial(_attn_kernel, n_heads=n_heads, dk=dk, shared_kv=shared_kv,
                          tq=tq, n_blocks=n_blocks, n_side=len(side),
                          with_ada=ada is not None),
        grid=(n_steps,),
        in_specs=[
            pl.BlockSpec((1, 1, n_blocks * tq, n_heads * dk), step_map),
            pl.BlockSpec((1, 1, tq, n_heads * dk), lambda j: q_map(next_first(j))),
            pl.BlockSpec((1, 1, s, n_kv * dk), lambda j: kv_map(n_blocks * j)),
            pl.BlockSpec((1, 1, s, n_kv * dk), lambda j: kv_map(next_first(j))),
            pl.BlockSpec((1, 1, s, n_kv * V_DIM), lambda j: kv_map(n_blocks * j)),
        ] + side_in_specs + ada_in_specs,
        out_specs=[pl.BlockSpec((1, 1, n_blocks * tq, n_heads * V_DIM), step_map)]
        + side_out_specs + ada_out_specs,
        out_shape=[jax.ShapeDtypeStruct((b, n_groups, s, n_heads * V_DIM), BF16)]
        + side_shapes + ada_shapes,
        scratch_shapes=[pltpu.VMEM((2, n_heads, tq, s), F32),
                        pltpu.VMEM((2, n_heads, tq, 1), F32)],
        compiler_params=pltpu.CompilerParams(
            dimension_semantics=("arbitrary",), vmem_limit_bytes=vmem_limit),
        name=name,
    )(q, q, k, k, v, *[w for w, _ in side], *ada_args)


def _outproj_kernel(oa_ref, ob_ref, x_ref, mod_ref, ga_ref, gb_ref, w_ref, gpost_ref, o_ref,
                    n_scr, y_scr):
    n_rows = n_scr.shape[0]

    def store_o(rows, val):
        o_ref[0, rows, :] = val

    def head_rows(ref, rows):
        return jnp.concatenate(
            [ref[0, g, rows, :] for g in range(ref.shape[1])], axis=-1).astype(F32)

    for rows in _row_chunks(n_rows):
        n_scr[rows, 0:MIX_A] = _rms(head_rows(oa_ref, rows), ga_ref[...]).astype(BF16)
        n_scr[rows, MIX_A:] = _rms(head_rows(ob_ref, rows), gb_ref[...]).astype(BF16)
    y_scr[...] = jnp.dot(n_scr[...], w_ref[...], preferred_element_type=F32)
    _gated_residual_rows(lambda rows: x_ref[0, rows, :], lambda rows: y_scr[rows, :],
                         store_o, n_rows, gpost_ref[...],
                         _mod_row(mod_ref, pl.program_id(0), MOD_GT_A))


def _outproj(o_a, o_b, x, mod, g_a, g_b, w_out, g_post):
    b, s, d = x.shape
    tm = OUT_PROJ_TM
    row = lambda bi, i: (bi, i, 0)
    const = lambda bi, i: (0, 0)
    return pl.pallas_call(
        _outproj_kernel,
        grid=(b, s // tm),
        in_specs=[
            pl.BlockSpec((1, N_GROUPS, tm, MIX_A // N_GROUPS), lambda bi, i: (bi, 0, i, 0)),
            pl.BlockSpec((1, N_GROUPS, tm, MIX_B // N_GROUPS), lambda bi, i: (bi, 0, i, 0)),
            pl.BlockSpec((1, tm, d), row),
            pl.BlockSpec(mod.shape, const),
            pl.BlockSpec((1, MIX_A), const),
            pl.BlockSpec((1, MIX_B), const),
            pl.BlockSpec((MIX_A + MIX_B, d), const, pipeline_mode=pl.Buffered(1)),
            pl.BlockSpec((1, d), const),
        ],
        out_specs=pl.BlockSpec((1, tm, d), row),
        out_shape=jax.ShapeDtypeStruct((b, s, d), F32),
        scratch_shapes=[pltpu.VMEM((tm, MIX_A + MIX_B), BF16), pltpu.VMEM((tm, d), F32)],
        compiler_params=pltpu.CompilerParams(
            dimension_semantics=("arbitrary", "arbitrary"),
            vmem_limit_bytes=VMEM_LIMIT),
        name="out_proj",
    )(o_a, o_b, x, mod, g_a, g_b, w_out, g_post)


def _mlp_kernel(x_ref, mod_ref, gpre_ref, w1_ref, w2_ref, gpost_ref,
                o_ref, h_scr, *, n_split):
    bi, f = pl.program_id(0), pl.program_id(2)
    last = pl.num_programs(2) - 1
    n_rows = h_scr.shape[0]
    parts = [slice(r, r + n_rows // n_split) for r in range(0, n_rows, n_rows // n_split)]

    def store_h(rows, val):
        h_scr[rows, :] = val

    def store_o(rows, val):
        o_ref[0, rows, :] = val

    def ffn(rows, assign_first=False):
        n_sub, _, sub = w1_ref.shape
        for c in range(n_sub):
            u = jnp.dot(h_scr[rows, :], w1_ref[c], preferred_element_type=F32)
            u = jnp.maximum(u, 0.0)
            y = jnp.dot((u * u).astype(BF16), w2_ref[c * sub:(c + 1) * sub, :],
                        preferred_element_type=F32)
            if assign_first and c == 0:
                o_ref[0, rows, :] = y
            else:
                o_ref[0, rows, :] += y

    @pl.when(f == 0)
    def _():
        for part in parts:
            _norm_mod_rows(lambda rows: x_ref[0, rows, :], store_h, part,
                           gpre_ref[...], _mod_row(mod_ref, bi, MOD_SC_M),
                           _mod_row(mod_ref, bi, MOD_SH_M))
            ffn(part, assign_first=True)

    @pl.when((f > 0) & (f < last))
    def _():
        ffn(slice(0, n_rows))

    @pl.when(f == last)
    def _():
        for part in parts:
            ffn(part)
            _gated_residual_rows(lambda rows: x_ref[0, rows, :], lambda rows: o_ref[0, rows, :],
                                 store_o, part, gpost_ref[...],
                                 _mod_row(mod_ref, bi, MOD_GT_M))


def _mlp(x, mod, g_pre, w1, w2, g_post):
    b, s, d = x.shape
    tm = MLP_TM
    n_sub = MLP_TF // MLP_SUB
    n_f, tf = w1.shape[0] // n_sub, MLP_TF
    row = lambda bi, i, f: (bi, i, 0)
    const = lambda bi, i, f: (0, 0)
    return pl.pallas_call(
        functools.partial(_mlp_kernel, n_split=2),
        grid=(b, s // tm, n_f),
        in_specs=[
            pl.BlockSpec((1, tm, d), row),
            pl.BlockSpec(mod.shape, const),
            pl.BlockSpec((1, d), const),
            pl.BlockSpec((n_sub, d, MLP_SUB), lambda bi, i, f: (f, 0, 0)),
            pl.BlockSpec((tf, d), lambda bi, i, f: (f, 0)),
            pl.BlockSpec((1, d), const),
        ],
        out_specs=pl.BlockSpec((1, tm, d), row),
        out_shape=jax.ShapeDtypeStruct((b, s, d), F32),
        scratch_shapes=[pltpu.VMEM((tm, d), BF16)],
        compiler_params=pltpu.CompilerParams(
            dimension_semantics=("arbitrary", "arbitrary", "arbitrary"),
            vmem_limit_bytes=VMEM_LIMIT),
        name="mlp",
    )(x, mod, g_pre, w1, w2, g_post)


def kernel(x, c, w_ada, b_ada, g_pre_attn, w_in, g_q_a, g_k_a, g_ckv, w_kv_b, g_out_a,
           g_out_b, w_out, g_post_attn, g_pre_mlp, w_mlp_in, w_mlp_out, g_post_mlp):
    b, s, d = x.shape
    depth = w_ada.shape[0]
    tabs = _rope_tables(s)

    for l in range(depth):
        b_ada_l = b_ada[l][None, :]
        n_early = N_MOD_EARLY * d
        mod_early = _ada(c, w_ada[l], b_ada_l, n_early)

        q_a, k_a, v_a, q_b, k_b, v_b = _inproj(
            x, mod_early, g_pre_attn[l][None, :], w_in[l].T, w_kv_b[l],
            g_q_a[l][None, :], g_k_a[l][None, :], g_ckv[l][None, :], tabs)

        o_b, wo_bf, mod_late = _attention(
            q_b, k_b, v_b, [(w_out[l], None)], n_heads=HB // N_GROUPS, dk=QK_B_PAD,
            shared_kv=False, name="attn_mla",
            ada=(c, w_ada[l], b_ada_l, n_early), vmem_limit=VMEM_LIMIT_BIG)
        o_a, w2_bf, w1_bf = _attention(
            q_a, k_a, v_a, [(w_mlp_out[l], None), (w_mlp_in[l], MLP_SUB)],
            n_heads=G_A, dk=HEAD_DIM, shared_kv=True, name="attn_gqa")

        x = _outproj(o_a, o_b, x, mod_late, g_out_a[l][None, :], g_out_b[l][None, :],
                     wo_bf, g_post_attn[l][None, :])

        x = _mlp(x, mod_late, g_pre_mlp[l][None, :], w1_bf, w2_bf, g_post_mlp[l][None, :])
    return x
```

```python
import functools
import math

import numpy as np
import jax
import jax.numpy as jnp
from jax import lax
from jax.experimental import pallas as pl
from jax.experimental.pallas import tpu as pltpu

D_MODEL = 2048
GRID_W = 64
ROPE_THETA = 10000.0
EPS = 1e-6

HEAD_DIM = 128
HA = 8
HKV = 2
G_A = HA // HKV

HB = 8
N_GROUPS = 2
QK_NOPE = 128
QK_ROPE = 64
V_DIM = 128
KV_RANK = 512
QK_B = QK_NOPE + QK_ROPE
QK_B_PAD = 256

W_QA = HA * HEAD_DIM
W_KA = HKV * HEAD_DIM
W_VA = HKV * HEAD_DIM
W_QB = HB * QK_B
MIX_A = HA * HEAD_DIM
MIX_B = HB * V_DIM
N_MOD = 6
N_MOD_EARLY = 2
MOD_SH_A, MOD_SC_A = range(N_MOD_EARLY)
MOD_GT_A, MOD_SH_M, MOD_SC_M, MOD_GT_M = range(N_MOD - N_MOD_EARLY)

LOG2_E = math.log2(math.e)
LANES = 128
BF16_ROWS = 16
ROW_CHUNK = BF16_ROWS

ADA_TN = 512
IN_PROJ_TM = 512
W_STAGE_ROWS = 304
ATTN_TQ = 256
ATTN_BLOCKS = 4
OUT_PROJ_TM = 512
MLP_TM = 512
MLP_TF = 2048
MLP_SUB = 1024
VMEM_LIMIT = 56 * 1024 * 1024
VMEM_LIMIT_BIG = 60 * 1024 * 1024

F32 = jnp.float32
BF16 = jnp.bfloat16


def _rope_tables(seq_len):
    parts = _rope_table(seq_len, HEAD_DIM) + _rope_table(seq_len, QK_ROPE)
    return jnp.asarray(np.concatenate(parts, axis=1), F32)


def _rope_table(seq_len, dim):
    pos = np.arange(seq_len)
    row = (pos // GRID_W).astype(np.float64)
    col = (pos % GRID_W).astype(np.float64)
    half = dim // 2
    inv = ROPE_THETA ** (-np.arange(0, half, 2, dtype=np.float64) / half)
    ang_r = row[:, None] * inv[None, :]
    ang_c = col[:, None] * inv[None, :]
    ang = np.concatenate([ang_r, ang_r, ang_c, ang_c], axis=-1)
    cos, sin = np.cos(ang), np.sin(ang)
    quarter = dim // 4
    sign = np.where((np.arange(dim) % half) < quarter, -1.0, 1.0)
    sin = sin * sign[None, :]
    reps = LANES // dim
    return [np.tile(cos, (1, reps)), np.tile(sin, (1, reps))]


def _largest_divisor(n, at_most, multiple_of):
    return max(k for k in range(multiple_of, at_most + 1, multiple_of) if n % k == 0)


def _log2(n):
    assert n > 0 and n & (n - 1) == 0, n
    return n.bit_length() - 1


def _mod_row(mod_ref, bi, idx):
    return mod_ref[pl.ds(bi, 1), idx * D_MODEL:(idx + 1) * D_MODEL]


def _unit_rms(x):
    ms = jnp.mean(x * x, axis=-1, keepdims=True)
    return x * lax.rsqrt(ms + EPS)


def _rms(x, g):
    return _unit_rms(x) * g


def _row_chunks(span):
    span = span if isinstance(span, slice) else slice(0, span)
    return [slice(r, r + ROW_CHUNK) for r in range(span.start, span.stop, ROW_CHUNK)]


def _norm_mod_rows(load, store, span, g, sc, sh):
    gain = g * (1.0 + sc)
    for rows in _row_chunks(span):
        store(rows, (_unit_rms(load(rows)) * gain + sh).astype(BF16))


def _gated_residual_rows(load_x, load_y, store, span, g, gt):
    gain = gt * g
    for rows in _row_chunks(span):
        store(rows, load_x(rows) + _unit_rms(load_y(rows)) * gain)


def _dot_nt(a, w_rows):
    return lax.dot_general(a, w_rows, (((1,), (1,)), ((), ())), preferred_element_type=F32)


def _rope(x, cos, sin_signed, quarter):
    lane = lax.broadcasted_iota(jnp.int32, x.shape, 1)
    take_up = (lane % (2 * quarter)) < quarter
    up = pltpu.roll(x, LANES - quarter, axis=1)
    down = pltpu.roll(x, quarter, axis=1)
    rot = jnp.where(take_up, up, down)
    return x * cos + rot * sin_signed


def _ada_columns(c, w, b):
    c_act = (c * jax.nn.sigmoid(c)).astype(BF16)
    return jnp.dot(c_act, w.astype(BF16), preferred_element_type=F32) + b


def _ada_columns_t(c, w, b, m):
    c_act = (c * jax.nn.sigmoid(c)).astype(BF16)
    out_t = lax.dot_general(w.astype(BF16), c_act, (((0,), (1,)), ((), ())),
                            preferred_element_type=F32)
    return out_t.T[0:m, :] + b


def _ada_kernel(c_ref, w_ref, b_ref, o_ref):
    o_ref[...] = _ada_columns(c_ref[...], w_ref[...], b_ref[...])


def _ada(c, w_ada, b_ada, n):
    m, d = c.shape
    tn = ADA_TN
    return pl.pallas_call(
        _ada_kernel,
        grid=(n // tn,),
        in_specs=[
            pl.BlockSpec((m, d), lambda j: (0, 0)),
            pl.BlockSpec((d, tn), lambda j: (0, j)),
            pl.BlockSpec((1, tn), lambda j: (0, j)),
        ],
        out_specs=pl.BlockSpec((m, tn), lambda j: (0, j)),
        out_shape=jax.ShapeDtypeStruct((m, n), F32),
        compiler_params=pltpu.CompilerParams(
            dimension_semantics=("arbitrary",), vmem_limit_bytes=VMEM_LIMIT),
        name="ada_mod",
    )(c, w_ada, b_ada)


def _store_cols(ref, col0, val):
    g, c = divmod(col0, ref.shape[3])
    ref[0, g, :, c:c + val.shape[1]] = val


def _load_as_bf16(chunks, stage, sem):
    def copy(c):
        src, row0, n, _ = chunks[c]
        slot = c % 2
        return pltpu.make_async_copy(
            src.at[row0:row0 + n, :], stage.at[slot, 0:n, :], sem.at[slot])

    copy(0).start()
    for c, (_, row0, n, dst) in enumerate(chunks):
        if c + 1 < len(chunks):
            copy(c + 1).start()
        copy(c).wait()
        dst[row0:row0 + n, :] = stage[c % 2, 0:n, :].astype(BF16)


def _inproj_kernel(x_ref, mod_ref, gpre_ref, w_hbm, wkv_hbm, gq_ref, gk_ref,
                   gckv_ref, tab_ref,
                   qa_ref, ka_ref, va_ref, qb_ref, kb_ref, vb_ref,
                   h_scr, w_ref, wkv_ref, stage, sem):
    tm = h_scr.shape[0]
    bi = pl.program_id(0)

    @pl.when((bi == 0) & (pl.program_id(1) == 0))
    def _():
        chunks = []
        for src, dst in ((w_hbm, w_ref), (wkv_hbm, wkv_ref)):
            n = _largest_divisor(dst.shape[0], stage.shape[1], BF16_ROWS)
            chunks += [(src, r, n, dst) for r in range(0, dst.shape[0], n)]
        _load_as_bf16(chunks, stage, sem)

    def store_h(rows, val):
        h_scr[rows, :] = val

    _norm_mod_rows(lambda rows: x_ref[0, rows, :], store_h, tm, gpre_ref[...],
                   _mod_row(mod_ref, bi, MOD_SC_A), _mod_row(mod_ref, bi, MOD_SH_A))
    hb = h_scr[...]
    cosa, sina, cosb, sinb = (tab_ref[:, i * LANES:(i + 1) * LANES] for i in range(4))
    scale_a = LOG2_E / math.sqrt(HEAD_DIM)
    scale_b = LOG2_E / math.sqrt(QK_B)

    qa = _dot_nt(hb, w_ref[0:W_QA, :])
    gq = gq_ref[...] * scale_a
    for hd in range(HA):
        sl = slice(hd * HEAD_DIM, (hd + 1) * HEAD_DIM)
        q = _rope(_rms(qa[:, sl], gq), cosa, sina, HEAD_DIM // 4)
        _store_cols(qa_ref, hd * HEAD_DIM, q.astype(BF16))

    kva = _dot_nt(hb, w_ref[W_QA:W_QA + W_KA + W_VA, :])
    for hd in range(HKV):
        sl = slice(hd * HEAD_DIM, (hd + 1) * HEAD_DIM)
        k = _rope(_rms(kva[:, sl], gk_ref[...]), cosa, sina, HEAD_DIM // 4)
        _store_cols(ka_ref, hd * HEAD_DIM, k.astype(BF16))
        _store_cols(va_ref, hd * HEAD_DIM,
                    kva[:, W_KA + hd * HEAD_DIM:W_KA + (hd + 1) * HEAD_DIM].astype(BF16))

    o_qb = W_QA + W_KA + W_VA
    qb = _dot_nt(hb, w_ref[o_qb:o_qb + W_QB, :])
    lane = lax.broadcasted_iota(jnp.int32, (tm, LANES), 1)
    low = lane < QK_ROPE
    swap = lambda t: pltpu.roll(t, QK_ROPE, axis=1)
    for pair in range(HB // 2):
        t0, t1, t2 = (qb[:, (3 * pair + i) * LANES:(3 * pair + i + 1) * LANES] for i in range(3))
        r1 = _rope(t1, cosb, sinb, QK_ROPE // 4)
        r2 = _rope(t2, cosb, sinb, QK_ROPE // 4)
        heads = (
            (2 * pair, t0, jnp.where(low, r1, 0.0)),
            (2 * pair + 1, jnp.where(low, swap(t1), swap(t2)), jnp.where(low, swap(r2), 0.0)),
        )
        for hd, nope, pe in heads:
            base = hd * QK_B_PAD
            _store_cols(qb_ref, base, (nope * scale_b).astype(BF16))
            _store_cols(qb_ref, base + QK_NOPE, (pe * scale_b).astype(BF16))

    o_ckv = o_qb + W_QB
    w_rest = jnp.concatenate(
        [w_ref[o_ckv:, :], jnp.zeros((LANES - QK_ROPE, w_ref.shape[1]), BF16)], axis=0)
    rest = _dot_nt(hb, w_rest)
    ckv = _rms(rest[:, :KV_RANK], gckv_ref[...]).astype(BF16)
    kv = jnp.dot(ckv, wkv_ref[...], preferred_element_type=F32)
    kpe = _rope(rest[:, KV_RANK:], cosb, sinb, QK_ROPE // 4).astype(BF16)
    for hd in range(HB):
        base = hd * (QK_NOPE + V_DIM)
        _store_cols(kb_ref, hd * QK_B_PAD, kv[:, base:base + QK_NOPE].astype(BF16))
        _store_cols(kb_ref, hd * QK_B_PAD + QK_NOPE, kpe)
        _store_cols(vb_ref, hd * V_DIM,
                    kv[:, base + QK_NOPE:base + QK_NOPE + V_DIM].astype(BF16))


def _inproj(x, mod, g_pre, w_in_p, w_kv, g_q, g_k, g_ckv, tabs):
    b, s, d = x.shape
    tm = IN_PROJ_TM
    n_in = w_in_p.shape[0]
    row = lambda bi, i: (bi, i, 0)
    const = lambda bi, i: (0, 0)
    tab = lambda bi, i: (i, 0)
    vec = lambda n: pl.BlockSpec((1, n), const)
    out_widths = (MIX_A, W_KA, W_VA, HB * QK_B_PAD, HB * QK_B_PAD, MIX_B)
    return pl.pallas_call(
        _inproj_kernel,
        grid=(b, s // tm),
        in_specs=[
            pl.BlockSpec((1, tm, d), row),
            pl.BlockSpec(mod.shape, const),
            vec(d),
            pl.BlockSpec(memory_space=pl.ANY),
            pl.BlockSpec(memory_space=pl.ANY),
            vec(HEAD_DIM), vec(HEAD_DIM), vec(KV_RANK),
            pl.BlockSpec((tm, 4 * LANES), tab),
        ],
        out_specs=[pl.BlockSpec((1, N_GROUPS, tm, w // N_GROUPS), lambda bi, i: (bi, 0, i, 0))
                   for w in out_widths],
        out_shape=[jax.ShapeDtypeStruct((b, N_GROUPS, s, w // N_GROUPS), BF16)
                   for w in out_widths],
        scratch_shapes=[
            pltpu.VMEM((tm, d), BF16),
            pltpu.VMEM((n_in, d), BF16),
            pltpu.VMEM(w_kv.shape, BF16),
            pltpu.VMEM((2, W_STAGE_ROWS, d), F32),
            pltpu.SemaphoreType.DMA((2,)),
        ],
        compiler_params=pltpu.CompilerParams(
            dimension_semantics=("arbitrary", "arbitrary"),
            vmem_limit_bytes=VMEM_LIMIT),
        name="in_proj",
    )(x, mod, g_pre, w_in_p, w_kv, g_q, g_k, g_ckv, tabs)


def _side_specs(side, n_steps, step_of):
    in_specs, out_specs, shapes = [], [], []
    for w, col_block in side:
        rows, cols = w.shape[0] // n_steps, w.shape[1]
        in_specs.append(pl.BlockSpec((rows, cols), lambda *g: (step_of(*g), 0)))
        if col_block is None:
            out_specs.append(pl.BlockSpec((rows, cols), lambda *g: (step_of(*g), 0)))
            shapes.append(jax.ShapeDtypeStruct(w.shape, BF16))
        else:
            n_cb = cols // col_block
            out_specs.append(
                pl.BlockSpec((n_cb, rows, col_block), lambda *g: (0, step_of(*g), 0)))
            shapes.append(jax.ShapeDtypeStruct((n_cb, w.shape[0], col_block), BF16))
    return in_specs, out_specs, shapes


def _side_cast(side_in, side_out, part, n_parts):
    for w_ref, wb_ref in zip(side_in, side_out):
        n = w_ref.shape[0] // n_parts
        rows = slice(part * n, (part + 1) * n)
        if len(wb_ref.shape) == 2:
            wb_ref[rows, :] = w_ref[rows, :].astype(BF16)
        else:
            cb = wb_ref.shape[2]
            for f in range(wb_ref.shape[0]):
                wb_ref[f, rows, :] = w_ref[rows, f * cb:(f + 1) * cb].astype(BF16)


def _attn_kernel(*refs, n_heads, dk, shared_kv, tq, n_blocks, n_side, with_ada):
    q_ref, qn_ref, k_ref, kn_ref, v_ref = refs[:5]
    rest = list(refs[5:])
    side_in = [rest.pop(0) for _ in range(n_side)]
    ada_in = [rest.pop(0) for _ in range(3 if with_ada else 0)]
    o_ref = rest.pop(0)
    side_out = [rest.pop(0) for _ in range(n_side)]
    ada_out = [rest.pop(0) for _ in range(1 if with_ada else 0)]
    s_scr, m_scr = rest

    def scores(i, slot):
        nxt = i == n_blocks
        keys_ref = kn_ref if nxt else k_ref
        for hd in range(n_heads):
            kv_hd = 0 if shared_kv else hd
            k = keys_ref[0, 0, :, kv_hd * dk:(kv_hd + 1) * dk]
            if nxt:
                q = qn_ref[0, 0, :, hd * dk:(hd + 1) * dk]
            else:
                q = q_ref[0, 0, i * tq:(i + 1) * tq, hd * dk:(hd + 1) * dk]
            s = lax.dot_general(q, k, (((1,), (1,)), ((), ())), preferred_element_type=F32)
            s_scr[slot, hd] = s
            m_scr[slot, hd] = jnp.max(s, axis=-1, keepdims=True)

    def softmax_pv(slot, row0):
        for hd in range(n_heads):
            kv_hd = 0 if shared_kv else hd
            v = v_ref[0, 0, :, kv_hd * V_DIM:(kv_hd + 1) * V_DIM]
            p = jnp.exp2(s_scr[slot, hd] - m_scr[slot, hd]).astype(BF16)
            v_ext = jnp.concatenate([v, jnp.ones_like(v)], axis=1)
            o_ext = jnp.dot(p, v_ext, preferred_element_type=F32)
            o = o_ext[:, :V_DIM] / o_ext[:, V_DIM:]
            o_ref[0, 0, row0:row0 + tq, hd * V_DIM:(hd + 1) * V_DIM] = o.astype(BF16)

    @pl.when(pl.program_id(0) == 0)
    def _():
        scores(0, 0)

    for i in range(n_blocks):
        scores(i + 1, (i + 1) % 2)
        _side_cast(side_in, side_out, i, n_blocks)
        if with_ada and i == 0:
            c_ref, wada_ref, bada_ref = ada_in
            ada_out[0][...] = _ada_columns_t(c_ref[...], wada_ref[...], bada_ref[...],
                                             ada_out[0].shape[0])
        softmax_pv(i % 2, i * tq)


def _attention(q, k, v, side, *, n_heads, dk, shared_kv, name, ada=None,
               vmem_limit=VMEM_LIMIT):
    b, n_groups, s, _ = q.shape
    tq, n_blocks = ATTN_TQ, ATTN_BLOCKS
    n_kv = 1 if shared_kv else n_heads
    n_blk = s // tq
    assert n_blocks % 2 == 0 and n_blk % n_blocks == 0
    n_total = b * n_groups * n_blk
    n_steps = n_total // n_blocks

    sh_blk, sh_grp, sh_nb = (_log2(n) for n in (n_blk, n_groups, n_blocks))

    def q_map(t):
        grp = jnp.right_shift(t, sh_blk)
        return (jnp.right_shift(grp, sh_grp), grp & (n_groups - 1), t & (n_blk - 1), 0)

    def kv_map(t):
        bi, g, _, _ = q_map(t)
        return (bi, g, 0, 0)

    def step_map(j):
        bi, g, i, _ = q_map(n_blocks * j)
        return (bi, g, jnp.right_shift(i, sh_nb), 0)

    def next_first(j):
        return jnp.minimum(n_blocks * (j + 1), n_total - 1)

    side_in_specs, side_out_specs, side_shapes = _side_specs(side, n_steps, lambda j: j)
    ada_args, ada_in_specs, ada_out_specs, ada_shapes = [], [], [], []
    if ada is not None:
        c, w_ada, b_ada, first_col = ada
        n_cols = w_ada.shape[1] - first_col
        tn = n_cols // n_steps
        skip = first_col // tn
        n_c = c.shape[0]
        c = jnp.pad(c, ((0, LANES - n_c), (0, 0)))
        ada_args = [c, w_ada, b_ada]
        ada_in_specs = [
            pl.BlockSpec(c.shape, lambda j: (0, 0)),
            pl.BlockSpec((w_ada.shape[0], tn), lambda j: (0, skip + j)),
            pl.BlockSpec((1, tn), lambda j: (0, skip + j)),
        ]
        ada_out_specs = [pl.BlockSpec((n_c, tn), lambda j: (0, j))]
        ada_shapes = [jax.ShapeDtypeStruct((n_c, n_cols), F32)]
    return pl.pallas_call(
        functools.partial(_attn_kernel, n_heads=n_heads, dk=dk, shared_kv=shared_kv,
                          tq=tq, n_blocks=n_blocks, n_side=len(side),
                          with_ada=ada is not None),
        grid=(n_steps,),
        in_specs=[
            pl.BlockSpec((1, 1, n_blocks * tq, n_heads * dk), step_map),
            pl.BlockSpec((1, 1, tq, n_heads * dk), lambda j: q_map(next_first(j))),
            pl.BlockSpec((1, 1, s, n_kv * dk), lambda j: kv_map(n_blocks * j)),
            pl.BlockSpec((1, 1, s, n_kv * dk), lambda j: kv_map(next_first(j))),
            pl.BlockSpec((1, 1, s, n_kv * V_DIM), lambda j: kv_map(n_blocks * j)),
        ] + side_in_specs + ada_in_specs,
        out_specs=[pl.BlockSpec((1, 1, n_blocks * tq, n_heads * V_DIM), step_map)]
        + side_out_specs + ada_out_specs,
        out_shape=[jax.ShapeDtypeStruct((b, n_groups, s, n_heads * V_DIM), BF16)]
        + side_shapes + ada_shapes,
        scratch_shapes=[pltpu.VMEM((2, n_heads, tq, s), F32),
                        pltpu.VMEM((2, n_heads, tq, 1), F32)],
        compiler_params=pltpu.CompilerParams(
            dimension_semantics=("arbitrary",), vmem_limit_bytes=vmem_limit),
        name=name,
    )(q, q, k, k, v, *[w for w, _ in side], *ada_args)


def _outproj_kernel(oa_ref, ob_ref, x_ref, mod_ref, ga_ref, gb_ref, w_ref, gpost_ref, o_ref,
                    n_scr, y_scr):
    n_rows = n_scr.shape[0]

    def store_o(rows, val):
        o_ref[0, rows, :] = val

    def head_rows(ref, rows):
        return jnp.concatenate(
            [ref[0, g, rows, :] for g in range(ref.shape[1])], axis=-1).astype(F32)

    for rows in _row_chunks(n_rows):
        n_scr[rows, 0:MIX_A] = _rms(head_rows(oa_ref, rows), ga_ref[...]).astype(BF16)
        n_scr[rows, MIX_A:] = _rms(head_rows(ob_ref, rows), gb_ref[...]).astype(BF16)
    y_scr[...] = jnp.dot(n_scr[...], w_ref[...], preferred_element_type=F32)
    _gated_residual_rows(lambda rows: x_ref[0, rows, :], lambda rows: y_scr[rows, :],
                         store_o, n_rows, gpost_ref[...],
                         _mod_row(mod_ref, pl.program_id(0), MOD_GT_A))


def _outproj(o_a, o_b, x, mod, g_a, g_b, w_out, g_post):
    b, s, d = x.shape
    tm = OUT_PROJ_TM
    row = lambda bi, i: (bi, i, 0)
    const = lambda bi, i: (0, 0)
    return pl.pallas_call(
        _outproj_kernel,
        grid=(b, s // tm),
        in_specs=[
            pl.BlockSpec((1, N_GROUPS, tm, MIX_A // N_GROUPS), lambda bi, i: (bi, 0, i, 0)),
            pl.BlockSpec((1, N_GROUPS, tm, MIX_B // N_GROUPS), lambda bi, i: (bi, 0, i, 0)),
            pl.BlockSpec((1, tm, d), row),
            pl.BlockSpec(mod.shape, const),
            pl.BlockSpec((1, MIX_A), const),
            pl.BlockSpec((1, MIX_B), const),
            pl.BlockSpec((MIX_A + MIX_B, d), const, pipeline_mode=pl.Buffered(1)),
            pl.BlockSpec((1, d), const),
        ],
        out_specs=pl.BlockSpec((1, tm, d), row),
        out_shape=jax.ShapeDtypeStruct((b, s, d), F32),
        scratch_shapes=[pltpu.VMEM((tm, MIX_A + MIX_B), BF16), pltpu.VMEM((tm, d), F32)],
        compiler_params=pltpu.CompilerParams(
            dimension_semantics=("arbitrary", "arbitrary"),
            vmem_limit_bytes=VMEM_LIMIT),
        name="out_proj",
    )(o_a, o_b, x, mod, g_a, g_b, w_out, g_post)


def _mlp_kernel(x_ref, mod_ref, gpre_ref, w1_ref, w2_ref, gpost_ref,
                o_ref, h_scr, *, n_split):
    bi, f = pl.program_id(0), pl.program_id(2)
    last = pl.num_programs(2) - 1
    n_rows = h_scr.shape[0]
    parts = [slice(r, r + n_rows // n_split) for r in range(0, n_rows, n_rows // n_split)]

    def store_h(rows, val):
        h_scr[rows, :] = val

    def store_o(rows, val):
        o_ref[0, rows, :] = val

    def ffn(rows, assign_first=False):
        n_sub, _, sub = w1_ref.shape
        for c in range(n_sub):
            u = jnp.dot(h_scr[rows, :], w1_ref[c], preferred_element_type=F32)
            u = jnp.maximum(u, 0.0)
            y = jnp.dot((u * u).astype(BF16), w2_ref[c * sub:(c + 1) * sub, :],
                        preferred_element_type=F32)
            if assign_first and c == 0:
                o_ref[0, rows, :] = y
            else:
                o_ref[0, rows, :] += y

    @pl.when(f == 0)
    def _():
        for part in parts:
            _norm_mod_rows(lambda rows: x_ref[0, rows, :], store_h, part,
                           gpre_ref[...], _mod_row(mod_ref, bi, MOD_SC_M),
                           _mod_row(mod_ref, bi, MOD_SH_M))
            ffn(part, assign_first=True)

    @pl.when((f > 0) & (f < last))
    def _():
        ffn(slice(0, n_rows))

    @pl.when(f == last)
    def _():
        for part in parts:
            ffn(part)
            _gated_residual_rows(lambda rows: x_ref[0, rows, :], lambda rows: o_ref[0, rows, :],
                                 store_o, part, gpost_ref[...],
                                 _mod_row(mod_ref, bi, MOD_GT_M))


def _mlp(x, mod, g_pre, w1, w2, g_post):
    b, s, d = x.shape
    tm = MLP_TM
    n_sub = MLP_TF // MLP_SUB
    n_f, tf = w1.shape[0] // n_sub, MLP_TF
    row = lambda bi, i, f: (bi, i, 0)
    const = lambda bi, i, f: (0, 0)
    return pl.pallas_call(
        functools.partial(_mlp_kernel, n_split=2),
        grid=(b, s // tm, n_f),
        in_specs=[
            pl.BlockSpec((1, tm, d), row),
            pl.BlockSpec(mod.shape, const),
            pl.BlockSpec((1, d), const),
            pl.BlockSpec((n_sub, d, MLP_SUB), lambda bi, i, f: (f, 0, 0)),
            pl.BlockSpec((tf, d), lambda bi, i, f: (f, 0)),
            pl.BlockSpec((1, d), const),
        ],
        out_specs=pl.BlockSpec((1, tm, d), row),
        out_shape=jax.ShapeDtypeStruct((b, s, d), F32),
        scratch_shapes=[pltpu.VMEM((tm, d), BF16)],
        compiler_params=pltpu.CompilerParams(
            dimension_semantics=("arbitrary", "arbitrary", "arbitrary"),
            vmem_limit_bytes=VMEM_LIMIT),
        name="mlp",
    )(x, mod, g_pre, w1, w2, g_post)


def kernel(x, c, w_ada, b_ada, g_pre_attn, w_in, g_q_a, g_k_a, g_ckv, w_kv_b, g_out_a,
           g_out_b, w_out, g_post_attn, g_pre_mlp, w_mlp_in, w_mlp_out, g_post_mlp):
    b, s, d = x.shape
    depth = w_ada.shape[0]
    tabs = _rope_tables(s)

    for l in range(depth):
        b_ada_l = b_ada[l][None, :]
        n_early = N_MOD_EARLY * d
        mod_early = _ada(c, w_ada[l], b_ada_l, n_early)

        q_a, k_a, v_a, q_b, k_b, v_b = _inproj(
            x, mod_early, g_pre_attn[l][None, :], w_in[l].T, w_kv_b[l],
            g_q_a[l][None, :], g_k_a[l][None, :], g_ckv[l][None, :], tabs)

        o_b, wo_bf, mod_late = _attention(
            q_b, k_b, v_b, [(w_out[l], None)], n_heads=HB // N_GROUPS, dk=QK_B_PAD,
            shared_kv=False, name="attn_mla",
            ada=(c, w_ada[l], b_ada_l, n_early), vmem_limit=VMEM_LIMIT_BIG)
        o_a, w2_bf, w1_bf = _attention(
            q_a, k_a, v_a, [(w_mlp_out[l], None), (w_mlp_in[l], MLP_SUB)],
            n_heads=G_A, dk=HEAD_DIM, shared_kv=True, name="attn_gqa")

        x = _outproj(o_a, o_b, x, mod_late, g_out_a[l][None, :], g_out_b[l][None, :],
                     wo_bf, g_post_attn[l][None, :])

        x = _mlp(x, mod_late, g_pre_mlp[l][None, :], w1_bf, w2_bf, g_post_mlp[l][None, :])
    return x
```

```python
import functools
import math

import numpy as np
import jax
import jax.numpy as jnp
from jax import lax
from jax.experimental import pallas as pl
from jax.experimental.pallas import tpu as pltpu

D_MODEL = 2048
GRID_W = 64
ROPE_THETA = 10000.0
EPS = 1e-6

HEAD_DIM = 128
HA = 8
HKV = 2
G_A = HA // HKV

HB = 8
N_GROUPS = 2
QK_NOPE = 128
QK_ROPE = 64
V_DIM = 128
KV_RANK = 512
QK_B = QK_NOPE + QK_ROPE
QK_B_PAD = 256

W_QA = HA * HEAD_DIM
W_KA = HKV * HEAD_DIM
W_VA = HKV * HEAD_DIM
W_QB = HB * QK_B
MIX_A = HA * HEAD_DIM
MIX_B = HB * V_DIM
N_MOD = 6
N_MOD_EARLY = 2
MOD_SH_A, MOD_SC_A = range(N_MOD_EARLY)
MOD_GT_A, MOD_SH_M, MOD_SC_M, MOD_GT_M = range(N_MOD - N_MOD_EARLY)

LOG2_E = math.log2(math.e)
LANES = 128
BF16_ROWS = 16
ROW_CHUNK = BF16_ROWS

ADA_TN = 512
IN_PROJ_TM = 512
W_STAGE_ROWS = 608
ATTN_TQ = 256
ATTN_BLOCKS = 4
OUT_PROJ_TM = 512
MLP_TM = 512
MLP_TF = 2048
MLP_SUB = 1024
VMEM_LIMIT = 56 * 1024 * 1024
VMEM_LIMIT_BIG = 60 * 1024 * 1024

F32 = jnp.float32
BF16 = jnp.bfloat16


def _rope_tables(seq_len):
    parts = _rope_table(seq_len, HEAD_DIM) + _rope_table(seq_len, QK_ROPE)
    return jnp.asarray(np.concatenate(parts, axis=1), F32)


def _rope_table(seq_len, dim):
    pos = np.arange(seq_len)
    row = (pos // GRID_W).astype(np.float64)
    col = (pos % GRID_W).astype(np.float64)
    half = dim // 2
    inv = ROPE_THETA ** (-np.arange(0, half, 2, dtype=np.float64) / half)
    ang_r = row[:, None] * inv[None, :]
    ang_c = col[:, None] * inv[None, :]
    ang = np.concatenate([ang_r, ang_r, ang_c, ang_c], axis=-1)
    cos, sin = np.cos(ang), np.sin(ang)
    quarter = dim // 4
    sign = np.where((np.arange(dim) % half) < quarter, -1.0, 1.0)
    sin = sin * sign[None, :]
    reps = LANES // dim
    return [np.tile(cos, (1, reps)), np.tile(sin, (1, reps))]


def _largest_divisor(n, at_most, multiple_of):
    return max(k for k in range(multiple_of, at_most + 1, multiple_of) if n % k == 0)


def _log2(n):
    assert n > 0 and n & (n - 1) == 0, n
    return n.bit_length() - 1


def _mod_row(mod_ref, bi, idx):
    return mod_ref[pl.ds(bi, 1), idx * D_MODEL:(idx + 1) * D_MODEL]


def _unit_rms(x):
    ms = jnp.mean(x * x, axis=-1, keepdims=True)
    return x * lax.rsqrt(ms + EPS)


def _rms(x, g):
    return _unit_rms(x) * g


def _row_chunks(span):
    span = span if isinstance(span, slice) else slice(0, span)
    return [slice(r, r + ROW_CHUNK) for r in range(span.start, span.stop, ROW_CHUNK)]


def _norm_mod_rows(load, store, span, g, sc, sh):
    gain = g * (1.0 + sc)
    for rows in _row_chunks(span):
        store(rows, (_unit_rms(load(rows)) * gain + sh).astype(BF16))


def _gated_residual_rows(load_x, load_y, store, span, g, gt):
    gain = gt * g
    for rows in _row_chunks(span):
        store(rows, load_x(rows) + _unit_rms(load_y(rows)) * gain)


def _dot_nt(a, w_rows):
    return lax.dot_general(a, w_rows, (((1,), (1,)), ((), ())), preferred_element_type=F32)


def _rope(x, cos, sin_signed, quarter):
    lane = lax.broadcasted_iota(jnp.int32, x.shape, 1)
    take_up = (lane % (2 * quarter)) < quarter
    up = pltpu.roll(x, LANES - quarter, axis=1)
    down = pltpu.roll(x, quarter, axis=1)
    rot = jnp.where(take_up, up, down)
    return x * cos + rot * sin_signed


def _ada_columns(c, w, b):
    c_act = (c * jax.nn.sigmoid(c)).astype(BF16)
    return jnp.dot(c_act, w.astype(BF16), preferred_element_type=F32) + b


def _ada_kernel(c_ref, w_ref, b_ref, o_ref):
    o_ref[...] = _ada_columns(c_ref[...], w_ref[...], b_ref[...])


def _ada(c, w_ada, b_ada, n):
    m, d = c.shape
    tn = ADA_TN
    return pl.pallas_call(
        _ada_kernel,
        grid=(n // tn,),
        in_specs=[
            pl.BlockSpec((m, d), lambda j: (0, 0)),
            pl.BlockSpec((d, tn), lambda j: (0, j)),
            pl.BlockSpec((1, tn), lambda j: (0, j)),
        ],
        out_specs=pl.BlockSpec((m, tn), lambda j: (0, j)),
        out_shape=jax.ShapeDtypeStruct((m, n), F32),
        compiler_params=pltpu.CompilerParams(
            dimension_semantics=("arbitrary",), vmem_limit_bytes=VMEM_LIMIT),
        name="ada_mod",
    )(c, w_ada, b_ada)


def _store_cols(ref, col0, val):
    g, c = divmod(col0, ref.shape[3])
    ref[0, g, :, c:c + val.shape[1]] = val


def _load_as_bf16(chunks, stage, sem):
    def copy(c):
        src, row0, n, _ = chunks[c]
        slot = c % 2
        return pltpu.make_async_copy(
            src.at[row0:row0 + n, :], stage.at[slot, 0:n, :], sem.at[slot])

    copy(0).start()
    for c, (_, row0, n, dst) in enumerate(chunks):
        if c + 1 < len(chunks):
            copy(c + 1).start()
        copy(c).wait()
        dst[row0:row0 + n, :] = stage[c % 2, 0:n, :].astype(BF16)


def _inproj_kernel(x_ref, mod_ref, gpre_ref, w_hbm, wkv_hbm, gq_ref, gk_ref,
                   gckv_ref, tab_ref,
                   qa_ref, ka_ref, va_ref, qb_ref, kb_ref, vb_ref,
                   h_scr, w_ref, wkv_ref, stage, sem):
    tm = h_scr.shape[0]
    bi = pl.program_id(0)

    @pl.when((bi == 0) & (pl.program_id(1) == 0))
    def _():
        chunks = []
        for src, dst in ((w_hbm, w_ref), (wkv_hbm, wkv_ref)):
            n = _largest_divisor(dst.shape[0], stage.shape[1], BF16_ROWS)
            chunks += [(src, r, n, dst) for r in range(0, dst.shape[0], n)]
        _load_as_bf16(chunks, stage, sem)

    def store_h(rows, val):
        h_scr[rows, :] = val

    _norm_mod_rows(lambda rows: x_ref[0, rows, :], store_h, tm, gpre_ref[...],
                   _mod_row(mod_ref, bi, MOD_SC_A), _mod_row(mod_ref, bi, MOD_SH_A))
    hb = h_scr[...]
    cosa, sina, cosb, sinb = (tab_ref[:, i * LANES:(i + 1) * LANES] for i in range(4))
    scale_a = LOG2_E / math.sqrt(HEAD_DIM)
    scale_b = LOG2_E / math.sqrt(QK_B)

    qa = _dot_nt(hb, w_ref[0:W_QA, :])
    gq = gq_ref[...] * scale_a
    for hd in range(HA):
        sl = slice(hd * HEAD_DIM, (hd + 1) * HEAD_DIM)
        q = _rope(_rms(qa[:, sl], gq), cosa, sina, HEAD_DIM // 4)
        _store_cols(qa_ref, hd * HEAD_DIM, q.astype(BF16))

    kva = _dot_nt(hb, w_ref[W_QA:W_QA + W_KA + W_VA, :])
    for hd in range(HKV):
        sl = slice(hd * HEAD_DIM, (hd + 1) * HEAD_DIM)
        k = _rope(_rms(kva[:, sl], gk_ref[...]), cosa, sina, HEAD_DIM // 4)
        _store_cols(ka_ref, hd * HEAD_DIM, k.astype(BF16))
        _store_cols(va_ref, hd * HEAD_DIM,
                    kva[:, W_KA + hd * HEAD_DIM:W_KA + (hd + 1) * HEAD_DIM].astype(BF16))

    o_qb = W_QA + W_KA + W_VA
    qb = _dot_nt(hb, w_ref[o_qb:o_qb + W_QB, :])
    lane = lax.broadcasted_iota(jnp.int32, (tm, LANES), 1)
    low = lane < QK_ROPE
    swap = lambda t: pltpu.roll(t, QK_ROPE, axis=1)
    for pair in range(HB // 2):
        t0, t1, t2 = (qb[:, (3 * pair + i) * LANES:(3 * pair + i + 1) * LANES] for i in range(3))
        r1 = _rope(t1, cosb, sinb, QK_ROPE // 4)
        r2 = _rope(t2, cosb, sinb, QK_ROPE // 4)
        heads = (
            (2 * pair, t0, jnp.where(low, r1, 0.0)),
            (2 * pair + 1, jnp.where(low, swap(t1), swap(t2)), jnp.where(low, swap(r2), 0.0)),
        )
        for hd, nope, pe in heads:
            base = hd * QK_B_PAD
            _store_cols(qb_ref, base, (nope * scale_b).astype(BF16))
            _store_cols(qb_ref, base + QK_NOPE, (pe * scale_b).astype(BF16))

    o_ckv = o_qb + W_QB
    w_rest = jnp.concatenate(
        [w_ref[o_ckv:, :], jnp.zeros((LANES - QK_ROPE, w_ref.shape[1]), BF16)], axis=0)
    rest = _dot_nt(hb, w_rest)
    ckv = _rms(rest[:, :KV_RANK], gckv_ref[...]).astype(BF16)
    kv = jnp.dot(ckv, wkv_ref[...], preferred_element_type=F32)
    kpe = _rope(rest[:, KV_RANK:], cosb, sinb, QK_ROPE // 4).astype(BF16)
    for hd in range(HB):
        base = hd * (QK_NOPE + V_DIM)
        _store_cols(kb_ref, hd * QK_B_PAD, kv[:, base:base + QK_NOPE].astype(BF16))
        _store_cols(kb_ref, hd * QK_B_PAD + QK_NOPE, kpe)
        _store_cols(vb_ref, hd * V_DIM,
                    kv[:, base + QK_NOPE:base + QK_NOPE + V_DIM].astype(BF16))


def _inproj(x, mod, g_pre, w_in_p, w_kv, g_q, g_k, g_ckv, tabs):
    b, s, d = x.shape
    tm = IN_PROJ_TM
    n_in = w_in_p.shape[0]
    row = lambda bi, i: (bi, i, 0)
    const = lambda bi, i: (0, 0)
    tab = lambda bi, i: (i, 0)
    vec = lambda n: pl.BlockSpec((1, n), const)
    out_widths = (MIX_A, W_KA, W_VA, HB * QK_B_PAD, HB * QK_B_PAD, MIX_B)
    return pl.pallas_call(
        _inproj_kernel,
        grid=(b, s // tm),
        in_specs=[
            pl.BlockSpec((1, tm, d), row),
            pl.BlockSpec(mod.shape, const),
            vec(d),
            pl.BlockSpec(memory_space=pl.ANY),
            pl.BlockSpec(memory_space=pl.ANY),
            vec(HEAD_DIM), vec(HEAD_DIM), vec(KV_RANK),
            pl.BlockSpec((tm, 4 * LANES), tab),
        ],
        out_specs=[pl.BlockSpec((1, N_GROUPS, tm, w // N_GROUPS), lambda bi, i: (bi, 0, i, 0))
                   for w in out_widths],
        out_shape=[jax.ShapeDtypeStruct((b, N_GROUPS, s, w // N_GROUPS), BF16)
                   for w in out_widths],
        scratch_shapes=[
            pltpu.VMEM((tm, d), BF16),
            pltpu.VMEM((n_in, d), BF16),
            pltpu.VMEM(w_kv.shape, BF16),
            pltpu.VMEM((2, W_STAGE_ROWS, d), F32),
            pltpu.SemaphoreType.DMA((2,)),
        ],
        compiler_params=pltpu.CompilerParams(
            dimension_semantics=("arbitrary", "arbitrary"),
            vmem_limit_bytes=VMEM_LIMIT_BIG),
        name="in_proj",
    )(x, mod, g_pre, w_in_p, w_kv, g_q, g_k, g_ckv, tabs)


def _side_specs(side, n_steps, step_of):
    in_specs, out_specs, shapes = [], [], []
    for w, col_block in side:
        rows, cols = w.shape[0] // n_steps, w.shape[1]
        in_specs.append(pl.BlockSpec((rows, cols), lambda *g: (step_of(*g), 0)))
        if col_block is None:
            out_specs.append(pl.BlockSpec((rows, cols), lambda *g: (step_of(*g), 0)))
            shapes.append(jax.ShapeDtypeStruct(w.shape, BF16))
        else:
            n_cb = cols // col_block
            out_specs.append(
                pl.BlockSpec((n_cb, rows, col_block), lambda *g: (0, step_of(*g), 0)))
            shapes.append(jax.ShapeDtypeStruct((n_cb, w.shape[0], col_block), BF16))
    return in_specs, out_specs, shapes


def _side_cast(side_in, side_out, part, n_parts):
    for w_ref, wb_ref in zip(side_in, side_out):
        n = w_ref.shape[0] // n_parts
        rows = slice(part * n, (part + 1) * n)
        if len(wb_ref.shape) == 2:
            wb_ref[rows, :] = w_ref[rows, :].astype(BF16)
        else:
            cb = wb_ref.shape[2]
            for f in range(wb_ref.shape[0]):
                wb_ref[f, rows, :] = w_ref[rows, f * cb:(f + 1) * cb].astype(BF16)


def _attn_kernel(*refs, n_heads, dk, shared_kv, tq, n_blocks, n_side, with_ada):
    q_ref, qn_ref, k_ref, kn_ref, v_ref = refs[:5]
    rest = list(refs[5:])
    side_in = [rest.pop(0) for _ in range(n_side)]
    ada_in = [rest.pop(0) for _ in range(3 if with_ada else 0)]
    o_ref = rest.pop(0)
    side_out = [rest.pop(0) for _ in range(n_side)]
    ada_out = [rest.pop(0) for _ in range(1 if with_ada else 0)]
    s_scr, m_scr = rest

    def scores(i, slot):
        nxt = i == n_blocks
        keys_ref = kn_ref if nxt else k_ref
        for hd in range(n_heads):
            kv_hd = 0 if shared_kv else hd
            k = keys_ref[0, 0, :, kv_hd * dk:(kv_hd + 1) * dk]
            if nxt:
                q = qn_ref[0, 0, :, hd * dk:(hd + 1) * dk]
            else:
                q = q_ref[0, 0, i * tq:(i + 1) * tq, hd * dk:(hd + 1) * dk]
            s = lax.dot_general(q, k, (((1,), (1,)), ((), ())), preferred_element_type=F32)
            s_scr[slot, hd] = s
            m_scr[slot, hd] = jnp.max(s, axis=-1, keepdims=True)

    def softmax_pv(slot, row0):
        for hd in range(n_heads):
            kv_hd = 0 if shared_kv else hd
            v = v_ref[0, 0, :, kv_hd * V_DIM:(kv_hd + 1) * V_DIM]
            p = jnp.exp2(s_scr[slot, hd] - m_scr[slot, hd]).astype(BF16)
            v_ext = jnp.concatenate([v, jnp.ones_like(v)], axis=1)
            o_ext = jnp.dot(p, v_ext, preferred_element_type=F32)
            o = o_ext[:, :V_DIM] / o_ext[:, V_DIM:]
            o_ref[0, 0, row0:row0 + tq, hd * V_DIM:(hd + 1) * V_DIM] = o.astype(BF16)

    @pl.when(pl.program_id(0) == 0)
    def _():
        scores(0, 0)

    for i in range(n_blocks):
        scores(i + 1, (i + 1) % 2)
        _side_cast(side_in, side_out, i, n_blocks)
        if with_ada and i == 0:
            c_ref, wada_ref, bada_ref = ada_in
            ada_out[0][...] = _ada_columns(c_ref[...], wada_ref[...], bada_ref[...])
        softmax_pv(i % 2, i * tq)


def _attention(q, k, v, side, *, n_heads, dk, shared_kv, name, ada=None,
               vmem_limit=VMEM_LIMIT):
    b, n_groups, s, _ = q.shape
    tq, n_blocks = ATTN_TQ, ATTN_BLOCKS
    n_kv = 1 if shared_kv else n_heads
    n_blk = s // tq
    assert n_blocks % 2 == 0 and n_blk % n_blocks == 0
    n_total = b * n_groups * n_blk
    n_steps = n_total // n_blocks

    sh_blk, sh_grp, sh_nb = (_log2(n) for n in (n_blk, n_groups, n_blocks))

    def q_map(t):
        grp = jnp.right_shift(t, sh_blk)
        return (jnp.right_shift(grp, sh_grp), grp & (n_groups - 1), t & (n_blk - 1), 0)

    def kv_map(t):
        bi, g, _, _ = q_map(t)
        return (bi, g, 0, 0)

    def step_map(j):
        bi, g, i, _ = q_map(n_blocks * j)
        return (bi, g, jnp.right_shift(i, sh_nb), 0)

    def next_first(j):
        return jnp.minimum(n_blocks * (j + 1), n_total - 1)

    side_in_specs, side_out_specs, side_shapes = _side_specs(side, n_steps, lambda j: j)
    ada_args, ada_in_specs, ada_out_specs, ada_shapes = [], [], [], []
    if ada is not None:
        c, w_ada, b_ada, first_col = ada
        n_cols = w_ada.shape[1] - first_col
        tn = n_cols // n_steps
        skip = first_col // tn
        ada_args = [c, w_ada, b_ada]
        ada_in_specs = [
            pl.BlockSpec(c.shape, lambda j: (0, 0)),
            pl.BlockSpec((w_ada.shape[0], tn), lambda j: (0, skip + j)),
            pl.BlockSpec((1, tn), lambda j: (0, skip + j)),
        ]
        ada_out_specs = [pl.BlockSpec((c.shape[0], tn), lambda j: (0, j))]
        ada_shapes = [jax.ShapeDtypeStruct((c.shape[0], n_cols), F32)]
    return pl.pallas_call(
        functools.partial(_attn_kernel, n_heads=n_heads, dk=dk, shared_kv=shared_kv,
                          tq=tq, n_blocks=n_blocks, n_side=len(side),
                          with_ada=ada is not None),
        grid=(n_steps,),
        in_specs=[
            pl.BlockSpec((1, 1, n_blocks * tq, n_heads * dk), step_map),
            pl.BlockSpec((1, 1, tq, n_heads * dk), lambda j: q_map(next_first(j))),
            pl.BlockSpec((1, 1, s, n_kv * dk), lambda j: kv_map(n_blocks * j)),
            pl.BlockSpec((1, 1, s, n_kv * dk), lambda j: kv_map(next_first(j))),
            pl.BlockSpec((1, 1, s, n_kv * V_DIM), lambda j: kv_map(n_blocks * j)),
        ] + side_in_specs + ada_in_specs,
        out_specs=[pl.BlockSpec((1, 1, n_blocks * tq, n_heads * V_DIM), step_map)]
        + side_out_specs + ada_out_specs,
        out_shape=[jax.ShapeDtypeStruct((b, n_groups, s, n_heads * V_DIM), BF16)]
        + side_shapes + ada_shapes,
        scratch_shapes=[pltpu.VMEM((2, n_heads, tq, s), F32),
                        pltpu.VMEM((2, n_heads, tq, 1), F32)],
        compiler_params=pltpu.CompilerParams(
            dimension_semantics=("arbitrary",), vmem_limit_bytes=vmem_limit),
        name=name,
    )(q, q, k, k, v, *[w for w, _ in side], *ada_args)


def _outproj_kernel(oa_ref, ob_ref, x_ref, mod_ref, ga_ref, gb_ref, w_ref, gpost_ref, o_ref,
                    n_scr, y_scr):
    n_rows = n_scr.shape[0]

    def store_o(rows, val):
        o_ref[0, rows, :] = val

    def head_rows(ref, rows):
        return jnp.concatenate(
            [ref[0, g, rows, :] for g in range(ref.shape[1])], axis=-1).astype(F32)

    for rows in _row_chunks(n_rows):
        n_scr[rows, 0:MIX_A] = _rms(head_rows(oa_ref, rows), ga_ref[...]).astype(BF16)
        n_scr[rows, MIX_A:] = _rms(head_rows(ob_ref, rows), gb_ref[...]).astype(BF16)
    y_scr[...] = jnp.dot(n_scr[...], w_ref[...], preferred_element_type=F32)
    _gated_residual_rows(lambda rows: x_ref[0, rows, :], lambda rows: y_scr[rows, :],
                         store_o, n_rows, gpost_ref[...],
                         _mod_row(mod_ref, pl.program_id(0), MOD_GT_A))


def _outproj(o_a, o_b, x, mod, g_a, g_b, w_out, g_post):
    b, s, d = x.shape
    tm = OUT_PROJ_TM
    row = lambda bi, i: (bi, i, 0)
    const = lambda bi, i: (0, 0)
    return pl.pallas_call(
        _outproj_kernel,
        grid=(b, s // tm),
        in_specs=[
            pl.BlockSpec((1, N_GROUPS, tm, MIX_A // N_GROUPS), lambda bi, i: (bi, 0, i, 0)),
            pl.BlockSpec((1, N_GROUPS, tm, MIX_B // N_GROUPS), lambda bi, i: (bi, 0, i, 0)),
            pl.BlockSpec((1, tm, d), row),
            pl.BlockSpec(mod.shape, const),
            pl.BlockSpec((1, MIX_A), const),
            pl.BlockSpec((1, MIX_B), const),
            pl.BlockSpec((MIX_A + MIX_B, d), const, pipeline_mode=pl.Buffered(1)),
            pl.BlockSpec((1, d), const),
        ],
        out_specs=pl.BlockSpec((1, tm, d), row),
        out_shape=jax.ShapeDtypeStruct((b, s, d), F32),
        scratch_shapes=[pltpu.VMEM((tm, MIX_A + MIX_B), BF16), pltpu.VMEM((tm, d), F32)],
        compiler_params=pltpu.CompilerParams(
            dimension_semantics=("arbitrary", "arbitrary"),
            vmem_limit_bytes=VMEM_LIMIT),
        name="out_proj",
    )(o_a, o_b, x, mod, g_a, g_b, w_out, g_post)


def _mlp_kernel(x_ref, mod_ref, gpre_ref, w1_ref, w2_ref, gpost_ref,
                o_ref, h_scr, *, n_split):
    bi, f = pl.program_id(0), pl.program_id(2)
    last = pl.num_programs(2) - 1
    n_rows = h_scr.shape[0]
    parts = [slice(r, r + n_rows // n_split) for r in range(0, n_rows, n_rows // n_split)]

    def store_h(rows, val):
        h_scr[rows, :] = val

    def store_o(rows, val):
        o_ref[0, rows, :] = val

    def ffn(rows, assign_first=False):
        n_sub, _, sub = w1_ref.shape
        for c in range(n_sub):
            u = jnp.dot(h_scr[rows, :], w1_ref[c], preferred_element_type=F32)
            u = jnp.maximum(u, 0.0)
            y = jnp.dot((u * u).astype(BF16), w2_ref[c * sub:(c + 1) * sub, :],
                        preferred_element_type=F32)
            if assign_first and c == 0:
                o_ref[0, rows, :] = y
            else:
                o_ref[0, rows, :] += y

    @pl.when(f == 0)
    def _():
        for part in parts:
            _norm_mod_rows(lambda rows: x_ref[0, rows, :], store_h, part,
                           gpre_ref[...], _mod_row(mod_ref, bi, MOD_SC_M),
                           _mod_row(mod_ref, bi, MOD_SH_M))
            ffn(part, assign_first=True)

    @pl.when((f > 0) & (f < last))
    def _():
        ffn(slice(0, n_rows))

    @pl.when(f == last)
    def _():
        for part in parts:
            ffn(part)
            _gated_residual_rows(lambda rows: x_ref[0, rows, :], lambda rows: o_ref[0, rows, :],
                                 store_o, part, gpost_ref[...],
                                 _mod_row(mod_ref, bi, MOD_GT_M))


def _mlp(x, mod, g_pre, w1, w2, g_post):
    b, s, d = x.shape
    tm = MLP_TM
    n_sub = MLP_TF // MLP_SUB
    n_f, tf = w1.shape[0] // n_sub, MLP_TF
    row = lambda bi, i, f: (bi, i, 0)
    const = lambda bi, i, f: (0, 0)
    return pl.pallas_call(
        functools.partial(_mlp_kernel, n_split=2),
        grid=(b, s // tm, n_f),
        in_specs=[
            pl.BlockSpec((1, tm, d), row),
            pl.BlockSpec(mod.shape, const),
            pl.BlockSpec((1, d), const),
            pl.BlockSpec((n_sub, d, MLP_SUB), lambda bi, i, f: (f, 0, 0)),
            pl.BlockSpec((tf, d), lambda bi, i, f: (f, 0)),
            pl.BlockSpec((1, d), const),
        ],
        out_specs=pl.BlockSpec((1, tm, d), row),
        out_shape=jax.ShapeDtypeStruct((b, s, d), F32),
        scratch_shapes=[pltpu.VMEM((tm, d), BF16)],
        compiler_params=pltpu.CompilerParams(
            dimension_semantics=("arbitrary", "arbitrary", "arbitrary"),
            vmem_limit_bytes=VMEM_LIMIT),
        name="mlp",
    )(x, mod, g_pre, w1, w2, g_post)


def kernel(x, c, w_ada, b_ada, g_pre_attn, w_in, g_q_a, g_k_a, g_ckv, w_kv_b, g_out_a,
           g_out_b, w_out, g_post_attn, g_pre_mlp, w_mlp_in, w_mlp_out, g_post_mlp):
    b, s, d = x.shape
    depth = w_ada.shape[0]
    tabs = _rope_tables(s)

    for l in range(depth):
        b_ada_l = b_ada[l][None, :]
        n_early = N_MOD_EARLY * d
        mod_early = _ada(c, w_ada[l], b_ada_l, n_early)

        q_a, k_a, v_a, q_b, k_b, v_b = _inproj(
            x, mod_early, g_pre_attn[l][None, :], w_in[l].T, w_kv_b[l],
            g_q_a[l][None, :], g_k_a[l][None, :], g_ckv[l][None, :], tabs)

        o_b, wo_bf, mod_late = _attention(
            q_b, k_b, v_b, [(w_out[l], None)], n_heads=HB // N_GROUPS, dk=QK_B_PAD,
            shared_kv=False, name="attn_mla",
            ada=(c, w_ada[l], b_ada_l, n_early), vmem_limit=VMEM_LIMIT_BIG)
        o_a, w2_bf, w1_bf = _attention(
            q_a, k_a, v_a, [(w_mlp_out[l], None), (w_mlp_in[l], MLP_SUB)],
            n_heads=G_A, dk=HEAD_DIM, shared_kv=True, name="attn_gqa")

        x = _outproj(o_a, o_b, x, mod_late, g_out_a[l][None, :], g_out_b[l][None, :],
                     wo_bf, g_post_attn[l][None, :])

        x = _mlp(x, mod_late, g_pre_mlp[l][None, :], w1_bf, w2_bf, g_post_mlp[l][None, :])
    return x
```
